```python
import math
import jax, jax.numpy as jnp
from jax import lax
import numpy as np

D_MODEL = 1024
BATCH = 4
SEQ = 8192
DEPTH = 4

N_M = 4
HD_M = 128
MW = N_M * HD_M
MLSTM_CHUNK = 128
CONV_W = 5
N_A = 4
HD_A = 64
AW = N_A * 2 * HD_A
D_MIX = MW + AW
Q_BLOCK = 128
REL_BUCKETS = 32
REL_MAX_DIST = 128
D_FF = 4 * D_MODEL
EPS = 1e-6
C_QK_M = 0
C_V_M = 2 * MW
C_O_M = 3 * MW
C_G = 4 * MW
C_Q_A = C_G + 4 * N_M
C_K_A = C_Q_A + AW
C_V_A = C_K_A + AW
D_IN = C_V_A + AW

kernel_name = 'hybrid_mlstm_diffattn_block'


def rms(x, g):
    xf = x.astype(jnp.float32)
    return xf * lax.rsqrt(jnp.mean(xf * xf, axis=-1, keepdims=True) + EPS) * g.astype(jnp.float32)


def t5_bucket(rel):
    nb = REL_BUCKETS // 2
    max_exact = nb // 2
    n = jnp.abs(rel)
    is_small = n < max_exact
    nf = jnp.maximum(n, 1).astype(jnp.float32)
    large = max_exact + (jnp.log(nf / max_exact) / math.log(REL_MAX_DIST / max_exact) * (nb - max_exact)).astype(jnp.int32)
    large = jnp.minimum(large, nb - 1)
    return jnp.where(rel > 0, nb, 0) + jnp.where(is_small, n, large)


def dwconv_centred(x, w, b):
    C = x.shape[-1]
    y = lax.conv_general_dilated(x, w[:, None, :].astype(x.dtype), window_strides=(1,), padding='SAME',
                                 dimension_numbers=('NWC', 'WIO', 'NWC'), feature_group_count=C)
    return y + b.astype(x.dtype)


def to_heads(t, H, d):
    B, S, _ = t.shape
    return t.reshape(B, S, H, d).transpose(0, 2, 1, 3)


def mlstm_chunkwise(q, k, v, log_i, log_f):
    B, H, S, d = q.shape
    L = MLSTM_CHUNK
    nc = S // L

    def chunks(t):
        return jnp.moveaxis(t.reshape(B, H, nc, L, *t.shape[3:]), 2, 0)

    tri = jnp.tril(jnp.ones((L, L), dtype=bool))

    def step(carry, inp):
        C, n, m = carry
        qc, kc, vc, ic, fc = inp
        b = jnp.cumsum(fc, axis=-1)
        D = jnp.where(tri, b[..., :, None] - b[..., None, :] + ic[..., None, :], -jnp.inf)
        inter = b + m[..., None]
        m_t = jnp.maximum(inter, jnp.max(D, axis=-1))
        scale = jnp.exp(inter - m_t)
        wqk = jnp.einsum('bhtd,bhsd->bhts', qc, kc) * jnp.exp(D - m_t[..., None])
        num = scale[..., None] * jnp.einsum('bhtk,bhkv->bhtv', qc, C) + jnp.einsum('bhts,bhsv->bhtv', wqk, vc)
        den = scale * jnp.einsum('bhtk,bhk->bht', qc, n) + jnp.sum(wqk, axis=-1)
        h = num / jnp.maximum(jnp.abs(den), jnp.exp(-m_t))[..., None]
        bL = b[..., -1]
        g = bL[..., None] - b + ic
        m_new = jnp.maximum(bL + m, jnp.max(g, axis=-1))
        decay = jnp.exp(bL + m - m_new)
        ws = jnp.exp(g - m_new[..., None])
        C_new = decay[..., None, None] * C + jnp.einsum('bhs,bhsk,bhsv->bhkv', ws, kc, vc)
        n_new = decay[..., None] * n + jnp.einsum('bhs,bhsk->bhk', ws, kc)
        return (C_new, n_new, m_new), h

    init = (jnp.zeros((B, H, d, d), jnp.float32), jnp.zeros((B, H, d), jnp.float32), jnp.zeros((B, H), jnp.float32))
    _, hs = lax.scan(step, init, (chunks(q), chunks(k), chunks(v), chunks(log_i), chunks(log_f)))
    return jnp.moveaxis(hs, 0, 2).reshape(B, H, S, d)


def mlstm_group(qk, v, o, gates, gate_b, norm_g):
    B, S, _ = v.shape
    q = to_heads(qk[..., :MW], N_M, HD_M).astype(jnp.float32)
    k = to_heads(qk[..., MW:], N_M, HD_M).astype(jnp.float32) * (HD_M ** -0.5)
    vh = to_heads(v, N_M, HD_M).astype(jnp.float32)
    g = gates.astype(jnp.float32).reshape(B, S, 4, N_M) + gate_b.astype(jnp.float32)
    g = g.transpose(2, 0, 3, 1)
    hf = mlstm_chunkwise(q, k, vh, g[0], jax.nn.log_sigmoid(g[1]))
    fl = lambda t: jnp.flip(t, axis=2)
    hb = fl(mlstm_chunkwise(fl(q), fl(k), fl(vh), fl(g[2]), fl(jax.nn.log_sigmoid(g[3]))))
    hs = rms(hf + hb, norm_g.reshape(N_M, 1, HD_M))
    hs = hs.transpose(0, 2, 1, 3).reshape(B, S, MW)
    return jax.nn.sigmoid(o.astype(jnp.float32)) * hs


def diff_attention_group(qa, ka, va, q_g, k_g, lam_vec, lam_init, rel_bias, sub_g):
    B, S, _ = qa.shape
    q = qa.reshape(B, S, N_A, 2, HD_A).transpose(0, 3, 2, 1, 4)
    k = ka.reshape(B, S, N_A, 2, HD_A).transpose(0, 3, 2, 1, 4)
    q = rms(q, q_g) * (HD_A ** -0.5)
    k = rms(k, k_g)
    vh = va.reshape(B, S, N_A, 2 * HD_A).transpose(0, 2, 1, 3).astype(jnp.float32)
    lv = lam_vec.astype(jnp.float32)
    lam = jnp.exp(jnp.sum(lv[0] * lv[1])) - jnp.exp(jnp.sum(lv[2] * lv[3])) + lam_init
    nq = S // Q_BLOCK
    qb = q.reshape(B, 2, N_A, nq, Q_BLOCK, HD_A).transpose(3, 0, 1, 2, 4, 5)
    starts = jnp.arange(nq, dtype=jnp.int32) * Q_BLOCK
    kpos = jnp.arange(S, dtype=jnp.int32)
    table = rel_bias.astype(jnp.float32)

    def block(args):
        qblk, start = args
        qpos = start + jnp.arange(Q_BLOCK, dtype=jnp.int32)
        bias = table[t5_bucket(kpos[None, :] - qpos[:, None])].transpose(2, 0, 1)
        logits = jnp.einsum('bmhqd,bmhkd->bmhqk', qblk, k) + bias
        p = jax.nn.softmax(logits, axis=-1)
        a = p[:, 0] - lam * p[:, 1]
        return jnp.einsum('bhqk,bhkv->bhqv', a, vh)

    out = lax.map(block, (qb, starts))
    out = out.transpose(1, 2, 0, 3, 4).reshape(B, N_A, S, 2 * HD_A)
    out = rms(out, sub_g.reshape(N_A, 1, 2 * HD_A)) * (1.0 - lam_init)
    return out.transpose(0, 2, 1, 3).reshape(B, S, AW)


def setup_inputs(seed: int = 0) -> dict:
    key = jax.random.key(seed)
    ks = jax.random.split(key, 20)
    f32 = jnp.float32
    nrm = lambda k, s, sc: jax.random.normal(k, s, f32) * sc
    forget_base = jnp.array([0.0, 1.0, 0.0, 1.0], f32)[:, None] * jnp.linspace(3.0, 6.0, N_M, dtype=f32)[None, :]
    return {
        'x': nrm(ks[0], (BATCH, SEQ, D_MODEL), 1.0),
        'norm1_g': 1.0 + nrm(ks[1], (DEPTH, D_MODEL), 0.02),
        'w_in': nrm(ks[2], (DEPTH, D_MODEL, D_IN), D_MODEL ** -0.5),
        'conv_w': nrm(ks[3], (DEPTH, CONV_W, 2 * MW), CONV_W ** -0.5),
        'conv_b': nrm(ks[4], (DEPTH, 2 * MW), 0.01),
        'gate_b': forget_base[None] + nrm(ks[5], (DEPTH, 4, N_M), 0.1),
        'mlstm_norm_g': 1.0 + nrm(ks[6], (DEPTH, MW), 0.02),
        'q_norm_g': 1.0 + nrm(ks[7], (DEPTH, HD_A), 0.02),
        'k_norm_g': 1.0 + nrm(ks[8], (DEPTH, HD_A), 0.02),
        'lambdas': nrm(ks[9], (DEPTH, 4, HD_A), 0.1),
        'diff_norm_g': 1.0 + nrm(ks[10], (DEPTH, AW), 0.02),
        'rel_bias': nrm(ks[11], (REL_BUCKETS, N_A), 0.5),
        'w_out': nrm(ks[12], (DEPTH, D_MIX, D_MODEL), D_MIX ** -0.5),
        'norm2_g': 1.0 + nrm(ks[13], (DEPTH, D_MODEL), 0.02),
        'w_ff1': nrm(ks[14], (DEPTH, D_MODEL, D_FF), D_MODEL ** -0.5),
        'w_ff2': nrm(ks[15], (DEPTH, D_FF, D_MODEL), D_FF ** -0.5),
    }


def reference(x, norm1_g, w_in, conv_w, conv_b, gate_b, mlstm_norm_g, q_norm_g, k_norm_g, lambdas,
              diff_norm_g, rel_bias, w_out, norm2_g, w_ff1, w_ff2):
    dt = x.dtype
    for l in range(DEPTH):
        lam_init = 0.8 - 0.6 * math.exp(-0.3 * l)
        h = rms(x, norm1_g[l]).astype(dt)
        p = h @ w_in[l]
        qk_m = jax.nn.silu(dwconv_centred(p[..., C_QK_M:C_V_M], conv_w[l], conv_b[l]))
        mix_a = mlstm_group(qk_m, p[..., C_V_M:C_O_M], p[..., C_O_M:C_G], p[..., C_G:C_Q_A],
                            gate_b[l], mlstm_norm_g[l])
        mix_b = diff_attention_group(p[..., C_Q_A:C_K_A], p[..., C_K_A:C_V_A], p[..., C_V_A:D_IN],
                                     q_norm_g[l], k_norm_g[l], lambdas[l], lam_init, rel_bias, diff_norm_g[l])
        mixed = jnp.concatenate([mix_a, mix_b], axis=-1).astype(dt)
        x = x + (mixed @ w_out[l]).astype(dt)
        u = rms(x, norm2_g[l]).astype(dt) @ w_ff1[l]
        u = jnp.square(jax.nn.relu(u))
        x = x + (u @ w_ff2[l]).astype(dt)
    return x
```

```python
import functools
import math

import jax
import jax.numpy as jnp
from jax import lax
from jax.experimental import pallas as pl
from jax.experimental.pallas import tpu as pltpu

F32 = jnp.float32
BF16 = jnp.bfloat16

N_M = 4
HD_M = 128
MW = N_M * HD_M
MLSTM_CHUNK = 128
CONV_W = 5
N_A = 4
HD_A = 64
AW = N_A * 2 * HD_A
REL_BUCKETS = 32
REL_MAX_DIST = 128
EPS = 1e-6
N_GATES = 4 * N_M
C_G = 4 * MW
LANES = 128
HALO = 16
VMEM_LIMIT = 56 * 1024 * 1024


def _dot(a, b):
    return jnp.dot(a, b, preferred_element_type=F32)


def _dot_nt(a, b):
    return lax.dot_general(a, b, (((1,), (1,)), ((), ())), preferred_element_type=F32)


def _dot_tn(a, b):
    return lax.dot_general(a, b, (((0,), (0,)), ((), ())), preferred_element_type=F32)


def _dot_exact(a, b):
    return jnp.dot(a, b, preferred_element_type=F32, precision=lax.Precision.HIGHEST)


def _sigmoid(x):
    return 1.0 / (1.0 + jnp.exp(-x))


def _log_sigmoid(x):
    return jnp.minimum(x, 0.0) - jnp.log1p(jnp.exp(-jnp.abs(x)))


def _resident(shape):
    nd = len(shape)
    return pl.BlockSpec(shape, lambda *_: (0,) * nd, pipeline_mode=pl.Buffered(1))


def _inproj_kernel(x_ref, g1_ref, w_ref, wg_ref, qg_ref, kg_ref, bd_ref,
                   pm_ref, gates_ref, qn_ref, kn_ref, va_ref):
    x = x_ref[...]
    ms = jnp.mean(x * x, axis=-1, keepdims=True)
    h = (x * lax.rsqrt(ms + EPS) * g1_ref[...]).astype(BF16)

    def proj(c0, width):
        return _dot(h, w_ref[:, c0:c0 + width])

    for j in range(4 * MW // 512):
        pm_ref[:, j * 512:(j + 1) * 512] = proj(j * 512, 512).astype(BF16)
    gates_ref[...] = _dot(h, wg_ref[...])[:, :N_GATES]

    bd = bd_ref[...]

    def qk_norm(c0, g_ref, scale, out_ref):
        for hh in range(N_A):
            t = proj(c0 + hh * LANES, LANES)
            sq = t * t
            hi = sq.astype(BF16)
            lo = (sq - hi.astype(F32)).astype(BF16)
            msq = _dot(hi, bd) + _dot(lo, bd)
            y = t * lax.rsqrt(msq + EPS) * g_ref[...]
            if scale != 1.0:
                y = y * scale
            out_ref[:, hh * LANES:(hh + 1) * LANES] = y.astype(BF16)

    qk_norm(4 * MW, qg_ref, HD_A ** -0.5, qn_ref)
    qk_norm(4 * MW + AW, kg_ref, 1.0, kn_ref)
    va_ref[...] = proj(4 * MW + 2 * AW, AW).astype(BF16)


def _inproj(x, g1, w_main, w_gate, qg, kg, bd, tm):
    n, d = x.shape
    wcols = w_main.shape[1]
    row = lambda i: (i, 0)
    return pl.pallas_call(
        _inproj_kernel,
        grid=(n // tm,),
        in_specs=[
            pl.BlockSpec((tm, d), row),
            _resident((1, d)),
            _resident((d, wcols)),
            _resident((d, LANES)),
            _resident((1, LANES)),
            _resident((1, LANES)),
            _resident((LANES, LANES)),
        ],
        out_specs=[
            pl.BlockSpec((tm, 4 * MW), row),
            pl.BlockSpec((tm, N_GATES), row),
            pl.BlockSpec((tm, AW), row),
            pl.BlockSpec((tm, AW), row),
            pl.BlockSpec((tm, AW), row),
        ],
        out_shape=[
            jax.ShapeDtypeStruct((n, 4 * MW), BF16),
            jax.ShapeDtypeStruct((n, N_GATES), F32),
            jax.ShapeDtypeStruct((n, AW), BF16),
            jax.ShapeDtypeStruct((n, AW), BF16),
            jax.ShapeDtypeStruct((n, AW), BF16),
        ],
        compiler_params=pltpu.CompilerParams(
            dimension_semantics=("parallel",), vmem_limit_bytes=VMEM_LIMIT),
        name="inproj",
    )(x, g1, w_main, w_gate, qg, kg, bd)


def _mlstm_kernel(qkp_f, qkc_f, qkn_f, v_f, g_f, gt_f,
                  qkp_b, qkc_b, qkn_b, v_b, g_b, gt_b,
                  cw_ref, cb_ref, gbr_ref, gbc_ref,
                  hf_ref, hb_ref,
                  c_s, n_s, m_s, ext_s):
    L = MLSTM_CHUNK
    c = pl.program_id(1)
    last = pl.num_programs(1) - 1

    @pl.when(c == 0)
    def _():
        c_s[...] = jnp.zeros_like(c_s)
        n_s[...] = jnp.zeros_like(n_s)
        m_s[...] = jnp.zeros_like(m_s)

    row = lax.broadcasted_iota(jnp.int32, (L, L), 0)
    col = lax.broadcasted_iota(jnp.int32, (L, L), 1)
    lower = col <= row
    upper = row <= col
    lower_f = lower.astype(F32)
    upper_f = upper.astype(F32)

    dirs = (
        (qkp_f, qkc_f, qkn_f, v_f, g_f, gt_f, hf_ref, c > 0, c < last),
        (qkp_b, qkc_b, qkn_b, v_b, g_b, gt_b, hb_ref, c < last, c > 0),
    )
    for d, (qkp, qkc, qkn, v_ref, g_ref, gt_ref, out_ref, has_prev, has_next) in enumerate(dirs):
        ext = ext_s.at[d]
        ext[0:HALO, :] = jnp.where(has_prev, qkp[...].astype(F32), 0.0)
        ext[HALO:HALO + L, :] = qkc[...].astype(F32)
        ext[HALO + L:2 * HALO + L, :] = jnp.where(has_next, qkn[...].astype(F32), 0.0)
        conv = cb_ref[...]
        for j in range(CONV_W):
            off = HALO - CONV_W // 2 + j
            conv = conv + ext[off:off + L, :] * cw_ref[j:j + 1, :]
        qk = conv * _sigmoid(conv)

        gates = g_ref[...] + gbr_ref[...]
        gates_t = gt_ref[...] + gbc_ref[...]
        logf = _log_sigmoid(gates)
        logf_t = _log_sigmoid(gates_t)
        cum = _dot_exact(lower_f, logf)
        cum_t = _dot_exact(logf_t, upper_f)
        if d == 1:
            cum = cum[L - 1:L, :] - cum + logf
            cum_t = cum_t[:, L - 1:L] - cum_t + logf_t
        i_off = 2 * N_M * d
        f_off = i_off + N_M
        mask = lower if d == 0 else upper

        for hh in range(N_M):
            idx = d * N_M + hh
            b_col = cum[:, f_off + hh:f_off + hh + 1]
            i_col = gates[:, i_off + hh:i_off + hh + 1]
            b_row = cum_t[f_off + hh:f_off + hh + 1, :]
            i_row = gates_t[i_off + hh:i_off + hh + 1, :]
            dmat = jnp.where(mask, b_col - b_row + i_row, -jnp.inf)
            m_prev = m_s[idx][:, :1]
            inter = b_col + m_prev
            m_t = jnp.maximum(inter, jnp.max(dmat, axis=-1, keepdims=True))
            scale = jnp.exp(inter - m_t)
            q = qk[:, hh * HD_M:(hh + 1) * HD_M]
            k = qk[:, MW + hh * HD_M:MW + (hh + 1) * HD_M] * (HD_M ** -0.5)
            qb = q.astype(BF16)
            kb = k.astype(BF16)
            vb = v_ref[:, hh * HD_M:(hh + 1) * HD_M]
            w = _dot_nt(qb, kb) * jnp.exp(dmat - m_t)
            c_prev = c_s[idx]
            n_prev = n_s[idx]
            num = scale * _dot(qb, c_prev.astype(BF16)) + _dot(w.astype(BF16), vb)
            den = (scale * jnp.sum(q * n_prev, axis=-1, keepdims=True)
                   + jnp.sum(w, axis=-1, keepdims=True))
            hval = num / jnp.maximum(jnp.abs(den), jnp.exp(-m_t))
            out_ref[:, hh * HD_M:(hh + 1) * HD_M] = hval.astype(out_ref.dtype)

            b_end = b_col[L - 1:L, :] if d == 0 else b_col[0:1, :]
            g_col = b_end - b_col + i_col
            m_new = jnp.maximum(b_end + m_prev, jnp.max(g_col, axis=0, keepdims=True))
            decay = jnp.exp(b_end + m_prev - m_new)
            kw = k * jnp.exp(g_col - m_new)
            c_s[idx] = decay * c_prev + _dot_tn(kw.astype(BF16), vb)
            n_s[idx] = decay * n_prev + jnp.sum(kw, axis=0, keepdims=True)
            m_s[idx] = jnp.broadcast_to(m_new, (1, LANES))


def _mlstm(pm, gates, gates_t, conv_w, conv_b, gb_row, gb_col, batch, seq):
    L = MLSTM_CHUNK
    nc = seq // L
    hpc = L // HALO
    nhalo = batch * seq // HALO

    def chunk_f(b, c):
        return b * nc + c

    def chunk_b(b, c):
        return b * nc + (nc - 1 - c)

    def specs(chunk):
        prev_i = lambda b, c: (jnp.maximum(chunk(b, c) * hpc - 1, 0), 0)
        next_i = lambda b, c: (jnp.minimum((chunk(b, c) + 1) * hpc, nhalo - 1), 0)
        return [
            pl.BlockSpec((HALO, 2 * MW), prev_i),
            pl.BlockSpec((L, 2 * MW), lambda b, c: (chunk(b, c), 0)),
            pl.BlockSpec((HALO, 2 * MW), next_i),
            pl.BlockSpec((L, MW), lambda b, c: (chunk(b, c), 2)),
            pl.BlockSpec((L, N_GATES), lambda b, c: (chunk(b, c), 0)),
            pl.BlockSpec((N_GATES, L), lambda b, c: (0, chunk(b, c))),
        ]

    n = batch * seq
    return pl.pallas_call(
        _mlstm_kernel,
        grid=(batch, nc),
        in_specs=specs(chunk_f) + specs(chunk_b) + [
            _resident((CONV_W, 2 * MW)),
            _resident((1, 2 * MW)),
            _resident((1, N_GATES)),
            _resident((N_GATES, 1)),
        ],
        out_specs=[
            pl.BlockSpec((L, MW), lambda b, c: (chunk_f(b, c), 0)),
            pl.BlockSpec((L, MW), lambda b, c: (chunk_b(b, c), 0)),
        ],
        out_shape=[jax.ShapeDtypeStruct((n, MW), BF16)] * 2,
        scratch_shapes=[
            pltpu.VMEM((2 * N_M, HD_M, HD_M), F32),
            pltpu.VMEM((2 * N_M, 1, HD_M), F32),
            pltpu.VMEM((2 * N_M, 1, LANES), F32),
            pltpu.VMEM((2, L + 2 * HALO, 2 * MW), F32),
        ],
        compiler_params=pltpu.CompilerParams(
            dimension_semantics=("parallel", "arbitrary"), vmem_limit_bytes=VMEM_LIMIT),
        name="mlstm",
    )(pm, pm, pm, pm, gates, gates_t, pm, pm, pm, pm, gates, gates_t,
      conv_w, conv_b, gb_row, gb_col)


def _bias_kernel(rb_ref, out_ref, *, tile):
    hh = pl.program_id(0)
    dd = pl.program_id(1)
    kk = lax.broadcasted_iota(jnp.int32, (tile, tile), 0)
    qq = lax.broadcasted_iota(jnp.int32, (tile, tile), 1)
    rel = (dd - 1) * tile + kk - qq
    nb = REL_BUCKETS // 2
    max_exact = nb // 2
    n = jnp.abs(rel)
    nf = jnp.maximum(n, 1).astype(F32)
    large = max_exact + (jnp.log(nf / max_exact) / math.log(REL_MAX_DIST / max_exact)
                         * (nb - max_exact)).astype(jnp.int32)
    large = jnp.minimum(large, nb - 1)
    bucket = jnp.where(rel > 0, nb, 0) + jnp.where(n < max_exact, n, large)
    val = jnp.zeros((tile, tile), F32)
    for bkt in range(REL_BUCKETS):
        val = jnp.where(bucket == bkt, rb_ref[bkt, hh], val)
    out_ref[0, 0] = val


def _bias_tiles(rel_bias, tile):
    return pl.pallas_call(
        functools.partial(_bias_kernel, tile=tile),
        grid=(N_A, 3),
        in_specs=[pl.BlockSpec(memory_space=pltpu.SMEM)],
        out_specs=pl.BlockSpec((1, 1, tile, tile), lambda h, d: (h, d, 0, 0)),
        out_shape=jax.ShapeDtypeStruct((N_A, 3, tile, tile), F32),
        name="rel_bias_tiles",
    )(rel_bias)


def _far_bucket(tile):
    assert tile >= REL_MAX_DIST
    return REL_BUCKETS // 2 - 1


def _attn_kernel(rb_ref, li_ref, q_ref, k_ref, v_ref, bias_ref, lam_ref, sg_ref,
                 o_ref,
                 qt1, qt2, m1, l1, acc1, m2, l2, acc2, *, far):
    hh = pl.program_id(1)
    qi = pl.program_id(2)
    ki = pl.program_id(3)
    tq = q_ref.shape[0]

    @pl.when(ki == 0)
    def _():
        qt = q_ref[...].astype(F32).T
        rid = lax.broadcasted_iota(jnp.int32, qt.shape, 0)
        qt1[...] = jnp.where(rid < HD_A, qt, 0.0).astype(BF16)
        qt2[...] = jnp.where(rid >= HD_A, qt, 0.0).astype(BF16)
        for m, l, acc in ((m1, l1, acc1), (m2, l2, acc2)):
            m[...] = jnp.full_like(m, -jnp.inf)
            l[...] = jnp.zeros_like(l)
            acc[...] = jnp.zeros_like(acc)

    def step(add_bias):
        k = k_ref[...]
        v = v_ref[...]
        for qt, m, l, acc in ((qt1, m1, l1, acc1), (qt2, m2, l2, acc2)):
            s = add_bias(_dot(k, qt[...]))
            m_new = jnp.maximum(m[...], jnp.max(s, axis=0, keepdims=True))
            alpha = jnp.exp(m[...] - m_new)
            p = jnp.exp(s - m_new)
            l[...] = alpha * l[...] + jnp.sum(p, axis=0, keepdims=True)
            acc[...] = alpha * acc[...] + _dot_tn(v, p.astype(BF16))
            m[...] = m_new

    delta = ki - qi

    @pl.when(jnp.abs(delta) >= 2)
    def _():
        cst = jnp.where(delta < 0, rb_ref[far, hh], rb_ref[REL_BUCKETS // 2 + far, hh])
        step(lambda s: s + cst)

    @pl.when(jnp.abs(delta) <= 1)
    def _():
        step(lambda s: s + bias_ref[0, delta + 1])

    @pl.when(ki == pl.num_programs(3) - 1)
    def _():
        lv = lam_ref[...]
        lam = (jnp.exp(jnp.sum(lv[0:1] * lv[1:2], axis=-1, keepdims=True))
               - jnp.exp(jnp.sum(lv[2:3] * lv[3:4], axis=-1, keepdims=True)) + li_ref[0, 0])
        o = acc1[...] / l1[...] - lam * (acc2[...] / l2[...])
        ms = jnp.mean(o * o, axis=0, keepdims=True)
        o = o * lax.rsqrt(ms + EPS) * sg_ref[0] * li_ref[0, 1]
        o_ref[...] = o.T.astype(o_ref.dtype)


def _attention(qn, kn, va, bias_t, rel_bias, lam_init, lambdas, sub_g_col, batch, seq, tile):
    nt = seq // tile
    vd = 2 * HD_A
    far = _far_bucket(tile)
    smem = pl.BlockSpec(memory_space=pltpu.SMEM)
    return pl.pallas_call(
        functools.partial(_attn_kernel, far=far),
        grid=(batch, N_A, nt, nt),
        in_specs=[
            smem,
            smem,
            pl.BlockSpec((tile, vd), lambda b, h, i, j: (b * nt + i, h)),
            pl.BlockSpec((tile, vd), lambda b, h, i, j: (b * nt + j, h)),
            pl.BlockSpec((tile, vd), lambda b, h, i, j: (b * nt + j, h)),
            pl.BlockSpec((1, 3, tile, tile), lambda b, h, i, j: (h, 0, 0, 0)),
            pl.BlockSpec((4, HD_A), lambda b, h, i, j: (0, 0)),
            pl.BlockSpec((1, vd, 1), lambda b, h, i, j: (h, 0, 0)),
        ],
        out_specs=pl.BlockSpec((tile, vd), lambda b, h, i, j: (b * nt + i, h)),
        out_shape=jax.ShapeDtypeStruct((batch * seq, AW), BF16),
        scratch_shapes=[
            pltpu.VMEM((vd, tile), BF16),
            pltpu.VMEM((vd, tile), BF16),
            pltpu.VMEM((1, tile), F32),
            pltpu.VMEM((1, tile), F32),
            pltpu.VMEM((vd, tile), F32),
            pltpu.VMEM((1, tile), F32),
            pltpu.VMEM((1, tile), F32),
            pltpu.VMEM((vd, tile), F32),
        ],
        compiler_params=pltpu.CompilerParams(
            dimension_semantics=("parallel", "parallel", "parallel", "arbitrary"),
            vmem_limit_bytes=VMEM_LIMIT),
        name="diff_attention",
    )(rel_bias, lam_init, qn, kn, va, bias_t, lambdas, sub_g_col)


def _outffn_kernel(x_ref, hf_ref, hb_ref, og_ref, mb_ref, ng_ref, wo_ref, g2_ref, w1_ref, w2_ref,
                   out_ref, *, ff_chunk):
    hs = hf_ref[...].astype(F32) + hb_ref[...].astype(F32)
    x1 = x_ref[...] + _dot(mb_ref[...], wo_ref[MW:, :])
    for hh in range(N_M):
        sl = slice(hh * HD_M, (hh + 1) * HD_M)
        t = hs[:, sl]
        ms = jnp.mean(t * t, axis=-1, keepdims=True)
        mix = t * lax.rsqrt(ms + EPS) * ng_ref[:, sl] * _sigmoid(og_ref[:, sl].astype(F32))
        x1 = x1 + _dot(mix.astype(BF16), wo_ref[sl, :])
    ms = jnp.mean(x1 * x1, axis=-1, keepdims=True)
    h2 = (x1 * lax.rsqrt(ms + EPS) * g2_ref[...]).astype(BF16)
    out_ref[...] = x1
    for j in range(w1_ref.shape[1] // ff_chunk):
        sl = slice(j * ff_chunk, (j + 1) * ff_chunk)
        u = jnp.maximum(_dot(h2, w1_ref[:, sl]), 0.0)
        out_ref[...] += _dot((u * u).astype(BF16), w2_ref[sl, :])


def _outffn(x, hf, hb, pm, mixb, ng, wo, g2, w1, w2, tm):
    n, d = x.shape
    dff = w1.shape[1]
    row = lambda i: (i, 0)
    return pl.pallas_call(
        functools.partial(_outffn_kernel, ff_chunk=1024),
        grid=(n // tm,),
        in_specs=[
            pl.BlockSpec((tm, d), row),
            pl.BlockSpec((tm, MW), row),
            pl.BlockSpec((tm, MW), row),
            pl.BlockSpec((tm, MW), lambda i: (i, 3)),
            pl.BlockSpec((tm, AW), row),
            _resident((1, MW)),
            _resident((MW + AW, d)),
            _resident((1, d)),
            _resident((d, dff)),
            _resident((dff, d)),
        ],
        out_specs=pl.BlockSpec((tm, d), row),
        out_shape=jax.ShapeDtypeStruct((n, d), F32),
        compiler_params=pltpu.CompilerParams(
            dimension_semantics=("parallel",), vmem_limit_bytes=VMEM_LIMIT),
        name="outproj_ffn",
    )(x, hf, hb, pm, mixb, ng, wo, g2, w1, w2)


def kernel(x, norm1_g, w_in, conv_w, conv_b, gate_b, mlstm_norm_g, q_norm_g, k_norm_g, lambdas,
           diff_norm_g, rel_bias, w_out, norm2_g, w_ff1, w_ff2):
    batch, seq, d = x.shape
    depth = w_in.shape[0]
    n = batch * seq
    tm = min(512, n)
    tile = min(512, seq)
    assert seq % MLSTM_CHUNK == 0 and seq % tile == 0 and n % tm == 0

    w_main = jnp.concatenate([w_in[:, :, :C_G], w_in[:, :, C_G + N_GATES:]], axis=-1).astype(BF16)
    w_gate = jnp.pad(w_in[:, :, C_G:C_G + N_GATES], ((0, 0), (0, 0), (0, LANES - N_GATES))).astype(BF16)
    w_out_b = w_out.astype(BF16)
    w_ff1_b = w_ff1.astype(BF16)
    w_ff2_b = w_ff2.astype(BF16)
    lane = jnp.arange(LANES)
    bd = jnp.where((lane[:, None] // HD_A) == (lane[None, :] // HD_A), 1.0 / HD_A, 0.0).astype(BF16)

    bias_t = _bias_tiles(rel_bias.astype(F32), tile)

    xf = x.reshape(n, d)
    for l in range(depth):
        lam_init = 0.8 - 0.6 * math.exp(-0.3 * l)
        pm, gates, qn, kn, va = _inproj(
            xf, norm1_g[l][None, :], w_main[l], w_gate[l],
            jnp.tile(q_norm_g[l], 2)[None, :], jnp.tile(k_norm_g[l], 2)[None, :], bd, tm)
        gb = gate_b[l].reshape(1, N_GATES).astype(F32)
        hf, hb = _mlstm(pm, gates, gates.T, conv_w[l], conv_b[l][None, :], gb, gb.T, batch, seq)
        mixb = _attention(qn, kn, va, bias_t, rel_bias.astype(F32),
                          jnp.array([[lam_init, 1.0 - lam_init]], F32), lambdas[l],
                          diff_norm_g[l].reshape(N_A, 2 * HD_A, 1), batch, seq, tile)
        xf = _outffn(xf, hf, hb, pm, mixb, mlstm_norm_g[l][None, :], w_out_b[l],
                     norm2_g[l][None, :], w_ff1_b[l], w_ff2_b[l], tm)
    return xf.reshape(batch, seq, d)
```

```python
import functools
import math

import jax
import jax.numpy as jnp
from jax import lax
from jax.experimental import pallas as pl
from jax.experimental.pallas import tpu as pltpu

F32 = jnp.float32
BF16 = jnp.bfloat16

N_M = 4
HD_M = 128
MW = N_M * HD_M
MLSTM_CHUNK = 128
CONV_W = 5
N_A = 4
HD_A = 64
AW = N_A * 2 * HD_A
REL_BUCKETS = 32
REL_MAX_DIST = 128
EPS = 1e-6
N_GATES = 4 * N_M
C_G = 4 * MW
LANES = 128
HALO = 16
VMEM_LIMIT = 56 * 1024 * 1024
LOG2E = math.log2(math.e)
VT_ROWS = 2 * HD_A + HALO


def _dot(a, b):
    return jnp.dot(a, b, preferred_element_type=F32)


def _dot_nt(a, b):
    return lax.dot_general(a, b, (((1,), (1,)), ((), ())), preferred_element_type=F32)


def _dot_tn(a, b):
    return lax.dot_general(a, b, (((0,), (0,)), ((), ())), preferred_element_type=F32)


def _dot_exact(a, b):
    return jnp.dot(a, b, preferred_element_type=F32, precision=lax.Precision.HIGHEST)


def _sigmoid(x):
    return 1.0 / (1.0 + jnp.exp(-x))


def _log_sigmoid(x):
    return jnp.minimum(x, 0.0) - jnp.log1p(jnp.exp(-jnp.abs(x)))


def _resident(shape):
    nd = len(shape)
    return pl.BlockSpec(shape, lambda *_: (0,) * nd, pipeline_mode=pl.Buffered(1))


def _inproj_kernel(x_ref, g1_ref, w_ref, wg_ref, qg_ref, kg_ref, bd_ref,
                   pm_ref, gates_ref, qn_ref, k1_ref, k2_ref, vt_ref):
    x = x_ref[...]
    ms = jnp.mean(x * x, axis=-1, keepdims=True)
    h = (x * lax.rsqrt(ms + EPS) * g1_ref[...]).astype(BF16)

    def proj(c0, width):
        return _dot(h, w_ref[:, c0:c0 + width])

    for j in range(4 * MW // 512):
        pm_ref[:, j * 512:(j + 1) * 512] = proj(j * 512, 512).astype(BF16)
    gates_ref[...] = _dot(h, wg_ref[...])[:, :N_GATES]

    bd = bd_ref[...]

    def qk_norm(c0, g_ref):
        t = proj(c0, LANES)
        sq = t * t
        hi = sq.astype(BF16)
        lo = (sq - hi.astype(F32)).astype(BF16)
        msq = _dot(hi, bd) + _dot(lo, bd)
        return t * lax.rsqrt(msq + EPS) * g_ref[...]

    lane = lax.broadcasted_iota(jnp.int32, (x.shape[0], LANES), 1)
    ones_cols = jnp.where(lane < HD_A + 2, 1.0, 0.0)
    for hh in range(N_A):
        sl = slice(hh * LANES, (hh + 1) * LANES)
        qn_ref[:, sl] = (qk_norm(4 * MW + hh * LANES, qg_ref) * (HD_A ** -0.5 * LOG2E)).astype(BF16)
        kn = qk_norm(4 * MW + AW + hh * LANES, kg_ref)
        k1_ref[:, sl] = jnp.where(lane < HD_A, kn, ones_cols).astype(BF16)
        k2_ref[:, sl] = jnp.where(lane < HD_A, pltpu.roll(kn, HD_A, 1), ones_cols).astype(BF16)
        vt = proj(4 * MW + 2 * AW + hh * LANES, LANES).T
        vt_ref[0, hh * VT_ROWS:hh * VT_ROWS + 2 * HD_A, :] = vt.astype(BF16)
        vt_ref[0, hh * VT_ROWS + 2 * HD_A:(hh + 1) * VT_ROWS, :] = jnp.ones((HALO, x.shape[0]), BF16)


def _inproj(x, g1, w_main, w_gate, qg, kg, bd, tm):
    n, d = x.shape
    wcols = w_main.shape[1]
    row = lambda i: (i, 0)
    return pl.pallas_call(
        _inproj_kernel,
        grid=(n // tm,),
        in_specs=[
            pl.BlockSpec((tm, d), row),
            _resident((1, d)),
            _resident((d, wcols)),
            _resident((d, LANES)),
            _resident((1, LANES)),
            _resident((1, LANES)),
            _resident((LANES, LANES)),
        ],
        out_specs=[
            pl.BlockSpec((tm, 4 * MW), row),
            pl.BlockSpec((tm, N_GATES), row),
            pl.BlockSpec((tm, AW), row),
            pl.BlockSpec((tm, AW), row),
            pl.BlockSpec((tm, AW), row),
            pl.BlockSpec((1, N_A * VT_ROWS, tm), lambda i: (i, 0, 0)),
        ],
        out_shape=[
            jax.ShapeDtypeStruct((n, 4 * MW), BF16),
            jax.ShapeDtypeStruct((n, N_GATES), F32),
            jax.ShapeDtypeStruct((n, AW), BF16),
            jax.ShapeDtypeStruct((n, AW), BF16),
            jax.ShapeDtypeStruct((n, AW), BF16),
            jax.ShapeDtypeStruct((n // tm, N_A * VT_ROWS, tm), BF16),
        ],
        compiler_params=pltpu.CompilerParams(
            dimension_semantics=("parallel",), vmem_limit_bytes=VMEM_LIMIT),
        name="inproj",
    )(x, g1, w_main, w_gate, qg, kg, bd)


def _mlstm_kernel(qkp_f, qkc_f, qkn_f, v_f, g_f, gt_f,
                  qkp_b, qkc_b, qkn_b, v_b, g_b, gt_b,
                  cw_ref, cb_ref, gbr_ref, gbc_ref,
                  hf_ref, hb_ref,
                  c_s, n_s, m_s, ext_s):
    L = MLSTM_CHUNK
    c = pl.program_id(1)
    last = pl.num_programs(1) - 1

    @pl.when(c == 0)
    def _():
        c_s[...] = jnp.zeros_like(c_s)
        n_s[...] = jnp.zeros_like(n_s)
        m_s[...] = jnp.zeros_like(m_s)

    row = lax.broadcasted_iota(jnp.int32, (L, L), 0)
    col = lax.broadcasted_iota(jnp.int32, (L, L), 1)
    lower = col <= row
    upper = row <= col
    lower_f = lower.astype(F32)
    upper_f = upper.astype(F32)

    dirs = (
        (qkp_f, qkc_f, qkn_f, v_f, g_f, gt_f, hf_ref, c > 0, c < last),
        (qkp_b, qkc_b, qkn_b, v_b, g_b, gt_b, hb_ref, c < last, c > 0),
    )
    for d, (qkp, qkc, qkn, v_ref, g_ref, gt_ref, out_ref, has_prev, has_next) in enumerate(dirs):
        ext = ext_s.at[d]
        ext[0:HALO, :] = jnp.where(has_prev, qkp[...].astype(F32), 0.0)
        ext[HALO:HALO + L, :] = qkc[...].astype(F32)
        ext[HALO + L:2 * HALO + L, :] = jnp.where(has_next, qkn[...].astype(F32), 0.0)
        conv = cb_ref[...]
        for j in range(CONV_W):
            off = HALO - CONV_W // 2 + j
            conv = conv + ext[off:off + L, :] * cw_ref[j:j + 1, :]
        qk = conv * _sigmoid(conv)

        gates = g_ref[...] + gbr_ref[...]
        gates_t = gt_ref[...] + gbc_ref[...]
        logf = _log_sigmoid(gates)
        logf_t = _log_sigmoid(gates_t)
        cum = _dot_exact(lower_f, logf)
        cum_t = _dot_exact(logf_t, upper_f)
        if d == 1:
            cum = cum[L - 1:L, :] - cum + logf
            cum_t = cum_t[:, L - 1:L] - cum_t + logf_t
        i_off = 2 * N_M * d
        f_off = i_off + N_M
        mask = lower if d == 0 else upper

        for hh in range(N_M):
            idx = d * N_M + hh
            b_col = cum[:, f_off + hh:f_off + hh + 1]
            i_col = gates[:, i_off + hh:i_off + hh + 1]
            b_row = cum_t[f_off + hh:f_off + hh + 1, :]
            i_row = gates_t[i_off + hh:i_off + hh + 1, :]
            dmat = jnp.where(mask, b_col - b_row + i_row, -jnp.inf)
            m_prev = m_s[idx][:, :1]
            inter = b_col + m_prev
            m_t = jnp.maximum(inter, jnp.max(dmat, axis=-1, keepdims=True))
            scale = jnp.exp(inter - m_t)
            q = qk[:, hh * HD_M:(hh + 1) * HD_M]
            k = qk[:, MW + hh * HD_M:MW + (hh + 1) * HD_M] * (HD_M ** -0.5)
            qb = q.astype(BF16)
            kb = k.astype(BF16)
            vb = v_ref[:, hh * HD_M:(hh + 1) * HD_M]
            w = _dot_nt(qb, kb) * jnp.exp(dmat - m_t)
            c_prev = c_s[idx]
            n_prev = n_s[idx]
            num = scale * _dot(qb, c_prev.astype(BF16)) + _dot(w.astype(BF16), vb)
            den = (scale * jnp.sum(q * n_prev, axis=-1, keepdims=True)
                   + jnp.sum(w, axis=-1, keepdims=True))
            hval = num / jnp.maximum(jnp.abs(den), jnp.exp(-m_t))
            out_ref[:, hh * HD_M:(hh + 1) * HD_M] = hval.astype(out_ref.dtype)

            b_end = b_col[L - 1:L, :] if d == 0 else b_col[0:1, :]
            g_col = b_end - b_col + i_col
            m_new = jnp.maximum(b_end + m_prev, jnp.max(g_col, axis=0, keepdims=True))
            decay = jnp.exp(b_end + m_prev - m_new)
            kw = k * jnp.exp(g_col - m_new)
            c_s[idx] = decay * c_prev + _dot_tn(kw.astype(BF16), vb)
            n_s[idx] = decay * n_prev + jnp.sum(kw, axis=0, keepdims=True)
            m_s[idx] = jnp.broadcast_to(m_new, (1, LANES))


def _mlstm(pm, gates, gates_t, conv_w, conv_b, gb_row, gb_col, batch, seq):
    L = MLSTM_CHUNK
    nc = seq // L
    hpc = L // HALO
    nhalo = batch * seq // HALO

    def chunk_f(b, c):
        return b * nc + c

    def chunk_b(b, c):
        return b * nc + (nc - 1 - c)

    def specs(chunk):
        prev_i = lambda b, c: (jnp.maximum(chunk(b, c) * hpc - 1, 0), 0)
        next_i = lambda b, c: (jnp.minimum((chunk(b, c) + 1) * hpc, nhalo - 1), 0)
        return [
            pl.BlockSpec((HALO, 2 * MW), prev_i),
            pl.BlockSpec((L, 2 * MW), lambda b, c: (chunk(b, c), 0)),
            pl.BlockSpec((HALO, 2 * MW), next_i),
            pl.BlockSpec((L, MW), lambda b, c: (chunk(b, c), 2)),
            pl.BlockSpec((L, N_GATES), lambda b, c: (chunk(b, c), 0)),
            pl.BlockSpec((N_GATES, L), lambda b, c: (0, chunk(b, c))),
        ]

    n = batch * seq
    return pl.pallas_call(
        _mlstm_kernel,
        grid=(batch, nc),
        in_specs=specs(chunk_f) + specs(chunk_b) + [
            _resident((CONV_W, 2 * MW)),
            _resident((1, 2 * MW)),
            _resident((1, N_GATES)),
            _resident((N_GATES, 1)),
        ],
        out_specs=[
            pl.BlockSpec((L, MW), lambda b, c: (chunk_f(b, c), 0)),
            pl.BlockSpec((L, MW), lambda b, c: (chunk_b(b, c), 0)),
        ],
        out_shape=[jax.ShapeDtypeStruct((n, MW), BF16)] * 2,
        scratch_shapes=[
            pltpu.VMEM((2 * N_M, HD_M, HD_M), F32),
            pltpu.VMEM((2 * N_M, 1, HD_M), F32),
            pltpu.VMEM((2 * N_M, 1, LANES), F32),
            pltpu.VMEM((2, L + 2 * HALO, 2 * MW), F32),
        ],
        compiler_params=pltpu.CompilerParams(
            dimension_semantics=("parallel", "arbitrary"), vmem_limit_bytes=VMEM_LIMIT),
        name="mlstm",
    )(pm, pm, pm, pm, gates, gates_t, pm, pm, pm, pm, gates, gates_t,
      conv_w, conv_b, gb_row, gb_col)


def _bias_kernel(rb_ref, out_ref, *, tile):
    hh = pl.program_id(0)
    dd = pl.program_id(1)
    kk = lax.broadcasted_iota(jnp.int32, (tile, tile), 0)
    qq = lax.broadcasted_iota(jnp.int32, (tile, tile), 1)
    rel = (dd - 1) * tile + kk - qq
    nb = REL_BUCKETS // 2
    max_exact = nb // 2
    n = jnp.abs(rel)
    nf = jnp.maximum(n, 1).astype(F32)
    large = max_exact + (jnp.log(nf / max_exact) / math.log(REL_MAX_DIST / max_exact)
                         * (nb - max_exact)).astype(jnp.int32)
    large = jnp.minimum(large, nb - 1)
    bucket = jnp.where(rel > 0, nb, 0) + jnp.where(n < max_exact, n, large)
    val = jnp.zeros((tile, tile), F32)
    for bkt in range(REL_BUCKETS):
        val = jnp.where(bucket == bkt, rb_ref[bkt, hh], val)
    out_ref[0, 0] = val * LOG2E


def _bias_tiles(rel_bias, tile):
    return pl.pallas_call(
        functools.partial(_bias_kernel, tile=tile),
        grid=(N_A, 3),
        in_specs=[pl.BlockSpec(memory_space=pltpu.SMEM)],
        out_specs=pl.BlockSpec((1, 1, tile, tile), lambda h, d: (h, d, 0, 0)),
        out_shape=jax.ShapeDtypeStruct((N_A, 3, tile, tile), F32),
        name="rel_bias_tiles",
    )(rel_bias)


def _far_bucket(tile):
    assert tile >= REL_MAX_DIST
    return REL_BUCKETS // 2 - 1


def _attn_kernel(rb_ref, li_ref, q_ref, k1_ref, k2_ref, vt_ref, bias_ref, lam_ref, sg_ref,
                 o_ref,
                 qt_s, m_s, acc_s, *, far):
    hh = pl.program_id(1)
    qi = pl.program_id(2)
    tq = q_ref.shape[0]
    tk = vt_ref.shape[2]
    nk = vt_ref.shape[0]
    vd = 2 * HD_A
    LEFT, RIGHT, NEAR = 0, 1, 2

    qt = q_ref[...].astype(F32).T
    rid = lax.broadcasted_iota(jnp.int32, qt.shape, 0)
    bases = (qt, jnp.concatenate([qt[HD_A:], qt[:HD_A]], axis=0))
    consts = (rb_ref[far, hh] * LOG2E, rb_ref[REL_BUCKETS // 2 + far, hh] * LOG2E, 0.0)
    for kind, cst in enumerate(consts):
        cvec = jnp.full(qt.shape, cst, F32)
        c_hi = cvec.astype(BF16).astype(F32)
        extra = jnp.where(rid == HD_A, c_hi, jnp.where(rid == HD_A + 1, cvec - c_hi, 0.0))
        for i, base in enumerate(bases):
            qt_s[2 * kind + i] = jnp.where(rid < HD_A, base, extra).astype(BF16)
    m_s[...] = jnp.full_like(m_s, -jnp.inf)
    acc_s[...] = jnp.zeros_like(acc_s)

    def tile_step(j, kind, bias):
        rows = pl.ds(pl.multiple_of(j * tk, tk), tk)
        vt = vt_ref[j]
        for i, k_ref in enumerate((k1_ref, k2_ref)):
            s = _dot(k_ref[rows, :], qt_s[2 * kind + i])
            if bias is not None:
                s = s + bias
            m_old = m_s[i]
            m_new = jnp.maximum(m_old, jnp.max(s, axis=0, keepdims=True))
            alpha = jnp.exp2(m_old - m_new)
            p = jnp.exp2(s - m_new).astype(BF16)
            acc_s[i] = alpha * acc_s[i] + _dot(vt, p)
            m_s[i] = m_new

    def far_loop(lo, hi, kind):
        def body(j, carry):
            tile_step(j, kind, None)
            return carry
        lax.fori_loop(lo, hi, body, 0)

    far_loop(0, qi - 1, LEFT)
    for dlt in (-1, 0, 1):
        @pl.when(jnp.logical_and(qi + dlt >= 0, qi + dlt < nk))
        def _():
            tile_step(qi + dlt, NEAR, bias_ref[0, dlt + 1])
    far_loop(qi + 2, nk, RIGHT)

    lv = lam_ref[...]
    lam = (jnp.exp(jnp.sum(lv[0:1] * lv[1:2], axis=-1, keepdims=True))
           - jnp.exp(jnp.sum(lv[2:3] * lv[3:4], axis=-1, keepdims=True)) + li_ref[0, 0])
    a1 = acc_s[0]
    a2 = acc_s[1]
    o = a1[:vd] / a1[vd:vd + 1] - lam * (a2[:vd] / a2[vd:vd + 1])
    ms = jnp.mean(o * o, axis=0, keepdims=True)
    o = o * lax.rsqrt(ms + EPS) * sg_ref[0] * li_ref[0, 1]
    o_ref[...] = o.T.astype(o_ref.dtype)


def _attention(qn, k1, k2, vt, bias_t, rel_bias, lam_init, lambdas, sub_g_col, batch, seq, tile):
    nt = seq // tile
    vd = 2 * HD_A
    far = _far_bucket(tile)
    smem = pl.BlockSpec(memory_space=pltpu.SMEM)
    return pl.pallas_call(
        functools.partial(_attn_kernel, far=far),
        grid=(batch, N_A, nt),
        in_specs=[
            smem,
            smem,
            pl.BlockSpec((tile, vd), lambda b, h, i: (b * nt + i, h)),
            pl.BlockSpec((seq, vd), lambda b, h, i: (b, h)),
            pl.BlockSpec((seq, vd), lambda b, h, i: (b, h)),
            pl.BlockSpec((nt, VT_ROWS, tile), lambda b, h, i: (b, h, 0)),
            pl.BlockSpec((1, 3, tile, tile), lambda b, h, i: (h, 0, 0, 0)),
            pl.BlockSpec((4, HD_A), lambda b, h, i: (0, 0)),
            pl.BlockSpec((1, vd, 1), lambda b, h, i: (h, 0, 0)),
        ],
        out_specs=pl.BlockSpec((tile, vd), lambda b, h, i: (b * nt + i, h)),
        out_shape=jax.ShapeDtypeStruct((batch * seq, AW), BF16),
        scratch_shapes=[
            pltpu.VMEM((6, vd, tile), BF16),
            pltpu.VMEM((2, 1, tile), F32),
            pltpu.VMEM((2, VT_ROWS, tile), F32),
        ],
        compiler_params=pltpu.CompilerParams(
            dimension_semantics=("parallel", "parallel", "arbitrary"),
            vmem_limit_bytes=VMEM_LIMIT),
        name="diff_attention",
    )(rel_bias, lam_init, qn, k1, k2, vt, bias_t, lambdas, sub_g_col)


def _outffn_kernel(x_ref, hf_ref, hb_ref, og_ref, mb_ref, ng_ref, wo_ref, g2_ref, w1_ref, w2_ref,
                   out_ref, *, ff_chunk):
    hs = hf_ref[...].astype(F32) + hb_ref[...].astype(F32)
    x1 = x_ref[...] + _dot(mb_ref[...], wo_ref[MW:, :])
    for hh in range(N_M):
        sl = slice(hh * HD_M, (hh + 1) * HD_M)
        t = hs[:, sl]
        ms = jnp.mean(t * t, axis=-1, keepdims=True)
        mix = t * lax.rsqrt(ms + EPS) * ng_ref[:, sl] * _sigmoid(og_ref[:, sl].astype(F32))
        x1 = x1 + _dot(mix.astype(BF16), wo_ref[sl, :])
    ms = jnp.mean(x1 * x1, axis=-1, keepdims=True)
    h2 = (x1 * lax.rsqrt(ms + EPS) * g2_ref[...]).astype(BF16)
    out_ref[...] = x1
    for j in range(w1_ref.shape[1] // ff_chunk):
        sl = slice(j * ff_chunk, (j + 1) * ff_chunk)
        u = jnp.maximum(_dot(h2, w1_ref[:, sl]), 0.0)
        out_ref[...] += _dot((u * u).astype(BF16), w2_ref[sl, :])


def _outffn(x, hf, hb, pm, mixb, ng, wo, g2, w1, w2, tm):
    n, d = x.shape
    dff = w1.shape[1]
    row = lambda i: (i, 0)
    return pl.pallas_call(
        functools.partial(_outffn_kernel, ff_chunk=1024),
        grid=(n // tm,),
        in_specs=[
            pl.BlockSpec((tm, d), row),
            pl.BlockSpec((tm, MW), row),
            pl.BlockSpec((tm, MW), row),
            pl.BlockSpec((tm, MW), lambda i: (i, 3)),
            pl.BlockSpec((tm, AW), row),
            _resident((1, MW)),
            _resident((MW + AW, d)),
            _resident((1, d)),
            _resident((d, dff)),
            _resident((dff, d)),
        ],
        out_specs=pl.BlockSpec((tm, d), row),
        out_shape=jax.ShapeDtypeStruct((n, d), F32),
        compiler_params=pltpu.CompilerParams(
            dimension_semantics=("parallel",), vmem_limit_bytes=VMEM_LIMIT),
        name="outproj_ffn",
    )(x, hf, hb, pm, mixb, ng, wo, g2, w1, w2)


def kernel(x, norm1_g, w_in, conv_w, conv_b, gate_b, mlstm_norm_g, q_norm_g, k_norm_g, lambdas,
           diff_norm_g, rel_bias, w_out, norm2_g, w_ff1, w_ff2):
    batch, seq, d = x.shape
    depth = w_in.shape[0]
    n = batch * seq
    tm = min(512, n)
    tile = min(512, seq)
    assert seq % MLSTM_CHUNK == 0 and seq % tile == 0 and tm == tile

    w_main = jnp.concatenate([w_in[:, :, :C_G], w_in[:, :, C_G + N_GATES:]], axis=-1).astype(BF16)
    w_gate = jnp.pad(w_in[:, :, C_G:C_G + N_GATES], ((0, 0), (0, 0), (0, LANES - N_GATES))).astype(BF16)
    w_out_b = w_out.astype(BF16)
    w_ff1_b = w_ff1.astype(BF16)
    w_ff2_b = w_ff2.astype(BF16)
    lane = jnp.arange(LANES)
    bd = jnp.where((lane[:, None] // HD_A) == (lane[None, :] // HD_A), 1.0 / HD_A, 0.0).astype(BF16)

    bias_t = _bias_tiles(rel_bias.astype(F32), tile)

    xf = x.reshape(n, d)
    for l in range(depth):
        lam_init = 0.8 - 0.6 * math.exp(-0.3 * l)
        pm, gates, qn, k1, k2, vt = _inproj(
            xf, norm1_g[l][None, :], w_main[l], w_gate[l],
            jnp.tile(q_norm_g[l], 2)[None, :], jnp.tile(k_norm_g[l], 2)[None, :], bd, tm)
        gb = gate_b[l].reshape(1, N_GATES).astype(F32)
        hf, hb = _mlstm(pm, gates, gates.T, conv_w[l], conv_b[l][None, :], gb, gb.T, batch, seq)
        mixb = _attention(qn, k1, k2, vt, bias_t, rel_bias.astype(F32),
                          jnp.array([[lam_init, 1.0 - lam_init]], F32), lambdas[l],
                          diff_norm_g[l].reshape(N_A, 2 * HD_A, 1), batch, seq, tile)
        xf = _outffn(xf, hf, hb, pm, mixb, mlstm_norm_g[l][None, :], w_out_b[l],
                     norm2_g[l][None, :], w_ff1_b[l], w_ff2_b[l], tm)
    return xf.reshape(batch, seq, d)
```

```python
import functools
import math

import jax
import jax.numpy as jnp
from jax import lax
from jax.experimental import pallas as pl
from jax.experimental.pallas import tpu as pltpu

F32 = jnp.float32
BF16 = jnp.bfloat16

N_M = 4
HD_M = 128
MW = N_M * HD_M
MLSTM_CHUNK = 128
CONV_W = 5
N_A = 4
HD_A = 64
AW = N_A * 2 * HD_A
REL_BUCKETS = 32
REL_MAX_DIST = 128
EPS = 1e-6
N_GATES = 4 * N_M
C_G = 4 * MW
LANES = 128
HALO = 16
VMEM_LIMIT = 56 * 1024 * 1024
LOG2E = math.log2(math.e)
VT_ROWS = 2 * HD_A + HALO


def _dot(a, b):
    return jnp.dot(a, b, preferred_element_type=F32)


def _dot_nt(a, b):
    return lax.dot_general(a, b, (((1,), (1,)), ((), ())), preferred_element_type=F32)


def _dot_tn(a, b):
    return lax.dot_general(a, b, (((0,), (0,)), ((), ())), preferred_element_type=F32)


def _dot_exact(a, b):
    return jnp.dot(a, b, preferred_element_type=F32, precision=lax.Precision.HIGHEST)


def _sigmoid(x):
    return 1.0 / (1.0 + jnp.exp(-x))


def _log_sigmoid(x):
    return jnp.minimum(x, 0.0) - jnp.log1p(jnp.exp(-jnp.abs(x)))


def _resident(shape):
    nd = len(shape)
    return pl.BlockSpec(shape, lambda *_: (0,) * nd, pipeline_mode=pl.Buffered(1))


def _inproj_kernel(x_ref, g1_ref, w_ref, wg_ref, qg_ref, kg_ref, bd_ref,
                   pm_ref, gates_ref, qn_ref, k1_ref, k2_ref, vt_ref):
    x = x_ref[...]
    ms = jnp.mean(x * x, axis=-1, keepdims=True)
    h = (x * lax.rsqrt(ms + EPS) * g1_ref[...]).astype(BF16)

    def proj(c0, width):
        return _dot(h, w_ref[:, c0:c0 + width])

    for j in range(4 * MW // 512):
        pm_ref[:, j * 512:(j + 1) * 512] = proj(j * 512, 512).astype(BF16)
    gates_ref[...] = _dot(h, wg_ref[...])[:, :N_GATES]

    bd = bd_ref[...]

    def qk_norm(c0, g_ref):
        t = proj(c0, LANES)
        sq = t * t
        hi = sq.astype(BF16)
        lo = (sq - hi.astype(F32)).astype(BF16)
        msq = _dot(hi, bd) + _dot(lo, bd)
        return t * lax.rsqrt(msq + EPS) * g_ref[...]

    lane = lax.broadcasted_iota(jnp.int32, (x.shape[0], LANES), 1)
    ones_cols = jnp.where(lane < HD_A + 2, 1.0, 0.0)
    for hh in range(N_A):
        sl = slice(hh * LANES, (hh + 1) * LANES)
        qn_ref[:, sl] = (qk_norm(4 * MW + hh * LANES, qg_ref) * (HD_A ** -0.5 * LOG2E)).astype(BF16)
        kn = qk_norm(4 * MW + AW + hh * LANES, kg_ref)
        k1_ref[:, sl] = jnp.where(lane < HD_A, kn, ones_cols).astype(BF16)
        k2_ref[:, sl] = jnp.where(lane < HD_A, pltpu.roll(kn, HD_A, 1), ones_cols).astype(BF16)
        vt = proj(4 * MW + 2 * AW + hh * LANES, LANES).T
        vt_ref[0, hh * VT_ROWS:hh * VT_ROWS + 2 * HD_A, :] = vt.astype(BF16)
        vt_ref[0, hh * VT_ROWS + 2 * HD_A:(hh + 1) * VT_ROWS, :] = jnp.ones((HALO, x.shape[0]), BF16)


def _inproj(x, g1, w_main, w_gate, qg, kg, bd, tm):
    n, d = x.shape
    wcols = w_main.shape[1]
    row = lambda i: (i, 0)
    return pl.pallas_call(
        _inproj_kernel,
        grid=(n // tm,),
        in_specs=[
            pl.BlockSpec((tm, d), row),
            _resident((1, d)),
            _resident((d, wcols)),
            _resident((d, LANES)),
            _resident((1, LANES)),
            _resident((1, LANES)),
            _resident((LANES, LANES)),
        ],
        out_specs=[
            pl.BlockSpec((tm, 4 * MW), row),
            pl.BlockSpec((tm, N_GATES), row),
            pl.BlockSpec((tm, AW), row),
            pl.BlockSpec((tm, AW), row),
            pl.BlockSpec((tm, AW), row),
            pl.BlockSpec((1, N_A * VT_ROWS, tm), lambda i: (i, 0, 0)),
        ],
        out_shape=[
            jax.ShapeDtypeStruct((n, 4 * MW), BF16),
            jax.ShapeDtypeStruct((n, N_GATES), F32),
            jax.ShapeDtypeStruct((n, AW), BF16),
            jax.ShapeDtypeStruct((n, AW), BF16),
            jax.ShapeDtypeStruct((n, AW), BF16),
            jax.ShapeDtypeStruct((n // tm, N_A * VT_ROWS, tm), BF16),
        ],
        compiler_params=pltpu.CompilerParams(
            dimension_semantics=("parallel",), vmem_limit_bytes=VMEM_LIMIT),
        name="inproj",
    )(x, g1, w_main, w_gate, qg, kg, bd)


def _mlstm_kernel(qkp_f, qkc_f, qkn_f, v_f, g_f, gt_f,
                  qkp_b, qkc_b, qkn_b, v_b, g_b, gt_b,
                  cw_ref, cb_ref, gbr_ref, gbc_ref,
                  hf_ref, hb_ref,
                  c_s, n_s, m_s, ext_s):
    L = MLSTM_CHUNK
    c = pl.program_id(1)
    last = pl.num_programs(1) - 1

    @pl.when(c == 0)
    def _():
        c_s[...] = jnp.zeros_like(c_s)
        n_s[...] = jnp.zeros_like(n_s)
        m_s[...] = jnp.zeros_like(m_s)

    row = lax.broadcasted_iota(jnp.int32, (L, L), 0)
    col = lax.broadcasted_iota(jnp.int32, (L, L), 1)
    lower = col <= row
    upper = row <= col
    lower_f = lower.astype(F32)
    upper_f = upper.astype(F32)

    dirs = (
        (qkp_f, qkc_f, qkn_f, v_f, g_f, gt_f, hf_ref, c > 0, c < last),
        (qkp_b, qkc_b, qkn_b, v_b, g_b, gt_b, hb_ref, c < last, c > 0),
    )
    for d, (qkp, qkc, qkn, v_ref, g_ref, gt_ref, out_ref, has_prev, has_next) in enumerate(dirs):
        ext = ext_s.at[d]
        ext[0:HALO, :] = jnp.where(has_prev, qkp[...].astype(F32), 0.0)
        ext[HALO:HALO + L, :] = qkc[...].astype(F32)
        ext[HALO + L:2 * HALO + L, :] = jnp.where(has_next, qkn[...].astype(F32), 0.0)
        conv = cb_ref[...]
        for j in range(CONV_W):
            off = HALO - CONV_W // 2 + j
            conv = conv + ext[off:off + L, :] * cw_ref[j:j + 1, :]
        qk = conv * _sigmoid(conv)

        gates = g_ref[...] + gbr_ref[...]
        gates_t = gt_ref[...] + gbc_ref[...]
        logf = _log_sigmoid(gates)
        logf_t = _log_sigmoid(gates_t)
        cum = _dot_exact(lower_f, logf)
        cum_t = _dot_exact(logf_t, upper_f)
        if d == 1:
            cum = cum[L - 1:L, :] - cum + logf
            cum_t = cum_t[:, L - 1:L] - cum_t + logf_t
        i_off = 2 * N_M * d
        f_off = i_off + N_M
        mask = lower if d == 0 else upper

        for hh in range(N_M):
            idx = d * N_M + hh
            b_col = cum[:, f_off + hh:f_off + hh + 1]
            i_col = gates[:, i_off + hh:i_off + hh + 1]
            b_row = cum_t[f_off + hh:f_off + hh + 1, :]
            i_row = gates_t[i_off + hh:i_off + hh + 1, :]
            dmat = jnp.where(mask, b_col - b_row + i_row, -jnp.inf)
            m_prev = m_s[idx][:, :1]
            inter = b_col + m_prev
            m_t = jnp.maximum(inter, jnp.max(dmat, axis=-1, keepdims=True))
            scale = jnp.exp(inter - m_t)
            q = qk[:, hh * HD_M:(hh + 1) * HD_M]
            k = qk[:, MW + hh * HD_M:MW + (hh + 1) * HD_M] * (HD_M ** -0.5)
            qb = q.astype(BF16)
            kb = k.astype(BF16)
            vb = v_ref[:, hh * HD_M:(hh + 1) * HD_M]
            w = _dot_nt(qb, kb) * jnp.exp(dmat - m_t)
            c_prev = c_s[idx]
            n_prev = n_s[idx]
            num = scale * _dot(qb, c_prev.astype(BF16)) + _dot(w.astype(BF16), vb)
            den = (scale * jnp.sum(q * n_prev, axis=-1, keepdims=True)
                   + jnp.sum(w, axis=-1, keepdims=True))
            hval = num / jnp.maximum(jnp.abs(den), jnp.exp(-m_t))
            out_ref[:, hh * HD_M:(hh + 1) * HD_M] = hval.astype(out_ref.dtype)

            b_end = b_col[L - 1:L, :] if d == 0 else b_col[0:1, :]
            g_col = b_end - b_col + i_col
            m_new = jnp.maximum(b_end + m_prev, jnp.max(g_col, axis=0, keepdims=True))
            decay = jnp.exp(b_end + m_prev - m_new)
            kw = k * jnp.exp(g_col - m_new)
            c_s[idx] = decay * c_prev + _dot_tn(kw.astype(BF16), vb)
            n_s[idx] = decay * n_prev + jnp.sum(kw, axis=0, keepdims=True)
            m_s[idx] = jnp.broadcast_to(m_new, (1, LANES))


def _mlstm(pm, gates, gates_t, conv_w, conv_b, gb_row, gb_col, batch, seq):
    L = MLSTM_CHUNK
    nc = seq // L
    hpc = L // HALO
    nhalo = batch * seq // HALO

    def chunk_f(b, c):
        return b * nc + c

    def chunk_b(b, c):
        return b * nc + (nc - 1 - c)

    def specs(chunk):
        prev_i = lambda b, c: (jnp.maximum(chunk(b, c) * hpc - 1, 0), 0)
        next_i = lambda b, c: (jnp.minimum((chunk(b, c) + 1) * hpc, nhalo - 1), 0)
        return [
            pl.BlockSpec((HALO, 2 * MW), prev_i),
            pl.BlockSpec((L, 2 * MW), lambda b, c: (chunk(b, c), 0)),
            pl.BlockSpec((HALO, 2 * MW), next_i),
            pl.BlockSpec((L, MW), lambda b, c: (chunk(b, c), 2)),
            pl.BlockSpec((L, N_GATES), lambda b, c: (chunk(b, c), 0)),
            pl.BlockSpec((N_GATES, L), lambda b, c: (0, chunk(b, c))),
        ]

    n = batch * seq
    return pl.pallas_call(
        _mlstm_kernel,
        grid=(batch, nc),
        in_specs=specs(chunk_f) + specs(chunk_b) + [
            _resident((CONV_W, 2 * MW)),
            _resident((1, 2 * MW)),
            _resident((1, N_GATES)),
            _resident((N_GATES, 1)),
        ],
        out_specs=[
            pl.BlockSpec((L, MW), lambda b, c: (chunk_f(b, c), 0)),
            pl.BlockSpec((L, MW), lambda b, c: (chunk_b(b, c), 0)),
        ],
        out_shape=[jax.ShapeDtypeStruct((n, MW), BF16)] * 2,
        scratch_shapes=[
            pltpu.VMEM((2 * N_M, HD_M, HD_M), F32),
            pltpu.VMEM((2 * N_M, 1, HD_M), F32),
            pltpu.VMEM((2 * N_M, 1, LANES), F32),
            pltpu.VMEM((2, L + 2 * HALO, 2 * MW), F32),
        ],
        compiler_params=pltpu.CompilerParams(
            dimension_semantics=("parallel", "arbitrary"), vmem_limit_bytes=VMEM_LIMIT),
        name="mlstm",
    )(pm, pm, pm, pm, gates, gates_t, pm, pm, pm, pm, gates, gates_t,
      conv_w, conv_b, gb_row, gb_col)


def _bias_kernel(rb_ref, out_ref, *, tile):
    hh = pl.program_id(0)
    dd = pl.program_id(1)
    kk = lax.broadcasted_iota(jnp.int32, (tile, tile), 0)
    qq = lax.broadcasted_iota(jnp.int32, (tile, tile), 1)
    rel = (dd - 1) * tile + kk - qq
    nb = REL_BUCKETS // 2
    max_exact = nb // 2
    n = jnp.abs(rel)
    nf = jnp.maximum(n, 1).astype(F32)
    large = max_exact + (jnp.log(nf / max_exact) / math.log(REL_MAX_DIST / max_exact)
                         * (nb - max_exact)).astype(jnp.int32)
    large = jnp.minimum(large, nb - 1)
    bucket = jnp.where(rel > 0, nb, 0) + jnp.where(n < max_exact, n, large)
    val = jnp.zeros((tile, tile), F32)
    for bkt in range(REL_BUCKETS):
        val = jnp.where(bucket == bkt, rb_ref[bkt, hh], val)
    out_ref[0, 0] = val * LOG2E


def _bias_tiles(rel_bias, tile):
    return pl.pallas_call(
        functools.partial(_bias_kernel, tile=tile),
        grid=(N_A, 3),
        in_specs=[pl.BlockSpec(memory_space=pltpu.SMEM)],
        out_specs=pl.BlockSpec((1, 1, tile, tile), lambda h, d: (h, d, 0, 0)),
        out_shape=jax.ShapeDtypeStruct((N_A, 3, tile, tile), F32),
        name="rel_bias_tiles",
    )(rel_bias)


def _far_bucket(tile):
    assert tile >= REL_MAX_DIST
    return REL_BUCKETS // 2 - 1


def _attn_kernel(rb_ref, li_ref, q_ref, k1_ref, k2_ref, vt_ref, bias_ref, lam_ref, sg_ref,
                 o_ref,
                 qt_s, m_s, acc_s, s0_s, s1_s, p0_s, p1_s, a0_s, a1_s, *, far):
    hh = pl.program_id(1)
    qi = pl.program_id(2)
    tq = q_ref.shape[0]
    tk = vt_ref.shape[2]
    nk = vt_ref.shape[0]
    vd = 2 * HD_A
    LEFT, RIGHT, NEAR = 0, 1, 2

    qt = q_ref[...].astype(F32).T
    rid = lax.broadcasted_iota(jnp.int32, qt.shape, 0)
    bases = (qt, jnp.concatenate([qt[HD_A:], qt[:HD_A]], axis=0))
    consts = (rb_ref[far, hh] * LOG2E, rb_ref[REL_BUCKETS // 2 + far, hh] * LOG2E, 0.0)
    for kind, cst in enumerate(consts):
        cvec = jnp.full(qt.shape, cst, F32)
        c_hi = cvec.astype(BF16).astype(F32)
        extra = jnp.where(rid == HD_A, c_hi, jnp.where(rid == HD_A + 1, cvec - c_hi, 0.0))
        for i, base in enumerate(bases):
            qt_s[2 * kind + i] = jnp.where(rid < HD_A, base, extra).astype(BF16)
    m_s[...] = jnp.full_like(m_s, -jnp.inf)
    acc_s[...] = jnp.zeros_like(acc_s)

    def logits(j, kind, s_buf):
        rows = pl.ds(pl.multiple_of(j * tk, tk), tk)
        for i, k_ref in enumerate((k1_ref, k2_ref)):
            s_buf[i] = _dot(k_ref[rows, :], qt_s[2 * kind + i])

    def softmax(s_buf, p_buf, a_buf, bias=None):
        for i in range(2):
            s = s_buf[i]
            if bias is not None:
                s = s + bias
            m_old = m_s[i]
            m_new = jnp.maximum(m_old, jnp.max(s, axis=0, keepdims=True))
            a_buf[i] = jnp.exp2(m_old - m_new)
            p_buf[i] = jnp.exp2(s - m_new).astype(BF16)
            m_s[i] = m_new

    def values(j, p_buf, a_buf):
        vt = vt_ref[j]
        for i in range(2):
            acc_s[i] = a_buf[i] * acc_s[i] + _dot(vt, p_buf[i])

    def whole_tile(j, kind, bias=None):
        logits(j, kind, s0_s)
        softmax(s0_s, p0_s, a0_s, bias)
        values(j, p0_s, a0_s)

    lo = jnp.maximum(qi - 1, 0)
    n_near = jnp.minimum(qi + 1, nk - 1) - lo + 1
    for dlt in (-1, 0, 1):
        @pl.when(jnp.logical_and(qi + dlt >= 0, qi + dlt < nk))
        def _():
            whole_tile(qi + dlt, NEAR, bias_ref[0, dlt + 1])

    n_far = nk - n_near
    tile_of = lambda t: jnp.minimum(jnp.where(t < lo, t, t + n_near), nk - 1)
    kind_of = lambda t: jnp.where(t < lo, LEFT, RIGHT)

    @pl.when(n_far % 2 == 1)
    def _():
        whole_tile(tile_of(n_far - 1), kind_of(n_far - 1))

    n_pipe = n_far - n_far % 2
    clamp = lambda t: jnp.clip(t, 0, jnp.maximum(n_pipe - 1, 0))
    p1_s[...] = jnp.zeros_like(p1_s)
    a1_s[...] = jnp.ones_like(a1_s)
    logits(tile_of(clamp(0)), kind_of(clamp(0)), s0_s)

    def pair(u, carry):
        t0 = 2 * u
        values(tile_of(clamp(t0 - 1)), p1_s, a1_s)
        logits(tile_of(t0 + 1), kind_of(t0 + 1), s1_s)
        softmax(s0_s, p0_s, a0_s)
        values(tile_of(t0), p0_s, a0_s)
        logits(tile_of(clamp(t0 + 2)), kind_of(clamp(t0 + 2)), s0_s)
        softmax(s1_s, p1_s, a1_s)
        return carry

    lax.fori_loop(0, n_pipe // 2, pair, 0)
    values(tile_of(clamp(n_pipe - 1)), p1_s, a1_s)

    lv = lam_ref[...]
    lam = (jnp.exp(jnp.sum(lv[0:1] * lv[1:2], axis=-1, keepdims=True))
           - jnp.exp(jnp.sum(lv[2:3] * lv[3:4], axis=-1, keepdims=True)) + li_ref[0, 0])
    a1 = acc_s[0]
    a2 = acc_s[1]
    o = a1[:vd] / a1[vd:vd + 1] - lam * (a2[:vd] / a2[vd:vd + 1])
    ms = jnp.mean(o * o, axis=0, keepdims=True)
    o = o * lax.rsqrt(ms + EPS) * sg_ref[0] * li_ref[0, 1]
    o_ref[...] = o.T.astype(o_ref.dtype)


def _attention(qn, k1, k2, vt, bias_t, rel_bias, lam_init, lambdas, sub_g_col, batch, seq, tile):
    nt = seq // tile
    vd = 2 * HD_A
    far = _far_bucket(tile)
    smem = pl.BlockSpec(memory_space=pltpu.SMEM)
    return pl.pallas_call(
        functools.partial(_attn_kernel, far=far),
        grid=(batch, N_A, nt),
        in_specs=[
            smem,
            smem,
            pl.BlockSpec((tile, vd), lambda b, h, i: (b * nt + i, h)),
            pl.BlockSpec((seq, vd), lambda b, h, i: (b, h)),
            pl.BlockSpec((seq, vd), lambda b, h, i: (b, h)),
            pl.BlockSpec((nt, VT_ROWS, tile), lambda b, h, i: (b, h, 0)),
            pl.BlockSpec((1, 3, tile, tile), lambda b, h, i: (h, 0, 0, 0)),
            pl.BlockSpec((4, HD_A), lambda b, h, i: (0, 0)),
            pl.BlockSpec((1, vd, 1), lambda b, h, i: (h, 0, 0)),
        ],
        out_specs=pl.BlockSpec((tile, vd), lambda b, h, i: (b * nt + i, h)),
        out_shape=jax.ShapeDtypeStruct((batch * seq, AW), BF16),
        scratch_shapes=[
            pltpu.VMEM((6, vd, tile), BF16),
            pltpu.VMEM((2, 1, tile), F32),
            pltpu.VMEM((2, VT_ROWS, tile), F32),
            pltpu.VMEM((2, tile, tile), F32),
            pltpu.VMEM((2, tile, tile), F32),
            pltpu.VMEM((2, tile, tile), BF16),
            pltpu.VMEM((2, tile, tile), BF16),
            pltpu.VMEM((2, 1, tile), F32),
            pltpu.VMEM((2, 1, tile), F32),
        ],
        compiler_params=pltpu.CompilerParams(
            dimension_semantics=("parallel", "parallel", "arbitrary"),
            vmem_limit_bytes=VMEM_LIMIT),
        name="diff_attention",
    )(rel_bias, lam_init, qn, k1, k2, vt, bias_t, lambdas, sub_g_col)


def _outffn_kernel(x_ref, hf_ref, hb_ref, og_ref, mb_ref, ng_ref, wo_ref, g2_ref, w1_ref, w2_ref,
                   out_ref, *, ff_chunk):
    hs = hf_ref[...].astype(F32) + hb_ref[...].astype(F32)
    x1 = x_ref[...] + _dot(mb_ref[...], wo_ref[MW:, :])
    for hh in range(N_M):
        sl = slice(hh * HD_M, (hh + 1) * HD_M)
        t = hs[:, sl]
        ms = jnp.mean(t * t, axis=-1, keepdims=True)
        mix = t * lax.rsqrt(ms + EPS) * ng_ref[:, sl] * _sigmoid(og_ref[:, sl].astype(F32))
        x1 = x1 + _dot(mix.astype(BF16), wo_ref[sl, :])
    ms = jnp.mean(x1 * x1, axis=-1, keepdims=True)
    h2 = (x1 * lax.rsqrt(ms + EPS) * g2_ref[...]).astype(BF16)
    out_ref[...] = x1
    for j in range(w1_ref.shape[1] // ff_chunk):
        sl = slice(j * ff_chunk, (j + 1) * ff_chunk)
        u = jnp.maximum(_dot(h2, w1_ref[:, sl]), 0.0)
        out_ref[...] += _dot((u * u).astype(BF16), w2_ref[sl, :])


def _outffn(x, hf, hb, pm, mixb, ng, wo, g2, w1, w2, tm):
    n, d = x.shape
    dff = w1.shape[1]
    row = lambda i: (i, 0)
    return pl.pallas_call(
        functools.partial(_outffn_kernel, ff_chunk=1024),
        grid=(n // tm,),
        in_specs=[
            pl.BlockSpec((tm, d), row),
            pl.BlockSpec((tm, MW), row),
            pl.BlockSpec((tm, MW), row),
            pl.BlockSpec((tm, MW), lambda i: (i, 3)),
            pl.BlockSpec((tm, AW), row),
            _resident((1, MW)),
            _resident((MW + AW, d)),
            _resident((1, d)),
            _resident((d, dff)),
            _resident((dff, d)),
        ],
        out_specs=pl.BlockSpec((tm, d), row),
        out_shape=jax.ShapeDtypeStruct((n, d), F32),
        compiler_params=pltpu.CompilerParams(
            dimension_semantics=("parallel",), vmem_limit_bytes=VMEM_LIMIT),
        name="outproj_ffn",
    )(x, hf, hb, pm, mixb, ng, wo, g2, w1, w2)


def kernel(x, norm1_g, w_in, conv_w, conv_b, gate_b, mlstm_norm_g, q_norm_g, k_norm_g, lambdas,
           diff_norm_g, rel_bias, w_out, norm2_g, w_ff1, w_ff2):
    batch, seq, d = x.shape
    depth = w_in.shape[0]
    n = batch * seq
    tm = min(512, n)
    tile = min(512, seq)
    assert seq % MLSTM_CHUNK == 0 and seq % tile == 0 and tm == tile

    w_main = jnp.concatenate([w_in[:, :, :C_G], w_in[:, :, C_G + N_GATES:]], axis=-1).astype(BF16)
    w_gate = jnp.pad(w_in[:, :, C_G:C_G + N_GATES], ((0, 0), (0, 0), (0, LANES - N_GATES))).astype(BF16)
    w_out_b = w_out.astype(BF16)
    w_ff1_b = w_ff1.astype(BF16)
    w_ff2_b = w_ff2.astype(BF16)
    lane = jnp.arange(LANES)
    bd = jnp.where((lane[:, None] // HD_A) == (lane[None, :] // HD_A), 1.0 / HD_A, 0.0).astype(BF16)

    bias_t = _bias_tiles(rel_bias.astype(F32), tile)

    xf = x.reshape(n, d)
    for l in range(depth):
        lam_init = 0.8 - 0.6 * math.exp(-0.3 * l)
        pm, gates, qn, k1, k2, vt = _inproj(
            xf, norm1_g[l][None, :], w_main[l], w_gate[l],
            jnp.tile(q_norm_g[l], 2)[None, :], jnp.tile(k_norm_g[l], 2)[None, :], bd, tm)
        gb = gate_b[l].reshape(1, N_GATES).astype(F32)
        hf, hb = _mlstm(pm, gates, gates.T, conv_w[l], conv_b[l][None, :], gb, gb.T, batch, seq)
        mixb = _attention(qn, k1, k2, vt, bias_t, rel_bias.astype(F32),
                          jnp.array([[lam_init, 1.0 - lam_init]], F32), lambdas[l],
                          diff_norm_g[l].reshape(N_A, 2 * HD_A, 1), batch, seq, tile)
        xf = _outffn(xf, hf, hb, pm, mixb, mlstm_norm_g[l][None, :], w_out_b[l],
                     norm2_g[l][None, :], w_ff1_b[l], w_ff2_b[l], tm)
    return xf.reshape(batch, seq, d)
```

```python
import functools
import math

import jax
import jax.numpy as jnp
from jax import lax
from jax.experimental import pallas as pl
from jax.experimental.pallas import tpu as pltpu

F32 = jnp.float32
BF16 = jnp.bfloat16

N_M = 4
HD_M = 128
MW = N_M * HD_M
MLSTM_CHUNK = 128
CONV_W = 5
N_A = 4
HD_A = 64
AW = N_A * 2 * HD_A
REL_BUCKETS = 32
REL_MAX_DIST = 128
EPS = 1e-6
N_GATES = 4 * N_M
C_G = 4 * MW
LANES = 128
HALO = 16
VMEM_LIMIT = 56 * 1024 * 1024
LOG2E = math.log2(math.e)
VT_ROWS = 2 * HD_A + HALO
BIAS_ROWS = 3
LOGIT_SPAN_LIMIT = 96.0


def _dot(a, b):
    return jnp.dot(a, b, preferred_element_type=F32)


def _dot_nt(a, b):
    return lax.dot_general(a, b, (((1,), (1,)), ((), ())), preferred_element_type=F32)


def _dot_tn(a, b):
    return lax.dot_general(a, b, (((0,), (0,)), ((), ())), preferred_element_type=F32)


def _dot_exact(a, b):
    return jnp.dot(a, b, preferred_element_type=F32, precision=lax.Precision.HIGHEST)


def _sigmoid(x):
    return 1.0 / (1.0 + jnp.exp(-x))


def _log_sigmoid(x):
    return jnp.minimum(x, 0.0) - jnp.log1p(jnp.exp(-jnp.abs(x)))


def _resident(shape):
    nd = len(shape)
    return pl.BlockSpec(shape, lambda *_: (0,) * nd, pipeline_mode=pl.Buffered(1))


def _inproj_kernel(x_ref, g1_ref, w_ref, wg_ref, qg_ref, kg_ref, bd_ref,
                   pm_ref, gates_ref, qn_ref, k1_ref, k2_ref, vt_ref, kn2_ref):
    x = x_ref[...]
    ms = jnp.mean(x * x, axis=-1, keepdims=True)
    h = (x * lax.rsqrt(ms + EPS) * g1_ref[...]).astype(BF16)

    def proj(c0, width):
        return _dot(h, w_ref[:, c0:c0 + width])

    for j in range(4 * MW // 512):
        pm_ref[:, j * 512:(j + 1) * 512] = proj(j * 512, 512).astype(BF16)
    gates_ref[...] = _dot(h, wg_ref[...])[:, :N_GATES]

    bd = bd_ref[...]

    def qk_norm(c0, g_ref):
        t = proj(c0, LANES)
        sq = t * t
        hi = sq.astype(BF16)
        lo = (sq - hi.astype(F32)).astype(BF16)
        msq = _dot(hi, bd) + _dot(lo, bd)
        return t * lax.rsqrt(msq + EPS) * g_ref[...]

    lane = lax.broadcasted_iota(jnp.int32, (x.shape[0], LANES), 1)
    ones_cols = jnp.where(lane < HD_A + BIAS_ROWS, 1.0, 0.0)
    for hh in range(N_A):
        sl = slice(hh * LANES, (hh + 1) * LANES)
        qn_ref[:, sl] = (qk_norm(4 * MW + hh * LANES, qg_ref) * (HD_A ** -0.5 * LOG2E)).astype(BF16)
        kn = qk_norm(4 * MW + AW + hh * LANES, kg_ref)
        k1_ref[:, sl] = jnp.where(lane < HD_A, kn, ones_cols).astype(BF16)
        k2_ref[:, sl] = jnp.where(lane < HD_A, pltpu.roll(kn, HD_A, 1), ones_cols).astype(BF16)
        sq = kn * kn
        hi = sq.astype(BF16)
        lo = (sq - hi.astype(F32)).astype(BF16)
        kn2_ref[0, :, sl] = jnp.max((_dot(hi, bd) + _dot(lo, bd)) * HD_A, axis=0, keepdims=True)
        vt = proj(4 * MW + 2 * AW + hh * LANES, LANES).T
        vt_ref[0, hh * VT_ROWS:hh * VT_ROWS + 2 * HD_A, :] = vt.astype(BF16)
        vt_ref[0, hh * VT_ROWS + 2 * HD_A:(hh + 1) * VT_ROWS, :] = jnp.ones((HALO, x.shape[0]), BF16)


def _inproj(x, g1, w_main, w_gate, qg, kg, bd, tm):
    n, d = x.shape
    wcols = w_main.shape[1]
    row = lambda i: (i, 0)
    return pl.pallas_call(
        _inproj_kernel,
        grid=(n // tm,),
        in_specs=[
            pl.BlockSpec((tm, d), row),
            _resident((1, d)),
            _resident((d, wcols)),
            _resident((d, LANES)),
            _resident((1, LANES)),
            _resident((1, LANES)),
            _resident((LANES, LANES)),
        ],
        out_specs=[
            pl.BlockSpec((tm, 4 * MW), row),
            pl.BlockSpec((tm, N_GATES), row),
            pl.BlockSpec((tm, AW), row),
            pl.BlockSpec((tm, AW), row),
            pl.BlockSpec((tm, AW), row),
            pl.BlockSpec((1, N_A * VT_ROWS, tm), lambda i: (i, 0, 0)),
            pl.BlockSpec((1, 1, AW), lambda i: (i, 0, 0)),
        ],
        out_shape=[
            jax.ShapeDtypeStruct((n, 4 * MW), BF16),
            jax.ShapeDtypeStruct((n, N_GATES), F32),
            jax.ShapeDtypeStruct((n, AW), BF16),
            jax.ShapeDtypeStruct((n, AW), BF16),
            jax.ShapeDtypeStruct((n, AW), BF16),
            jax.ShapeDtypeStruct((n // tm, N_A * VT_ROWS, tm), BF16),
            jax.ShapeDtypeStruct((n // tm, 1, AW), F32),
        ],
        compiler_params=pltpu.CompilerParams(
            dimension_semantics=("parallel",), vmem_limit_bytes=VMEM_LIMIT),
        name="inproj",
    )(x, g1, w_main, w_gate, qg, kg, bd)


def _mlstm_kernel(qkp_f, qkc_f, qkn_f, v_f, g_f, gt_f,
                  qkp_b, qkc_b, qkn_b, v_b, g_b, gt_b,
                  cw_ref, cb_ref, gbr_ref, gbc_ref,
                  hf_ref, hb_ref,
                  c_s, n_s, m_s, ext_s):
    L = MLSTM_CHUNK
    c = pl.program_id(1)
    last = pl.num_programs(1) - 1

    @pl.when(c == 0)
    def _():
        c_s[...] = jnp.zeros_like(c_s)
        n_s[...] = jnp.zeros_like(n_s)
        m_s[...] = jnp.zeros_like(m_s)

    row = lax.broadcasted_iota(jnp.int32, (L, L), 0)
    col = lax.broadcasted_iota(jnp.int32, (L, L), 1)
    lower = col <= row
    upper = row <= col
    lower_f = lower.astype(F32)
    upper_f = upper.astype(F32)

    dirs = (
        (qkp_f, qkc_f, qkn_f, v_f, g_f, gt_f, hf_ref, c > 0, c < last),
        (qkp_b, qkc_b, qkn_b, v_b, g_b, gt_b, hb_ref, c < last, c > 0),
    )
    for d, (qkp, qkc, qkn, v_ref, g_ref, gt_ref, out_ref, has_prev, has_next) in enumerate(dirs):
        ext = ext_s.at[d]
        ext[0:HALO, :] = jnp.where(has_prev, qkp[...].astype(F32), 0.0)
        ext[HALO:HALO + L, :] = qkc[...].astype(F32)
        ext[HALO + L:2 * HALO + L, :] = jnp.where(has_next, qkn[...].astype(F32), 0.0)
        conv = cb_ref[...]
        for j in range(CONV_W):
            off = HALO - CONV_W // 2 + j
            conv = conv + ext[off:off + L, :] * cw_ref[j:j + 1, :]
        qk = conv * _sigmoid(conv)

        gates = g_ref[...] + gbr_ref[...]
        gates_t = gt_ref[...] + gbc_ref[...]
        logf = _log_sigmoid(gates)
        logf_t = _log_sigmoid(gates_t)
        cum = _dot_exact(lower_f, logf)
        cum_t = _dot_exact(logf_t, upper_f)
        if d == 1:
            cum = cum[L - 1:L, :] - cum + logf
            cum_t = cum_t[:, L - 1:L] - cum_t + logf_t
        i_off = 2 * N_M * d
        f_off = i_off + N_M
        mask = lower if d == 0 else upper

        for hh in range(N_M):
            idx = d * N_M + hh
            b_col = cum[:, f_off + hh:f_off + hh + 1]
            i_col = gates[:, i_off + hh:i_off + hh + 1]
            b_row = cum_t[f_off + hh:f_off + hh + 1, :]
            i_row = gates_t[i_off + hh:i_off + hh + 1, :]
            dmat = jnp.where(mask, b_col - b_row + i_row, -jnp.inf)
            m_prev = m_s[idx][:, :1]
            inter = b_col + m_prev
            m_t = jnp.maximum(inter, jnp.max(dmat, axis=-1, keepdims=True))
            scale = jnp.exp(inter - m_t)
            q = qk[:, hh * HD_M:(hh + 1) * HD_M]
            k = qk[:, MW + hh * HD_M:MW + (hh + 1) * HD_M] * (HD_M ** -0.5)
            qb = q.astype(BF16)
            kb = k.astype(BF16)
            vb = v_ref[:, hh * HD_M:(hh + 1) * HD_M]
            w = _dot_nt(qb, kb) * jnp.exp(dmat - m_t)
            c_prev = c_s[idx]
            n_prev = n_s[idx]
            num = scale * _dot(qb, c_prev.astype(BF16)) + _dot(w.astype(BF16), vb)
            den = (scale * jnp.sum(q * n_prev, axis=-1, keepdims=True)
                   + jnp.sum(w, axis=-1, keepdims=True))
            hval = num / jnp.maximum(jnp.abs(den), jnp.exp(-m_t))
            out_ref[:, hh * HD_M:(hh + 1) * HD_M] = hval.astype(out_ref.dtype)

            b_end = b_col[L - 1:L, :] if d == 0 else b_col[0:1, :]
            g_col = b_end - b_col + i_col
            m_new = jnp.maximum(b_end + m_prev, jnp.max(g_col, axis=0, keepdims=True))
            decay = jnp.exp(b_end + m_prev - m_new)
            kw = k * jnp.exp(g_col - m_new)
            c_s[idx] = decay * c_prev + _dot_tn(kw.astype(BF16), vb)
            n_s[idx] = decay * n_prev + jnp.sum(kw, axis=0, keepdims=True)
            m_s[idx] = jnp.broadcast_to(m_new, (1, LANES))


def _mlstm(pm, gates, gates_t, conv_w, conv_b, gb_row, gb_col, batch, seq):
    L = MLSTM_CHUNK
    nc = seq // L
    hpc = L // HALO
    nhalo = batch * seq // HALO

    def chunk_f(b, c):
        return b * nc + c

    def chunk_b(b, c):
        return b * nc + (nc - 1 - c)

    def specs(chunk):
        prev_i = lambda b, c: (jnp.maximum(chunk(b, c) * hpc - 1, 0), 0)
        next_i = lambda b, c: (jnp.minimum((chunk(b, c) + 1) * hpc, nhalo - 1), 0)
        return [
            pl.BlockSpec((HALO, 2 * MW), prev_i),
            pl.BlockSpec((L, 2 * MW), lambda b, c: (chunk(b, c), 0)),
            pl.BlockSpec((HALO, 2 * MW), next_i),
            pl.BlockSpec((L, MW), lambda b, c: (chunk(b, c), 2)),
            pl.BlockSpec((L, N_GATES), lambda b, c: (chunk(b, c), 0)),
            pl.BlockSpec((N_GATES, L), lambda b, c: (0, chunk(b, c))),
        ]

    n = batch * seq
    return pl.pallas_call(
        _mlstm_kernel,
        grid=(batch, nc),
        in_specs=specs(chunk_f) + specs(chunk_b) + [
            _resident((CONV_W, 2 * MW)),
            _resident((1, 2 * MW)),
            _resident((1, N_GATES)),
            _resident((N_GATES, 1)),
        ],
        out_specs=[
            pl.BlockSpec((L, MW), lambda b, c: (chunk_f(b, c), 0)),
            pl.BlockSpec((L, MW), lambda b, c: (chunk_b(b, c), 0)),
        ],
        out_shape=[jax.ShapeDtypeStruct((n, MW), BF16)] * 2,
        scratch_shapes=[
            pltpu.VMEM((2 * N_M, HD_M, HD_M), F32),
            pltpu.VMEM((2 * N_M, 1, HD_M), F32),
            pltpu.VMEM((2 * N_M, 1, LANES), F32),
            pltpu.VMEM((2, L + 2 * HALO, 2 * MW), F32),
        ],
        compiler_params=pltpu.CompilerParams(
            dimension_semantics=("parallel", "arbitrary"), vmem_limit_bytes=VMEM_LIMIT),
        name="mlstm",
    )(pm, pm, pm, pm, gates, gates_t, pm, pm, pm, pm, gates, gates_t,
      conv_w, conv_b, gb_row, gb_col)


def _bias_kernel(rb_ref, out_ref, *, tile):
    hh = pl.program_id(0)
    dd = pl.program_id(1)
    kk = lax.broadcasted_iota(jnp.int32, (tile, tile), 0)
    qq = lax.broadcasted_iota(jnp.int32, (tile, tile), 1)
    rel = (dd - 1) * tile + kk - qq
    nb = REL_BUCKETS // 2
    max_exact = nb // 2
    n = jnp.abs(rel)
    nf = jnp.maximum(n, 1).astype(F32)
    large = max_exact + (jnp.log(nf / max_exact) / math.log(REL_MAX_DIST / max_exact)
                         * (nb - max_exact)).astype(jnp.int32)
    large = jnp.minimum(large, nb - 1)
    bucket = jnp.where(rel > 0, nb, 0) + jnp.where(n < max_exact, n, large)
    val = jnp.zeros((tile, tile), F32)
    for bkt in range(REL_BUCKETS):
        val = jnp.where(bucket == bkt, rb_ref[bkt, hh], val)
    out_ref[0, 0] = val * LOG2E


def _bias_tiles(rel_bias, tile):
    return pl.pallas_call(
        functools.partial(_bias_kernel, tile=tile),
        grid=(N_A, 3),
        in_specs=[pl.BlockSpec(memory_space=pltpu.SMEM)],
        out_specs=pl.BlockSpec((1, 1, tile, tile), lambda h, d: (h, d, 0, 0)),
        out_shape=jax.ShapeDtypeStruct((N_A, 3, tile, tile), F32),
        name="rel_bias_tiles",
    )(rel_bias)


def _far_bucket(tile):
    assert tile >= REL_MAX_DIST
    return REL_BUCKETS // 2 - 1


def _attn_kernel(rb_ref, li_ref, q_ref, k1_ref, k2_ref, vt_ref, kn2_ref, bias_ref, lam_ref, sg_ref,
                 o_ref,
                 qt_s, m_s, acc_s, s0_s, s1_s, p0_s, p1_s, a0_s, a1_s, *, far):
    hh = pl.program_id(1)
    qi = pl.program_id(2)
    tq = q_ref.shape[0]
    tk = vt_ref.shape[2]
    nk = vt_ref.shape[0]
    vd = 2 * HD_A
    LEFT, RIGHT, NEAR = 0, 1, 2

    qt = q_ref[...].astype(F32).T
    rid = lax.broadcasted_iota(jnp.int32, qt.shape, 0)
    bases = (qt, jnp.concatenate([qt[HD_A:], qt[:HD_A]], axis=0))
    consts = (rb_ref[far, hh] * LOG2E, rb_ref[REL_BUCKETS // 2 + far, hh] * LOG2E, 0.0)

    def build_queries(shifts):
        for kind, cst in enumerate(consts):
            for i, base in enumerate(bases):
                off = cst if shifts is None else cst - shifts[i]
                rest = jnp.broadcast_to(off, qt.shape).astype(F32)
                extra = jnp.zeros(qt.shape, F32)
                for r in range(BIAS_ROWS):
                    part = rest.astype(BF16).astype(F32)
                    extra = jnp.where(rid == HD_A + r, part, extra)
                    rest = rest - part
                qt_s[2 * kind + i] = jnp.where(rid < HD_A, base, extra).astype(BF16)

    acc_s[...] = jnp.zeros_like(acc_s)

    lo = jnp.maximum(qi - 1, 0)
    n_near = jnp.minimum(qi + 1, nk - 1) - lo + 1
    n_far = nk - n_near
    tile_of = lambda t: jnp.minimum(jnp.where(t < lo, t, t + n_near), nk - 1)
    kind_of = lambda t: jnp.where(t < lo, LEFT, RIGHT)
    n_pipe = n_far - n_far % 2
    clamp = lambda t: jnp.clip(t, 0, jnp.maximum(n_pipe - 1, 0))

    def key_rows(j):
        return pl.ds(pl.multiple_of(j * tk, tk), tk)

    bucket_vals = [rb_ref[bkt, hh] for bkt in range(REL_BUCKETS)]
    bias_max = functools.reduce(jnp.maximum, bucket_vals) * LOG2E
    bias_min = functools.reduce(jnp.minimum, bucket_vals) * LOG2E
    kmax2 = jnp.max(kn2_ref[...], axis=0)
    shifts = []
    reach_max = 0.0
    for i, base in enumerate(bases):
        q2 = jnp.sum(base[:HD_A] * base[:HD_A], axis=0, keepdims=True)
        reach = jnp.sqrt(q2 * kmax2[:, i * HD_A:i * HD_A + 1])
        shifts.append(reach + bias_max)
        reach_max = jnp.maximum(reach_max, jnp.max(reach))
    span = 2.0 * reach_max + (bias_max - bias_min)
    bounded = span <= LOGIT_SPAN_LIMIT

    @pl.when(bounded)
    def _():
        build_queries(shifts)

        def numerators(j, kind, p_buf, bias=None):
            for i, k_ref in enumerate((k1_ref, k2_ref)):
                s = _dot(k_ref[key_rows(j), :], qt_s[2 * kind + i])
                if bias is not None:
                    s = s + bias
                p_buf[i] = jnp.exp2(s).astype(BF16)

        def accumulate(j, p_buf):
            vt = vt_ref[j]
            for i in range(2):
                acc_s[i] += _dot(vt, p_buf[i])

        for dlt in (-1, 0, 1):
            @pl.when(jnp.logical_and(qi + dlt >= 0, qi + dlt < nk))
            def _():
                numerators(qi + dlt, NEAR, p0_s, bias_ref[0, dlt + 1])
                accumulate(qi + dlt, p0_s)

        @pl.when(n_far % 2 == 1)
        def _():
            numerators(tile_of(n_far - 1), kind_of(n_far - 1), p0_s)
            accumulate(tile_of(n_far - 1), p0_s)

        numerators(tile_of(clamp(0)), kind_of(clamp(0)), p0_s)

        def pair(u, carry):
            t0 = 2 * u
            numerators(tile_of(t0 + 1), kind_of(t0 + 1), p1_s)
            accumulate(tile_of(t0), p0_s)
            numerators(tile_of(clamp(t0 + 2)), kind_of(clamp(t0 + 2)), p0_s)
            accumulate(tile_of(t0 + 1), p1_s)
            return carry

        lax.fori_loop(0, n_pipe // 2, pair, 0)

    @pl.when(jnp.logical_not(bounded))
    def _():
        _online_softmax_tiles(qi, nk, build_queries, key_rows, tile_of, kind_of, clamp, n_far, n_pipe,
                              k1_ref, k2_ref, vt_ref, bias_ref,
                              qt_s, m_s, acc_s, s0_s, s1_s, p0_s, p1_s, a0_s, a1_s)

    lv = lam_ref[...]
    lam = (jnp.exp(jnp.sum(lv[0:1] * lv[1:2], axis=-1, keepdims=True))
           - jnp.exp(jnp.sum(lv[2:3] * lv[3:4], axis=-1, keepdims=True)) + li_ref[0, 0])
    a1 = acc_s[0]
    a2 = acc_s[1]
    o = a1[:vd] / a1[vd:vd + 1] - lam * (a2[:vd] / a2[vd:vd + 1])
    ms = jnp.mean(o * o, axis=0, keepdims=True)
    o = o * lax.rsqrt(ms + EPS) * sg_ref[0] * li_ref[0, 1]
    o_ref[...] = o.T.astype(o_ref.dtype)


def _online_softmax_tiles(qi, nk, build_queries, key_rows, tile_of, kind_of, clamp, n_far, n_pipe,
                          k1_ref, k2_ref, vt_ref, bias_ref,
                          qt_s, m_s, acc_s, s0_s, s1_s, p0_s, p1_s, a0_s, a1_s):
    LEFT, RIGHT, NEAR = 0, 1, 2
    build_queries(None)
    m_s[...] = jnp.full_like(m_s, -jnp.inf)

    def logits(j, kind, s_buf):
        for i, k_ref in enumerate((k1_ref, k2_ref)):
            s_buf[i] = _dot(k_ref[key_rows(j), :], qt_s[2 * kind + i])

    def softmax(s_buf, p_buf, a_buf, bias=None):
        for i in range(2):
            s = s_buf[i]
            if bias is not None:
                s = s + bias
            m_old = m_s[i]
            m_new = jnp.maximum(m_old, jnp.max(s, axis=0, keepdims=True))
            a_buf[i] = jnp.exp2(m_old - m_new)
            p_buf[i] = jnp.exp2(s - m_new).astype(BF16)
            m_s[i] = m_new

    def values(j, p_buf, a_buf):
        vt = vt_ref[j]
        for i in range(2):
            acc_s[i] = a_buf[i] * acc_s[i] + _dot(vt, p_buf[i])

    def whole_tile(j, kind, bias=None):
        logits(j, kind, s0_s)
        softmax(s0_s, p0_s, a0_s, bias)
        values(j, p0_s, a0_s)

    for dlt in (-1, 0, 1):
        @pl.when(jnp.logical_and(qi + dlt >= 0, qi + dlt < nk))
        def _():
            whole_tile(qi + dlt, NEAR, bias_ref[0, dlt + 1])

    @pl.when(n_far % 2 == 1)
    def _():
        whole_tile(tile_of(n_far - 1), kind_of(n_far - 1))

    p1_s[...] = jnp.zeros_like(p1_s)
    a1_s[...] = jnp.ones_like(a1_s)
    logits(tile_of(clamp(0)), kind_of(clamp(0)), s0_s)

    def pair(u, carry):
        t0 = 2 * u
        values(tile_of(clamp(t0 - 1)), p1_s, a1_s)
        logits(tile_of(t0 + 1), kind_of(t0 + 1), s1_s)
        softmax(s0_s, p0_s, a0_s)
        values(tile_of(t0), p0_s, a0_s)
        logits(tile_of(clamp(t0 + 2)), kind_of(clamp(t0 + 2)), s0_s)
        softmax(s1_s, p1_s, a1_s)
        return carry

    lax.fori_loop(0, n_pipe // 2, pair, 0)
    values(tile_of(clamp(n_pipe - 1)), p1_s, a1_s)


def _attention(qn, k1, k2, vt, kn2, bias_t, rel_bias, lam_init, lambdas, sub_g_col, batch, seq, tile):
    nt = seq // tile
    vd = 2 * HD_A
    far = _far_bucket(tile)
    smem = pl.BlockSpec(memory_space=pltpu.SMEM)
    return pl.pallas_call(
        functools.partial(_attn_kernel, far=far),
        grid=(batch, N_A, nt),
        in_specs=[
            smem,
            smem,
            pl.BlockSpec((tile, vd), lambda b, h, i: (b * nt + i, h)),
            pl.BlockSpec((seq, vd), lambda b, h, i: (b, h)),
            pl.BlockSpec((seq, vd), lambda b, h, i: (b, h)),
            pl.BlockSpec((nt, VT_ROWS, tile), lambda b, h, i: (b, h, 0)),
            pl.BlockSpec((nt, 1, vd), lambda b, h, i: (b, 0, h)),
            pl.BlockSpec((1, 3, tile, tile), lambda b, h, i: (h, 0, 0, 0)),
            pl.BlockSpec((4, HD_A), lambda b, h, i: (0, 0)),
            pl.BlockSpec((1, vd, 1), lambda b, h, i: (h, 0, 0)),
        ],
        out_specs=pl.BlockSpec((tile, vd), lambda b, h, i: (b * nt + i, h)),
        out_shape=jax.ShapeDtypeStruct((batch * seq, AW), BF16),
        scratch_shapes=[
            pltpu.VMEM((6, vd, tile), BF16),
            pltpu.VMEM((2, 1, tile), F32),
            pltpu.VMEM((2, VT_ROWS, tile), F32),
            pltpu.VMEM((2, tile, tile), F32),
            pltpu.VMEM((2, tile, tile), F32),
            pltpu.VMEM((2, tile, tile), BF16),
            pltpu.VMEM((2, tile, tile), BF16),
            pltpu.VMEM((2, 1, tile), F32),
            pltpu.VMEM((2, 1, tile), F32),
        ],
        compiler_params=pltpu.CompilerParams(
            dimension_semantics=("parallel", "parallel", "arbitrary"),
            vmem_limit_bytes=VMEM_LIMIT),
        name="diff_attention",
    )(rel_bias, lam_init, qn, k1, k2, vt, kn2, bias_t, lambdas, sub_g_col)


def _outffn_kernel(x_ref, hf_ref, hb_ref, og_ref, mb_ref, ng_ref, wo_ref, g2_ref, w1_ref, w2_ref,
                   out_ref, *, ff_chunk):
    hs = hf_ref[...].astype(F32) + hb_ref[...].astype(F32)
    x1 = x_ref[...] + _dot(mb_ref[...], wo_ref[MW:, :])
    for hh in range(N_M):
        sl = slice(hh * HD_M, (hh + 1) * HD_M)
        t = hs[:, sl]
        ms = jnp.mean(t * t, axis=-1, keepdims=True)
        mix = t * lax.rsqrt(ms + EPS) * ng_ref[:, sl] * _sigmoid(og_ref[:, sl].astype(F32))
        x1 = x1 + _dot(mix.astype(BF16), wo_ref[sl, :])
    ms = jnp.mean(x1 * x1, axis=-1, keepdims=True)
    h2 = (x1 * lax.rsqrt(ms + EPS) * g2_ref[...]).astype(BF16)
    out_ref[...] = x1
    for j in range(w1_ref.shape[1] // ff_chunk):
        sl = slice(j * ff_chunk, (j + 1) * ff_chunk)
        u = jnp.maximum(_dot(h2, w1_ref[:, sl]), 0.0)
        out_ref[...] += _dot((u * u).astype(BF16), w2_ref[sl, :])


def _outffn(x, hf, hb, pm, mixb, ng, wo, g2, w1, w2, tm):
    n, d = x.shape
    dff = w1.shape[1]
    row = lambda i: (i, 0)
    return pl.pallas_call(
        functools.partial(_outffn_kernel, ff_chunk=1024),
        grid=(n // tm,),
        in_specs=[
            pl.BlockSpec((tm, d), row),
            pl.BlockSpec((tm, MW), row),
            pl.BlockSpec((tm, MW), row),
            pl.BlockSpec((tm, MW), lambda i: (i, 3)),
            pl.BlockSpec((tm, AW), row),
            _resident((1, MW)),
            _resident((MW + AW, d)),
            _resident((1, d)),
            _resident((d, dff)),
            _resident((dff, d)),
        ],
        out_specs=pl.BlockSpec((tm, d), row),
        out_shape=jax.ShapeDtypeStruct((n, d), F32),
        compiler_params=pltpu.CompilerParams(
            dimension_semantics=("parallel",), vmem_limit_bytes=VMEM_LIMIT),
        name="outproj_ffn",
    )(x, hf, hb, pm, mixb, ng, wo, g2, w1, w2)


def kernel(x, norm1_g, w_in, conv_w, conv_b, gate_b, mlstm_norm_g, q_norm_g, k_norm_g, lambdas,
           diff_norm_g, rel_bias, w_out, norm2_g, w_ff1, w_ff2):
    batch, seq, d = x.shape
    depth = w_in.shape[0]
    n = batch * seq
    tm = min(512, n)
    tile = min(512, seq)
    assert seq % MLSTM_CHUNK == 0 and seq % tile == 0 and tm == tile

    w_main = jnp.concatenate([w_in[:, :, :C_G], w_in[:, :, C_G + N_GATES:]], axis=-1).astype(BF16)
    w_gate = jnp.pad(w_in[:, :, C_G:C_G + N_GATES], ((0, 0), (0, 0), (0, LANES - N_GATES))).astype(BF16)
    w_out_b = w_out.astype(BF16)
    w_ff1_b = w_ff1.astype(BF16)
    w_ff2_b = w_ff2.astype(BF16)
    lane = jnp.arange(LANES)
    bd = jnp.where((lane[:, None] // HD_A) == (lane[None, :] // HD_A), 1.0 / HD_A, 0.0).astype(BF16)

    bias_t = _bias_tiles(rel_bias.astype(F32), tile)

    xf = x.reshape(n, d)
    for l in range(depth):
        lam_init = 0.8 - 0.6 * math.exp(-0.3 * l)
        pm, gates, qn, k1, k2, vt, kn2 = _inproj(
            xf, norm1_g[l][None, :], w_main[l], w_gate[l],
            jnp.tile(q_norm_g[l], 2)[None, :], jnp.tile(k_norm_g[l], 2)[None, :], bd, tm)
        gb = gate_b[l].reshape(1, N_GATES).astype(F32)
        hf, hb = _mlstm(pm, gates, gates.T, conv_w[l], conv_b[l][None, :], gb, gb.T, batch, seq)
        mixb = _attention(qn, k1, k2, vt, kn2, bias_t, rel_bias.astype(F32),
                          jnp.array([[lam_init, 1.0 - lam_init]], F32), lambdas[l],
                          diff_norm_g[l].reshape(N_A, 2 * HD_A, 1), batch, seq, tile)
        xf = _outffn(xf, hf, hb, pm, mixb, mlstm_norm_g[l][None, :], w_out_b[l],
                     norm2_g[l][None, :], w_ff1_b[l], w_ff2_b[l], tm)
    return xf.reshape(batch, seq, d)
```

```python
import functools
import math

import jax
import jax.numpy as jnp
from jax import lax
from jax.experimental import pallas as pl
from jax.experimental.pallas import tpu as pltpu

F32 = jnp.float32
BF16 = jnp.bfloat16

N_M = 4
HD_M = 128
MW = N_M * HD_M
MLSTM_CHUNK = 128
CONV_W = 5
N_A = 4
HD_A = 64
AW = N_A * 2 * HD_A
REL_BUCKETS = 32
REL_MAX_DIST = 128
EPS = 1e-6
N_GATES = 4 * N_M
C_G = 4 * MW
LANES = 128
HALO = 16
VMEM_LIMIT = 56 * 1024 * 1024
LOG2E = math.log2(math.e)
VT_ROWS = 2 * HD_A + HALO
BIAS_ROWS = 3
LOGIT_SPAN_LIMIT = 96.0


def _dot(a, b):
    return jnp.dot(a, b, preferred_element_type=F32)


def _dot_nt(a, b):
    return lax.dot_general(a, b, (((1,), (1,)), ((), ())), preferred_element_type=F32)


def _dot_tn(a, b):
    return lax.dot_general(a, b, (((0,), (0,)), ((), ())), preferred_element_type=F32)


def _dot_exact(a, b):
    return jnp.dot(a, b, preferred_element_type=F32, precision=lax.Precision.HIGHEST)


def _sigmoid(x):
    return 1.0 / (1.0 + jnp.exp(-x))


def _log_sigmoid(x):
    return jnp.minimum(x, 0.0) - jnp.log1p(jnp.exp(-jnp.abs(x)))


def _resident(shape):
    nd = len(shape)
    return pl.BlockSpec(shape, lambda *_: (0,) * nd, pipeline_mode=pl.Buffered(1))


def _inproj_kernel(x_ref, g1_ref, w_ref, wg_ref, qg_ref, kg_ref, bd_ref,
                   pm_ref, gates_ref, qn_ref, k1_ref, k2_ref, vt_ref):
    x = x_ref[...]
    ms = jnp.mean(x * x, axis=-1, keepdims=True)
    h = (x * lax.rsqrt(ms + EPS) * g1_ref[...]).astype(BF16)

    def proj(c0, width):
        return _dot(h, w_ref[:, c0:c0 + width])

    for j in range(4 * MW // 512):
        pm_ref[:, j * 512:(j + 1) * 512] = proj(j * 512, 512).astype(BF16)
    gates_ref[...] = _dot(h, wg_ref[...])[:, :N_GATES]

    bd = bd_ref[...]
    pair = 2 * LANES

    def qk_norm(c0, g_ref):
        t = proj(c0, pair)
        msq = _dot((t * t).astype(BF16), bd)
        return t * lax.rsqrt(msq + EPS) * g_ref[...]

    lane = lax.broadcasted_iota(jnp.int32, (x.shape[0], pair), 1) % LANES
    ones_cols = jnp.where(lane < HD_A + BIAS_ROWS, 1.0, 0.0)
    for hp in range(N_A // 2):
        sl = slice(hp * pair, (hp + 1) * pair)
        qn_ref[:, sl] = (qk_norm(4 * MW + hp * pair, qg_ref) * (HD_A ** -0.5 * LOG2E)).astype(BF16)
        kn = qk_norm(4 * MW + AW + hp * pair, kg_ref)
        k1_ref[:, sl] = jnp.where(lane < HD_A, kn, ones_cols).astype(BF16)
        k2_ref[:, sl] = jnp.where(lane < HD_A, pltpu.roll(kn, pair - HD_A, 1), ones_cols).astype(BF16)
    for hh in range(N_A):
        vt = proj(4 * MW + 2 * AW + hh * LANES, LANES).T
        vt_ref[0, hh * VT_ROWS:hh * VT_ROWS + 2 * HD_A, :] = vt.astype(BF16)
        vt_ref[0, hh * VT_ROWS + 2 * HD_A:(hh + 1) * VT_ROWS, :] = jnp.ones((HALO, x.shape[0]), BF16)


def _inproj(x, g1, w_main, w_gate, qg, kg, bd, tm):
    n, d = x.shape
    wcols = w_main.shape[1]
    row = lambda i: (i, 0)
    return pl.pallas_call(
        _inproj_kernel,
        grid=(n // tm,),
        in_specs=[
            pl.BlockSpec((tm, d), row),
            _resident((1, d)),
            _resident((d, wcols)),
            _resident((d, LANES)),
            _resident((1, 2 * LANES)),
            _resident((1, 2 * LANES)),
            _resident((2 * LANES, 2 * LANES)),
        ],
        out_specs=[
            pl.BlockSpec((tm, 4 * MW), row),
            pl.BlockSpec((tm, N_GATES), row),
            pl.BlockSpec((tm, AW), row),
            pl.BlockSpec((tm, AW), row),
            pl.BlockSpec((tm, AW), row),
            pl.BlockSpec((1, N_A * VT_ROWS, tm), lambda i: (i, 0, 0)),
        ],
        out_shape=[
            jax.ShapeDtypeStruct((n, 4 * MW), BF16),
            jax.ShapeDtypeStruct((n, N_GATES), F32),
            jax.ShapeDtypeStruct((n, AW), BF16),
            jax.ShapeDtypeStruct((n, AW), BF16),
            jax.ShapeDtypeStruct((n, AW), BF16),
            jax.ShapeDtypeStruct((n // tm, N_A * VT_ROWS, tm), BF16),
        ],
        compiler_params=pltpu.CompilerParams(
            dimension_semantics=("parallel",), vmem_limit_bytes=VMEM_LIMIT),
        name="inproj",
    )(x, g1, w_main, w_gate, qg, kg, bd)


def _conv_kernel(prev_ref, cur_ref, next_ref, cw_ref, cb_ref, out_ref, ext_s, *, tiles_per_seq):
    rows = cur_ref.shape[0]
    pos = pl.program_id(0) % tiles_per_seq
    ext_s[0:HALO, :] = jnp.where(pos > 0, prev_ref[...].astype(F32), 0.0)
    ext_s[HALO:HALO + rows, :] = cur_ref[...].astype(F32)
    ext_s[HALO + rows:2 * HALO + rows, :] = jnp.where(pos < tiles_per_seq - 1, next_ref[...].astype(F32), 0.0)
    conv = cb_ref[...]
    for j in range(CONV_W):
        off = HALO - CONV_W // 2 + j
        conv = conv + ext_s[off:off + rows, :] * cw_ref[j:j + 1, :]
    y = conv * _sigmoid(conv)
    out_ref[:, :MW] = y[:, :MW].astype(BF16)
    out_ref[:, MW:] = (y[:, MW:] * (HD_M ** -0.5)).astype(BF16)


def _conv(pm, conv_w, conv_b, seq, rows):
    n = pm.shape[0]
    hpt = rows // HALO
    nhalo = n // HALO
    return pl.pallas_call(
        functools.partial(_conv_kernel, tiles_per_seq=seq // rows),
        grid=(n // rows,),
        in_specs=[
            pl.BlockSpec((HALO, 2 * MW), lambda i: (jnp.maximum(i * hpt - 1, 0), 0)),
            pl.BlockSpec((rows, 2 * MW), lambda i: (i, 0)),
            pl.BlockSpec((HALO, 2 * MW), lambda i: (jnp.minimum((i + 1) * hpt, nhalo - 1), 0)),
            _resident((CONV_W, 2 * MW)),
            _resident((1, 2 * MW)),
        ],
        out_specs=pl.BlockSpec((rows, 2 * MW), lambda i: (i, 0)),
        out_shape=jax.ShapeDtypeStruct((n, 2 * MW), BF16),
        scratch_shapes=[pltpu.VMEM((rows + 2 * HALO, 2 * MW), F32)],
        compiler_params=pltpu.CompilerParams(
            dimension_semantics=("parallel",), vmem_limit_bytes=VMEM_LIMIT),
        name="mlstm_conv",
    )(pm, pm, pm, conv_w, conv_b)


def _mlstm_kernel(qk_f, v_f, g_f, gt_f,
                  qk_b, v_b, g_b, gt_b,
                  gbr_ref, gbc_ref,
                  hf_ref, hb_ref,
                  c_s, n_s, m_s):
    L = MLSTM_CHUNK
    c = pl.program_id(1)

    @pl.when(c == 0)
    def _():
        c_s[...] = jnp.zeros_like(c_s)
        n_s[...] = jnp.zeros_like(n_s)
        m_s[...] = jnp.zeros_like(m_s)

    row = lax.broadcasted_iota(jnp.int32, (L, L), 0)
    col = lax.broadcasted_iota(jnp.int32, (L, L), 1)
    lower = col <= row
    upper = row <= col
    lower_f = lower.astype(F32)
    upper_f = upper.astype(F32)

    dirs = ((qk_f, v_f, g_f, gt_f, hf_ref), (qk_b, v_b, g_b, gt_b, hb_ref))
    for d, (qk_ref, v_ref, g_ref, gt_ref, out_ref) in enumerate(dirs):
        gates = g_ref[...] + gbr_ref[...]
        gates_t = gt_ref[...] + gbc_ref[...]
        logf = _log_sigmoid(gates)
        logf_t = _log_sigmoid(gates_t)
        cum = _dot_exact(lower_f, logf)
        cum_t = _dot_exact(logf_t, upper_f)
        if d == 1:
            cum = cum[L - 1:L, :] - cum + logf
            cum_t = cum_t[:, L - 1:L] - cum_t + logf_t
        i_off = 2 * N_M * d
        f_off = i_off + N_M
        mask = lower if d == 0 else upper

        for hh in range(N_M):
            idx = d * N_M + hh
            b_col = cum[:, f_off + hh:f_off + hh + 1]
            i_col = gates[:, i_off + hh:i_off + hh + 1]
            b_row = cum_t[f_off + hh:f_off + hh + 1, :]
            i_row = gates_t[i_off + hh:i_off + hh + 1, :]
            dmat = jnp.where(mask, b_col - b_row + i_row, -jnp.inf)
            m_prev = m_s[idx][:, :1]
            inter = b_col + m_prev
            m_t = jnp.maximum(inter, jnp.max(dmat, axis=-1, keepdims=True))
            scale = jnp.exp(inter - m_t)
            qb = qk_ref[:, hh * HD_M:(hh + 1) * HD_M]
            kb = qk_ref[:, MW + hh * HD_M:MW + (hh + 1) * HD_M]
            q = qb.astype(F32)
            k = kb.astype(F32)
            vb = v_ref[:, hh * HD_M:(hh + 1) * HD_M]
            w = _dot_nt(qb, kb) * jnp.exp(dmat - m_t)
            c_prev = c_s[idx]
            n_prev = n_s[idx]
            num = scale * _dot(qb, c_prev.astype(BF16)) + _dot(w.astype(BF16), vb)
            den = (scale * jnp.sum(q * n_prev, axis=-1, keepdims=True)
                   + jnp.sum(w, axis=-1, keepdims=True))
            hval = num / jnp.maximum(jnp.abs(den), jnp.exp(-m_t))
            out_ref[:, hh * HD_M:(hh + 1) * HD_M] = hval.astype(out_ref.dtype)

            b_end = b_col[L - 1:L, :] if d == 0 else b_col[0:1, :]
            g_col = b_end - b_col + i_col
            m_new = jnp.maximum(b_end + m_prev, jnp.max(g_col, axis=0, keepdims=True))
            decay = jnp.exp(b_end + m_prev - m_new)
            kw = k * jnp.exp(g_col - m_new)
            c_s[idx] = decay * c_prev + _dot_tn(kw.astype(BF16), vb)
            n_s[idx] = decay * n_prev + jnp.sum(kw, axis=0, keepdims=True)
            m_s[idx] = jnp.broadcast_to(m_new, (1, LANES))


def _mlstm(qk, pm, gates, gates_t, gb_row, gb_col, batch, seq):
    L = MLSTM_CHUNK
    nc = seq // L

    def chunk_f(b, c):
        return b * nc + c

    def chunk_b(b, c):
        return b * nc + (nc - 1 - c)

    def specs(chunk):
        return [
            pl.BlockSpec((L, 2 * MW), lambda b, c: (chunk(b, c), 0)),
            pl.BlockSpec((L, MW), lambda b, c: (chunk(b, c), 2)),
            pl.BlockSpec((L, N_GATES), lambda b, c: (chunk(b, c), 0)),
            pl.BlockSpec((N_GATES, L), lambda b, c: (0, chunk(b, c))),
        ]

    n = batch * seq
    return pl.pallas_call(
        _mlstm_kernel,
        grid=(batch, nc),
        in_specs=specs(chunk_f) + specs(chunk_b) + [
            _resident((1, N_GATES)),
            _resident((N_GATES, 1)),
        ],
        out_specs=[
            pl.BlockSpec((L, MW), lambda b, c: (chunk_f(b, c), 0)),
            pl.BlockSpec((L, MW), lambda b, c: (chunk_b(b, c), 0)),
        ],
        out_shape=[jax.ShapeDtypeStruct((n, MW), BF16)] * 2,
        scratch_shapes=[
            pltpu.VMEM((2 * N_M, HD_M, HD_M), F32),
            pltpu.VMEM((2 * N_M, 1, HD_M), F32),
            pltpu.VMEM((2 * N_M, 1, LANES), F32),
        ],
        compiler_params=pltpu.CompilerParams(
            dimension_semantics=("parallel", "arbitrary"), vmem_limit_bytes=VMEM_LIMIT),
        name="mlstm",
    )(qk, pm, gates, gates_t, qk, pm, gates, gates_t, gb_row, gb_col)


def _bias_kernel(rb_ref, out_ref, *, tile):
    hh = pl.program_id(0)
    dd = pl.program_id(1)
    kk = lax.broadcasted_iota(jnp.int32, (tile, tile), 0)
    qq = lax.broadcasted_iota(jnp.int32, (tile, tile), 1)
    rel = (dd - 1) * tile + kk - qq
    nb = REL_BUCKETS // 2
    max_exact = nb // 2
    n = jnp.abs(rel)
    nf = jnp.maximum(n, 1).astype(F32)
    large = max_exact + (jnp.log(nf / max_exact) / math.log(REL_MAX_DIST / max_exact)
                         * (nb - max_exact)).astype(jnp.int32)
    large = jnp.minimum(large, nb - 1)
    bucket = jnp.where(rel > 0, nb, 0) + jnp.where(n < max_exact, n, large)
    val = jnp.zeros((tile, tile), F32)
    for bkt in range(REL_BUCKETS):
        val = jnp.where(bucket == bkt, rb_ref[bkt, hh], val)
    out_ref[0, 0] = val * LOG2E


def _bias_tiles(rel_bias, tile):
    return pl.pallas_call(
        functools.partial(_bias_kernel, tile=tile),
        grid=(N_A, 3),
        in_specs=[pl.BlockSpec(memory_space=pltpu.SMEM)],
        out_specs=pl.BlockSpec((1, 1, tile, tile), lambda h, d: (h, d, 0, 0)),
        out_shape=jax.ShapeDtypeStruct((N_A, 3, tile, tile), F32),
        name="rel_bias_tiles",
    )(rel_bias)


def _far_bucket(tile):
    assert tile >= REL_MAX_DIST
    return REL_BUCKETS // 2 - 1


def _attn_kernel(rb_ref, li_ref, q_ref, k1_ref, k2_ref, vt_ref, qg_ref, kg_ref, bias_ref, lam_ref, sg_ref,
                 o_ref,
                 qt_s, m_s, acc_s, s0_s, s1_s, p0_s, p1_s, a0_s, a1_s, *, far):
    hh = pl.program_id(1)
    qi = pl.program_id(2)
    tq = q_ref.shape[0]
    tk = vt_ref.shape[2]
    nk = vt_ref.shape[0]
    vd = 2 * HD_A
    LEFT, RIGHT, NEAR = 0, 1, 2

    qt = q_ref[...].astype(F32).T
    rid = lax.broadcasted_iota(jnp.int32, qt.shape, 0)
    bases = (qt, jnp.concatenate([qt[HD_A:], qt[:HD_A]], axis=0))
    consts = (rb_ref[far, hh] * LOG2E, rb_ref[REL_BUCKETS // 2 + far, hh] * LOG2E, 0.0)

    def build_queries(shift):
        for kind, cst in enumerate(consts):
            rest = jnp.full(qt.shape, cst - shift, F32)
            extra = jnp.zeros(qt.shape, F32)
            for r in range(BIAS_ROWS):
                part = rest.astype(BF16).astype(F32)
                extra = jnp.where(rid == HD_A + r, part, extra)
                rest = rest - part
            for i, base in enumerate(bases):
                qt_s[2 * kind + i] = jnp.where(rid < HD_A, base, extra).astype(BF16)

    acc_s[...] = jnp.zeros_like(acc_s)

    lo = jnp.maximum(qi - 1, 0)
    n_near = jnp.minimum(qi + 1, nk - 1) - lo + 1
    n_far = nk - n_near
    tile_of = lambda t: jnp.minimum(jnp.where(t < lo, t, t + n_near), nk - 1)
    kind_of = lambda t: jnp.where(t < lo, LEFT, RIGHT)
    n_pipe = n_far - n_far % 2
    clamp = lambda t: jnp.clip(t, 0, jnp.maximum(n_pipe - 1, 0))

    def key_rows(j):
        return pl.ds(pl.multiple_of(j * tk, tk), tk)

    bucket_vals = [rb_ref[bkt, hh] for bkt in range(REL_BUCKETS)]
    bias_max = functools.reduce(jnp.maximum, bucket_vals) * LOG2E
    bias_min = functools.reduce(jnp.minimum, bucket_vals) * LOG2E
    reach = (HD_A ** 0.5 * LOG2E) * jnp.max(jnp.abs(qg_ref[...])) * jnp.max(jnp.abs(kg_ref[...]))
    shift = reach + bias_max
    span = 2.0 * reach + (bias_max - bias_min)
    bounded = span <= LOGIT_SPAN_LIMIT

    @pl.when(bounded)
    def _():
        build_queries(shift)

        def numerators(j, kind, p_buf, bias=None):
            for i, k_ref in enumerate((k1_ref, k2_ref)):
                s = _dot(k_ref[key_rows(j), :], qt_s[2 * kind + i])
                if bias is not None:
                    s = s + bias
                p_buf[i] = jnp.exp2(s).astype(BF16)

        def accumulate(j, p_buf):
            vt = vt_ref[j]
            for i in range(2):
                acc_s[i] += _dot(vt, p_buf[i])

        for dlt in (-1, 0, 1):
            @pl.when(jnp.logical_and(qi + dlt >= 0, qi + dlt < nk))
            def _():
                numerators(qi + dlt, NEAR, p0_s, bias_ref[0, dlt + 1])
                accumulate(qi + dlt, p0_s)

        @pl.when(n_far % 2 == 1)
        def _():
            numerators(tile_of(n_far - 1), kind_of(n_far - 1), p0_s)
            accumulate(tile_of(n_far - 1), p0_s)

        numerators(tile_of(clamp(0)), kind_of(clamp(0)), p0_s)

        def pair(u, carry):
            t0 = 2 * u
            numerators(tile_of(t0 + 1), kind_of(t0 + 1), p1_s)
            accumulate(tile_of(t0), p0_s)
            numerators(tile_of(clamp(t0 + 2)), kind_of(clamp(t0 + 2)), p0_s)
            accumulate(tile_of(t0 + 1), p1_s)
            return carry

        lax.fori_loop(0, n_pipe // 2, pair, 0)

    @pl.when(jnp.logical_not(bounded))
    def _():
        _online_softmax_tiles(qi, nk, build_queries, key_rows, tile_of, kind_of, clamp, n_far, n_pipe,
                              k1_ref, k2_ref, vt_ref, bias_ref,
                              qt_s, m_s, acc_s, s0_s, s1_s, p0_s, p1_s, a0_s, a1_s)

    lv = lam_ref[...]
    lam = (jnp.exp(jnp.sum(lv[0:1] * lv[1:2], axis=-1, keepdims=True))
           - jnp.exp(jnp.sum(lv[2:3] * lv[3:4], axis=-1, keepdims=True)) + li_ref[0, 0])
    a1 = acc_s[0]
    a2 = acc_s[1]
    o = a1[:vd] / a1[vd:vd + 1] - lam * (a2[:vd] / a2[vd:vd + 1])
    ms = jnp.mean(o * o, axis=0, keepdims=True)
    o = o * lax.rsqrt(ms + EPS) * sg_ref[0] * li_ref[0, 1]
    o_ref[...] = o.T.astype(o_ref.dtype)


def _online_softmax_tiles(qi, nk, build_queries, key_rows, tile_of, kind_of, clamp, n_far, n_pipe,
                          k1_ref, k2_ref, vt_ref, bias_ref,
                          qt_s, m_s, acc_s, s0_s, s1_s, p0_s, p1_s, a0_s, a1_s):
    LEFT, RIGHT, NEAR = 0, 1, 2
    build_queries(0.0)
    m_s[...] = jnp.full_like(m_s, -jnp.inf)

    def logits(j, kind, s_buf):
        for i, k_ref in enumerate((k1_ref, k2_ref)):
            s_buf[i] = _dot(k_ref[key_rows(j), :], qt_s[2 * kind + i])

    def softmax(s_buf, p_buf, a_buf, bias=None):
        for i in range(2):
            s = s_buf[i]
            if bias is not None:
                s = s + bias
            m_old = m_s[i]
            m_new = jnp.maximum(m_old, jnp.max(s, axis=0, keepdims=True))
            a_buf[i] = jnp.exp2(m_old - m_new)
            p_buf[i] = jnp.exp2(s - m_new).astype(BF16)
            m_s[i] = m_new

    def values(j, p_buf, a_buf):
        vt = vt_ref[j]
        for i in range(2):
            acc_s[i] = a_buf[i] * acc_s[i] + _dot(vt, p_buf[i])

    def whole_tile(j, kind, bias=None):
        logits(j, kind, s0_s)
        softmax(s0_s, p0_s, a0_s, bias)
        values(j, p0_s, a0_s)

    for dlt in (-1, 0, 1):
        @pl.when(jnp.logical_and(qi + dlt >= 0, qi + dlt < nk))
        def _():
            whole_tile(qi + dlt, NEAR, bias_ref[0, dlt + 1])

    @pl.when(n_far % 2 == 1)
    def _():
        whole_tile(tile_of(n_far - 1), kind_of(n_far - 1))

    p1_s[...] = jnp.zeros_like(p1_s)
    a1_s[...] = jnp.ones_like(a1_s)
    logits(tile_of(clamp(0)), kind_of(clamp(0)), s0_s)

    def pair(u, carry):
        t0 = 2 * u
        values(tile_of(clamp(t0 - 1)), p1_s, a1_s)
        logits(tile_of(t0 + 1), kind_of(t0 + 1), s1_s)
        softmax(s0_s, p0_s, a0_s)
        values(tile_of(t0), p0_s, a0_s)
        logits(tile_of(clamp(t0 + 2)), kind_of(clamp(t0 + 2)), s0_s)
        softmax(s1_s, p1_s, a1_s)
        return carry

    lax.fori_loop(0, n_pipe // 2, pair, 0)
    values(tile_of(clamp(n_pipe - 1)), p1_s, a1_s)


def _attention(qn, k1, k2, vt, qg, kg, bias_t, rel_bias, lam_init, lambdas, sub_g_col, batch, seq, tile):
    nt = seq // tile
    vd = 2 * HD_A
    far = _far_bucket(tile)
    smem = pl.BlockSpec(memory_space=pltpu.SMEM)
    return pl.pallas_call(
        functools.partial(_attn_kernel, far=far),
        grid=(batch, N_A, nt),
        in_specs=[
            smem,
            smem,
            pl.BlockSpec((tile, vd), lambda b, h, i: (b * nt + i, h)),
            pl.BlockSpec((seq, vd), lambda b, h, i: (b, h)),
            pl.BlockSpec((seq, vd), lambda b, h, i: (b, h)),
            pl.BlockSpec((nt, VT_ROWS, tile), lambda b, h, i: (b, h, 0)),
            pl.BlockSpec((1, 2 * LANES), lambda b, h, i: (0, 0)),
            pl.BlockSpec((1, 2 * LANES), lambda b, h, i: (0, 0)),
            pl.BlockSpec((1, 3, tile, tile), lambda b, h, i: (h, 0, 0, 0)),
            pl.BlockSpec((4, HD_A), lambda b, h, i: (0, 0)),
            pl.BlockSpec((1, vd, 1), lambda b, h, i: (h, 0, 0)),
        ],
        out_specs=pl.BlockSpec((tile, vd), lambda b, h, i: (b * nt + i, h)),
        out_shape=jax.ShapeDtypeStruct((batch * seq, AW), BF16),
        scratch_shapes=[
            pltpu.VMEM((6, vd, tile), BF16),
            pltpu.VMEM((2, 1, tile), F32),
            pltpu.VMEM((2, VT_ROWS, tile), F32),
            pltpu.VMEM((2, tile, tile), F32),
            pltpu.VMEM((2, tile, tile), F32),
            pltpu.VMEM((2, tile, tile), BF16),
            pltpu.VMEM((2, tile, tile), BF16),
            pltpu.VMEM((2, 1, tile), F32),
            pltpu.VMEM((2, 1, tile), F32),
        ],
        compiler_params=pltpu.CompilerParams(
            dimension_semantics=("parallel", "parallel", "arbitrary"),
            vmem_limit_bytes=VMEM_LIMIT),
        name="diff_attention",
    )(rel_bias, lam_init, qn, k1, k2, vt, qg, kg, bias_t, lambdas, sub_g_col)


def _outffn_kernel(x_ref, hf_ref, hb_ref, og_ref, mb_ref, ng_ref, wo_ref, g2_ref, w1_ref, w2_ref,
                   out_ref, *, ff_chunk):
    hs = hf_ref[...].astype(F32) + hb_ref[...].astype(F32)
    x1 = x_ref[...] + _dot(mb_ref[...], wo_ref[MW:, :])
    for hh in range(N_M):
        sl = slice(hh * HD_M, (hh + 1) * HD_M)
        t = hs[:, sl]
        ms = jnp.mean(t * t, axis=-1, keepdims=True)
        mix = t * lax.rsqrt(ms + EPS) * ng_ref[:, sl] * _sigmoid(og_ref[:, sl].astype(F32))
        x1 = x1 + _dot(mix.astype(BF16), wo_ref[sl, :])
    ms = jnp.mean(x1 * x1, axis=-1, keepdims=True)
    h2 = (x1 * lax.rsqrt(ms + EPS) * g2_ref[...]).astype(BF16)
    out_ref[...] = x1
    for j in range(w1_ref.shape[1] // ff_chunk):
        sl = slice(j * ff_chunk, (j + 1) * ff_chunk)
        u = jnp.maximum(_dot(h2, w1_ref[:, sl]), 0.0)
        out_ref[...] += _dot((u * u).astype(BF16), w2_ref[sl, :])


def _outffn(x, hf, hb, pm, mixb, ng, wo, g2, w1, w2, tm):
    n, d = x.shape
    dff = w1.shape[1]
    row = lambda i: (i, 0)
    return pl.pallas_call(
        functools.partial(_outffn_kernel, ff_chunk=1024),
        grid=(n // tm,),
        in_specs=[
            pl.BlockSpec((tm, d), row),
            pl.BlockSpec((tm, MW), row),
            pl.BlockSpec((tm, MW), row),
            pl.BlockSpec((tm, MW), lambda i: (i, 3)),
            pl.BlockSpec((tm, AW), row),
            _resident((1, MW)),
            _resident((MW + AW, d)),
            _resident((1, d)),
            _resident((d, dff)),
            _resident((dff, d)),
        ],
        out_specs=pl.BlockSpec((tm, d), row),
        out_shape=jax.ShapeDtypeStruct((n, d), F32),
        compiler_params=pltpu.CompilerParams(
            dimension_semantics=("parallel",), vmem_limit_bytes=VMEM_LIMIT),
        name="outproj_ffn",
    )(x, hf, hb, pm, mixb, ng, wo, g2, w1, w2)


def kernel(x, norm1_g, w_in, conv_w, conv_b, gate_b, mlstm_norm_g, q_norm_g, k_norm_g, lambdas,
           diff_norm_g, rel_bias, w_out, norm2_g, w_ff1, w_ff2):
    batch, seq, d = x.shape
    depth = w_in.shape[0]
    n = batch * seq
    tm = min(512, n)
    tile = min(512, seq)
    assert seq % MLSTM_CHUNK == 0 and seq % tile == 0 and tm == tile

    w_main = jnp.concatenate([w_in[:, :, :C_G], w_in[:, :, C_G + N_GATES:]], axis=-1).astype(BF16)
    w_gate = jnp.pad(w_in[:, :, C_G:C_G + N_GATES], ((0, 0), (0, 0), (0, LANES - N_GATES))).astype(BF16)
    w_out_b = w_out.astype(BF16)
    w_ff1_b = w_ff1.astype(BF16)
    w_ff2_b = w_ff2.astype(BF16)
    lane = jnp.arange(2 * LANES)
    bd = jnp.where((lane[:, None] // HD_A) == (lane[None, :] // HD_A), 1.0 / HD_A, 0.0).astype(BF16)

    bias_t = _bias_tiles(rel_bias.astype(F32), tile)

    xf = x.reshape(n, d)
    for l in range(depth):
        lam_init = 0.8 - 0.6 * math.exp(-0.3 * l)
        qg = jnp.tile(q_norm_g[l], 2 * LANES // HD_A)[None, :]
        kg = jnp.tile(k_norm_g[l], 2 * LANES // HD_A)[None, :]
        pm, gates, qn, k1, k2, vt = _inproj(
            xf, norm1_g[l][None, :], w_main[l], w_gate[l], qg, kg, bd, tm)
        gb = gate_b[l].reshape(1, N_GATES).astype(F32)
        qk = _conv(pm, conv_w[l], conv_b[l][None, :], seq, tile)
        hf, hb = _mlstm(qk, pm, gates, gates.T, gb, gb.T, batch, seq)
        mixb = _attention(qn, k1, k2, vt, qg, kg, bias_t, rel_bias.astype(F32),
                          jnp.array([[lam_init, 1.0 - lam_init]], F32), lambdas[l],
                          diff_norm_g[l].reshape(N_A, 2 * HD_A, 1), batch, seq, tile)
        xf = _outffn(xf, hf, hb, pm, mixb, mlstm_norm_g[l][None, :], w_out_b[l],
                     norm2_g[l][None, :], w_ff1_b[l], w_ff2_b[l], tm)
    return xf.reshape(batch, seq, d)
```

```python
import functools
import math

import jax
import jax.numpy as jnp
from jax import lax
from jax.experimental import pallas as pl
from jax.experimental.pallas import tpu as pltpu

F32 = jnp.float32
BF16 = jnp.bfloat16

N_M = 4
HD_M = 128
MW = N_M * HD_M
MLSTM_CHUNK = 128
CONV_W = 5
N_A = 4
HD_A = 64
AW = N_A * 2 * HD_A
REL_BUCKETS = 32
REL_MAX_DIST = 128
EPS = 1e-6
N_GATES = 4 * N_M
C_G = 4 * MW
LANES = 128
HALO = 16
VMEM_LIMIT = 56 * 1024 * 1024
LOG2E = math.log2(math.e)
VT_ROWS = 2 * HD_A + HALO
BIAS_ROWS = 3
LOGIT_SPAN_LIMIT = 96.0
ATTN_WINDOW = 4


def _dot(a, b):
    return jnp.dot(a, b, preferred_element_type=F32)


def _dot_nt(a, b):
    return lax.dot_general(a, b, (((1,), (1,)), ((), ())), preferred_element_type=F32)


def _dot_tn(a, b):
    return lax.dot_general(a, b, (((0,), (0,)), ((), ())), preferred_element_type=F32)


def _dot_exact(a, b):
    return jnp.dot(a, b, preferred_element_type=F32, precision=lax.Precision.HIGHEST)


def _sigmoid(x):
    return 1.0 / (1.0 + jnp.exp(-x))


def _log_sigmoid(x):
    return jnp.minimum(x, 0.0) - jnp.log1p(jnp.exp(-jnp.abs(x)))


def _resident(shape):
    nd = len(shape)
    return pl.BlockSpec(shape, lambda *_: (0,) * nd, pipeline_mode=pl.Buffered(1))


def _inproj_kernel(x_ref, g1_ref, w_ref, wg_ref, qg_ref, kg_ref, bd_ref,
                   pm_ref, gates_ref, qn_ref, k1_ref, k2_ref, vt_ref):
    x = x_ref[...]
    ms = jnp.mean(x * x, axis=-1, keepdims=True)
    h = (x * lax.rsqrt(ms + EPS) * g1_ref[...]).astype(BF16)

    def proj(c0, width):
        return _dot(h, w_ref[:, c0:c0 + width])

    for j in range(4 * MW // 512):
        pm_ref[:, j * 512:(j + 1) * 512] = proj(j * 512, 512).astype(BF16)
    gates_ref[...] = _dot(h, wg_ref[...])[:, :N_GATES]

    bd = bd_ref[...]
    pair = 2 * LANES

    def qk_norm(c0, g_ref):
        t = proj(c0, pair)
        msq = _dot((t * t).astype(BF16), bd)
        return t * lax.rsqrt(msq + EPS) * g_ref[...]

    lane = lax.broadcasted_iota(jnp.int32, (x.shape[0], pair), 1) % LANES
    ones_cols = jnp.where(lane < HD_A + BIAS_ROWS, 1.0, 0.0)
    for hp in range(N_A // 2):
        sl = slice(hp * pair, (hp + 1) * pair)
        qn_ref[:, sl] = (qk_norm(4 * MW + hp * pair, qg_ref) * (HD_A ** -0.5 * LOG2E)).astype(BF16)
        kn = qk_norm(4 * MW + AW + hp * pair, kg_ref)
        k1_ref[:, sl] = jnp.where(lane < HD_A, kn, ones_cols).astype(BF16)
        k2_ref[:, sl] = jnp.where(lane < HD_A, pltpu.roll(kn, pair - HD_A, 1), ones_cols).astype(BF16)
    for hh in range(N_A):
        vt = proj(4 * MW + 2 * AW + hh * LANES, LANES).T
        vt_ref[0, hh * VT_ROWS:hh * VT_ROWS + 2 * HD_A, :] = vt.astype(BF16)
        vt_ref[0, hh * VT_ROWS + 2 * HD_A:(hh + 1) * VT_ROWS, :] = jnp.ones((HALO, x.shape[0]), BF16)


def _inproj(x, g1, w_main, w_gate, qg, kg, bd, tm):
    n, d = x.shape
    wcols = w_main.shape[1]
    row = lambda i: (i, 0)
    return pl.pallas_call(
        _inproj_kernel,
        grid=(n // tm,),
        in_specs=[
            pl.BlockSpec((tm, d), row),
            _resident((1, d)),
            _resident((d, wcols)),
            _resident((d, LANES)),
            _resident((1, 2 * LANES)),
            _resident((1, 2 * LANES)),
            _resident((2 * LANES, 2 * LANES)),
        ],
        out_specs=[
            pl.BlockSpec((tm, 4 * MW), row),
            pl.BlockSpec((tm, N_GATES), row),
            pl.BlockSpec((tm, AW), row),
            pl.BlockSpec((tm, AW), row),
            pl.BlockSpec((tm, AW), row),
            pl.BlockSpec((1, N_A * VT_ROWS, tm), lambda i: (i, 0, 0)),
        ],
        out_shape=[
            jax.ShapeDtypeStruct((n, 4 * MW), BF16),
            jax.ShapeDtypeStruct((n, N_GATES), F32),
            jax.ShapeDtypeStruct((n, AW), BF16),
            jax.ShapeDtypeStruct((n, AW), BF16),
            jax.ShapeDtypeStruct((n, AW), BF16),
            jax.ShapeDtypeStruct((n // tm, N_A * VT_ROWS, tm), BF16),
        ],
        compiler_params=pltpu.CompilerParams(
            dimension_semantics=("parallel",), vmem_limit_bytes=VMEM_LIMIT),
        name="inproj",
    )(x, g1, w_main, w_gate, qg, kg, bd)


def _conv_kernel(prev_ref, cur_ref, next_ref, cw_ref, cb_ref, out_ref, ext_s, *, tiles_per_seq):
    rows = cur_ref.shape[0]
    pos = pl.program_id(0) % tiles_per_seq
    ext_s[0:HALO, :] = jnp.where(pos > 0, prev_ref[...].astype(F32), 0.0)
    ext_s[HALO:HALO + rows, :] = cur_ref[...].astype(F32)
    ext_s[HALO + rows:2 * HALO + rows, :] = jnp.where(pos < tiles_per_seq - 1, next_ref[...].astype(F32), 0.0)
    conv = cb_ref[...]
    for j in range(CONV_W):
        off = HALO - CONV_W // 2 + j
        conv = conv + ext_s[off:off + rows, :] * cw_ref[j:j + 1, :]
    y = conv * _sigmoid(conv)
    out_ref[:, :MW] = y[:, :MW].astype(BF16)
    out_ref[:, MW:] = (y[:, MW:] * (HD_M ** -0.5)).astype(BF16)


def _conv(pm, conv_w, conv_b, seq, rows):
    n = pm.shape[0]
    hpt = rows // HALO
    nhalo = n // HALO
    return pl.pallas_call(
        functools.partial(_conv_kernel, tiles_per_seq=seq // rows),
        grid=(n // rows,),
        in_specs=[
            pl.BlockSpec((HALO, 2 * MW), lambda i: (jnp.maximum(i * hpt - 1, 0), 0)),
            pl.BlockSpec((rows, 2 * MW), lambda i: (i, 0)),
            pl.BlockSpec((HALO, 2 * MW), lambda i: (jnp.minimum((i + 1) * hpt, nhalo - 1), 0)),
            _resident((CONV_W, 2 * MW)),
            _resident((1, 2 * MW)),
        ],
        out_specs=pl.BlockSpec((rows, 2 * MW), lambda i: (i, 0)),
        out_shape=jax.ShapeDtypeStruct((n, 2 * MW), BF16),
        scratch_shapes=[pltpu.VMEM((rows + 2 * HALO, 2 * MW), F32)],
        compiler_params=pltpu.CompilerParams(
            dimension_semantics=("parallel",), vmem_limit_bytes=VMEM_LIMIT),
        name="mlstm_conv",
    )(pm, pm, pm, conv_w, conv_b)


def _mlstm_kernel(qk_f, v_f, g_f, gt_f,
                  qk_b, v_b, g_b, gt_b,
                  gbr_ref, gbc_ref,
                  hf_ref, hb_ref,
                  c_s, n_s, m_s):
    L = MLSTM_CHUNK
    c = pl.program_id(1)

    @pl.when(c == 0)
    def _():
        c_s[...] = jnp.zeros_like(c_s)
        n_s[...] = jnp.zeros_like(n_s)
        m_s[...] = jnp.zeros_like(m_s)

    row = lax.broadcasted_iota(jnp.int32, (L, L), 0)
    col = lax.broadcasted_iota(jnp.int32, (L, L), 1)
    lower = col <= row
    upper = row <= col
    lower_f = lower.astype(F32)
    upper_f = upper.astype(F32)

    dirs = ((qk_f, v_f, g_f, gt_f, hf_ref), (qk_b, v_b, g_b, gt_b, hb_ref))
    for d, (qk_ref, v_ref, g_ref, gt_ref, out_ref) in enumerate(dirs):
        gates = g_ref[...] + gbr_ref[...]
        gates_t = gt_ref[...] + gbc_ref[...]
        logf = _log_sigmoid(gates)
        logf_t = _log_sigmoid(gates_t)
        cum = _dot_exact(lower_f, logf)
        cum_t = _dot_exact(logf_t, upper_f)
        if d == 1:
            cum = cum[L - 1:L, :] - cum + logf
            cum_t = cum_t[:, L - 1:L] - cum_t + logf_t
        i_off = 2 * N_M * d
        f_off = i_off + N_M
        mask = lower if d == 0 else upper

        for hh in range(N_M):
            idx = d * N_M + hh
            b_col = cum[:, f_off + hh:f_off + hh + 1]
            i_col = gates[:, i_off + hh:i_off + hh + 1]
            b_row = cum_t[f_off + hh:f_off + hh + 1, :]
            i_row = gates_t[i_off + hh:i_off + hh + 1, :]
            dmat = jnp.where(mask, b_col - b_row + i_row, -jnp.inf)
            m_prev = m_s[idx][:, :1]
            inter = b_col + m_prev
            m_t = jnp.maximum(inter, jnp.max(dmat, axis=-1, keepdims=True))
            scale = jnp.exp(inter - m_t)
            qb = qk_ref[:, hh * HD_M:(hh + 1) * HD_M]
            kb = qk_ref[:, MW + hh * HD_M:MW + (hh + 1) * HD_M]
            q = qb.astype(F32)
            k = kb.astype(F32)
            vb = v_ref[:, hh * HD_M:(hh + 1) * HD_M]
            w = _dot_nt(qb, kb) * jnp.exp(dmat - m_t)
            c_prev = c_s[idx]
            n_prev = n_s[idx]
            num = scale * _dot(qb, c_prev.astype(BF16)) + _dot(w.astype(BF16), vb)
            den = (scale * jnp.sum(q * n_prev, axis=-1, keepdims=True)
                   + jnp.sum(w, axis=-1, keepdims=True))
            hval = num / jnp.maximum(jnp.abs(den), jnp.exp(-m_t))
            out_ref[:, hh * HD_M:(hh + 1) * HD_M] = hval.astype(out_ref.dtype)

            b_end = b_col[L - 1:L, :] if d == 0 else b_col[0:1, :]
            g_col = b_end - b_col + i_col
            m_new = jnp.maximum(b_end + m_prev, jnp.max(g_col, axis=0, keepdims=True))
            decay = jnp.exp(b_end + m_prev - m_new)
            kw = k * jnp.exp(g_col - m_new)
            c_s[idx] = decay * c_prev + _dot_tn(kw.astype(BF16), vb)
            n_s[idx] = decay * n_prev + jnp.sum(kw, axis=0, keepdims=True)
            m_s[idx] = jnp.broadcast_to(m_new, (1, LANES))


def _mlstm(qk, pm, gates, gates_t, gb_row, gb_col, batch, seq):
    L = MLSTM_CHUNK
    nc = seq // L

    def chunk_f(b, c):
        return b * nc + c

    def chunk_b(b, c):
        return b * nc + (nc - 1 - c)

    def specs(chunk):
        return [
            pl.BlockSpec((L, 2 * MW), lambda b, c: (chunk(b, c), 0)),
            pl.BlockSpec((L, MW), lambda b, c: (chunk(b, c), 2)),
            pl.BlockSpec((L, N_GATES), lambda b, c: (chunk(b, c), 0)),
            pl.BlockSpec((N_GATES, L), lambda b, c: (0, chunk(b, c))),
        ]

    n = batch * seq
    return pl.pallas_call(
        _mlstm_kernel,
        grid=(batch, nc),
        in_specs=specs(chunk_f) + specs(chunk_b) + [
            _resident((1, N_GATES)),
            _resident((N_GATES, 1)),
        ],
        out_specs=[
            pl.BlockSpec((L, MW), lambda b, c: (chunk_f(b, c), 0)),
            pl.BlockSpec((L, MW), lambda b, c: (chunk_b(b, c), 0)),
        ],
        out_shape=[jax.ShapeDtypeStruct((n, MW), BF16)] * 2,
        scratch_shapes=[
            pltpu.VMEM((2 * N_M, HD_M, HD_M), F32),
            pltpu.VMEM((2 * N_M, 1, HD_M), F32),
            pltpu.VMEM((2 * N_M, 1, LANES), F32),
        ],
        compiler_params=pltpu.CompilerParams(
            dimension_semantics=("parallel", "arbitrary"), vmem_limit_bytes=VMEM_LIMIT),
        name="mlstm",
    )(qk, pm, gates, gates_t, qk, pm, gates, gates_t, gb_row, gb_col)


def _bias_kernel(rb_ref, out_ref, *, tile):
    hh = pl.program_id(0)
    dd = pl.program_id(1)
    kk = lax.broadcasted_iota(jnp.int32, (tile, tile), 0)
    qq = lax.broadcasted_iota(jnp.int32, (tile, tile), 1)
    rel = (dd - 1) * tile + kk - qq
    nb = REL_BUCKETS // 2
    max_exact = nb // 2
    n = jnp.abs(rel)
    nf = jnp.maximum(n, 1).astype(F32)
    large = max_exact + (jnp.log(nf / max_exact) / math.log(REL_MAX_DIST / max_exact)
                         * (nb - max_exact)).astype(jnp.int32)
    large = jnp.minimum(large, nb - 1)
    bucket = jnp.where(rel > 0, nb, 0) + jnp.where(n < max_exact, n, large)
    val = jnp.zeros((tile, tile), F32)
    for bkt in range(REL_BUCKETS):
        val = jnp.where(bucket == bkt, rb_ref[bkt, hh], val)
    out_ref[0, 0] = jnp.where(dd < 3, val * LOG2E, 0.0)


def _bias_tiles(rel_bias, tile):
    return pl.pallas_call(
        functools.partial(_bias_kernel, tile=tile),
        grid=(N_A, 4),
        in_specs=[pl.BlockSpec(memory_space=pltpu.SMEM)],
        out_specs=pl.BlockSpec((1, 1, tile, tile), lambda h, d: (h, d, 0, 0)),
        out_shape=jax.ShapeDtypeStruct((N_A, 4, tile, tile), F32),
        name="rel_bias_tiles",
    )(rel_bias)


def _far_bucket(tile):
    assert tile >= REL_MAX_DIST
    return REL_BUCKETS // 2 - 1


def _attn_kernel(rb_ref, li_ref, q_ref, k1_ref, k2_ref, vt_ref, qg_ref, kg_ref, bias_ref, lam_ref, sg_ref,
                 o_ref,
                 qt_s, m_s, acc_s, p0_s, p1_s, *, far):
    hh = pl.program_id(1)
    qi = pl.program_id(2)
    tq = q_ref.shape[0]
    tk = vt_ref.shape[2]
    nk = vt_ref.shape[0]
    vd = 2 * HD_A
    LEFT, RIGHT, NEAR = 0, 1, 2

    qt = q_ref[...].astype(F32).T
    rid = lax.broadcasted_iota(jnp.int32, qt.shape, 0)
    bases = (qt, jnp.concatenate([qt[HD_A:], qt[:HD_A]], axis=0))
    consts = (rb_ref[far, hh] * LOG2E, rb_ref[REL_BUCKETS // 2 + far, hh] * LOG2E, 0.0)

    def build_queries(shift):
        for i, base in enumerate(bases):
            body = jnp.where(rid < HD_A, base, 0.0).astype(BF16)
            for kind in range(len(consts)):
                qt_s[2 * kind + i] = body
        rid_o = lax.broadcasted_iota(jnp.int32, (HALO, tq), 0)
        for kind, cst in enumerate(consts):
            rest = jnp.full((HALO, tq), cst - shift, F32)
            extra = jnp.zeros((HALO, tq), F32)
            for r in range(BIAS_ROWS):
                part = rest.astype(BF16).astype(F32)
                extra = jnp.where(rid_o == r, part, extra)
                rest = rest - part
            for i in range(2):
                qt_s[2 * kind + i, HD_A:HD_A + HALO, :] = extra.astype(BF16)

    acc_s[...] = jnp.zeros_like(acc_s)

    def key_rows(j):
        return pl.ds(pl.multiple_of(j * tk, tk), tk)

    bucket_vals = [rb_ref[bkt, hh] for bkt in range(REL_BUCKETS)]
    bias_max = functools.reduce(jnp.maximum, bucket_vals) * LOG2E
    bias_min = functools.reduce(jnp.minimum, bucket_vals) * LOG2E
    reach = (HD_A ** 0.5 * LOG2E) * jnp.max(jnp.abs(qg_ref[...])) * jnp.max(jnp.abs(kg_ref[...]))
    shift = reach + bias_max
    span = 2.0 * reach + (bias_max - bias_min)
    bounded = span <= LOGIT_SPAN_LIMIT

    @pl.when(bounded)
    def _():
        build_queries(shift)

        def numerators(j, kind, p_buf, bias=None):
            for i, k_ref in enumerate((k1_ref, k2_ref)):
                s = _dot(k_ref[key_rows(j), :], qt_s[2 * kind + i])
                if bias is not None:
                    s = s + bias
                p_buf[i] = jnp.exp2(s).astype(BF16)

        def accumulate(j, p_buf):
            vt = vt_ref[j]
            for i in range(2):
                acc_s[i] += _dot(vt, p_buf[i])

        w0 = jnp.clip(qi - 1, 0, nk - ATTN_WINDOW)
        far_tile = lambda t: jnp.where(t < w0, t, t + ATTN_WINDOW)
        far_kind = lambda t: jnp.where(t < w0, LEFT, RIGHT)
        p_bufs = (p0_s, p1_s)
        for r in range(ATTN_WINDOW):
            j = w0 + r
            dlt = j - qi
            near = jnp.abs(dlt) <= 1
            kind = jnp.where(near, NEAR, jnp.where(dlt < 0, LEFT, RIGHT))
            numerators(j, kind, p_bufs[r % 2], bias_ref[0, jnp.where(near, dlt + 1, 3)])
            if r > 0:
                accumulate(j - 1, p_bufs[(r - 1) % 2])
        last_window = w0 + ATTN_WINDOW - 1

        def pair(u, carry):
            t0 = 2 * u
            numerators(far_tile(t0), far_kind(t0), p0_s)
            accumulate(jnp.where(u == 0, last_window, far_tile(t0 - 1)), p1_s)
            numerators(far_tile(t0 + 1), far_kind(t0 + 1), p1_s)
            accumulate(far_tile(t0), p0_s)
            return carry

        n_far = nk - ATTN_WINDOW
        lax.fori_loop(0, n_far // 2, pair, 0)
        accumulate(far_tile(n_far - 1) if n_far else last_window, p1_s)

    @pl.when(jnp.logical_not(bounded))
    def _():
        _online_softmax_tiles(qi, nk, build_queries, key_rows, k1_ref, k2_ref, vt_ref, bias_ref,
                              qt_s, m_s, acc_s)

    lv = lam_ref[...]
    lam = (jnp.exp(jnp.sum(lv[0:1] * lv[1:2], axis=-1, keepdims=True))
           - jnp.exp(jnp.sum(lv[2:3] * lv[3:4], axis=-1, keepdims=True)) + li_ref[0, 0])
    a1 = acc_s[0]
    a2 = acc_s[1]
    o = a1[:vd] / a1[vd:vd + 1] - lam * (a2[:vd] / a2[vd:vd + 1])
    ms = jnp.mean(o * o, axis=0, keepdims=True)
    o = o * lax.rsqrt(ms + EPS) * sg_ref[0] * li_ref[0, 1]
    o_ref[...] = o.T.astype(o_ref.dtype)


def _online_softmax_tiles(qi, nk, build_queries, key_rows, k1_ref, k2_ref, vt_ref, bias_ref,
                          qt_s, m_s, acc_s):
    LEFT, RIGHT, NEAR = 0, 1, 2
    build_queries(0.0)
    m_s[...] = jnp.full_like(m_s, -jnp.inf)

    def one_tile(j, carry):
        dlt = j - qi
        near = jnp.abs(dlt) <= 1
        kind = jnp.where(near, NEAR, jnp.where(dlt < 0, LEFT, RIGHT))
        bias = bias_ref[0, jnp.where(near, dlt + 1, 3)]
        vt = vt_ref[j]
        for i, k_ref in enumerate((k1_ref, k2_ref)):
            s = _dot(k_ref[key_rows(j), :], qt_s[2 * kind + i]) + bias
            m_old = m_s[i]
            m_new = jnp.maximum(m_old, jnp.max(s, axis=0, keepdims=True))
            p = jnp.exp2(s - m_new).astype(BF16)
            acc_s[i] = jnp.exp2(m_old - m_new) * acc_s[i] + _dot(vt, p)
            m_s[i] = m_new
        return carry

    lax.fori_loop(0, nk, one_tile, 0)


def _attention(qn, k1, k2, vt, qg, kg, bias_t, rel_bias, lam_init, lambdas, sub_g_col, batch, seq, tile):
    nt = seq // tile
    vd = 2 * HD_A
    far = _far_bucket(tile)
    smem = pl.BlockSpec(memory_space=pltpu.SMEM)
    return pl.pallas_call(
        functools.partial(_attn_kernel, far=far),
        grid=(batch, N_A, nt),
        in_specs=[
            smem,
            smem,
            pl.BlockSpec((tile, vd), lambda b, h, i: (b * nt + i, h)),
            pl.BlockSpec((seq, vd), lambda b, h, i: (b, h)),
            pl.BlockSpec((seq, vd), lambda b, h, i: (b, h)),
            pl.BlockSpec((nt, VT_ROWS, tile), lambda b, h, i: (b, h, 0)),
            pl.BlockSpec((1, 2 * LANES), lambda b, h, i: (0, 0)),
            pl.BlockSpec((1, 2 * LANES), lambda b, h, i: (0, 0)),
            pl.BlockSpec((1, 4, tile, tile), lambda b, h, i: (h, 0, 0, 0)),
            pl.BlockSpec((4, HD_A), lambda b, h, i: (0, 0)),
            pl.BlockSpec((1, vd, 1), lambda b, h, i: (h, 0, 0)),
        ],
        out_specs=pl.BlockSpec((tile, vd), lambda b, h, i: (b * nt + i, h)),
        out_shape=jax.ShapeDtypeStruct((batch * seq, AW), BF16),
        scratch_shapes=[
            pltpu.VMEM((6, vd, tile), BF16),
            pltpu.VMEM((2, 1, tile), F32),
            pltpu.VMEM((2, VT_ROWS, tile), F32),
            pltpu.VMEM((2, tile, tile), BF16),
            pltpu.VMEM((2, tile, tile), BF16),
        ],
        compiler_params=pltpu.CompilerParams(
            dimension_semantics=("parallel", "parallel", "arbitrary"),
            vmem_limit_bytes=VMEM_LIMIT),
        name="diff_attention",
    )(rel_bias, lam_init, qn, k1, k2, vt, qg, kg, bias_t, lambdas, sub_g_col)


def _outffn_kernel(x_ref, hf_ref, hb_ref, og_ref, mb_ref, ng_ref, wo_ref, g2_ref, w1_ref, w2_ref,
                   out_ref, *, ff_chunk):
    hs = hf_ref[...].astype(F32) + hb_ref[...].astype(F32)
    x1 = x_ref[...] + _dot(mb_ref[...], wo_ref[MW:, :])
    for hh in range(N_M):
        sl = slice(hh * HD_M, (hh + 1) * HD_M)
        t = hs[:, sl]
        ms = jnp.mean(t * t, axis=-1, keepdims=True)
        mix = t * lax.rsqrt(ms + EPS) * ng_ref[:, sl] * _sigmoid(og_ref[:, sl].astype(F32))
        x1 = x1 + _dot(mix.astype(BF16), wo_ref[sl, :])
    ms = jnp.mean(x1 * x1, axis=-1, keepdims=True)
    h2 = (x1 * lax.rsqrt(ms + EPS) * g2_ref[...]).astype(BF16)
    out_ref[...] = x1
    for j in range(w1_ref.shape[1] // ff_chunk):
        sl = slice(j * ff_chunk, (j + 1) * ff_chunk)
        u = jnp.maximum(_dot(h2, w1_ref[:, sl]), 0.0)
        out_ref[...] += _dot((u * u).astype(BF16), w2_ref[sl, :])


def _outffn(x, hf, hb, pm, mixb, ng, wo, g2, w1, w2, tm):
    n, d = x.shape
    dff = w1.shape[1]
    row = lambda i: (i, 0)
    return pl.pallas_call(
        functools.partial(_outffn_kernel, ff_chunk=1024),
        grid=(n // tm,),
        in_specs=[
            pl.BlockSpec((tm, d), row),
            pl.BlockSpec((tm, MW), row),
            pl.BlockSpec((tm, MW), row),
            pl.BlockSpec((tm, MW), lambda i: (i, 3)),
            pl.BlockSpec((tm, AW), row),
            _resident((1, MW)),
            _resident((MW + AW, d)),
            _resident((1, d)),
            _resident((d, dff)),
            _resident((dff, d)),
        ],
        out_specs=pl.BlockSpec((tm, d), row),
        out_shape=jax.ShapeDtypeStruct((n, d), F32),
        compiler_params=pltpu.CompilerParams(
            dimension_semantics=("parallel",), vmem_limit_bytes=VMEM_LIMIT),
        name="outproj_ffn",
    )(x, hf, hb, pm, mixb, ng, wo, g2, w1, w2)


def kernel(x, norm1_g, w_in, conv_w, conv_b, gate_b, mlstm_norm_g, q_norm_g, k_norm_g, lambdas,
           diff_norm_g, rel_bias, w_out, norm2_g, w_ff1, w_ff2):
    batch, seq, d = x.shape
    depth = w_in.shape[0]
    n = batch * seq
    tm = min(512, n)
    tile = min(512, seq)
    assert seq % MLSTM_CHUNK == 0 and seq % tile == 0 and tm == tile
    assert seq // tile >= ATTN_WINDOW and (seq // tile - ATTN_WINDOW) % 2 == 0

    w_main = jnp.concatenate([w_in[:, :, :C_G], w_in[:, :, C_G + N_GATES:]], axis=-1).astype(BF16)
    w_gate = jnp.pad(w_in[:, :, C_G:C_G + N_GATES], ((0, 0), (0, 0), (0, LANES - N_GATES))).astype(BF16)
    w_out_b = w_out.astype(BF16)
    w_ff1_b = w_ff1.astype(BF16)
    w_ff2_b = w_ff2.astype(BF16)
    lane = jnp.arange(2 * LANES)
    bd = jnp.where((lane[:, None] // HD_A) == (lane[None, :] // HD_A), 1.0 / HD_A, 0.0).astype(BF16)

    bias_t = _bias_tiles(rel_bias.astype(F32), tile)

    xf = x.reshape(n, d)
    for l in range(depth):
        lam_init = 0.8 - 0.6 * math.exp(-0.3 * l)
        qg = jnp.tile(q_norm_g[l], 2 * LANES // HD_A)[None, :]
        kg = jnp.tile(k_norm_g[l], 2 * LANES // HD_A)[None, :]
        pm, gates, qn, k1, k2, vt = _inproj(
            xf, norm1_g[l][None, :], w_main[l], w_gate[l], qg, kg, bd, tm)
        gb = gate_b[l].reshape(1, N_GATES).astype(F32)
        qk = _conv(pm, conv_w[l], conv_b[l][None, :], seq, tile)
        hf, hb = _mlstm(qk, pm, gates, gates.T, gb, gb.T, batch, seq)
        mixb = _attention(qn, k1, k2, vt, qg, kg, bias_t, rel_bias.astype(F32),
                          jnp.array([[lam_init, 1.0 - lam_init]], F32), lambdas[l],
                          diff_norm_g[l].reshape(N_A, 2 * HD_A, 1), batch, seq, tile)
        xf = _outffn(xf, hf, hb, pm, mixb, mlstm_norm_g[l][None, :], w_out_b[l],
                     norm2_g[l][None, :], w_ff1_b[l], w_ff2_b[l], tm)
    return xf.reshape(batch, seq, d)
```

```python
import functools
import math

import jax
import jax.numpy as jnp
from jax import lax
from jax.experimental import pallas as pl
from jax.experimental.pallas import tpu as pltpu

F32 = jnp.float32
BF16 = jnp.bfloat16

N_M = 4
HD_M = 128
MW = N_M * HD_M
MLSTM_CHUNK = 128
CONV_W = 5
N_A = 4
HD_A = 64
AW = N_A * 2 * HD_A
REL_BUCKETS = 32
REL_MAX_DIST = 128
EPS = 1e-6
N_GATES = 4 * N_M
C_G = 4 * MW
LANES = 128
HALO = 16
VMEM_LIMIT = 56 * 1024 * 1024
LOG2E = math.log2(math.e)
VT_ROWS = 2 * HD_A + HALO
BIAS_ROWS = 3
LOGIT_SPAN_LIMIT = 96.0
ATTN_WINDOW = 4


def _dot(a, b):
    return jnp.dot(a, b, preferred_element_type=F32)


def _dot_nt(a, b):
    return lax.dot_general(a, b, (((1,), (1,)), ((), ())), preferred_element_type=F32)


def _dot_tn(a, b):
    return lax.dot_general(a, b, (((0,), (0,)), ((), ())), preferred_element_type=F32)


def _dot_exact(a, b):
    return jnp.dot(a, b, preferred_element_type=F32, precision=lax.Precision.HIGHEST)


def _sigmoid(x):
    return 1.0 / (1.0 + jnp.exp(-x))


def _log_sigmoid(x):
    return jnp.minimum(x, 0.0) - jnp.log1p(jnp.exp(-jnp.abs(x)))


def _resident(shape):
    nd = len(shape)
    return pl.BlockSpec(shape, lambda *_: (0,) * nd, pipeline_mode=pl.Buffered(1))


def _inproj_kernel(xp_ref, x_ref, xn_ref, g1_ref, w_ref, wg_ref, cw_ref, cb_ref, qg_ref, kg_ref, bd_ref,
                   qtm_ref, km_ref, vtm_ref, om_ref, gates_ref, qn_ref, k1_ref, k2_ref, vt_ref,
                   ext_s, *, tiles_per_seq):
    rows = x_ref.shape[0]
    pos = pl.program_id(0) % tiles_per_seq

    def normed(xv):
        ms = jnp.mean(xv * xv, axis=-1, keepdims=True)
        return (xv * lax.rsqrt(ms + EPS) * g1_ref[...]).astype(BF16)

    x = x_ref[...]
    h = normed(x)

    def proj(c0, width):
        return _dot(h, w_ref[:, c0:c0 + width])

    h_ext = jnp.concatenate([normed(xp_ref[...]), h, normed(xn_ref[...])], axis=0)
    rid = lax.broadcasted_iota(jnp.int32, (rows + 2 * HALO, 1), 0)
    inside = jnp.logical_and(jnp.logical_or(rid >= HALO, pos > 0),
                             jnp.logical_or(rid < HALO + rows, pos < tiles_per_seq - 1))
    ext_s[...] = jnp.where(inside, _dot(h_ext, w_ref[:, 0:2 * MW]), 0.0)
    conv = cb_ref[...]
    for j in range(CONV_W):
        off = HALO - CONV_W // 2 + j
        conv = conv + ext_s[off:off + rows, :] * cw_ref[j:j + 1, :]
    qk = conv * _sigmoid(conv)
    qtm_ref[...] = qk[:, :MW].T.astype(BF16)
    km_ref[...] = (qk[:, MW:] * (HD_M ** -0.5)).astype(BF16)

    ones_rows = jnp.ones((HALO, rows), BF16)
    pair = 2 * LANES

    def values_t(c0, out_ref):
        for hp in range(N_M // 2):
            v2 = proj(c0 + hp * pair, pair)
            for sub in range(2):
                hh = 2 * hp + sub
                out_ref[0, hh * VT_ROWS:hh * VT_ROWS + HD_M, :] = (
                    v2[:, sub * LANES:(sub + 1) * LANES].T.astype(BF16))
                out_ref[0, hh * VT_ROWS + HD_M:(hh + 1) * VT_ROWS, :] = ones_rows

    values_t(2 * MW, vtm_ref)
    om_ref[...] = proj(3 * MW, MW).astype(BF16)
    gates_ref[...] = _dot(h, wg_ref[...])[:, :N_GATES]

    bd = bd_ref[...]

    def qk_norm(c0, g_ref):
        t = proj(c0, pair)
        msq = _dot((t * t).astype(BF16), bd)
        return t * lax.rsqrt(msq + EPS) * g_ref[...]

    lane = lax.broadcasted_iota(jnp.int32, (x.shape[0], pair), 1) % LANES
    ones_cols = jnp.where(lane < HD_A + BIAS_ROWS, 1.0, 0.0)
    for hp in range(N_A // 2):
        sl = slice(hp * pair, (hp + 1) * pair)
        qn_ref[:, sl] = (qk_norm(4 * MW + hp * pair, qg_ref) * (HD_A ** -0.5 * LOG2E)).astype(BF16)
        kn = qk_norm(4 * MW + AW + hp * pair, kg_ref)
        k1_ref[:, sl] = jnp.where(lane < HD_A, kn, ones_cols).astype(BF16)
        k2_ref[:, sl] = jnp.where(lane < HD_A, pltpu.roll(kn, pair - HD_A, 1), ones_cols).astype(BF16)
    values_t(4 * MW + 2 * AW, vt_ref)


def _inproj(x, g1, w_main, w_gate, conv_w, conv_b, qg, kg, bd, seq, tm):
    n, d = x.shape
    wcols = w_main.shape[1]
    hpt = tm // HALO
    nhalo = n // HALO
    row = lambda i: (i, 0)
    vt_spec = pl.BlockSpec((1, N_A * VT_ROWS, tm), lambda i: (i, 0, 0))
    vt_shape = jax.ShapeDtypeStruct((n // tm, N_A * VT_ROWS, tm), BF16)
    return pl.pallas_call(
        functools.partial(_inproj_kernel, tiles_per_seq=seq // tm),
        grid=(n // tm,),
        in_specs=[
            pl.BlockSpec((HALO, d), lambda i: (jnp.maximum(i * hpt - 1, 0), 0)),
            pl.BlockSpec((tm, d), row),
            pl.BlockSpec((HALO, d), lambda i: (jnp.minimum((i + 1) * hpt, nhalo - 1), 0)),
            _resident((1, d)),
            _resident((d, wcols)),
            _resident((d, LANES)),
            _resident((CONV_W, 2 * MW)),
            _resident((1, 2 * MW)),
            _resident((1, 2 * LANES)),
            _resident((1, 2 * LANES)),
            _resident((2 * LANES, 2 * LANES)),
        ],
        out_specs=[
            pl.BlockSpec((MW, tm), lambda i: (0, i)),
            pl.BlockSpec((tm, MW), row),
            vt_spec,
            pl.BlockSpec((tm, MW), row),
            pl.BlockSpec((tm, N_GATES), row),
            pl.BlockSpec((tm, AW), row),
            pl.BlockSpec((tm, AW), row),
            pl.BlockSpec((tm, AW), row),
            vt_spec,
        ],
        out_shape=[
            jax.ShapeDtypeStruct((MW, n), BF16),
            jax.ShapeDtypeStruct((n, MW), BF16),
            vt_shape,
            jax.ShapeDtypeStruct((n, MW), BF16),
            jax.ShapeDtypeStruct((n, N_GATES), F32),
            jax.ShapeDtypeStruct((n, AW), BF16),
            jax.ShapeDtypeStruct((n, AW), BF16),
            jax.ShapeDtypeStruct((n, AW), BF16),
            vt_shape,
        ],
        scratch_shapes=[pltpu.VMEM((tm + 2 * HALO, 2 * MW), F32)],
        compiler_params=pltpu.CompilerParams(
            dimension_semantics=("parallel",), vmem_limit_bytes=VMEM_LIMIT),
        name="inproj",
    )(x, x, x, g1, w_main, w_gate, conv_w, conv_b, qg, kg, bd)


def _mlstm_kernel(qt_f, k_f, vt_f, g_f, gt_f,
                  qt_b, k_b, vt_b, g_b, gt_b,
                  gbr_ref, gbc_ref,
                  hf_ref, hb_ref,
                  c_s, m_s):
    L = MLSTM_CHUNK
    c = pl.program_id(1)

    @pl.when(c == 0)
    def _():
        c_s[...] = jnp.zeros_like(c_s)
        m_s[...] = jnp.zeros_like(m_s)

    row = lax.broadcasted_iota(jnp.int32, (L, L), 0)
    col = lax.broadcasted_iota(jnp.int32, (L, L), 1)
    lower = col <= row
    upper = row <= col
    lower_f = lower.astype(F32)
    upper_f = upper.astype(F32)

    chains = []
    dirs = ((qt_f, k_f, vt_f, g_f, gt_f, hf_ref), (qt_b, k_b, vt_b, g_b, gt_b, hb_ref))
    for d, (qt_ref, k_ref, vt_ref, g_ref, gt_ref, out_ref) in enumerate(dirs):
        gates = g_ref[...] + gbr_ref[...]
        gates_t = gt_ref[...] + gbc_ref[...]
        logf = _log_sigmoid(gates)
        logf_t = _log_sigmoid(gates_t)
        cum = _dot_exact(lower_f, logf)
        cum_t = _dot_exact(logf_t, upper_f)
        if d == 1:
            cum = cum[L - 1:L, :] - cum + logf
            cum_t = cum_t[:, L - 1:L] - cum_t + logf_t
        i_off = 2 * N_M * d
        f_off = i_off + N_M
        src = cum[:, f_off:f_off + N_M] - gates[:, i_off:i_off + N_M]
        for hh in range(N_M):
            chains.append(dict(
                idx=d * N_M + hh, out_ref=out_ref, cols=slice(hh * HD_M, (hh + 1) * HD_M),
                mask=upper if d == 0 else lower,
                src_col=src[:, hh:hh + 1],
                b_row=cum_t[f_off + hh:f_off + hh + 1, :],
                i_row=gates_t[i_off + hh:i_off + hh + 1, :],
                end=slice(L - 1, L) if d == 0 else slice(0, 1),
                qt=qt_ref[hh * HD_M:(hh + 1) * HD_M, :],
                kb=k_ref[:, hh * HD_M:(hh + 1) * HD_M],
                vt=vt_ref[0, hh * VT_ROWS:(hh + 1) * VT_ROWS, :],
            ))

    for ch in chains:
        ch["s"] = _dot(ch["kb"], ch["qt"])
        ch["c_prev"] = c_s[ch["idx"]]
        ch["inter_num"] = _dot(ch["c_prev"].astype(BF16), ch["qt"])
    for ch in chains:
        dmat = jnp.where(ch["mask"], ch["b_row"] - ch["src_col"], -jnp.inf)
        ch["m_prev"] = m_s[ch["idx"]][:, :1]
        inter = ch["b_row"] + ch["m_prev"]
        ch["m_t"] = jnp.maximum(inter, jnp.max(dmat, axis=0, keepdims=True))
        ch["scale"] = jnp.exp(inter - ch["m_t"])
        ch["w"] = (ch["s"] * jnp.exp(dmat - ch["m_t"])).astype(BF16)
    for ch in chains:
        num = ch["scale"] * ch["inter_num"] + _dot(ch["vt"], ch["w"])
        den = num[HD_M:HD_M + 1]
        ht = num[:HD_M] / jnp.maximum(jnp.abs(den), jnp.exp(-ch["m_t"]))
        ch["out_ref"][:, ch["cols"]] = ht.T.astype(ch["out_ref"].dtype)
    for ch in chains:
        b_end = ch["b_row"][:, ch["end"]]
        g_row = b_end - ch["b_row"] + ch["i_row"]
        m_new = jnp.maximum(b_end + ch["m_prev"], jnp.max(g_row, axis=-1, keepdims=True))
        decay = jnp.exp(b_end + ch["m_prev"] - m_new)
        vt_w = (ch["vt"].astype(F32) * jnp.exp(g_row - m_new)).astype(BF16)
        c_s[ch["idx"]] = decay * ch["c_prev"] + _dot(vt_w, ch["kb"])
        m_s[ch["idx"]] = jnp.broadcast_to(m_new, (1, LANES))


def _mlstm(qt, k, vt, gates, gates_t, gb_row, gb_col, batch, seq, tm):
    L = MLSTM_CHUNK
    nc = seq // L
    cpt = tm // L

    def chunk_f(b, c):
        return b * nc + c

    def chunk_b(b, c):
        return b * nc + (nc - 1 - c)

    def specs(chunk):
        return [
            pl.BlockSpec((MW, L), lambda b, c: (0, chunk(b, c))),
            pl.BlockSpec((L, MW), lambda b, c: (chunk(b, c), 0)),
            pl.BlockSpec((1, N_M * VT_ROWS, L), lambda b, c: (chunk(b, c) // cpt, 0, chunk(b, c) % cpt)),
            pl.BlockSpec((L, N_GATES), lambda b, c: (chunk(b, c), 0)),
            pl.BlockSpec((N_GATES, L), lambda b, c: (0, chunk(b, c))),
        ]

    n = batch * seq
    return pl.pallas_call(
        _mlstm_kernel,
        grid=(batch, nc),
        in_specs=specs(chunk_f) + specs(chunk_b) + [
            _resident((1, N_GATES)),
            _resident((N_GATES, 1)),
        ],
        out_specs=[
            pl.BlockSpec((L, MW), lambda b, c: (chunk_f(b, c), 0)),
            pl.BlockSpec((L, MW), lambda b, c: (chunk_b(b, c), 0)),
        ],
        out_shape=[jax.ShapeDtypeStruct((n, MW), BF16)] * 2,
        scratch_shapes=[
            pltpu.VMEM((2 * N_M, VT_ROWS, HD_M), F32),
            pltpu.VMEM((2 * N_M, 1, LANES), F32),
        ],
        compiler_params=pltpu.CompilerParams(
            dimension_semantics=("parallel", "arbitrary"), vmem_limit_bytes=VMEM_LIMIT),
        name="mlstm",
    )(qt, k, vt, gates, gates_t, qt, k, vt, gates, gates_t, gb_row, gb_col)


def _bias_kernel(rb_ref, out_ref, *, tile):
    hh = pl.program_id(0)
    dd = pl.program_id(1)
    kk = lax.broadcasted_iota(jnp.int32, (tile, tile), 0)
    qq = lax.broadcasted_iota(jnp.int32, (tile, tile), 1)
    rel = (dd - 1) * tile + kk - qq
    nb = REL_BUCKETS // 2
    max_exact = nb // 2
    n = jnp.abs(rel)
    nf = jnp.maximum(n, 1).astype(F32)
    large = max_exact + (jnp.log(nf / max_exact) / math.log(REL_MAX_DIST / max_exact)
                         * (nb - max_exact)).astype(jnp.int32)
    large = jnp.minimum(large, nb - 1)
    bucket = jnp.where(rel > 0, nb, 0) + jnp.where(n < max_exact, n, large)
    val = jnp.zeros((tile, tile), F32)
    for bkt in range(REL_BUCKETS):
        val = jnp.where(bucket == bkt, rb_ref[bkt, hh], val)
    out_ref[0, 0] = jnp.where(dd < 3, val * LOG2E, 0.0)


def _bias_tiles(rel_bias, tile):
    return pl.pallas_call(
        functools.partial(_bias_kernel, tile=tile),
        grid=(N_A, 4),
        in_specs=[pl.BlockSpec(memory_space=pltpu.SMEM)],
        out_specs=pl.BlockSpec((1, 1, tile, tile), lambda h, d: (h, d, 0, 0)),
        out_shape=jax.ShapeDtypeStruct((N_A, 4, tile, tile), F32),
        name="rel_bias_tiles",
    )(rel_bias)


def _far_bucket(tile):
    assert tile >= REL_MAX_DIST
    return REL_BUCKETS // 2 - 1


def _attn_kernel(rb_ref, li_ref, q_ref, k1_ref, k2_ref, vt_ref, qg_ref, kg_ref, bias_ref, lam_ref, sg_ref,
                 o_ref,
                 qt_s, m_s, acc_s, p0_s, p1_s, *, far):
    hh = pl.program_id(1)
    qi = pl.program_id(2)
    tq = q_ref.shape[0]
    tk = vt_ref.shape[2]
    nk = vt_ref.shape[0]
    vd = 2 * HD_A
    LEFT, RIGHT, NEAR = 0, 1, 2

    qt = q_ref[...].astype(F32).T
    rid = lax.broadcasted_iota(jnp.int32, qt.shape, 0)
    bases = (qt, jnp.concatenate([qt[HD_A:], qt[:HD_A]], axis=0))
    consts = (rb_ref[far, hh] * LOG2E, rb_ref[REL_BUCKETS // 2 + far, hh] * LOG2E, 0.0)

    def build_queries(shift):
        for i, base in enumerate(bases):
            body = jnp.where(rid < HD_A, base, 0.0).astype(BF16)
            for kind in range(len(consts)):
                qt_s[2 * kind + i] = body
        rid_o = lax.broadcasted_iota(jnp.int32, (HALO, tq), 0)
        for kind, cst in enumerate(consts):
            rest = jnp.full((HALO, tq), cst - shift, F32)
            extra = jnp.zeros((HALO, tq), F32)
            for r in range(BIAS_ROWS):
                part = rest.astype(BF16).astype(F32)
                extra = jnp.where(rid_o == r, part, extra)
                rest = rest - part
            for i in range(2):
                qt_s[2 * kind + i, HD_A:HD_A + HALO, :] = extra.astype(BF16)

    acc_s[...] = jnp.zeros_like(acc_s)

    def key_rows(j):
        return pl.ds(pl.multiple_of(j * tk, tk), tk)

    bucket_vals = [rb_ref[bkt, hh] for bkt in range(REL_BUCKETS)]
    bias_max = functools.reduce(jnp.maximum, bucket_vals) * LOG2E
    bias_min = functools.reduce(jnp.minimum, bucket_vals) * LOG2E
    reach = (HD_A ** 0.5 * LOG2E) * jnp.max(jnp.abs(qg_ref[...])) * jnp.max(jnp.abs(kg_ref[...]))
    shift = reach + bias_max
    span = 2.0 * reach + (bias_max - bias_min)
    bounded = span <= LOGIT_SPAN_LIMIT

    @pl.when(bounded)
    def _():
        build_queries(shift)

        def numerators(j, kind, p_buf, bias=None):
            for i, k_ref in enumerate((k1_ref, k2_ref)):
                s = _dot(k_ref[key_rows(j), :], qt_s[2 * kind + i])
                if bias is not None:
                    s = s + bias
                p_buf[i] = jnp.exp2(s).astype(BF16)

        def accumulate(j, p_buf):
            vt = vt_ref[j]
            for i in range(2):
                acc_s[i] += _dot(vt, p_buf[i])

        w0 = jnp.clip(qi - 1, 0, nk - ATTN_WINDOW)
        far_tile = lambda t: jnp.where(t < w0, t, t + ATTN_WINDOW)
        far_kind = lambda t: jnp.where(t < w0, LEFT, RIGHT)
        p_bufs = (p0_s, p1_s)
        for r in range(ATTN_WINDOW):
            j = w0 + r
            dlt = j - qi
            near = jnp.abs(dlt) <= 1
            kind = jnp.where(near, NEAR, jnp.where(dlt < 0, LEFT, RIGHT))
            numerators(j, kind, p_bufs[r % 2], bias_ref[0, jnp.where(near, dlt + 1, 3)])
            if r > 0:
                accumulate(j - 1, p_bufs[(r - 1) % 2])
        last_window = w0 + ATTN_WINDOW - 1

        def pair(u, carry):
            t0 = 2 * u
            numerators(far_tile(t0), far_kind(t0), p0_s)
            accumulate(jnp.where(u == 0, last_window, far_tile(t0 - 1)), p1_s)
            numerators(far_tile(t0 + 1), far_kind(t0 + 1), p1_s)
            accumulate(far_tile(t0), p0_s)
            return carry

        n_far = nk - ATTN_WINDOW
        lax.fori_loop(0, n_far // 2, pair, 0)
        accumulate(far_tile(n_far - 1) if n_far else last_window, p1_s)

    @pl.when(jnp.logical_not(bounded))
    def _():
        _online_softmax_tiles(qi, nk, build_queries, key_rows, k1_ref, k2_ref, vt_ref, bias_ref,
                              qt_s, m_s, acc_s)

    lv = lam_ref[...]
    lam = (jnp.exp(jnp.sum(lv[0:1] * lv[1:2], axis=-1, keepdims=True))
           - jnp.exp(jnp.sum(lv[2:3] * lv[3:4], axis=-1, keepdims=True)) + li_ref[0, 0])
    a1 = acc_s[0]
    a2 = acc_s[1]
    o = a1[:vd] / a1[vd:vd + 1] - lam * (a2[:vd] / a2[vd:vd + 1])
    ms = jnp.mean(o * o, axis=0, keepdims=True)
    o = o * lax.rsqrt(ms + EPS) * sg_ref[0] * li_ref[0, 1]
    o_ref[...] = o.T.astype(o_ref.dtype)


def _online_softmax_tiles(qi, nk, build_queries, key_rows, k1_ref, k2_ref, vt_ref, bias_ref,
                          qt_s, m_s, acc_s):
    LEFT, RIGHT, NEAR = 0, 1, 2
    build_queries(0.0)
    m_s[...] = jnp.full_like(m_s, -jnp.inf)

    def one_tile(j, carry):
        dlt = j - qi
        near = jnp.abs(dlt) <= 1
        kind = jnp.where(near, NEAR, jnp.where(dlt < 0, LEFT, RIGHT))
        bias = bias_ref[0, jnp.where(near, dlt + 1, 3)]
        vt = vt_ref[j]
        for i, k_ref in enumerate((k1_ref, k2_ref)):
            s = _dot(k_ref[key_rows(j), :], qt_s[2 * kind + i]) + bias
            m_old = m_s[i]
            m_new = jnp.maximum(m_old, jnp.max(s, axis=0, keepdims=True))
            p = jnp.exp2(s - m_new).astype(BF16)
            acc_s[i] = jnp.exp2(m_old - m_new) * acc_s[i] + _dot(vt, p)
            m_s[i] = m_new
        return carry

    lax.fori_loop(0, nk, one_tile, 0)


def _attention(qn, k1, k2, vt, qg, kg, bias_t, rel_bias, lam_init, lambdas, sub_g_col, batch, seq, tile):
    nt = seq // tile
    vd = 2 * HD_A
    far = _far_bucket(tile)
    smem = pl.BlockSpec(memory_space=pltpu.SMEM)
    return pl.pallas_call(
        functools.partial(_attn_kernel, far=far),
        grid=(batch, N_A, nt),
        in_specs=[
            smem,
            smem,
            pl.BlockSpec((tile, vd), lambda b, h, i: (b * nt + i, h)),
            pl.BlockSpec((seq, vd), lambda b, h, i: (b, h)),
            pl.BlockSpec((seq, vd), lambda b, h, i: (b, h)),
            pl.BlockSpec((nt, VT_ROWS, tile), lambda b, h, i: (b, h, 0)),
            pl.BlockSpec((1, 2 * LANES), lambda b, h, i: (0, 0)),
            pl.BlockSpec((1, 2 * LANES), lambda b, h, i: (0, 0)),
            pl.BlockSpec((1, 4, tile, tile), lambda b, h, i: (h, 0, 0, 0)),
            pl.BlockSpec((4, HD_A), lambda b, h, i: (0, 0)),
            pl.BlockSpec((1, vd, 1), lambda b, h, i: (h, 0, 0)),
        ],
        out_specs=pl.BlockSpec((tile, vd), lambda b, h, i: (b * nt + i, h)),
        out_shape=jax.ShapeDtypeStruct((batch * seq, AW), BF16),
        scratch_shapes=[
            pltpu.VMEM((6, vd, tile), BF16),
            pltpu.VMEM((2, 1, tile), F32),
            pltpu.VMEM((2, VT_ROWS, tile), F32),
            pltpu.VMEM((2, tile, tile), BF16),
            pltpu.VMEM((2, tile, tile), BF16),
        ],
        compiler_params=pltpu.CompilerParams(
            dimension_semantics=("parallel", "parallel", "arbitrary"),
            vmem_limit_bytes=VMEM_LIMIT),
        name="diff_attention",
    )(rel_bias, lam_init, qn, k1, k2, vt, qg, kg, bias_t, lambdas, sub_g_col)


def _outffn_kernel(x_ref, hf_ref, hb_ref, og_ref, mb_ref, ng_ref, wo_ref, g2_ref, w1_ref, w2_ref,
                   out_ref, *, ff_chunk):
    hs = hf_ref[...].astype(F32) + hb_ref[...].astype(F32)
    x1 = x_ref[...] + _dot(mb_ref[...], wo_ref[MW:, :])
    for hh in range(N_M):
        sl = slice(hh * HD_M, (hh + 1) * HD_M)
        t = hs[:, sl]
        ms = jnp.mean(t * t, axis=-1, keepdims=True)
        mix = t * lax.rsqrt(ms + EPS) * ng_ref[:, sl] * _sigmoid(og_ref[:, sl].astype(F32))
        x1 = x1 + _dot(mix.astype(BF16), wo_ref[sl, :])
    ms = jnp.mean(x1 * x1, axis=-1, keepdims=True)
    h2 = (x1 * lax.rsqrt(ms + EPS) * g2_ref[...]).astype(BF16)
    out_ref[...] = x1
    for j in range(w1_ref.shape[1] // ff_chunk):
        sl = slice(j * ff_chunk, (j + 1) * ff_chunk)
        u = jnp.maximum(_dot(h2, w1_ref[:, sl]), 0.0)
        out_ref[...] += _dot((u * u).astype(BF16), w2_ref[sl, :])


def _outffn(x, hf, hb, om, mixb, ng, wo, g2, w1, w2, tm):
    n, d = x.shape
    dff = w1.shape[1]
    row = lambda i: (i, 0)
    return pl.pallas_call(
        functools.partial(_outffn_kernel, ff_chunk=1024),
        grid=(n // tm,),
        in_specs=[
            pl.BlockSpec((tm, d), row),
            pl.BlockSpec((tm, MW), row),
            pl.BlockSpec((tm, MW), row),
            pl.BlockSpec((tm, MW), row),
            pl.BlockSpec((tm, AW), row),
            _resident((1, MW)),
            _resident((MW + AW, d)),
            _resident((1, d)),
            _resident((d, dff)),
            _resident((dff, d)),
        ],
        out_specs=pl.BlockSpec((tm, d), row),
        out_shape=jax.ShapeDtypeStruct((n, d), F32),
        compiler_params=pltpu.CompilerParams(
            dimension_semantics=("parallel",), vmem_limit_bytes=VMEM_LIMIT),
        name="outproj_ffn",
    )(x, hf, hb, om, mixb, ng, wo, g2, w1, w2)


def kernel(x, norm1_g, w_in, conv_w, conv_b, gate_b, mlstm_norm_g, q_norm_g, k_norm_g, lambdas,
           diff_norm_g, rel_bias, w_out, norm2_g, w_ff1, w_ff2):
    batch, seq, d = x.shape
    depth = w_in.shape[0]
    n = batch * seq
    tm = min(512, n)
    tile = min(512, seq)
    assert seq % MLSTM_CHUNK == 0 and seq % tile == 0 and tm == tile
    assert seq // tile >= ATTN_WINDOW and (seq // tile - ATTN_WINDOW) % 2 == 0

    w_main = jnp.concatenate([w_in[:, :, :C_G], w_in[:, :, C_G + N_GATES:]], axis=-1).astype(BF16)
    w_gate = jnp.pad(w_in[:, :, C_G:C_G + N_GATES], ((0, 0), (0, 0), (0, LANES - N_GATES))).astype(BF16)
    w_out_b = w_out.astype(BF16)
    w_ff1_b = w_ff1.astype(BF16)
    w_ff2_b = w_ff2.astype(BF16)
    lane = jnp.arange(2 * LANES)
    bd = jnp.where((lane[:, None] // HD_A) == (lane[None, :] // HD_A), 1.0 / HD_A, 0.0).astype(BF16)

    bias_t = _bias_tiles(rel_bias.astype(F32), tile)

    xf = x.reshape(n, d)
    for l in range(depth):
        lam_init = 0.8 - 0.6 * math.exp(-0.3 * l)
        qg = jnp.tile(q_norm_g[l], 2 * LANES // HD_A)[None, :]
        kg = jnp.tile(k_norm_g[l], 2 * LANES // HD_A)[None, :]
        qtm, km, vtm, om, gates, qn, k1, k2, vt = _inproj(
            xf, norm1_g[l][None, :], w_main[l], w_gate[l], conv_w[l], conv_b[l][None, :],
            qg, kg, bd, seq, tm)
        gb = gate_b[l].reshape(1, N_GATES).astype(F32)
        hf, hb = _mlstm(qtm, km, vtm, gates, gates.T, gb, gb.T, batch, seq, tm)
        mixb = _attention(qn, k1, k2, vt, qg, kg, bias_t, rel_bias.astype(F32),
                          jnp.array([[lam_init, 1.0 - lam_init]], F32), lambdas[l],
                          diff_norm_g[l].reshape(N_A, 2 * HD_A, 1), batch, seq, tile)
        xf = _outffn(xf, hf, hb, om, mixb, mlstm_norm_g[l][None, :], w_out_b[l],
                     norm2_g[l][None, :], w_ff1_b[l], w_ff2_b[l], tm)
    return xf.reshape(batch, seq, d)
```

```python
import functools
import math

import jax
import jax.numpy as jnp
from jax import lax
from jax.experimental import pallas as pl
from jax.experimental.pallas import tpu as pltpu

F32 = jnp.float32
BF16 = jnp.bfloat16

N_M = 4
HD_M = 128
MW = N_M * HD_M
MLSTM_CHUNK = 128
CONV_W = 5
N_A = 4
HD_A = 64
AW = N_A * 2 * HD_A
REL_BUCKETS = 32
REL_MAX_DIST = 128
EPS = 1e-6
N_GATES = 4 * N_M
C_G = 4 * MW
LANES = 128
HALO = 16
VMEM_LIMIT = 56 * 1024 * 1024
LOG2E = math.log2(math.e)
VT_ROWS = 2 * HD_A + HALO
BIAS_ROWS = 3
LOGIT_SPAN_LIMIT = 96.0
ATTN_WINDOW = 4

def _dot(a, b):
    return jnp.dot(a, b, preferred_element_type=F32)


def _dot_nt(a, b):
    return lax.dot_general(a, b, (((1,), (1,)), ((), ())), preferred_element_type=F32)


def _dot_tn(a, b):
    return lax.dot_general(a, b, (((0,), (0,)), ((), ())), preferred_element_type=F32)


def _dot_exact(a, b):
    return jnp.dot(a, b, preferred_element_type=F32, precision=lax.Precision.HIGHEST)


def _sigmoid(x):
    return 1.0 / (1.0 + jnp.exp(-x))


def _log_sigmoid(x):
    return jnp.minimum(x, 0.0) - jnp.log1p(jnp.exp(-jnp.abs(x)))


def _resident(shape):
    nd = len(shape)
    return pl.BlockSpec(shape, lambda *_: (0,) * nd, pipeline_mode=pl.Buffered(1))


def _resident_layer(shape, layer):
    nd = len(shape)
    return pl.BlockSpec((None,) + tuple(shape), lambda *_: (layer,) + (0,) * nd,
                        pipeline_mode=pl.Buffered(1))


def _inproj_kernel(xp_ref, x_ref, xn_ref, g1_ref, w_ref, wg_ref, cw_ref, cb_ref, qg_ref, kg_ref, bd_ref,
                   qtm_ref, km_ref, vtm_ref, om_ref, gates_ref, qn_ref, k1_ref, k2_ref, vt_ref,
                   ext_s, *, tiles_per_seq):
    rows = x_ref.shape[0]
    pos = pl.program_id(0) % tiles_per_seq

    def normed(xv):
        ms = jnp.mean(xv * xv, axis=-1, keepdims=True)
        return (xv * lax.rsqrt(ms + EPS) * g1_ref[...]).astype(BF16)

    x = x_ref[...]
    h = normed(x)

    def proj(c0, width):
        return _dot(h, w_ref[:, c0:c0 + width])

    h_ext = jnp.concatenate([normed(xp_ref[...]), h, normed(xn_ref[...])], axis=0)
    rid = lax.broadcasted_iota(jnp.int32, (rows + 2 * HALO, 1), 0)
    inside = jnp.logical_and(jnp.logical_or(rid >= HALO, pos > 0),
                             jnp.logical_or(rid < HALO + rows, pos < tiles_per_seq - 1))
    ext_s[...] = jnp.where(inside, _dot(h_ext, w_ref[:, 0:2 * MW]), 0.0)

    def conv_head(cb):
        cols = slice(cb * LANES, (cb + 1) * LANES)
        conv = cb_ref[:, cols]
        for j in range(CONV_W):
            off = HALO - CONV_W // 2 + j
            conv = conv + ext_s[off:off + rows, cols] * cw_ref[j:j + 1, cols]
        y = conv * _sigmoid(conv)
        if cb < N_M:
            qtm_ref[cols, :] = y.T.astype(BF16)
        else:
            km_ref[:, cb * LANES - MW:(cb + 1) * LANES - MW] = (y * (HD_M ** -0.5)).astype(BF16)

    ones_rows = jnp.ones((HALO, rows), BF16)
    pair = 2 * LANES

    def values_t(c0, out_ref, hp):
        v2 = proj(c0 + hp * pair, pair)
        for sub in range(2):
            hh = 2 * hp + sub
            out_ref[0, hh * VT_ROWS:hh * VT_ROWS + HD_M, :] = (
                v2[:, sub * LANES:(sub + 1) * LANES].T.astype(BF16))
            out_ref[0, hh * VT_ROWS + HD_M:(hh + 1) * VT_ROWS, :] = ones_rows

    def out_gate():
        om_ref[...] = proj(3 * MW, MW).astype(BF16)
        gates_ref[...] = _dot(h, wg_ref[...])[:, :N_GATES]

    bd = bd_ref[...]

    def qk_norm(c0, g_ref):
        t = proj(c0, pair)
        msq = _dot((t * t).astype(BF16), bd)
        return t * lax.rsqrt(msq + EPS) * g_ref[...]

    lane = lax.broadcasted_iota(jnp.int32, (rows, pair), 1) % LANES
    ones_cols = jnp.where(lane < HD_A + BIAS_ROWS, 1.0, 0.0)

    def attn_q(hp):
        sl = slice(hp * pair, (hp + 1) * pair)
        qn_ref[:, sl] = (qk_norm(4 * MW + hp * pair, qg_ref) * (HD_A ** -0.5 * LOG2E)).astype(BF16)

    def attn_k(hp):
        sl = slice(hp * pair, (hp + 1) * pair)
        kn = qk_norm(4 * MW + AW + hp * pair, kg_ref)
        k1_ref[:, sl] = jnp.where(lane < HD_A, kn, ones_cols).astype(BF16)
        k2_ref[:, sl] = jnp.where(lane < HD_A, pltpu.roll(kn, pair - HD_A, 1), ones_cols).astype(BF16)

    projections = [
        functools.partial(values_t, 2 * MW, vtm_ref, 0), functools.partial(values_t, 2 * MW, vtm_ref, 1),
        out_gate,
        functools.partial(attn_q, 0), functools.partial(attn_k, 0),
        functools.partial(attn_q, 1), functools.partial(attn_k, 1),
        functools.partial(values_t, 4 * MW + 2 * AW, vt_ref, 0),
    ]
    for cb, projection in enumerate(projections):
        conv_head(cb)
        projection()
    values_t(4 * MW + 2 * AW, vt_ref, 1)


def _inproj(x, g1, w_main, w_gate, layer, conv_w, conv_b, qg, kg, bd, seq, tm):
    n, d = x.shape
    wcols = w_main.shape[2]
    hpt = tm // HALO
    nhalo = n // HALO
    row = lambda i: (i, 0)
    vt_spec = pl.BlockSpec((1, N_A * VT_ROWS, tm), lambda i: (i, 0, 0))
    vt_shape = jax.ShapeDtypeStruct((n // tm, N_A * VT_ROWS, tm), BF16)
    return pl.pallas_call(
        functools.partial(_inproj_kernel, tiles_per_seq=seq // tm),
        grid=(n // tm,),
        in_specs=[
            pl.BlockSpec((HALO, d), lambda i: (jnp.maximum(i * hpt - 1, 0), 0)),
            pl.BlockSpec((tm, d), row),
            pl.BlockSpec((HALO, d), lambda i: (jnp.minimum((i + 1) * hpt, nhalo - 1), 0)),
            _resident((1, d)),
            _resident_layer((d, wcols), layer),
            _resident_layer((d, LANES), layer),
            _resident((CONV_W, 2 * MW)),
            _resident((1, 2 * MW)),
            _resident((1, 2 * LANES)),
            _resident((1, 2 * LANES)),
            _resident((2 * LANES, 2 * LANES)),
        ],
        out_specs=[
            pl.BlockSpec((MW, tm), lambda i: (0, i)),
            pl.BlockSpec((tm, MW), row),
            vt_spec,
            pl.BlockSpec((tm, MW), row),
            pl.BlockSpec((tm, N_GATES), row),
            pl.BlockSpec((tm, AW), row),
            pl.BlockSpec((tm, AW), row),
            pl.BlockSpec((tm, AW), row),
            vt_spec,
        ],
        out_shape=[
            jax.ShapeDtypeStruct((MW, n), BF16),
            jax.ShapeDtypeStruct((n, MW), BF16),
            vt_shape,
            jax.ShapeDtypeStruct((n, MW), BF16),
            jax.ShapeDtypeStruct((n, N_GATES), F32),
            jax.ShapeDtypeStruct((n, AW), BF16),
            jax.ShapeDtypeStruct((n, AW), BF16),
            jax.ShapeDtypeStruct((n, AW), BF16),
            vt_shape,
        ],
        scratch_shapes=[pltpu.VMEM((tm + 2 * HALO, 2 * MW), F32)],
        compiler_params=pltpu.CompilerParams(
            dimension_semantics=("parallel",), vmem_limit_bytes=VMEM_LIMIT),
        name="inproj",
    )(x, x, x, g1, w_main, w_gate, conv_w, conv_b, qg, kg, bd)


def _mlstm_kernel(qt_f, k_f, vt_f, g_f, gt_f,
                  qt_b, k_b, vt_b, g_b, gt_b,
                  gbr_ref, gbc_ref,
                  hf_ref, hb_ref,
                  c_s, m_s):
    L = MLSTM_CHUNK
    c = pl.program_id(1)

    @pl.when(c == 0)
    def _():
        c_s[...] = jnp.zeros_like(c_s)
        m_s[...] = jnp.zeros_like(m_s)

    row = lax.broadcasted_iota(jnp.int32, (L, L), 0)
    col = lax.broadcasted_iota(jnp.int32, (L, L), 1)
    lower = col <= row
    upper = row <= col
    lower_f = lower.astype(F32)
    upper_f = upper.astype(F32)

    chains = []
    dirs = ((qt_f, k_f, vt_f, g_f, gt_f, hf_ref), (qt_b, k_b, vt_b, g_b, gt_b, hb_ref))
    for d, (qt_ref, k_ref, vt_ref, g_ref, gt_ref, out_ref) in enumerate(dirs):
        gates = g_ref[...] + gbr_ref[...]
        gates_t = gt_ref[...] + gbc_ref[...]
        logf = _log_sigmoid(gates)
        logf_t = _log_sigmoid(gates_t)
        cum = _dot_exact(lower_f, logf)
        cum_t = _dot_exact(logf_t, upper_f)
        if d == 1:
            cum = cum[L - 1:L, :] - cum + logf
            cum_t = cum_t[:, L - 1:L] - cum_t + logf_t
        i_off = 2 * N_M * d
        f_off = i_off + N_M
        src = cum[:, f_off:f_off + N_M] - gates[:, i_off:i_off + N_M]
        for hh in range(N_M):
            chains.append(dict(
                idx=d * N_M + hh, out_ref=out_ref, cols=slice(hh * HD_M, (hh + 1) * HD_M),
                mask=upper if d == 0 else lower,
                src_col=src[:, hh:hh + 1],
                b_row=cum_t[f_off + hh:f_off + hh + 1, :],
                i_row=gates_t[i_off + hh:i_off + hh + 1, :],
                end=slice(L - 1, L) if d == 0 else slice(0, 1),
                qt=qt_ref[hh * HD_M:(hh + 1) * HD_M, :],
                kb=k_ref[:, hh * HD_M:(hh + 1) * HD_M],
                vt=vt_ref[0, hh * VT_ROWS:(hh + 1) * VT_ROWS, :],
            ))

    for ch in chains:
        ch["s"] = _dot(ch["kb"], ch["qt"])
        ch["c_prev"] = c_s[ch["idx"]]
        ch["inter_num"] = _dot(ch["c_prev"].astype(BF16), ch["qt"])
    for ch in chains:
        dmat = jnp.where(ch["mask"], ch["b_row"] - ch["src_col"], -jnp.inf)
        ch["m_prev"] = m_s[ch["idx"]][:, :1]
        inter = ch["b_row"] + ch["m_prev"]
        ch["m_t"] = jnp.maximum(inter, jnp.max(dmat, axis=0, keepdims=True))
        ch["scale"] = jnp.exp(inter - ch["m_t"])
        ch["w"] = (ch["s"] * jnp.exp(dmat - ch["m_t"])).astype(BF16)
    for ch in chains:
        num = ch["scale"] * ch["inter_num"] + _dot(ch["vt"], ch["w"])
        den = num[HD_M:HD_M + 1]
        ht = num[:HD_M] / jnp.maximum(jnp.abs(den), jnp.exp(-ch["m_t"]))
        ch["out_ref"][:, ch["cols"]] = ht.T.astype(ch["out_ref"].dtype)
    for ch in chains:
        b_end = ch["b_row"][:, ch["end"]]
        g_row = b_end - ch["b_row"] + ch["i_row"]
        m_new = jnp.maximum(b_end + ch["m_prev"], jnp.max(g_row, axis=-1, keepdims=True))
        decay = jnp.exp(b_end + ch["m_prev"] - m_new)
        vt_w = (ch["vt"].astype(F32) * jnp.exp(g_row - m_new)).astype(BF16)
        c_s[ch["idx"]] = decay * ch["c_prev"] + _dot(vt_w, ch["kb"])
        m_s[ch["idx"]] = jnp.broadcast_to(m_new, (1, LANES))


def _mlstm(qt, k, vt, gates, gates_t, gb_row, gb_col, batch, seq, tm):
    L = MLSTM_CHUNK
    nc = seq // L
    cpt = tm // L

    def chunk_f(b, c):
        return b * nc + c

    def chunk_b(b, c):
        return b * nc + (nc - 1 - c)

    def specs(chunk):
        return [
            pl.BlockSpec((MW, L), lambda b, c: (0, chunk(b, c))),
            pl.BlockSpec((L, MW), lambda b, c: (chunk(b, c), 0)),
            pl.BlockSpec((1, N_M * VT_ROWS, L), lambda b, c: (chunk(b, c) // cpt, 0, chunk(b, c) % cpt)),
            pl.BlockSpec((L, N_GATES), lambda b, c: (chunk(b, c), 0)),
            pl.BlockSpec((N_GATES, L), lambda b, c: (0, chunk(b, c))),
        ]

    n = batch * seq
    return pl.pallas_call(
        _mlstm_kernel,
        grid=(batch, nc),
        in_specs=specs(chunk_f) + specs(chunk_b) + [
            _resident((1, N_GATES)),
            _resident((N_GATES, 1)),
        ],
        out_specs=[
            pl.BlockSpec((L, MW), lambda b, c: (chunk_f(b, c), 0)),
            pl.BlockSpec((L, MW), lambda b, c: (chunk_b(b, c), 0)),
        ],
        out_shape=[jax.ShapeDtypeStruct((n, MW), BF16)] * 2,
        scratch_shapes=[
            pltpu.VMEM((2 * N_M, VT_ROWS, HD_M), F32),
            pltpu.VMEM((2 * N_M, 1, LANES), F32),
        ],
        compiler_params=pltpu.CompilerParams(
            dimension_semantics=("parallel", "arbitrary"), vmem_limit_bytes=VMEM_LIMIT),
        name="mlstm",
    )(qt, k, vt, gates, gates_t, qt, k, vt, gates, gates_t, gb_row, gb_col)


def _bias_kernel(rb_ref, out_ref, *, tile):
    hh = pl.program_id(0)
    dd = pl.program_id(1)
    kk = lax.broadcasted_iota(jnp.int32, (tile, tile), 0)
    qq = lax.broadcasted_iota(jnp.int32, (tile, tile), 1)
    rel = (dd - 1) * tile + kk - qq
    nb = REL_BUCKETS // 2
    max_exact = nb // 2
    n = jnp.abs(rel)
    nf = jnp.maximum(n, 1).astype(F32)
    large = max_exact + (jnp.log(nf / max_exact) / math.log(REL_MAX_DIST / max_exact)
                         * (nb - max_exact)).astype(jnp.int32)
    large = jnp.minimum(large, nb - 1)
    bucket = jnp.where(rel > 0, nb, 0) + jnp.where(n < max_exact, n, large)
    val = jnp.zeros((tile, tile), F32)
    for bkt in range(REL_BUCKETS):
        val = jnp.where(bucket == bkt, rb_ref[bkt, hh], val)
    out_ref[0, 0] = jnp.where(dd < 3, val * LOG2E, 0.0)


def _bias_tiles(rel_bias, tile):
    return pl.pallas_call(
        functools.partial(_bias_kernel, tile=tile),
        grid=(N_A, 4),
        in_specs=[pl.BlockSpec(memory_space=pltpu.SMEM)],
        out_specs=pl.BlockSpec((1, 1, tile, tile), lambda h, d: (h, d, 0, 0)),
        out_shape=jax.ShapeDtypeStruct((N_A, 4, tile, tile), F32),
        name="rel_bias_tiles",
    )(rel_bias)


def _far_bucket(tile):
    assert tile >= REL_MAX_DIST
    return REL_BUCKETS // 2 - 1


def _attn_kernel(rb_ref, li_ref, q_ref, k1_ref, k2_ref, vt_ref, qg_ref, kg_ref, bias_ref, lam_ref, sg_ref,
                 o_ref,
                 qt_s, m_s, acc_s, p0_s, p1_s, *, far):
    hh = pl.program_id(1)
    qi = pl.program_id(2)
    tq = q_ref.shape[0]
    tk = vt_ref.shape[2]
    nk = vt_ref.shape[0]
    vd = 2 * HD_A
    LEFT, RIGHT, NEAR = 0, 1, 2

    qt = q_ref[...].astype(F32).T
    rid = lax.broadcasted_iota(jnp.int32, qt.shape, 0)
    bases = (qt, jnp.concatenate([qt[HD_A:], qt[:HD_A]], axis=0))
    consts = (rb_ref[far, hh] * LOG2E, rb_ref[REL_BUCKETS // 2 + far, hh] * LOG2E, 0.0)

    def build_queries(shift):
        for i, base in enumerate(bases):
            body = jnp.where(rid < HD_A, base, 0.0).astype(BF16)
            for kind in range(len(consts)):
                qt_s[2 * kind + i] = body
        rid_o = lax.broadcasted_iota(jnp.int32, (HALO, tq), 0)
        for kind, cst in enumerate(consts):
            rest = jnp.full((HALO, tq), cst - shift, F32)
            extra = jnp.zeros((HALO, tq), F32)
            for r in range(BIAS_ROWS):
                part = rest.astype(BF16).astype(F32)
                extra = jnp.where(rid_o == r, part, extra)
                rest = rest - part
            for i in range(2):
                qt_s[2 * kind + i, HD_A:HD_A + HALO, :] = extra.astype(BF16)

    acc_s[...] = jnp.zeros_like(acc_s)

    def key_rows(j):
        return pl.ds(pl.multiple_of(j * tk, tk), tk)

    bucket_vals = [rb_ref[bkt, hh] for bkt in range(REL_BUCKETS)]
    bias_max = functools.reduce(jnp.maximum, bucket_vals) * LOG2E
    bias_min = functools.reduce(jnp.minimum, bucket_vals) * LOG2E
    reach = (HD_A ** 0.5 * LOG2E) * jnp.max(jnp.abs(qg_ref[...])) * jnp.max(jnp.abs(kg_ref[...]))
    shift = reach + bias_max
    span = 2.0 * reach + (bias_max - bias_min)
    bounded = span <= LOGIT_SPAN_LIMIT

    @pl.when(bounded)
    def _():
        build_queries(shift)

        def numerators(j, kind, p_buf, bias=None):
            for i, k_ref in enumerate((k1_ref, k2_ref)):
                s = _dot(k_ref[key_rows(j), :], qt_s[2 * kind + i])
                if bias is not None:
                    s = s + bias
                p_buf[i] = jnp.exp2(s).astype(BF16)

        def accumulate(j, p_buf):
            vt = vt_ref[j]
            for i in range(2):
                acc_s[i] += _dot(vt, p_buf[i])

        w0 = jnp.clip(qi - 1, 0, nk - ATTN_WINDOW)
        far_tile = lambda t: jnp.where(t < w0, t, t + ATTN_WINDOW)
        far_kind = lambda t: jnp.where(t < w0, LEFT, RIGHT)
        p_bufs = (p0_s, p1_s)
        for r in range(ATTN_WINDOW):
            j = w0 + r
            dlt = j - qi
            near = jnp.abs(dlt) <= 1
            kind = jnp.where(near, NEAR, jnp.where(dlt < 0, LEFT, RIGHT))
            numerators(j, kind, p_bufs[r % 2], bias_ref[0, jnp.where(near, dlt + 1, 3)])
            if r > 0:
                accumulate(j - 1, p_bufs[(r - 1) % 2])
        last_window = w0 + ATTN_WINDOW - 1

        def pair(u, carry):
            t0 = 2 * u
            numerators(far_tile(t0), far_kind(t0), p0_s)
            accumulate(jnp.where(u == 0, last_window, far_tile(t0 - 1)), p1_s)
            numerators(far_tile(t0 + 1), far_kind(t0 + 1), p1_s)
            accumulate(far_tile(t0), p0_s)
            return carry

        n_far = nk - ATTN_WINDOW
        lax.fori_loop(0, n_far // 2, pair, 0)
        accumulate(far_tile(n_far - 1) if n_far else last_window, p1_s)

    @pl.when(jnp.logical_not(bounded))
    def _():
        _online_softmax_tiles(qi, nk, build_queries, key_rows, k1_ref, k2_ref, vt_ref, bias_ref,
                              qt_s, m_s, acc_s)

    lv = lam_ref[...]
    lam = (jnp.exp(jnp.sum(lv[0:1] * lv[1:2], axis=-1, keepdims=True))
           - jnp.exp(jnp.sum(lv[2:3] * lv[3:4], axis=-1, keepdims=True)) + li_ref[0, 0])
    a1 = acc_s[0]
    a2 = acc_s[1]
    o = a1[:vd] / a1[vd:vd + 1] - lam * (a2[:vd] / a2[vd:vd + 1])
    ms = jnp.mean(o * o, axis=0, keepdims=True)
    o = o * lax.rsqrt(ms + EPS) * sg_ref[0] * li_ref[0, 1]
    o_ref[...] = o.T.astype(o_ref.dtype)


def _online_softmax_tiles(qi, nk, build_queries, key_rows, k1_ref, k2_ref, vt_ref, bias_ref,
                          qt_s, m_s, acc_s):
    LEFT, RIGHT, NEAR = 0, 1, 2
    build_queries(0.0)
    m_s[...] = jnp.full_like(m_s, -jnp.inf)

    def one_tile(j, carry):
        dlt = j - qi
        near = jnp.abs(dlt) <= 1
        kind = jnp.where(near, NEAR, jnp.where(dlt < 0, LEFT, RIGHT))
        bias = bias_ref[0, jnp.where(near, dlt + 1, 3)]
        vt = vt_ref[j]
        for i, k_ref in enumerate((k1_ref, k2_ref)):
            s = _dot(k_ref[key_rows(j), :], qt_s[2 * kind + i]) + bias
            m_old = m_s[i]
            m_new = jnp.maximum(m_old, jnp.max(s, axis=0, keepdims=True))
            p = jnp.exp2(s - m_new).astype(BF16)
            acc_s[i] = jnp.exp2(m_old - m_new) * acc_s[i] + _dot(vt, p)
            m_s[i] = m_new
        return carry

    lax.fori_loop(0, nk, one_tile, 0)


def _attention(qn, k1, k2, vt, qg, kg, bias_t, rel_bias, lam_init, lambdas, sub_g_col, batch, seq, tile):
    nt = seq // tile
    vd = 2 * HD_A
    far = _far_bucket(tile)
    smem = pl.BlockSpec(memory_space=pltpu.SMEM)
    return pl.pallas_call(
        functools.partial(_attn_kernel, far=far),
        grid=(batch, N_A, nt),
        in_specs=[
            smem,
            smem,
            pl.BlockSpec((tile, vd), lambda b, h, i: (b * nt + i, h)),
            pl.BlockSpec((seq, vd), lambda b, h, i: (b, h)),
            pl.BlockSpec((seq, vd), lambda b, h, i: (b, h)),
            pl.BlockSpec((nt, VT_ROWS, tile), lambda b, h, i: (b, h, 0)),
            pl.BlockSpec((1, 2 * LANES), lambda b, h, i: (0, 0)),
            pl.BlockSpec((1, 2 * LANES), lambda b, h, i: (0, 0)),
            pl.BlockSpec((1, 4, tile, tile), lambda b, h, i: (h, 0, 0, 0)),
            pl.BlockSpec((4, HD_A), lambda b, h, i: (0, 0)),
            pl.BlockSpec((1, vd, 1), lambda b, h, i: (h, 0, 0)),
        ],
        out_specs=pl.BlockSpec((tile, vd), lambda b, h, i: (b * nt + i, h)),
        out_shape=jax.ShapeDtypeStruct((batch * seq, AW), BF16),
        scratch_shapes=[
            pltpu.VMEM((6, vd, tile), BF16),
            pltpu.VMEM((2, 1, tile), F32),
            pltpu.VMEM((2, VT_ROWS, tile), F32),
            pltpu.VMEM((2, tile, tile), BF16),
            pltpu.VMEM((2, tile, tile), BF16),
        ],
        compiler_params=pltpu.CompilerParams(
            dimension_semantics=("parallel", "parallel", "arbitrary"),
            vmem_limit_bytes=VMEM_LIMIT),
        name="diff_attention",
    )(rel_bias, lam_init, qn, k1, k2, vt, qg, kg, bias_t, lambdas, sub_g_col)


def _outffn_kernel(x_ref, hf_ref, hb_ref, og_ref, mb_ref, ng_ref, wo_ref, g2_ref, w1_ref, w2_ref,
                   out_ref, *, ff_chunk):
    hs = hf_ref[...].astype(F32) + hb_ref[...].astype(F32)
    x1 = x_ref[...] + _dot(mb_ref[...], wo_ref[MW:, :])
    for hh in range(N_M):
        sl = slice(hh * HD_M, (hh + 1) * HD_M)
        t = hs[:, sl]
        ms = jnp.mean(t * t, axis=-1, keepdims=True)
        mix = t * lax.rsqrt(ms + EPS) * ng_ref[:, sl] * _sigmoid(og_ref[:, sl].astype(F32))
        x1 = x1 + _dot(mix.astype(BF16), wo_ref[sl, :])
    ms = jnp.mean(x1 * x1, axis=-1, keepdims=True)
    h2 = (x1 * lax.rsqrt(ms + EPS) * g2_ref[...]).astype(BF16)
    out_ref[...] = x1
    for j in range(w1_ref.shape[1] // ff_chunk):
        sl = slice(j * ff_chunk, (j + 1) * ff_chunk)
        u = jnp.maximum(_dot(h2, w1_ref[:, sl]), 0.0)
        out_ref[...] += _dot((u * u).astype(BF16), w2_ref[sl, :])


def _outffn(x, hf, hb, om, mixb, ng, wo, g2, w1, w2, layer, tm):
    n, d = x.shape
    dff = w1.shape[2]
    row = lambda i: (i, 0)
    return pl.pallas_call(
        functools.partial(_outffn_kernel, ff_chunk=1024),
        grid=(n // tm,),
        in_specs=[
            pl.BlockSpec((tm, d), row),
            pl.BlockSpec((tm, MW), row),
            pl.BlockSpec((tm, MW), row),
            pl.BlockSpec((tm, MW), row),
            pl.BlockSpec((tm, AW), row),
            _resident((1, MW)),
            _resident_layer((MW + AW, d), layer),
            _resident((1, d)),
            _resident_layer((d, dff), layer),
            _resident_layer((dff, d), layer),
        ],
        out_specs=pl.BlockSpec((tm, d), row),
        out_shape=jax.ShapeDtypeStruct((n, d), F32),
        compiler_params=pltpu.CompilerParams(
            dimension_semantics=("parallel",), vmem_limit_bytes=VMEM_LIMIT),
        name="outproj_ffn",
    )(x, hf, hb, om, mixb, ng, wo, g2, w1, w2)


def kernel(x, norm1_g, w_in, conv_w, conv_b, gate_b, mlstm_norm_g, q_norm_g, k_norm_g, lambdas,
           diff_norm_g, rel_bias, w_out, norm2_g, w_ff1, w_ff2):
    batch, seq, d = x.shape
    depth = w_in.shape[0]
    n = batch * seq
    tm = min(512, n)
    tile = min(512, seq)
    assert seq % MLSTM_CHUNK == 0 and seq % tile == 0 and tm == tile
    assert seq // tile >= ATTN_WINDOW and (seq // tile - ATTN_WINDOW) % 2 == 0

    w_main = jnp.concatenate([w_in[:, :, :C_G], w_in[:, :, C_G + N_GATES:]], axis=-1).astype(BF16)
    w_gate = jnp.pad(w_in[:, :, C_G:C_G + N_GATES], ((0, 0), (0, 0), (0, LANES - N_GATES))).astype(BF16)
    w_out_b = w_out.astype(BF16)
    w_ff1_b = w_ff1.astype(BF16)
    w_ff2_b = w_ff2.astype(BF16)
    lane = jnp.arange(2 * LANES)
    bd = jnp.where((lane[:, None] // HD_A) == (lane[None, :] // HD_A), 1.0 / HD_A, 0.0).astype(BF16)

    bias_t = _bias_tiles(rel_bias.astype(F32), tile)

    xf = x.reshape(n, d)
    for l in range(depth):
        lam_init = 0.8 - 0.6 * math.exp(-0.3 * l)
        qg = jnp.tile(q_norm_g[l], 2 * LANES // HD_A)[None, :]
        kg = jnp.tile(k_norm_g[l], 2 * LANES // HD_A)[None, :]
        qtm, km, vtm, om, gates, qn, k1, k2, vt = _inproj(
            xf, norm1_g[l][None, :], w_main, w_gate, l, conv_w[l], conv_b[l][None, :],
            qg, kg, bd, seq, tm)
        gb = gate_b[l].reshape(1, N_GATES).astype(F32)
        hf, hb = _mlstm(qtm, km, vtm, gates, gates.T, gb, gb.T, batch, seq, tm)
        mixb = _attention(qn, k1, k2, vt, qg, kg, bias_t, rel_bias.astype(F32),
                          jnp.array([[lam_init, 1.0 - lam_init]], F32), lambdas[l],
                          diff_norm_g[l].reshape(N_A, 2 * HD_A, 1), batch, seq, tile)
        xf = _outffn(xf, hf, hb, om, mixb, mlstm_norm_g[l][None, :], w_out_b,
                     norm2_g[l][None, :], w_ff1_b, w_ff2_b, l, tm)
    return xf.reshape(batch, seq, d)
```

```python
import functools
import math

import jax
import jax.numpy as jnp
from jax import lax
from jax.experimental import pallas as pl
from jax.experimental.pallas import tpu as pltpu

F32 = jnp.float32
BF16 = jnp.bfloat16

N_M = 4
HD_M = 128
MW = N_M * HD_M
MLSTM_CHUNK = 128
CONV_W = 5
N_A = 4
HD_A = 64
AW = N_A * 2 * HD_A
REL_BUCKETS = 32
REL_MAX_DIST = 128
EPS = 1e-6
N_GATES = 4 * N_M
C_G = 4 * MW
LANES = 128
HALO = 16
VMEM_LIMIT = 56 * 1024 * 1024
LOG2E = math.log2(math.e)
VT_ROWS = 2 * HD_A + HALO
BIAS_ROWS = 3
LOGIT_SPAN_LIMIT = 96.0
ATTN_WINDOW = 4

def _dot(a, b):
    return jnp.dot(a, b, preferred_element_type=F32)


def _dot_nt(a, b):
    return lax.dot_general(a, b, (((1,), (1,)), ((), ())), preferred_element_type=F32)


def _dot_tn(a, b):
    return lax.dot_general(a, b, (((0,), (0,)), ((), ())), preferred_element_type=F32)


def _dot_exact(a, b):
    return jnp.dot(a, b, preferred_element_type=F32, precision=lax.Precision.HIGHEST)


def _sigmoid(x):
    return 1.0 / (1.0 + jnp.exp(-x))


def _log_sigmoid(x):
    return jnp.minimum(x, 0.0) - jnp.log1p(jnp.exp(-jnp.abs(x)))


def _resident(shape):
    nd = len(shape)
    return pl.BlockSpec(shape, lambda *_: (0,) * nd, pipeline_mode=pl.Buffered(1))


def _resident_layer(shape, layer):
    nd = len(shape)
    return pl.BlockSpec((None,) + tuple(shape), lambda *_: (layer,) + (0,) * nd,
                        pipeline_mode=pl.Buffered(1))


def _inproj_kernel(xp_ref, x_ref, xn_ref, g1_ref, w_ref, wg_ref, gb_ref, cw_ref, cb_ref, qg_ref, kg_ref,
                   bd_ref,
                   qtm_ref, km_ref, vtm_ref, om_ref, gcol_ref, grow_ref, qn_ref, k1_ref, k2_ref, vt_ref,
                   ext_s, *, tiles_per_seq):
    rows = x_ref.shape[0]
    pos = pl.program_id(0) % tiles_per_seq

    def normed(xv):
        ms = jnp.mean(xv * xv, axis=-1, keepdims=True)
        return (xv * lax.rsqrt(ms + EPS) * g1_ref[...]).astype(BF16)

    x = x_ref[...]
    h = normed(x)

    def proj(c0, width):
        return _dot(h, w_ref[:, c0:c0 + width])

    h_ext = jnp.concatenate([normed(xp_ref[...]), h, normed(xn_ref[...])], axis=0)
    rid = lax.broadcasted_iota(jnp.int32, (rows + 2 * HALO, 1), 0)
    inside = jnp.logical_and(jnp.logical_or(rid >= HALO, pos > 0),
                             jnp.logical_or(rid < HALO + rows, pos < tiles_per_seq - 1))
    ext_s[...] = jnp.where(inside, _dot(h_ext, w_ref[:, 0:2 * MW]), 0.0)

    def conv_head(cb):
        cols = slice(cb * LANES, (cb + 1) * LANES)
        conv = cb_ref[:, cols]
        for j in range(CONV_W):
            off = HALO - CONV_W // 2 + j
            conv = conv + ext_s[off:off + rows, cols] * cw_ref[j:j + 1, cols]
        y = conv * _sigmoid(conv)
        if cb < N_M:
            qtm_ref[cols, :] = y.T.astype(BF16)
        else:
            km_ref[:, cb * LANES - MW:(cb + 1) * LANES - MW] = (y * (HD_M ** -0.5)).astype(BF16)

    ones_rows = jnp.ones((HALO, rows), BF16)
    pair = 2 * LANES

    def values_t(c0, out_ref, hp):
        v2 = proj(c0 + hp * pair, pair)
        for sub in range(2):
            hh = 2 * hp + sub
            out_ref[0, hh * VT_ROWS:hh * VT_ROWS + HD_M, :] = (
                v2[:, sub * LANES:(sub + 1) * LANES].T.astype(BF16))
            out_ref[0, hh * VT_ROWS + HD_M:(hh + 1) * VT_ROWS, :] = ones_rows

    def out_gate():
        om_ref[...] = proj(3 * MW, MW).astype(BF16)
        L = MLSTM_CHUNK
        gt = (_dot(h, wg_ref[...]) + gb_ref[...]).T[:N_GATES]
        logf = _log_sigmoid(gt)
        g_under_f = pltpu.roll(gt, N_M, 0)
        upper_f = (lax.broadcasted_iota(jnp.int32, (L, L), 0)
                   <= lax.broadcasted_iota(jnp.int32, (L, L), 1)).astype(F32)
        forward_rows = lax.broadcasted_iota(jnp.int32, (N_GATES, L), 0) < 2 * N_M
        for r in range(rows // L):
            sl = slice(r * L, (r + 1) * L)
            cum = _dot_exact(logf[:, sl], upper_f)
            suf = cum[:, L - 1:L] - cum + logf[:, sl]
            grow_ref[0:N_GATES, sl] = cum
            grow_ref[N_GATES:2 * N_GATES, sl] = gt[:, sl]
            grow_ref[2 * N_GATES:3 * N_GATES, sl] = suf
            src = jnp.where(forward_rows, cum, suf) - g_under_f[:, sl]
            src = jnp.concatenate([src, jnp.zeros((L - N_GATES, L), F32)], axis=0)
            gcol_ref[sl, :] = src.T[:, :N_GATES]

    bd = bd_ref[...]

    def qk_norm(c0, g_ref):
        t = proj(c0, pair)
        msq = _dot((t * t).astype(BF16), bd)
        return t * lax.rsqrt(msq + EPS) * g_ref[...]

    lane = lax.broadcasted_iota(jnp.int32, (rows, pair), 1) % LANES
    ones_cols = jnp.where(lane < HD_A + BIAS_ROWS, 1.0, 0.0)

    def attn_q(hp):
        sl = slice(hp * pair, (hp + 1) * pair)
        qn_ref[:, sl] = (qk_norm(4 * MW + hp * pair, qg_ref) * (HD_A ** -0.5 * LOG2E)).astype(BF16)

    def attn_k(hp):
        sl = slice(hp * pair, (hp + 1) * pair)
        kn = qk_norm(4 * MW + AW + hp * pair, kg_ref)
        k1_ref[:, sl] = jnp.where(lane < HD_A, kn, ones_cols).astype(BF16)
        k2_ref[:, sl] = jnp.where(lane < HD_A, pltpu.roll(kn, pair - HD_A, 1), ones_cols).astype(BF16)

    projections = [
        functools.partial(values_t, 2 * MW, vtm_ref, 0), functools.partial(values_t, 2 * MW, vtm_ref, 1),
        out_gate,
        functools.partial(attn_q, 0), functools.partial(attn_k, 0),
        functools.partial(attn_q, 1), functools.partial(attn_k, 1),
        functools.partial(values_t, 4 * MW + 2 * AW, vt_ref, 0),
    ]
    for cb, projection in enumerate(projections):
        conv_head(cb)
        projection()
    values_t(4 * MW + 2 * AW, vt_ref, 1)


def _inproj(x, g1, w_main, w_gate, layer, gate_b, conv_w, conv_b, qg, kg, bd, seq, tm):
    n, d = x.shape
    wcols = w_main.shape[2]
    hpt = tm // HALO
    nhalo = n // HALO
    row = lambda i: (i, 0)
    vt_spec = pl.BlockSpec((1, N_A * VT_ROWS, tm), lambda i: (i, 0, 0))
    vt_shape = jax.ShapeDtypeStruct((n // tm, N_A * VT_ROWS, tm), BF16)
    return pl.pallas_call(
        functools.partial(_inproj_kernel, tiles_per_seq=seq // tm),
        grid=(n // tm,),
        in_specs=[
            pl.BlockSpec((HALO, d), lambda i: (jnp.maximum(i * hpt - 1, 0), 0)),
            pl.BlockSpec((tm, d), row),
            pl.BlockSpec((HALO, d), lambda i: (jnp.minimum((i + 1) * hpt, nhalo - 1), 0)),
            _resident((1, d)),
            _resident_layer((d, wcols), layer),
            _resident_layer((d, LANES), layer),
            _resident((1, LANES)),
            _resident((CONV_W, 2 * MW)),
            _resident((1, 2 * MW)),
            _resident((1, 2 * LANES)),
            _resident((1, 2 * LANES)),
            _resident((2 * LANES, 2 * LANES)),
        ],
        out_specs=[
            pl.BlockSpec((MW, tm), lambda i: (0, i)),
            pl.BlockSpec((tm, MW), row),
            vt_spec,
            pl.BlockSpec((tm, MW), row),
            pl.BlockSpec((tm, N_GATES), row),
            pl.BlockSpec((3 * N_GATES, tm), lambda i: (0, i)),
            pl.BlockSpec((tm, AW), row),
            pl.BlockSpec((tm, AW), row),
            pl.BlockSpec((tm, AW), row),
            vt_spec,
        ],
        out_shape=[
            jax.ShapeDtypeStruct((MW, n), BF16),
            jax.ShapeDtypeStruct((n, MW), BF16),
            vt_shape,
            jax.ShapeDtypeStruct((n, MW), BF16),
            jax.ShapeDtypeStruct((n, N_GATES), F32),
            jax.ShapeDtypeStruct((3 * N_GATES, n), F32),
            jax.ShapeDtypeStruct((n, AW), BF16),
            jax.ShapeDtypeStruct((n, AW), BF16),
            jax.ShapeDtypeStruct((n, AW), BF16),
            vt_shape,
        ],
        scratch_shapes=[pltpu.VMEM((tm + 2 * HALO, 2 * MW), F32)],
        compiler_params=pltpu.CompilerParams(
            dimension_semantics=("parallel",), vmem_limit_bytes=VMEM_LIMIT),
        name="inproj",
    )(x, x, x, g1, w_main, w_gate, gate_b, conv_w, conv_b, qg, kg, bd)


def _mlstm_kernel(qt_f, k_f, vt_f, gc_f, gr_f,
                  qt_b, k_b, vt_b, gc_b, gr_b,
                  hf_ref, hb_ref,
                  c_s, m_s):
    L = MLSTM_CHUNK
    c = pl.program_id(1)

    @pl.when(c == 0)
    def _():
        c_s[...] = jnp.zeros_like(c_s)
        m_s[...] = jnp.zeros_like(m_s)

    row = lax.broadcasted_iota(jnp.int32, (L, L), 0)
    col = lax.broadcasted_iota(jnp.int32, (L, L), 1)
    lower = col <= row
    upper = row <= col

    chains = []
    dirs = ((qt_f, k_f, vt_f, gc_f, gr_f, hf_ref), (qt_b, k_b, vt_b, gc_b, gr_b, hb_ref))
    for d, (qt_ref, k_ref, vt_ref, gc_ref, gr_ref, out_ref) in enumerate(dirs):
        i_off = 2 * N_M * d
        f_off = i_off + N_M
        src = gc_ref[...]
        cum_rows = gr_ref[0:N_GATES, :] if d == 0 else gr_ref[2 * N_GATES:3 * N_GATES, :]
        gate_rows = gr_ref[N_GATES:2 * N_GATES, :]
        for hh in range(N_M):
            chains.append(dict(
                idx=d * N_M + hh, out_ref=out_ref, cols=slice(hh * HD_M, (hh + 1) * HD_M),
                mask=upper if d == 0 else lower,
                src_col=src[:, f_off + hh:f_off + hh + 1],
                b_row=cum_rows[f_off + hh:f_off + hh + 1, :],
                i_row=gate_rows[i_off + hh:i_off + hh + 1, :],
                end=slice(L - 1, L) if d == 0 else slice(0, 1),
                qt=qt_ref[hh * HD_M:(hh + 1) * HD_M, :],
                kb=k_ref[:, hh * HD_M:(hh + 1) * HD_M],
                vt=vt_ref[0, hh * VT_ROWS:(hh + 1) * VT_ROWS, :],
            ))

    for ch in chains:
        ch["s"] = _dot(ch["kb"], ch["qt"])
        ch["c_prev"] = c_s[ch["idx"]]
        ch["inter_num"] = _dot(ch["c_prev"].astype(BF16), ch["qt"])
    for ch in chains:
        dmat = jnp.where(ch["mask"], ch["b_row"] - ch["src_col"], -jnp.inf)
        ch["m_prev"] = m_s[ch["idx"]][:, :1]
        inter = ch["b_row"] + ch["m_prev"]
        ch["m_t"] = jnp.maximum(inter, jnp.max(dmat, axis=0, keepdims=True))
        ch["scale"] = jnp.exp(inter - ch["m_t"])
        ch["w"] = (ch["s"] * jnp.exp(dmat - ch["m_t"])).astype(BF16)
    for ch in chains:
        num = ch["scale"] * ch["inter_num"] + _dot(ch["vt"], ch["w"])
        den = num[HD_M:HD_M + 1]
        ht = num[:HD_M] / jnp.maximum(jnp.abs(den), jnp.exp(-ch["m_t"]))
        ch["out_ref"][:, ch["cols"]] = ht.T.astype(ch["out_ref"].dtype)
    for ch in chains:
        b_end = ch["b_row"][:, ch["end"]]
        g_row = b_end - ch["b_row"] + ch["i_row"]
        m_new = jnp.maximum(b_end + ch["m_prev"], jnp.max(g_row, axis=-1, keepdims=True))
        decay = jnp.exp(b_end + ch["m_prev"] - m_new)
        vt_w = (ch["vt"].astype(F32) * jnp.exp(g_row - m_new)).astype(BF16)
        c_s[ch["idx"]] = decay * ch["c_prev"] + _dot(vt_w, ch["kb"])
        m_s[ch["idx"]] = jnp.broadcast_to(m_new, (1, LANES))


def _mlstm(qt, k, vt, gate_cols, gate_rows, batch, seq, tm):
    L = MLSTM_CHUNK
    nc = seq // L
    cpt = tm // L

    def chunk_f(b, c):
        return b * nc + c

    def chunk_b(b, c):
        return b * nc + (nc - 1 - c)

    def specs(chunk):
        return [
            pl.BlockSpec((MW, L), lambda b, c: (0, chunk(b, c))),
            pl.BlockSpec((L, MW), lambda b, c: (chunk(b, c), 0)),
            pl.BlockSpec((1, N_M * VT_ROWS, L), lambda b, c: (chunk(b, c) // cpt, 0, chunk(b, c) % cpt)),
            pl.BlockSpec((L, N_GATES), lambda b, c: (chunk(b, c), 0)),
            pl.BlockSpec((3 * N_GATES, L), lambda b, c: (0, chunk(b, c))),
        ]

    n = batch * seq
    return pl.pallas_call(
        _mlstm_kernel,
        grid=(batch, nc),
        in_specs=specs(chunk_f) + specs(chunk_b),
        out_specs=[
            pl.BlockSpec((L, MW), lambda b, c: (chunk_f(b, c), 0)),
            pl.BlockSpec((L, MW), lambda b, c: (chunk_b(b, c), 0)),
        ],
        out_shape=[jax.ShapeDtypeStruct((n, MW), BF16)] * 2,
        scratch_shapes=[
            pltpu.VMEM((2 * N_M, VT_ROWS, HD_M), F32),
            pltpu.VMEM((2 * N_M, 1, LANES), F32),
        ],
        compiler_params=pltpu.CompilerParams(
            dimension_semantics=("parallel", "arbitrary"), vmem_limit_bytes=VMEM_LIMIT),
        name="mlstm",
    )(qt, k, vt, gate_cols, gate_rows, qt, k, vt, gate_cols, gate_rows)


def _bias_kernel(rb_ref, out_ref, *, tile):
    hh = pl.program_id(0)
    dd = pl.program_id(1)
    kk = lax.broadcasted_iota(jnp.int32, (tile, tile), 0)
    qq = lax.broadcasted_iota(jnp.int32, (tile, tile), 1)
    rel = (dd - 1) * tile + kk - qq
    nb = REL_BUCKETS // 2
    max_exact = nb // 2
    n = jnp.abs(rel)
    nf = jnp.maximum(n, 1).astype(F32)
    large = max_exact + (jnp.log(nf / max_exact) / math.log(REL_MAX_DIST / max_exact)
                         * (nb - max_exact)).astype(jnp.int32)
    large = jnp.minimum(large, nb - 1)
    bucket = jnp.where(rel > 0, nb, 0) + jnp.where(n < max_exact, n, large)
    val = jnp.zeros((tile, tile), F32)
    for bkt in range(REL_BUCKETS):
        val = jnp.where(bucket == bkt, rb_ref[bkt, hh], val)
    out_ref[0, 0] = jnp.where(dd < 3, val * LOG2E, 0.0)


def _bias_tiles(rel_bias, tile):
    return pl.pallas_call(
        functools.partial(_bias_kernel, tile=tile),
        grid=(N_A, 4),
        in_specs=[pl.BlockSpec(memory_space=pltpu.SMEM)],
        out_specs=pl.BlockSpec((1, 1, tile, tile), lambda h, d: (h, d, 0, 0)),
        out_shape=jax.ShapeDtypeStruct((N_A, 4, tile, tile), F32),
        name="rel_bias_tiles",
    )(rel_bias)


def _far_bucket(tile):
    assert tile >= REL_MAX_DIST
    return REL_BUCKETS // 2 - 1


def _attn_kernel(rb_ref, li_ref, q_ref, k1_ref, k2_ref, vt_ref, qg_ref, kg_ref, bias_ref, lam_ref, sg_ref,
                 o_ref,
                 qt_s, m_s, acc_s, p0_s, p1_s, *, far):
    hh = pl.program_id(1)
    qi = pl.program_id(2)
    tq = q_ref.shape[0]
    tk = vt_ref.shape[2]
    nk = vt_ref.shape[0]
    vd = 2 * HD_A
    LEFT, RIGHT, NEAR = 0, 1, 2

    qt = q_ref[...].astype(F32).T
    rid = lax.broadcasted_iota(jnp.int32, qt.shape, 0)
    bases = (qt, jnp.concatenate([qt[HD_A:], qt[:HD_A]], axis=0))
    consts = (rb_ref[far, hh] * LOG2E, rb_ref[REL_BUCKETS // 2 + far, hh] * LOG2E, 0.0)

    def build_queries(shift):
        for i, base in enumerate(bases):
            body = jnp.where(rid < HD_A, base, 0.0).astype(BF16)
            for kind in range(len(consts)):
                qt_s[2 * kind + i] = body
        rid_o = lax.broadcasted_iota(jnp.int32, (HALO, tq), 0)
        for kind, cst in enumerate(consts):
            rest = jnp.full((HALO, tq), cst - shift, F32)
            extra = jnp.zeros((HALO, tq), F32)
            for r in range(BIAS_ROWS):
                part = rest.astype(BF16).astype(F32)
                extra = jnp.where(rid_o == r, part, extra)
                rest = rest - part
            for i in range(2):
                qt_s[2 * kind + i, HD_A:HD_A + HALO, :] = extra.astype(BF16)

    acc_s[...] = jnp.zeros_like(acc_s)

    def key_rows(j):
        return pl.ds(pl.multiple_of(j * tk, tk), tk)

    bucket_vals = [rb_ref[bkt, hh] for bkt in range(REL_BUCKETS)]
    bias_max = functools.reduce(jnp.maximum, bucket_vals) * LOG2E
    bias_min = functools.reduce(jnp.minimum, bucket_vals) * LOG2E
    reach = (HD_A ** 0.5 * LOG2E) * jnp.max(jnp.abs(qg_ref[...])) * jnp.max(jnp.abs(kg_ref[...]))
    shift = reach + bias_max
    span = 2.0 * reach + (bias_max - bias_min)
    bounded = span <= LOGIT_SPAN_LIMIT

    @pl.when(bounded)
    def _():
        build_queries(shift)

        def numerators(j, kind, p_buf, bias=None):
            for i, k_ref in enumerate((k1_ref, k2_ref)):
                s = _dot(k_ref[key_rows(j), :], qt_s[2 * kind + i])
                if bias is not None:
                    s = s + bias
                p_buf[i] = jnp.exp2(s).astype(BF16)

        def accumulate(j, p_buf):
            vt = vt_ref[j]
            for i in range(2):
                acc_s[i] += _dot(vt, p_buf[i])

        w0 = jnp.clip(qi - 1, 0, nk - ATTN_WINDOW)
        far_tile = lambda t: jnp.where(t < w0, t, t + ATTN_WINDOW)
        far_kind = lambda t: jnp.where(t < w0, LEFT, RIGHT)
        p_bufs = (p0_s, p1_s)
        for r in range(ATTN_WINDOW):
            j = w0 + r
            dlt = j - qi
            near = jnp.abs(dlt) <= 1
            kind = jnp.where(near, NEAR, jnp.where(dlt < 0, LEFT, RIGHT))
            numerators(j, kind, p_bufs[r % 2], bias_ref[0, jnp.where(near, dlt + 1, 3)])
            if r > 0:
                accumulate(j - 1, p_bufs[(r - 1) % 2])
        last_window = w0 + ATTN_WINDOW - 1

        def pair(u, carry):
            t0 = 2 * u
            numerators(far_tile(t0), far_kind(t0), p0_s)
            accumulate(jnp.where(u == 0, last_window, far_tile(t0 - 1)), p1_s)
            numerators(far_tile(t0 + 1), far_kind(t0 + 1), p1_s)
            accumulate(far_tile(t0), p0_s)
            return carry

        n_far = nk - ATTN_WINDOW
        lax.fori_loop(0, n_far // 2, pair, 0)
        accumulate(far_tile(n_far - 1) if n_far else last_window, p1_s)

    @pl.when(jnp.logical_not(bounded))
    def _():
        _online_softmax_tiles(qi, nk, build_queries, key_rows, k1_ref, k2_ref, vt_ref, bias_ref,
                              qt_s, m_s, acc_s)

    lv = lam_ref[...]
    lam = (jnp.exp(jnp.sum(lv[0:1] * lv[1:2], axis=-1, keepdims=True))
           - jnp.exp(jnp.sum(lv[2:3] * lv[3:4], axis=-1, keepdims=True)) + li_ref[0, 0])
    a1 = acc_s[0]
    a2 = acc_s[1]
    o = a1[:vd] / a1[vd:vd + 1] - lam * (a2[:vd] / a2[vd:vd + 1])
    ms = jnp.mean(o * o, axis=0, keepdims=True)
    o = o * lax.rsqrt(ms + EPS) * sg_ref[0] * li_ref[0, 1]
    o_ref[...] = o.T.astype(o_ref.dtype)


def _online_softmax_tiles(qi, nk, build_queries, key_rows, k1_ref, k2_ref, vt_ref, bias_ref,
                          qt_s, m_s, acc_s):
    LEFT, RIGHT, NEAR = 0, 1, 2
    build_queries(0.0)
    m_s[...] = jnp.full_like(m_s, -jnp.inf)

    def one_tile(j, carry):
        dlt = j - qi
        near = jnp.abs(dlt) <= 1
        kind = jnp.where(near, NEAR, jnp.where(dlt < 0, LEFT, RIGHT))
        bias = bias_ref[0, jnp.where(near, dlt + 1, 3)]
        vt = vt_ref[j]
        for i, k_ref in enumerate((k1_ref, k2_ref)):
            s = _dot(k_ref[key_rows(j), :], qt_s[2 * kind + i]) + bias
            m_old = m_s[i]
            m_new = jnp.maximum(m_old, jnp.max(s, axis=0, keepdims=True))
            p = jnp.exp2(s - m_new).astype(BF16)
            acc_s[i] = jnp.exp2(m_old - m_new) * acc_s[i] + _dot(vt, p)
            m_s[i] = m_new
        return carry

    lax.fori_loop(0, nk, one_tile, 0)


def _attention(qn, k1, k2, vt, qg, kg, bias_t, rel_bias, lam_init, lambdas, sub_g_col, batch, seq, tile):
    nt = seq // tile
    vd = 2 * HD_A
    far = _far_bucket(tile)
    smem = pl.BlockSpec(memory_space=pltpu.SMEM)
    return pl.pallas_call(
        functools.partial(_attn_kernel, far=far),
        grid=(batch, N_A, nt),
        in_specs=[
            smem,
            smem,
            pl.BlockSpec((tile, vd), lambda b, h, i: (b * nt + i, h)),
            pl.BlockSpec((seq, vd), lambda b, h, i: (b, h)),
            pl.BlockSpec((seq, vd), lambda b, h, i: (b, h)),
            pl.BlockSpec((nt, VT_ROWS, tile), lambda b, h, i: (b, h, 0)),
            pl.BlockSpec((1, 2 * LANES), lambda b, h, i: (0, 0)),
            pl.BlockSpec((1, 2 * LANES), lambda b, h, i: (0, 0)),
            pl.BlockSpec((1, 4, tile, tile), lambda b, h, i: (h, 0, 0, 0)),
            pl.BlockSpec((4, HD_A), lambda b, h, i: (0, 0)),
            pl.BlockSpec((1, vd, 1), lambda b, h, i: (h, 0, 0)),
        ],
        out_specs=pl.BlockSpec((tile, vd), lambda b, h, i: (b * nt + i, h)),
        out_shape=jax.ShapeDtypeStruct((batch * seq, AW), BF16),
        scratch_shapes=[
            pltpu.VMEM((6, vd, tile), BF16),
            pltpu.VMEM((2, 1, tile), F32),
            pltpu.VMEM((2, VT_ROWS, tile), F32),
            pltpu.VMEM((2, tile, tile), BF16),
            pltpu.VMEM((2, tile, tile), BF16),
        ],
        compiler_params=pltpu.CompilerParams(
            dimension_semantics=("parallel", "parallel", "arbitrary"),
            vmem_limit_bytes=VMEM_LIMIT),
        name="diff_attention",
    )(rel_bias, lam_init, qn, k1, k2, vt, qg, kg, bias_t, lambdas, sub_g_col)


def _outffn_kernel(x_ref, hf_ref, hb_ref, og_ref, mb_ref, ng_ref, wo_ref, g2_ref, w1_ref, w2_ref,
                   out_ref, *, ff_chunk):
    hs = hf_ref[...].astype(F32) + hb_ref[...].astype(F32)
    x1 = x_ref[...] + _dot(mb_ref[...], wo_ref[MW:, :])
    for hh in range(N_M):
        sl = slice(hh * HD_M, (hh + 1) * HD_M)
        t = hs[:, sl]
        ms = jnp.mean(t * t, axis=-1, keepdims=True)
        mix = t * lax.rsqrt(ms + EPS) * ng_ref[:, sl] * _sigmoid(og_ref[:, sl].astype(F32))
        x1 = x1 + _dot(mix.astype(BF16), wo_ref[sl, :])
    ms = jnp.mean(x1 * x1, axis=-1, keepdims=True)
    h2 = (x1 * lax.rsqrt(ms + EPS) * g2_ref[...]).astype(BF16)
    out_ref[...] = x1
    for j in range(w1_ref.shape[1] // ff_chunk):
        sl = slice(j * ff_chunk, (j + 1) * ff_chunk)
        u = jnp.maximum(_dot(h2, w1_ref[:, sl]), 0.0)
        out_ref[...] += _dot((u * u).astype(BF16), w2_ref[sl, :])


def _outffn(x, hf, hb, om, mixb, ng, wo, g2, w1, w2, layer, tm):
    n, d = x.shape
    dff = w1.shape[2]
    row = lambda i: (i, 0)
    return pl.pallas_call(
        functools.partial(_outffn_kernel, ff_chunk=1024),
        grid=(n // tm,),
        in_specs=[
            pl.BlockSpec((tm, d), row),
            pl.BlockSpec((tm, MW), row),
            pl.BlockSpec((tm, MW), row),
            pl.BlockSpec((tm, MW), row),
            pl.BlockSpec((tm, AW), row),
            _resident((1, MW)),
            _resident_layer((MW + AW, d), layer),
            _resident((1, d)),
            _resident_layer((d, dff), layer),
            _resident_layer((dff, d), layer),
        ],
        out_specs=pl.BlockSpec((tm, d), row),
        out_shape=jax.ShapeDtypeStruct((n, d), F32),
        compiler_params=pltpu.CompilerParams(
            dimension_semantics=("parallel",), vmem_limit_bytes=VMEM_LIMIT),
        name="outproj_ffn",
    )(x, hf, hb, om, mixb, ng, wo, g2, w1, w2)


def kernel(x, norm1_g, w_in, conv_w, conv_b, gate_b, mlstm_norm_g, q_norm_g, k_norm_g, lambdas,
           diff_norm_g, rel_bias, w_out, norm2_g, w_ff1, w_ff2):
    batch, seq, d = x.shape
    depth = w_in.shape[0]
    n = batch * seq
    tm = min(512, n)
    tile = min(512, seq)
    assert seq % MLSTM_CHUNK == 0 and seq % tile == 0 and tm == tile
    assert seq // tile >= ATTN_WINDOW and (seq // tile - ATTN_WINDOW) % 2 == 0

    w_main = jnp.concatenate([w_in[:, :, :C_G], w_in[:, :, C_G + N_GATES:]], axis=-1).astype(BF16)
    w_gate = jnp.pad(w_in[:, :, C_G:C_G + N_GATES], ((0, 0), (0, 0), (0, LANES - N_GATES))).astype(BF16)
    w_out_b = w_out.astype(BF16)
    w_ff1_b = w_ff1.astype(BF16)
    w_ff2_b = w_ff2.astype(BF16)
    lane = jnp.arange(2 * LANES)
    bd = jnp.where((lane[:, None] // HD_A) == (lane[None, :] // HD_A), 1.0 / HD_A, 0.0).astype(BF16)

    bias_t = _bias_tiles(rel_bias.astype(F32), tile)

    xf = x.reshape(n, d)
    for l in range(depth):
        lam_init = 0.8 - 0.6 * math.exp(-0.3 * l)
        qg = jnp.tile(q_norm_g[l], 2 * LANES // HD_A)[None, :]
        kg = jnp.tile(k_norm_g[l], 2 * LANES // HD_A)[None, :]
        gb = jnp.pad(gate_b[l].reshape(1, N_GATES).astype(F32), ((0, 0), (0, LANES - N_GATES)))
        qtm, km, vtm, om, gate_cols, gate_rows, qn, k1, k2, vt = _inproj(
            xf, norm1_g[l][None, :], w_main, w_gate, l, gb, conv_w[l], conv_b[l][None, :],
            qg, kg, bd, seq, tm)
        hf, hb = _mlstm(qtm, km, vtm, gate_cols, gate_rows, batch, seq, tm)
        mixb = _attention(qn, k1, k2, vt, qg, kg, bias_t, rel_bias.astype(F32),
                          jnp.array([[lam_init, 1.0 - lam_init]], F32), lambdas[l],
                          diff_norm_g[l].reshape(N_A, 2 * HD_A, 1), batch, seq, tile)
        xf = _outffn(xf, hf, hb, om, mixb, mlstm_norm_g[l][None, :], w_out_b,
                     norm2_g[l][None, :], w_ff1_b, w_ff2_b, l, tm)
    return xf.reshape(batch, seq, d)
```

```python
import functools
import math

import jax
import jax.numpy as jnp
from jax import lax
from jax.experimental import pallas as pl
from jax.experimental.pallas import tpu as pltpu

F32 = jnp.float32
BF16 = jnp.bfloat16

N_M = 4
HD_M = 128
MW = N_M * HD_M
MLSTM_CHUNK = 128
CONV_W = 5
N_A = 4
HD_A = 64
AW = N_A * 2 * HD_A
REL_BUCKETS = 32
REL_MAX_DIST = 128
EPS = 1e-6
N_GATES = 4 * N_M
C_G = 4 * MW
LANES = 128
HALO = 16
VMEM_LIMIT = 56 * 1024 * 1024
LOG2E = math.log2(math.e)
VT_ROWS = 2 * HD_A + HALO
BIAS_ROWS = 3
LOGIT_SPAN_LIMIT = 96.0
ATTN_WINDOW = 4

def _dot(a, b):
    return jnp.dot(a, b, preferred_element_type=F32)


def _dot_nt(a, b):
    return lax.dot_general(a, b, (((1,), (1,)), ((), ())), preferred_element_type=F32)


def _dot_tn(a, b):
    return lax.dot_general(a, b, (((0,), (0,)), ((), ())), preferred_element_type=F32)


def _dot_exact(a, b):
    return jnp.dot(a, b, preferred_element_type=F32, precision=lax.Precision.HIGHEST)


def _sigmoid(x):
    return 1.0 / (1.0 + jnp.exp(-x))


def _log_sigmoid(x):
    return jnp.minimum(x, 0.0) - jnp.log1p(jnp.exp(-jnp.abs(x)))


def _resident(shape):
    nd = len(shape)
    return pl.BlockSpec(shape, lambda *_: (0,) * nd, pipeline_mode=pl.Buffered(1))


def _resident_layer(shape, layer):
    nd = len(shape)
    return pl.BlockSpec((None,) + tuple(shape), lambda *_: (layer,) + (0,) * nd,
                        pipeline_mode=pl.Buffered(1))


def _inproj_kernel(xp_ref, x_ref, xn_ref, g1_ref, w_ref, wg_ref, gb_ref, cw_ref, cb_ref, qg_ref, kg_ref,
                   bd_ref,
                   qtm_ref, km_ref, vtm_ref, om_ref, gcol_ref, grow_ref, qn_ref, k1_ref, k2_ref, vt_ref,
                   ext_s, *, tiles_per_seq):
    rows = x_ref.shape[0]
    pos = pl.program_id(0) % tiles_per_seq

    def normed(xv):
        ms = jnp.mean(xv * xv, axis=-1, keepdims=True)
        return (xv * lax.rsqrt(ms + EPS) * g1_ref[...]).astype(BF16)

    x = x_ref[...]
    h = normed(x)

    def proj(c0, width):
        return _dot(h, w_ref[:, c0:c0 + width])

    h_ext = jnp.concatenate([normed(xp_ref[...]), h, normed(xn_ref[...])], axis=0)
    rid = lax.broadcasted_iota(jnp.int32, (rows + 2 * HALO, 1), 0)
    inside = jnp.logical_and(jnp.logical_or(rid >= HALO, pos > 0),
                             jnp.logical_or(rid < HALO + rows, pos < tiles_per_seq - 1))
    ext_s[...] = jnp.where(inside, _dot(h_ext, w_ref[:, 0:2 * MW]), 0.0)

    def conv_head(cb):
        cols = slice(cb * LANES, (cb + 1) * LANES)
        conv = cb_ref[:, cols]
        for j in range(CONV_W):
            off = HALO - CONV_W // 2 + j
            conv = conv + ext_s[off:off + rows, cols] * cw_ref[j:j + 1, cols]
        y = conv * _sigmoid(conv)
        if cb < N_M:
            qtm_ref[cols, :] = y.T.astype(BF16)
        else:
            km_ref[:, cb * LANES - MW:(cb + 1) * LANES - MW] = (y * (HD_M ** -0.5)).astype(BF16)

    ones_rows = jnp.ones((HALO, rows), BF16)
    pair = 2 * LANES

    def values_t(c0, out_ref, hp):
        v2 = proj(c0 + hp * pair, pair)
        for sub in range(2):
            hh = 2 * hp + sub
            out_ref[0, hh * VT_ROWS:hh * VT_ROWS + HD_M, :] = (
                v2[:, sub * LANES:(sub + 1) * LANES].T.astype(BF16))
            out_ref[0, hh * VT_ROWS + HD_M:(hh + 1) * VT_ROWS, :] = ones_rows

    def out_gate():
        om_ref[...] = proj(3 * MW, MW).astype(BF16)
        L = MLSTM_CHUNK
        gt = (_dot(h, wg_ref[...]) + gb_ref[...]).T[:N_GATES]
        logf = _log_sigmoid(gt)
        g_under_f = pltpu.roll(gt, N_M, 0)
        upper_f = (lax.broadcasted_iota(jnp.int32, (L, L), 0)
                   <= lax.broadcasted_iota(jnp.int32, (L, L), 1)).astype(F32)
        forward_rows = lax.broadcasted_iota(jnp.int32, (N_GATES, L), 0) < 2 * N_M
        for r in range(rows // L):
            sl = slice(r * L, (r + 1) * L)
            cum = _dot_exact(logf[:, sl], upper_f)
            suf = cum[:, L - 1:L] - cum + logf[:, sl]
            grow_ref[0:N_GATES, sl] = cum
            grow_ref[N_GATES:2 * N_GATES, sl] = gt[:, sl]
            grow_ref[2 * N_GATES:3 * N_GATES, sl] = suf
            src = jnp.where(forward_rows, cum, suf) - g_under_f[:, sl]
            src = jnp.concatenate([src, jnp.zeros((L - N_GATES, L), F32)], axis=0)
            gcol_ref[sl, :] = src.T[:, :N_GATES]

    bd = bd_ref[...]

    def qk_norm(c0, g_ref):
        t = proj(c0, pair)
        msq = _dot((t * t).astype(BF16), bd)
        return t * lax.rsqrt(msq + EPS) * g_ref[...]

    lane = lax.broadcasted_iota(jnp.int32, (rows, pair), 1) % LANES
    ones_cols = jnp.where(lane < HD_A + BIAS_ROWS, 1.0, 0.0)

    def attn_q(hp):
        sl = slice(hp * pair, (hp + 1) * pair)
        qn_ref[:, sl] = (qk_norm(4 * MW + hp * pair, qg_ref) * (HD_A ** -0.5 * LOG2E)).astype(BF16)

    def attn_k(hp):
        sl = slice(hp * pair, (hp + 1) * pair)
        kn = qk_norm(4 * MW + AW + hp * pair, kg_ref)
        k1_ref[:, sl] = jnp.where(lane < HD_A, kn, ones_cols).astype(BF16)
        k2_ref[:, sl] = jnp.where(lane < HD_A, pltpu.roll(kn, pair - HD_A, 1), ones_cols).astype(BF16)

    projections = [
        functools.partial(values_t, 2 * MW, vtm_ref, 0), functools.partial(values_t, 2 * MW, vtm_ref, 1),
        out_gate,
        functools.partial(attn_q, 0), functools.partial(attn_k, 0),
        functools.partial(attn_q, 1), functools.partial(attn_k, 1),
        functools.partial(values_t, 4 * MW + 2 * AW, vt_ref, 0),
    ]
    for cb, projection in enumerate(projections):
        conv_head(cb)
        projection()
    values_t(4 * MW + 2 * AW, vt_ref, 1)


def _inproj(x, g1, w_main, w_gate, layer, gate_b, conv_w, conv_b, qg, kg, bd, seq, tm):
    n, d = x.shape
    wcols = w_main.shape[2]
    hpt = tm // HALO
    nhalo = n // HALO
    row = lambda i: (i, 0)
    vt_spec = pl.BlockSpec((1, N_A * VT_ROWS, tm), lambda i: (i, 0, 0))
    vt_shape = jax.ShapeDtypeStruct((n // tm, N_A * VT_ROWS, tm), BF16)
    return pl.pallas_call(
        functools.partial(_inproj_kernel, tiles_per_seq=seq // tm),
        grid=(n // tm,),
        in_specs=[
            pl.BlockSpec((HALO, d), lambda i: (jnp.maximum(i * hpt - 1, 0), 0)),
            pl.BlockSpec((tm, d), row),
            pl.BlockSpec((HALO, d), lambda i: (jnp.minimum((i + 1) * hpt, nhalo - 1), 0)),
            _resident((1, d)),
            _resident_layer((d, wcols), layer),
            _resident_layer((d, LANES), layer),
            _resident((1, LANES)),
            _resident((CONV_W, 2 * MW)),
            _resident((1, 2 * MW)),
            _resident((1, 2 * LANES)),
            _resident((1, 2 * LANES)),
            _resident((2 * LANES, 2 * LANES)),
        ],
        out_specs=[
            pl.BlockSpec((MW, tm), lambda i: (0, i)),
            pl.BlockSpec((tm, MW), row),
            vt_spec,
            pl.BlockSpec((tm, MW), row),
            pl.BlockSpec((tm, N_GATES), row),
            pl.BlockSpec((3 * N_GATES, tm), lambda i: (0, i)),
            pl.BlockSpec((tm, AW), row),
            pl.BlockSpec((tm, AW), row),
            pl.BlockSpec((tm, AW), row),
            vt_spec,
        ],
        out_shape=[
            jax.ShapeDtypeStruct((MW, n), BF16),
            jax.ShapeDtypeStruct((n, MW), BF16),
            vt_shape,
            jax.ShapeDtypeStruct((n, MW), BF16),
            jax.ShapeDtypeStruct((n, N_GATES), F32),
            jax.ShapeDtypeStruct((3 * N_GATES, n), F32),
            jax.ShapeDtypeStruct((n, AW), BF16),
            jax.ShapeDtypeStruct((n, AW), BF16),
            jax.ShapeDtypeStruct((n, AW), BF16),
            vt_shape,
        ],
        scratch_shapes=[pltpu.VMEM((tm + 2 * HALO, 2 * MW), F32)],
        compiler_params=pltpu.CompilerParams(
            dimension_semantics=("parallel",), vmem_limit_bytes=VMEM_LIMIT),
        name="inproj",
    )(x, x, x, g1, w_main, w_gate, gate_b, conv_w, conv_b, qg, kg, bd)


def _mlstm_kernel(qt_f, k_f, vt_f, gc_f, gr_f,
                  qt_b, k_b, vt_b, gc_b, gr_b,
                  hf_ref, hb_ref,
                  c_s, m_s):
    L = MLSTM_CHUNK
    c = pl.program_id(1)

    @pl.when(c == 0)
    def _():
        c_s[...] = jnp.zeros_like(c_s)
        m_s[...] = jnp.zeros_like(m_s)

    row = lax.broadcasted_iota(jnp.int32, (L, L), 0)
    col = lax.broadcasted_iota(jnp.int32, (L, L), 1)
    lower = col <= row
    upper = row <= col

    chains = []
    dirs = ((qt_f, k_f, vt_f, gc_f, gr_f, hf_ref), (qt_b, k_b, vt_b, gc_b, gr_b, hb_ref))
    for d, (qt_ref, k_ref, vt_ref, gc_ref, gr_ref, out_ref) in enumerate(dirs):
        i_off = 2 * N_M * d
        f_off = i_off + N_M
        src = gc_ref[...]
        cum_rows = gr_ref[0:N_GATES, :] if d == 0 else gr_ref[2 * N_GATES:3 * N_GATES, :]
        gate_rows = gr_ref[N_GATES:2 * N_GATES, :]
        for hh in range(N_M):
            chains.append(dict(
                idx=d * N_M + hh, out_ref=out_ref, cols=slice(hh * HD_M, (hh + 1) * HD_M),
                mask=upper if d == 0 else lower,
                src_col=src[:, f_off + hh:f_off + hh + 1],
                b_row=cum_rows[f_off + hh:f_off + hh + 1, :],
                i_row=gate_rows[i_off + hh:i_off + hh + 1, :],
                end=slice(L - 1, L) if d == 0 else slice(0, 1),
                qt=qt_ref[hh * HD_M:(hh + 1) * HD_M, :],
                kb=k_ref[:, hh * HD_M:(hh + 1) * HD_M],
                vt=vt_ref[0, hh * VT_ROWS:(hh + 1) * VT_ROWS, :],
            ))

    for ch in chains:
        ch["s"] = _dot(ch["kb"], ch["qt"])
        ch["c_prev"] = c_s[ch["idx"]]
        ch["inter_num"] = _dot(ch["c_prev"].astype(BF16), ch["qt"])
    for ch in chains:
        dmat = jnp.where(ch["mask"], ch["b_row"] - ch["src_col"], -jnp.inf)
        ch["m_prev"] = m_s[ch["idx"]][:, :1]
        inter = ch["b_row"] + ch["m_prev"]
        ch["m_t"] = jnp.maximum(inter, jnp.max(dmat, axis=0, keepdims=True))
        ch["scale"] = jnp.exp(inter - ch["m_t"])
        ch["w"] = (ch["s"] * jnp.exp(dmat - ch["m_t"])).astype(BF16)
    for ch in chains:
        num = ch["scale"] * ch["inter_num"] + _dot(ch["vt"], ch["w"])
        den = num[HD_M:HD_M + 1]
        ht = num[:HD_M] / jnp.maximum(jnp.abs(den), jnp.exp(-ch["m_t"]))
        ch["out_ref"][:, ch["cols"]] = ht.T.astype(ch["out_ref"].dtype)
    for ch in chains:
        b_end = ch["b_row"][:, ch["end"]]
        g_row = b_end - ch["b_row"] + ch["i_row"]
        m_new = jnp.maximum(b_end + ch["m_prev"], jnp.max(g_row, axis=-1, keepdims=True))
        decay = jnp.exp(b_end + ch["m_prev"] - m_new)
        vt_w = (ch["vt"].astype(F32) * jnp.exp(g_row - m_new)).astype(BF16)
        c_s[ch["idx"]] = decay * ch["c_prev"] + _dot(vt_w, ch["kb"])
        m_s[ch["idx"]] = jnp.broadcast_to(m_new, (1, LANES))


def _mlstm(qt, k, vt, gate_cols, gate_rows, batch, seq, tm):
    L = MLSTM_CHUNK
    nc = seq // L
    cpt = tm // L

    def chunk_f(b, c):
        return b * nc + c

    def chunk_b(b, c):
        return b * nc + (nc - 1 - c)

    def specs(chunk):
        return [
            pl.BlockSpec((MW, L), lambda b, c: (0, chunk(b, c))),
            pl.BlockSpec((L, MW), lambda b, c: (chunk(b, c), 0)),
            pl.BlockSpec((1, N_M * VT_ROWS, L), lambda b, c: (chunk(b, c) // cpt, 0, chunk(b, c) % cpt)),
            pl.BlockSpec((L, N_GATES), lambda b, c: (chunk(b, c), 0)),
            pl.BlockSpec((3 * N_GATES, L), lambda b, c: (0, chunk(b, c))),
        ]

    n = batch * seq
    return pl.pallas_call(
        _mlstm_kernel,
        grid=(batch, nc),
        in_specs=specs(chunk_f) + specs(chunk_b),
        out_specs=[
            pl.BlockSpec((L, MW), lambda b, c: (chunk_f(b, c), 0)),
            pl.BlockSpec((L, MW), lambda b, c: (chunk_b(b, c), 0)),
        ],
        out_shape=[jax.ShapeDtypeStruct((n, MW), BF16)] * 2,
        scratch_shapes=[
            pltpu.VMEM((2 * N_M, VT_ROWS, HD_M), F32),
            pltpu.VMEM((2 * N_M, 1, LANES), F32),
        ],
        compiler_params=pltpu.CompilerParams(
            dimension_semantics=("parallel", "arbitrary"), vmem_limit_bytes=VMEM_LIMIT),
        name="mlstm",
    )(qt, k, vt, gate_cols, gate_rows, qt, k, vt, gate_cols, gate_rows)


def _bias_kernel(rb_ref, out_ref, *, tile):
    hh = pl.program_id(0)
    dd = pl.program_id(1)
    kk = lax.broadcasted_iota(jnp.int32, (tile, tile), 0)
    qq = lax.broadcasted_iota(jnp.int32, (tile, tile), 1)
    rel = (dd - 1) * tile + kk - qq
    nb = REL_BUCKETS // 2
    max_exact = nb // 2
    n = jnp.abs(rel)
    nf = jnp.maximum(n, 1).astype(F32)
    large = max_exact + (jnp.log(nf / max_exact) / math.log(REL_MAX_DIST / max_exact)
                         * (nb - max_exact)).astype(jnp.int32)
    large = jnp.minimum(large, nb - 1)
    bucket = jnp.where(rel > 0, nb, 0) + jnp.where(n < max_exact, n, large)
    val = jnp.zeros((tile, tile), F32)
    for bkt in range(REL_BUCKETS):
        val = jnp.where(bucket == bkt, rb_ref[bkt, hh], val)
    out_ref[0, 0] = jnp.where(dd < 3, val * LOG2E, 0.0)


def _bias_tiles(rel_bias, tile):
    return pl.pallas_call(
        functools.partial(_bias_kernel, tile=tile),
        grid=(N_A, 4),
        in_specs=[pl.BlockSpec(memory_space=pltpu.SMEM)],
        out_specs=pl.BlockSpec((1, 1, tile, tile), lambda h, d: (h, d, 0, 0)),
        out_shape=jax.ShapeDtypeStruct((N_A, 4, tile, tile), F32),
        name="rel_bias_tiles",
    )(rel_bias)


def _far_bucket(tile):
    assert tile >= REL_MAX_DIST
    return REL_BUCKETS // 2 - 1


def _attn_kernel(rb_ref, li_ref, q_ref, k1_ref, k2_ref, vt_ref, qg_ref, kg_ref, bias_ref, lam_ref, sg_ref,
                 o_ref,
                 qt_s, m_s, acc_s, p0_s, p1_s, *, far):
    hh = pl.program_id(1)
    qi = pl.program_id(2)
    tq = q_ref.shape[0]
    tk = vt_ref.shape[2]
    nk = vt_ref.shape[0]
    vd = 2 * HD_A
    LEFT, RIGHT, NEAR = 0, 1, 2

    qt = q_ref[...].astype(F32).T
    rid = lax.broadcasted_iota(jnp.int32, qt.shape, 0)
    bases = (qt, jnp.concatenate([qt[HD_A:], qt[:HD_A]], axis=0))
    consts = (rb_ref[far, hh] * LOG2E, rb_ref[REL_BUCKETS // 2 + far, hh] * LOG2E, 0.0)

    def build_queries(shift):
        for i, base in enumerate(bases):
            body = jnp.where(rid < HD_A, base, 0.0).astype(BF16)
            for kind in range(len(consts)):
                qt_s[2 * kind + i] = body
        rid_o = lax.broadcasted_iota(jnp.int32, (HALO, tq), 0)
        for kind, cst in enumerate(consts):
            rest = jnp.full((HALO, tq), cst - shift, F32)
            extra = jnp.zeros((HALO, tq), F32)
            for r in range(BIAS_ROWS):
                part = rest.astype(BF16).astype(F32)
                extra = jnp.where(rid_o == r, part, extra)
                rest = rest - part
            for i in range(2):
                qt_s[2 * kind + i, HD_A:HD_A + HALO, :] = extra.astype(BF16)

    acc_s[...] = jnp.zeros_like(acc_s)

    def key_rows(j):
        return pl.ds(pl.multiple_of(j * tk, tk), tk)

    bucket_vals = [rb_ref[bkt, hh] for bkt in range(REL_BUCKETS)]
    bias_max = functools.reduce(jnp.maximum, bucket_vals) * LOG2E
    bias_min = functools.reduce(jnp.minimum, bucket_vals) * LOG2E
    reach = (HD_A ** 0.5 * LOG2E) * jnp.max(jnp.abs(qg_ref[...])) * jnp.max(jnp.abs(kg_ref[...]))
    shift = reach + bias_max
    span = 2.0 * reach + (bias_max - bias_min)
    bounded = span <= LOGIT_SPAN_LIMIT

    @pl.when(bounded)
    def _():
        build_queries(shift)

        def numerators(j, kind, p_buf, bias=None):
            for i, k_ref in enumerate((k1_ref, k2_ref)):
                s = _dot(k_ref[key_rows(j), :], qt_s[2 * kind + i])
                if bias is not None:
                    s = s + bias
                p_buf[i] = jnp.exp2(s).astype(BF16)

        def accumulate(j, p_buf):
            vt = vt_ref[j]
            for i in range(2):
                acc_s[i] += _dot(vt, p_buf[i])

        w0 = jnp.clip(qi - 1, 0, nk - ATTN_WINDOW)
        far_tile = lambda t: jnp.where(t < w0, t, t + ATTN_WINDOW)
        far_kind = lambda t: jnp.where(t < w0, LEFT, RIGHT)
        p_bufs = (p0_s, p1_s)
        for r in range(ATTN_WINDOW):
            j = w0 + r
            dlt = j - qi
            near = jnp.abs(dlt) <= 1
            kind = jnp.where(near, NEAR, jnp.where(dlt < 0, LEFT, RIGHT))
            numerators(j, kind, p_bufs[r % 2], bias_ref[0, jnp.where(near, dlt + 1, 3)])
            if r > 0:
                accumulate(j - 1, p_bufs[(r - 1) % 2])
        last_window = w0 + ATTN_WINDOW - 1

        def pair(u, carry):
            t0 = 2 * u
            numerators(far_tile(t0), far_kind(t0), p0_s)
            accumulate(jnp.where(u == 0, last_window, far_tile(t0 - 1)), p1_s)
            numerators(far_tile(t0 + 1), far_kind(t0 + 1), p1_s)
            accumulate(far_tile(t0), p0_s)
            return carry

        n_far = nk - ATTN_WINDOW
        for u in range(n_far // 2):
            pair(u, 0)
        accumulate(far_tile(n_far - 1) if n_far else last_window, p1_s)

    @pl.when(jnp.logical_not(bounded))
    def _():
        _online_softmax_tiles(qi, nk, build_queries, key_rows, k1_ref, k2_ref, vt_ref, bias_ref,
                              qt_s, m_s, acc_s)

    lv = lam_ref[...]
    lam = (jnp.exp(jnp.sum(lv[0:1] * lv[1:2], axis=-1, keepdims=True))
           - jnp.exp(jnp.sum(lv[2:3] * lv[3:4], axis=-1, keepdims=True)) + li_ref[0, 0])
    a1 = acc_s[0]
    a2 = acc_s[1]
    o = a1[:vd] / a1[vd:vd + 1] - lam * (a2[:vd] / a2[vd:vd + 1])
    ms = jnp.mean(o * o, axis=0, keepdims=True)
    o = o * lax.rsqrt(ms + EPS) * sg_ref[0] * li_ref[0, 1]
    o_ref[...] = o.T.astype(o_ref.dtype)


def _online_softmax_tiles(qi, nk, build_queries, key_rows, k1_ref, k2_ref, vt_ref, bias_ref,
                          qt_s, m_s, acc_s):
    LEFT, RIGHT, NEAR = 0, 1, 2
    build_queries(0.0)
    m_s[...] = jnp.full_like(m_s, -jnp.inf)

    def one_tile(j, carry):
        dlt = j - qi
        near = jnp.abs(dlt) <= 1
        kind = jnp.where(near, NEAR, jnp.where(dlt < 0, LEFT, RIGHT))
        bias = bias_ref[0, jnp.where(near, dlt + 1, 3)]
        vt = vt_ref[j]
        for i, k_ref in enumerate((k1_ref, k2_ref)):
            s = _dot(k_ref[key_rows(j), :], qt_s[2 * kind + i]) + bias
            m_old = m_s[i]
            m_new = jnp.maximum(m_old, jnp.max(s, axis=0, keepdims=True))
            p = jnp.exp2(s - m_new).astype(BF16)
            acc_s[i] = jnp.exp2(m_old - m_new) * acc_s[i] + _dot(vt, p)
            m_s[i] = m_new
        return carry

    lax.fori_loop(0, nk, one_tile, 0)


def _attention(qn, k1, k2, vt, qg, kg, bias_t, rel_bias, lam_init, lambdas, sub_g_col, batch, seq, tile):
    nt = seq // tile
    vd = 2 * HD_A
    far = _far_bucket(tile)
    smem = pl.BlockSpec(memory_space=pltpu.SMEM)
    return pl.pallas_call(
        functools.partial(_attn_kernel, far=far),
        grid=(batch, N_A, nt),
        in_specs=[
            smem,
            smem,
            pl.BlockSpec((tile, vd), lambda b, h, i: (b * nt + i, h)),
            pl.BlockSpec((seq, vd), lambda b, h, i: (b, h)),
            pl.BlockSpec((seq, vd), lambda b, h, i: (b, h)),
            pl.BlockSpec((nt, VT_ROWS, tile), lambda b, h, i: (b, h, 0)),
            pl.BlockSpec((1, 2 * LANES), lambda b, h, i: (0, 0)),
            pl.BlockSpec((1, 2 * LANES), lambda b, h, i: (0, 0)),
            pl.BlockSpec((1, 4, tile, tile), lambda b, h, i: (h, 0, 0, 0)),
            pl.BlockSpec((4, HD_A), lambda b, h, i: (0, 0)),
            pl.BlockSpec((1, vd, 1), lambda b, h, i: (h, 0, 0)),
        ],
        out_specs=pl.BlockSpec((tile, vd), lambda b, h, i: (b * nt + i, h)),
        out_shape=jax.ShapeDtypeStruct((batch * seq, AW), BF16),
        scratch_shapes=[
            pltpu.VMEM((6, vd, tile), BF16),
            pltpu.VMEM((2, 1, tile), F32),
            pltpu.VMEM((2, VT_ROWS, tile), F32),
            pltpu.VMEM((2, tile, tile), BF16),
            pltpu.VMEM((2, tile, tile), BF16),
        ],
        compiler_params=pltpu.CompilerParams(
            dimension_semantics=("parallel", "parallel", "arbitrary"),
            vmem_limit_bytes=VMEM_LIMIT),
        name="diff_attention",
    )(rel_bias, lam_init, qn, k1, k2, vt, qg, kg, bias_t, lambdas, sub_g_col)


def _outffn_kernel(x_ref, hf_ref, hb_ref, og_ref, mb_ref, ng_ref, wo_ref, g2_ref, w1_ref, w2_ref,
                   out_ref, *, ff_chunk):
    hs = hf_ref[...].astype(F32) + hb_ref[...].astype(F32)
    x1 = x_ref[...] + _dot(mb_ref[...], wo_ref[MW:, :])
    for hh in range(N_M):
        sl = slice(hh * HD_M, (hh + 1) * HD_M)
        t = hs[:, sl]
        ms = jnp.mean(t * t, axis=-1, keepdims=True)
        mix = t * lax.rsqrt(ms + EPS) * ng_ref[:, sl] * _sigmoid(og_ref[:, sl].astype(F32))
        x1 = x1 + _dot(mix.astype(BF16), wo_ref[sl, :])
    ms = jnp.mean(x1 * x1, axis=-1, keepdims=True)
    h2 = (x1 * lax.rsqrt(ms + EPS) * g2_ref[...]).astype(BF16)
    out_ref[...] = x1
    for j in range(w1_ref.shape[1] // ff_chunk):
        sl = slice(j * ff_chunk, (j + 1) * ff_chunk)
        u = jnp.maximum(_dot(h2, w1_ref[:, sl]), 0.0)
        out_ref[...] += _dot((u * u).astype(BF16), w2_ref[sl, :])


def _outffn(x, hf, hb, om, mixb, ng, wo, g2, w1, w2, layer, tm):
    n, d = x.shape
    dff = w1.shape[2]
    row = lambda i: (i, 0)
    return pl.pallas_call(
        functools.partial(_outffn_kernel, ff_chunk=1024),
        grid=(n // tm,),
        in_specs=[
            pl.BlockSpec((tm, d), row),
            pl.BlockSpec((tm, MW), row),
            pl.BlockSpec((tm, MW), row),
            pl.BlockSpec((tm, MW), row),
            pl.BlockSpec((tm, AW), row),
            _resident((1, MW)),
            _resident_layer((MW + AW, d), layer),
            _resident((1, d)),
            _resident_layer((d, dff), layer),
            _resident_layer((dff, d), layer),
        ],
        out_specs=pl.BlockSpec((tm, d), row),
        out_shape=jax.ShapeDtypeStruct((n, d), F32),
        compiler_params=pltpu.CompilerParams(
            dimension_semantics=("parallel",), vmem_limit_bytes=VMEM_LIMIT),
        name="outproj_ffn",
    )(x, hf, hb, om, mixb, ng, wo, g2, w1, w2)


def kernel(x, norm1_g, w_in, conv_w, conv_b, gate_b, mlstm_norm_g, q_norm_g, k_norm_g, lambdas,
           diff_norm_g, rel_bias, w_out, norm2_g, w_ff1, w_ff2):
    batch, seq, d = x.shape
    depth = w_in.shape[0]
    n = batch * seq
    tm = min(512, n)
    tile = min(512, seq)
    assert seq % MLSTM_CHUNK == 0 and seq % tile == 0 and tm == tile
    assert seq // tile >= ATTN_WINDOW and (seq // tile - ATTN_WINDOW) % 2 == 0

    w_main = jnp.concatenate([w_in[:, :, :C_G], w_in[:, :, C_G + N_GATES:]], axis=-1).astype(BF16)
    w_gate = jnp.pad(w_in[:, :, C_G:C_G + N_GATES], ((0, 0), (0, 0), (0, LANES - N_GATES))).astype(BF16)
    w_out_b = w_out.astype(BF16)
    w_ff1_b = w_ff1.astype(BF16)
    w_ff2_b = w_ff2.astype(BF16)
    lane = jnp.arange(2 * LANES)
    bd = jnp.where((lane[:, None] // HD_A) == (lane[None, :] // HD_A), 1.0 / HD_A, 0.0).astype(BF16)

    bias_t = _bias_tiles(rel_bias.astype(F32), tile)

    xf = x.reshape(n, d)
    for l in range(depth):
        lam_init = 0.8 - 0.6 * math.exp(-0.3 * l)
        qg = jnp.tile(q_norm_g[l], 2 * LANES // HD_A)[None, :]
        kg = jnp.tile(k_norm_g[l], 2 * LANES // HD_A)[None, :]
        gb = jnp.pad(gate_b[l].reshape(1, N_GATES).astype(F32), ((0, 0), (0, LANES - N_GATES)))
        qtm, km, vtm, om, gate_cols, gate_rows, qn, k1, k2, vt = _inproj(
            xf, norm1_g[l][None, :], w_main, w_gate, l, gb, conv_w[l], conv_b[l][None, :],
            qg, kg, bd, seq, tm)
        hf, hb = _mlstm(qtm, km, vtm, gate_cols, gate_rows, batch, seq, tm)
        mixb = _attention(qn, k1, k2, vt, qg, kg, bias_t, rel_bias.astype(F32),
                          jnp.array([[lam_init, 1.0 - lam_init]], F32), lambdas[l],
                          diff_norm_g[l].reshape(N_A, 2 * HD_A, 1), batch, seq, tile)
        xf = _outffn(xf, hf, hb, om, mixb, mlstm_norm_g[l][None, :], w_out_b,
                     norm2_g[l][None, :], w_ff1_b, w_ff2_b, l, tm)
    return xf.reshape(batch, seq, d)
```

```python
import functools
import math

import jax
import jax.numpy as jnp
from jax import lax
from jax.experimental import pallas as pl
from jax.experimental.pallas import tpu as pltpu

F32 = jnp.float32
BF16 = jnp.bfloat16

N_M = 4
HD_M = 128
MW = N_M * HD_M
MLSTM_CHUNK = 128
CONV_W = 5
N_A = 4
HD_A = 64
AW = N_A * 2 * HD_A
REL_BUCKETS = 32
REL_MAX_DIST = 128
EPS = 1e-6
N_GATES = 4 * N_M
C_G = 4 * MW
LANES = 128
HALO = 16
VMEM_LIMIT = 56 * 1024 * 1024
LOG2E = math.log2(math.e)
VT_ROWS = 2 * HD_A + HALO
BIAS_ROWS = 3
LOGIT_SPAN_LIMIT = 96.0
ATTN_WINDOW = 4
ATTN_Q_TILES = 2

def _dot(a, b):
    return jnp.dot(a, b, preferred_element_type=F32)


def _dot_nt(a, b):
    return lax.dot_general(a, b, (((1,), (1,)), ((), ())), preferred_element_type=F32)


def _dot_tn(a, b):
    return lax.dot_general(a, b, (((0,), (0,)), ((), ())), preferred_element_type=F32)


def _dot_exact(a, b):
    return jnp.dot(a, b, preferred_element_type=F32, precision=lax.Precision.HIGHEST)


def _sigmoid(x):
    return 1.0 / (1.0 + jnp.exp(-x))


def _log_sigmoid(x):
    return jnp.minimum(x, 0.0) - jnp.log1p(jnp.exp(-jnp.abs(x)))


def _resident(shape):
    nd = len(shape)
    return pl.BlockSpec(shape, lambda *_: (0,) * nd, pipeline_mode=pl.Buffered(1))


def _resident_layer(shape, layer):
    nd = len(shape)
    return pl.BlockSpec((None,) + tuple(shape), lambda *_: (layer,) + (0,) * nd,
                        pipeline_mode=pl.Buffered(1))


def _inproj_kernel(xp_ref, x_ref, xn_ref, g1_ref, w_ref, wg_ref, gb_ref, cw_ref, cb_ref, qg_ref, kg_ref,
                   bd_ref,
                   qtm_ref, km_ref, vtm_ref, om_ref, gcol_ref, grow_ref, qn_ref, k1_ref, k2_ref, vt_ref,
                   ext_s, *, tiles_per_seq):
    rows = x_ref.shape[0]
    pos = pl.program_id(0) % tiles_per_seq

    def normed(xv):
        ms = jnp.mean(xv * xv, axis=-1, keepdims=True)
        return (xv * lax.rsqrt(ms + EPS) * g1_ref[...]).astype(BF16)

    x = x_ref[...]
    h = normed(x)

    def proj(c0, width):
        return _dot(h, w_ref[:, c0:c0 + width])

    h_ext = jnp.concatenate([normed(xp_ref[...]), h, normed(xn_ref[...])], axis=0)
    rid = lax.broadcasted_iota(jnp.int32, (rows + 2 * HALO, 1), 0)
    inside = jnp.logical_and(jnp.logical_or(rid >= HALO, pos > 0),
                             jnp.logical_or(rid < HALO + rows, pos < tiles_per_seq - 1))
    ext_s[...] = jnp.where(inside, _dot(h_ext, w_ref[:, 0:2 * MW]), 0.0)

    def conv_head(cb):
        cols = slice(cb * LANES, (cb + 1) * LANES)
        conv = cb_ref[:, cols]
        for j in range(CONV_W):
            off = HALO - CONV_W // 2 + j
            conv = conv + ext_s[off:off + rows, cols] * cw_ref[j:j + 1, cols]
        y = conv * _sigmoid(conv)
        if cb < N_M:
            qtm_ref[cols, :] = y.T.astype(BF16)
        else:
            km_ref[:, cb * LANES - MW:(cb + 1) * LANES - MW] = (y * (HD_M ** -0.5)).astype(BF16)

    ones_rows = jnp.ones((HALO, rows), BF16)
    pair = 2 * LANES

    def values_t(c0, out_ref, hp):
        v2 = proj(c0 + hp * pair, pair)
        for sub in range(2):
            hh = 2 * hp + sub
            out_ref[0, hh * VT_ROWS:hh * VT_ROWS + HD_M, :] = (
                v2[:, sub * LANES:(sub + 1) * LANES].T.astype(BF16))
            out_ref[0, hh * VT_ROWS + HD_M:(hh + 1) * VT_ROWS, :] = ones_rows

    def out_gate():
        om_ref[...] = proj(3 * MW, MW).astype(BF16)
        L = MLSTM_CHUNK
        gt = (_dot(h, wg_ref[...]) + gb_ref[...]).T[:N_GATES]
        logf = _log_sigmoid(gt)
        g_under_f = pltpu.roll(gt, N_M, 0)
        upper_f = (lax.broadcasted_iota(jnp.int32, (L, L), 0)
                   <= lax.broadcasted_iota(jnp.int32, (L, L), 1)).astype(F32)
        forward_rows = lax.broadcasted_iota(jnp.int32, (N_GATES, L), 0) < 2 * N_M
        for r in range(rows // L):
            sl = slice(r * L, (r + 1) * L)
            cum = _dot_exact(logf[:, sl], upper_f)
            suf = cum[:, L - 1:L] - cum + logf[:, sl]
            grow_ref[0:N_GATES, sl] = cum
            grow_ref[N_GATES:2 * N_GATES, sl] = gt[:, sl]
            grow_ref[2 * N_GATES:3 * N_GATES, sl] = suf
            src = jnp.where(forward_rows, cum, suf) - g_under_f[:, sl]
            src = jnp.concatenate([src, jnp.zeros((L - N_GATES, L), F32)], axis=0)
            gcol_ref[sl, :] = src.T[:, :N_GATES]

    bd = bd_ref[...]

    def qk_norm(c0, g_ref):
        t = proj(c0, pair)
        msq = _dot((t * t).astype(BF16), bd)
        return t * lax.rsqrt(msq + EPS) * g_ref[...]

    lane = lax.broadcasted_iota(jnp.int32, (rows, pair), 1) % LANES
    ones_cols = jnp.where(lane < HD_A + BIAS_ROWS, 1.0, 0.0)

    def attn_q(hp):
        sl = slice(hp * pair, (hp + 1) * pair)
        qn_ref[:, sl] = (qk_norm(4 * MW + hp * pair, qg_ref) * (HD_A ** -0.5 * LOG2E)).astype(BF16)

    def attn_k(hp):
        sl = slice(hp * pair, (hp + 1) * pair)
        kn = qk_norm(4 * MW + AW + hp * pair, kg_ref)
        k1_ref[:, sl] = jnp.where(lane < HD_A, kn, ones_cols).astype(BF16)
        k2_ref[:, sl] = jnp.where(lane < HD_A, pltpu.roll(kn, pair - HD_A, 1), ones_cols).astype(BF16)

    projections = [
        functools.partial(values_t, 2 * MW, vtm_ref, 0), functools.partial(values_t, 2 * MW, vtm_ref, 1),
        out_gate,
        functools.partial(attn_q, 0), functools.partial(attn_k, 0),
        functools.partial(attn_q, 1), functools.partial(attn_k, 1),
        functools.partial(values_t, 4 * MW + 2 * AW, vt_ref, 0),
    ]
    for cb, projection in enumerate(projections):
        conv_head(cb)
        projection()
    values_t(4 * MW + 2 * AW, vt_ref, 1)


def _inproj(x, g1, w_main, w_gate, layer, gate_b, conv_w, conv_b, qg, kg, bd, seq, tm):
    n, d = x.shape
    wcols = w_main.shape[2]
    hpt = tm // HALO
    nhalo = n // HALO
    row = lambda i: (i, 0)
    vt_spec = pl.BlockSpec((1, N_A * VT_ROWS, tm), lambda i: (i, 0, 0))
    vt_shape = jax.ShapeDtypeStruct((n // tm, N_A * VT_ROWS, tm), BF16)
    return pl.pallas_call(
        functools.partial(_inproj_kernel, tiles_per_seq=seq // tm),
        grid=(n // tm,),
        in_specs=[
            pl.BlockSpec((HALO, d), lambda i: (jnp.maximum(i * hpt - 1, 0), 0)),
            pl.BlockSpec((tm, d), row),
            pl.BlockSpec((HALO, d), lambda i: (jnp.minimum((i + 1) * hpt, nhalo - 1), 0)),
            _resident((1, d)),
            _resident_layer((d, wcols), layer),
            _resident_layer((d, LANES), layer),
            _resident((1, LANES)),
            _resident((CONV_W, 2 * MW)),
            _resident((1, 2 * MW)),
            _resident((1, 2 * LANES)),
            _resident((1, 2 * LANES)),
            _resident((2 * LANES, 2 * LANES)),
        ],
        out_specs=[
            pl.BlockSpec((MW, tm), lambda i: (0, i)),
            pl.BlockSpec((tm, MW), row),
            vt_spec,
            pl.BlockSpec((tm, MW), row),
            pl.BlockSpec((tm, N_GATES), row),
            pl.BlockSpec((3 * N_GATES, tm), lambda i: (0, i)),
            pl.BlockSpec((tm, AW), row),
            pl.BlockSpec((tm, AW), row),
            pl.BlockSpec((tm, AW), row),
            vt_spec,
        ],
        out_shape=[
            jax.ShapeDtypeStruct((MW, n), BF16),
            jax.ShapeDtypeStruct((n, MW), BF16),
            vt_shape,
            jax.ShapeDtypeStruct((n, MW), BF16),
            jax.ShapeDtypeStruct((n, N_GATES), F32),
            jax.ShapeDtypeStruct((3 * N_GATES, n), F32),
            jax.ShapeDtypeStruct((n, AW), BF16),
            jax.ShapeDtypeStruct((n, AW), BF16),
            jax.ShapeDtypeStruct((n, AW), BF16),
            vt_shape,
        ],
        scratch_shapes=[pltpu.VMEM((tm + 2 * HALO, 2 * MW), F32)],
        compiler_params=pltpu.CompilerParams(
            dimension_semantics=("parallel",), vmem_limit_bytes=VMEM_LIMIT),
        name="inproj",
    )(x, x, x, g1, w_main, w_gate, gate_b, conv_w, conv_b, qg, kg, bd)


def _mlstm_kernel(qt_f, k_f, vt_f, gc_f, gr_f,
                  qt_b, k_b, vt_b, gc_b, gr_b,
                  hf_ref, hb_ref,
                  c_s, m_s):
    L = MLSTM_CHUNK
    chunks = k_f.shape[0] // L

    @pl.when(pl.program_id(1) == 0)
    def _():
        c_s[...] = jnp.zeros_like(c_s)
        m_s[...] = jnp.zeros_like(m_s)

    row = lax.broadcasted_iota(jnp.int32, (L, L), 0)
    col = lax.broadcasted_iota(jnp.int32, (L, L), 1)
    lower = col <= row
    upper = row <= col
    dirs = ((qt_f, k_f, vt_f, gc_f, gr_f, hf_ref), (qt_b, k_b, vt_b, gc_b, gr_b, hb_ref))
    state = [(c_s[idx], m_s[idx][:, :1]) for idx in range(2 * N_M)]

    for step in range(chunks):
        chains = []
        for d, (qt_ref, k_ref, vt_ref, gc_ref, gr_ref, out_ref) in enumerate(dirs):
            tok = slice(step * L, (step + 1) * L) if d == 0 else slice((chunks - 1 - step) * L,
                                                                       (chunks - step) * L)
            i_off = 2 * N_M * d
            f_off = i_off + N_M
            src = gc_ref[tok, :]
            cum_rows = gr_ref[0:N_GATES, tok] if d == 0 else gr_ref[2 * N_GATES:3 * N_GATES, tok]
            gate_rows = gr_ref[N_GATES:2 * N_GATES, tok]
            for hh in range(N_M):
                idx = d * N_M + hh
                chains.append(dict(
                    idx=idx, out_ref=out_ref, tok=tok, cols=slice(hh * HD_M, (hh + 1) * HD_M),
                    mask=upper if d == 0 else lower,
                    src_col=src[:, f_off + hh:f_off + hh + 1],
                    b_row=cum_rows[f_off + hh:f_off + hh + 1, :],
                    i_row=gate_rows[i_off + hh:i_off + hh + 1, :],
                    end=slice(L - 1, L) if d == 0 else slice(0, 1),
                    qt=qt_ref[hh * HD_M:(hh + 1) * HD_M, tok],
                    kb=k_ref[tok, hh * HD_M:(hh + 1) * HD_M],
                    vt=vt_ref[0, hh * VT_ROWS:(hh + 1) * VT_ROWS, tok],
                    c_prev=state[idx][0], m_prev=state[idx][1],
                ))

        for ch in chains:
            ch["s"] = _dot(ch["kb"], ch["qt"])
            ch["inter_num"] = _dot(ch["c_prev"].astype(BF16), ch["qt"])
        for ch in chains:
            dmat = jnp.where(ch["mask"], ch["b_row"] - ch["src_col"], -jnp.inf)
            inter = ch["b_row"] + ch["m_prev"]
            ch["m_t"] = jnp.maximum(inter, jnp.max(dmat, axis=0, keepdims=True))
            ch["scale"] = jnp.exp(inter - ch["m_t"])
            ch["w"] = (ch["s"] * jnp.exp(dmat - ch["m_t"])).astype(BF16)
        for ch in chains:
            num = ch["scale"] * ch["inter_num"] + _dot(ch["vt"], ch["w"])
            den = num[HD_M:HD_M + 1]
            ht = num[:HD_M] / jnp.maximum(jnp.abs(den), jnp.exp(-ch["m_t"]))
            ch["out_ref"][ch["tok"], ch["cols"]] = ht.T.astype(ch["out_ref"].dtype)
        for ch in chains:
            b_end = ch["b_row"][:, ch["end"]]
            g_row = b_end - ch["b_row"] + ch["i_row"]
            m_new = jnp.maximum(b_end + ch["m_prev"], jnp.max(g_row, axis=-1, keepdims=True))
            decay = jnp.exp(b_end + ch["m_prev"] - m_new)
            vt_w = (ch["vt"].astype(F32) * jnp.exp(g_row - m_new)).astype(BF16)
            state[ch["idx"]] = (decay * ch["c_prev"] + _dot(vt_w, ch["kb"]), m_new)

    for idx, (c_new, m_new) in enumerate(state):
        c_s[idx] = c_new
        m_s[idx] = jnp.broadcast_to(m_new, (1, LANES))


def _mlstm(qt, k, vt, gate_cols, gate_rows, batch, seq, tm):
    nt = seq // tm

    def tile_f(b, c):
        return b * nt + c

    def tile_b(b, c):
        return b * nt + (nt - 1 - c)

    def specs(tile):
        return [
            pl.BlockSpec((MW, tm), lambda b, c: (0, tile(b, c))),
            pl.BlockSpec((tm, MW), lambda b, c: (tile(b, c), 0)),
            pl.BlockSpec((1, N_M * VT_ROWS, tm), lambda b, c: (tile(b, c), 0, 0)),
            pl.BlockSpec((tm, N_GATES), lambda b, c: (tile(b, c), 0)),
            pl.BlockSpec((3 * N_GATES, tm), lambda b, c: (0, tile(b, c))),
        ]

    n = batch * seq
    return pl.pallas_call(
        _mlstm_kernel,
        grid=(batch, nt),
        in_specs=specs(tile_f) + specs(tile_b),
        out_specs=[
            pl.BlockSpec((tm, MW), lambda b, c: (tile_f(b, c), 0)),
            pl.BlockSpec((tm, MW), lambda b, c: (tile_b(b, c), 0)),
        ],
        out_shape=[jax.ShapeDtypeStruct((n, MW), BF16)] * 2,
        scratch_shapes=[
            pltpu.VMEM((2 * N_M, VT_ROWS, HD_M), F32),
            pltpu.VMEM((2 * N_M, 1, LANES), F32),
        ],
        compiler_params=pltpu.CompilerParams(
            dimension_semantics=("parallel", "arbitrary"), vmem_limit_bytes=VMEM_LIMIT),
        name="mlstm",
    )(qt, k, vt, gate_cols, gate_rows, qt, k, vt, gate_cols, gate_rows)


def _bias_kernel(rb_ref, out_ref, *, tile):
    hh = pl.program_id(0)
    dd = pl.program_id(1)
    kk = lax.broadcasted_iota(jnp.int32, (tile, tile), 0)
    qq = lax.broadcasted_iota(jnp.int32, (tile, tile), 1)
    rel = (dd - 1) * tile + kk - qq
    nb = REL_BUCKETS // 2
    max_exact = nb // 2
    n = jnp.abs(rel)
    nf = jnp.maximum(n, 1).astype(F32)
    large = max_exact + (jnp.log(nf / max_exact) / math.log(REL_MAX_DIST / max_exact)
                         * (nb - max_exact)).astype(jnp.int32)
    large = jnp.minimum(large, nb - 1)
    bucket = jnp.where(rel > 0, nb, 0) + jnp.where(n < max_exact, n, large)
    val = jnp.zeros((tile, tile), F32)
    for bkt in range(REL_BUCKETS):
        val = jnp.where(bucket == bkt, rb_ref[bkt, hh], val)
    out_ref[0, 0] = jnp.where(dd < 3, val * LOG2E, 0.0)


def _bias_tiles(rel_bias, tile):
    return pl.pallas_call(
        functools.partial(_bias_kernel, tile=tile),
        grid=(N_A, 4),
        in_specs=[pl.BlockSpec(memory_space=pltpu.SMEM)],
        out_specs=pl.BlockSpec((1, 1, tile, tile), lambda h, d: (h, d, 0, 0)),
        out_shape=jax.ShapeDtypeStruct((N_A, 4, tile, tile), F32),
        name="rel_bias_tiles",
    )(rel_bias)


def _far_bucket(tile):
    assert tile >= REL_MAX_DIST
    return REL_BUCKETS // 2 - 1


def _attn_kernel(rb_ref, li_ref, q_ref, k1_ref, k2_ref, vt_ref, qg_ref, kg_ref, bias_ref, lam_ref, sg_ref,
                 o_ref,
                 qt_s, m_s, acc_s, p0_s, p1_s, *, far):
    tq = vt_ref.shape[2]
    for sub in range(q_ref.shape[0] // tq):
        rows = pl.ds(sub * tq, tq)
        _attn_query_tile(pl.program_id(2) * (q_ref.shape[0] // tq) + sub,
                         rb_ref, li_ref, q_ref.at[rows], k1_ref, k2_ref, vt_ref, qg_ref, kg_ref,
                         bias_ref, lam_ref, sg_ref, o_ref.at[rows],
                         qt_s.at[sub], m_s.at[sub], acc_s.at[sub], p0_s.at[sub], p1_s.at[sub], far=far)


def _attn_query_tile(qi, rb_ref, li_ref, q_ref, k1_ref, k2_ref, vt_ref, qg_ref, kg_ref, bias_ref,
                     lam_ref, sg_ref, o_ref, qt_s, m_s, acc_s, p0_s, p1_s, *, far):
    hh = pl.program_id(1)
    tq = q_ref.shape[0]
    tk = vt_ref.shape[2]
    nk = vt_ref.shape[0]
    vd = 2 * HD_A
    LEFT, RIGHT, NEAR = 0, 1, 2

    qt = q_ref[...].astype(F32).T
    rid = lax.broadcasted_iota(jnp.int32, qt.shape, 0)
    bases = (qt, jnp.concatenate([qt[HD_A:], qt[:HD_A]], axis=0))
    consts = (rb_ref[far, hh] * LOG2E, rb_ref[REL_BUCKETS // 2 + far, hh] * LOG2E, 0.0)

    def build_queries(shift):
        for i, base in enumerate(bases):
            body = jnp.where(rid < HD_A, base, 0.0).astype(BF16)
            for kind in range(len(consts)):
                qt_s[2 * kind + i] = body
        rid_o = lax.broadcasted_iota(jnp.int32, (HALO, tq), 0)
        for kind, cst in enumerate(consts):
            rest = jnp.full((HALO, tq), cst - shift, F32)
            extra = jnp.zeros((HALO, tq), F32)
            for r in range(BIAS_ROWS):
                part = rest.astype(BF16).astype(F32)
                extra = jnp.where(rid_o == r, part, extra)
                rest = rest - part
            for i in range(2):
                qt_s[2 * kind + i, HD_A:HD_A + HALO, :] = extra.astype(BF16)

    acc_s[...] = jnp.zeros_like(acc_s)

    def key_rows(j):
        return pl.ds(pl.multiple_of(j * tk, tk), tk)

    bucket_vals = [rb_ref[bkt, hh] for bkt in range(REL_BUCKETS)]
    bias_max = functools.reduce(jnp.maximum, bucket_vals) * LOG2E
    bias_min = functools.reduce(jnp.minimum, bucket_vals) * LOG2E
    reach = (HD_A ** 0.5 * LOG2E) * jnp.max(jnp.abs(qg_ref[...])) * jnp.max(jnp.abs(kg_ref[...]))
    shift = reach + bias_max
    span = 2.0 * reach + (bias_max - bias_min)
    bounded = span <= LOGIT_SPAN_LIMIT

    @pl.when(bounded)
    def _():
        build_queries(shift)

        def numerators(j, kind, p_buf, bias=None):
            for i, k_ref in enumerate((k1_ref, k2_ref)):
                s = _dot(k_ref[key_rows(j), :], qt_s[2 * kind + i])
                if bias is not None:
                    s = s + bias
                p_buf[i] = jnp.exp2(s).astype(BF16)

        def accumulate(j, p_buf):
            vt = vt_ref[j]
            for i in range(2):
                acc_s[i] += _dot(vt, p_buf[i])

        w0 = jnp.clip(qi - 1, 0, nk - ATTN_WINDOW)
        far_tile = lambda t: jnp.where(t < w0, t, t + ATTN_WINDOW)
        far_kind = lambda t: jnp.where(t < w0, LEFT, RIGHT)
        p_bufs = (p0_s, p1_s)
        for r in range(ATTN_WINDOW):
            j = w0 + r
            dlt = j - qi
            near = jnp.abs(dlt) <= 1
            kind = jnp.where(near, NEAR, jnp.where(dlt < 0, LEFT, RIGHT))
            numerators(j, kind, p_bufs[r % 2], bias_ref[0, jnp.where(near, dlt + 1, 3)])
            if r > 0:
                accumulate(j - 1, p_bufs[(r - 1) % 2])
        last_window = w0 + ATTN_WINDOW - 1

        def pair(u, carry):
            t0 = 2 * u
            numerators(far_tile(t0), far_kind(t0), p0_s)
            accumulate(jnp.where(u == 0, last_window, far_tile(t0 - 1)), p1_s)
            numerators(far_tile(t0 + 1), far_kind(t0 + 1), p1_s)
            accumulate(far_tile(t0), p0_s)
            return carry

        n_far = nk - ATTN_WINDOW
        for u in range(n_far // 2):
            pair(u, 0)
        accumulate(far_tile(n_far - 1) if n_far else last_window, p1_s)

    @pl.when(jnp.logical_not(bounded))
    def _():
        _online_softmax_tiles(qi, nk, build_queries, key_rows, k1_ref, k2_ref, vt_ref, bias_ref,
                              qt_s, m_s, acc_s)

    lv = lam_ref[...]
    lam = (jnp.exp(jnp.sum(lv[0:1] * lv[1:2], axis=-1, keepdims=True))
           - jnp.exp(jnp.sum(lv[2:3] * lv[3:4], axis=-1, keepdims=True)) + li_ref[0, 0])
    a1 = acc_s[0]
    a2 = acc_s[1]
    o = a1[:vd] / a1[vd:vd + 1] - lam * (a2[:vd] / a2[vd:vd + 1])
    ms = jnp.mean(o * o, axis=0, keepdims=True)
    o = o * lax.rsqrt(ms + EPS) * sg_ref[0] * li_ref[0, 1]
    o_ref[...] = o.T.astype(o_ref.dtype)


def _online_softmax_tiles(qi, nk, build_queries, key_rows, k1_ref, k2_ref, vt_ref, bias_ref,
                          qt_s, m_s, acc_s):
    LEFT, RIGHT, NEAR = 0, 1, 2
    build_queries(0.0)
    m_s[...] = jnp.full_like(m_s, -jnp.inf)

    def one_tile(j, carry):
        dlt = j - qi
        near = jnp.abs(dlt) <= 1
        kind = jnp.where(near, NEAR, jnp.where(dlt < 0, LEFT, RIGHT))
        bias = bias_ref[0, jnp.where(near, dlt + 1, 3)]
        vt = vt_ref[j]
        for i, k_ref in enumerate((k1_ref, k2_ref)):
            s = _dot(k_ref[key_rows(j), :], qt_s[2 * kind + i]) + bias
            m_old = m_s[i]
            m_new = jnp.maximum(m_old, jnp.max(s, axis=0, keepdims=True))
            p = jnp.exp2(s - m_new).astype(BF16)
            acc_s[i] = jnp.exp2(m_old - m_new) * acc_s[i] + _dot(vt, p)
            m_s[i] = m_new
        return carry

    lax.fori_loop(0, nk, one_tile, 0)


def _attention(qn, k1, k2, vt, qg, kg, bias_t, rel_bias, lam_init, lambdas, sub_g_col, batch, seq, tile):
    nt = seq // tile
    nq = nt // ATTN_Q_TILES
    vd = 2 * HD_A
    far = _far_bucket(tile)
    smem = pl.BlockSpec(memory_space=pltpu.SMEM)
    q_rows = ATTN_Q_TILES * tile
    return pl.pallas_call(
        functools.partial(_attn_kernel, far=far),
        grid=(batch, N_A, nq),
        in_specs=[
            smem,
            smem,
            pl.BlockSpec((q_rows, vd), lambda b, h, i: (b * nq + i, h)),
            pl.BlockSpec((seq, vd), lambda b, h, i: (b, h)),
            pl.BlockSpec((seq, vd), lambda b, h, i: (b, h)),
            pl.BlockSpec((nt, VT_ROWS, tile), lambda b, h, i: (b, h, 0)),
            pl.BlockSpec((1, 2 * LANES), lambda b, h, i: (0, 0)),
            pl.BlockSpec((1, 2 * LANES), lambda b, h, i: (0, 0)),
            pl.BlockSpec((1, 4, tile, tile), lambda b, h, i: (h, 0, 0, 0)),
            pl.BlockSpec((4, HD_A), lambda b, h, i: (0, 0)),
            pl.BlockSpec((1, vd, 1), lambda b, h, i: (h, 0, 0)),
        ],
        out_specs=pl.BlockSpec((q_rows, vd), lambda b, h, i: (b * nq + i, h)),
        out_shape=jax.ShapeDtypeStruct((batch * seq, AW), BF16),
        scratch_shapes=[
            pltpu.VMEM((ATTN_Q_TILES, 6, vd, tile), BF16),
            pltpu.VMEM((ATTN_Q_TILES, 2, 1, tile), F32),
            pltpu.VMEM((ATTN_Q_TILES, 2, VT_ROWS, tile), F32),
            pltpu.VMEM((ATTN_Q_TILES, 2, tile, tile), BF16),
            pltpu.VMEM((ATTN_Q_TILES, 2, tile, tile), BF16),
        ],
        compiler_params=pltpu.CompilerParams(
            dimension_semantics=("parallel", "parallel", "arbitrary"),
            vmem_limit_bytes=VMEM_LIMIT),
        name="diff_attention",
    )(rel_bias, lam_init, qn, k1, k2, vt, qg, kg, bias_t, lambdas, sub_g_col)


def _outffn_kernel(x_ref, hf_ref, hb_ref, og_ref, mb_ref, ng_ref, wo_ref, g2_ref, w1_ref, w2_ref,
                   out_ref, *, ff_chunk):
    hs = hf_ref[...].astype(F32) + hb_ref[...].astype(F32)
    x1 = x_ref[...] + _dot(mb_ref[...], wo_ref[MW:, :])
    for hh in range(N_M):
        sl = slice(hh * HD_M, (hh + 1) * HD_M)
        t = hs[:, sl]
        ms = jnp.mean(t * t, axis=-1, keepdims=True)
        mix = t * lax.rsqrt(ms + EPS) * ng_ref[:, sl] * _sigmoid(og_ref[:, sl].astype(F32))
        x1 = x1 + _dot(mix.astype(BF16), wo_ref[sl, :])
    ms = jnp.mean(x1 * x1, axis=-1, keepdims=True)
    h2 = (x1 * lax.rsqrt(ms + EPS) * g2_ref[...]).astype(BF16)
    out_ref[...] = x1
    for j in range(w1_ref.shape[1] // ff_chunk):
        sl = slice(j * ff_chunk, (j + 1) * ff_chunk)
        u = jnp.maximum(_dot(h2, w1_ref[:, sl]), 0.0)
        out_ref[...] += _dot((u * u).astype(BF16), w2_ref[sl, :])


def _outffn(x, hf, hb, om, mixb, ng, wo, g2, w1, w2, layer, tm):
    n, d = x.shape
    dff = w1.shape[2]
    row = lambda i: (i, 0)
    return pl.pallas_call(
        functools.partial(_outffn_kernel, ff_chunk=1024),
        grid=(n // tm,),
        in_specs=[
            pl.BlockSpec((tm, d), row),
            pl.BlockSpec((tm, MW), row),
            pl.BlockSpec((tm, MW), row),
            pl.BlockSpec((tm, MW), row),
            pl.BlockSpec((tm, AW), row),
            _resident((1, MW)),
            _resident_layer((MW + AW, d), layer),
            _resident((1, d)),
            _resident_layer((d, dff), layer),
            _resident_layer((dff, d), layer),
        ],
        out_specs=pl.BlockSpec((tm, d), row),
        out_shape=jax.ShapeDtypeStruct((n, d), F32),
        compiler_params=pltpu.CompilerParams(
            dimension_semantics=("parallel",), vmem_limit_bytes=VMEM_LIMIT),
        name="outproj_ffn",
    )(x, hf, hb, om, mixb, ng, wo, g2, w1, w2)


def kernel(x, norm1_g, w_in, conv_w, conv_b, gate_b, mlstm_norm_g, q_norm_g, k_norm_g, lambdas,
           diff_norm_g, rel_bias, w_out, norm2_g, w_ff1, w_ff2):
    batch, seq, d = x.shape
    depth = w_in.shape[0]
    n = batch * seq
    tm = min(512, n)
    tile = min(512, seq)
    assert seq % MLSTM_CHUNK == 0 and seq % tile == 0 and tm == tile
    assert seq // tile >= ATTN_WINDOW and (seq // tile - ATTN_WINDOW) % 2 == 0
    assert (seq // tile) % ATTN_Q_TILES == 0

    w_main = jnp.concatenate([w_in[:, :, :C_G], w_in[:, :, C_G + N_GATES:]], axis=-1).astype(BF16)
    w_gate = jnp.pad(w_in[:, :, C_G:C_G + N_GATES], ((0, 0), (0, 0), (0, LANES - N_GATES))).astype(BF16)
    w_out_b = w_out.astype(BF16)
    w_ff1_b = w_ff1.astype(BF16)
    w_ff2_b = w_ff2.astype(BF16)
    lane = jnp.arange(2 * LANES)
    bd = jnp.where((lane[:, None] // HD_A) == (lane[None, :] // HD_A), 1.0 / HD_A, 0.0).astype(BF16)

    bias_t = _bias_tiles(rel_bias.astype(F32), tile)

    xf = x.reshape(n, d)
    for l in range(depth):
        lam_init = 0.8 - 0.6 * math.exp(-0.3 * l)
        qg = jnp.tile(q_norm_g[l], 2 * LANES // HD_A)[None, :]
        kg = jnp.tile(k_norm_g[l], 2 * LANES // HD_A)[None, :]
        gb = jnp.pad(gate_b[l].reshape(1, N_GATES).astype(F32), ((0, 0), (0, LANES - N_GATES)))
        qtm, km, vtm, om, gate_cols, gate_rows, qn, k1, k2, vt = _inproj(
            xf, norm1_g[l][None, :], w_main, w_gate, l, gb, conv_w[l], conv_b[l][None, :],
            qg, kg, bd, seq, tm)
        hf, hb = _mlstm(qtm, km, vtm, gate_cols, gate_rows, batch, seq, tm)
        mixb = _attention(qn, k1, k2, vt, qg, kg, bias_t, rel_bias.astype(F32),
                          jnp.array([[lam_init, 1.0 - lam_init]], F32), lambdas[l],
                          diff_norm_g[l].reshape(N_A, 2 * HD_A, 1), batch, seq, tile)
        xf = _outffn(xf, hf, hb, om, mixb, mlstm_norm_g[l][None, :], w_out_b,
                     norm2_g[l][None, :], w_ff1_b, w_ff2_b, l, tm)
    return xf.reshape(batch, seq, d)
```

```python
import functools
import math

import jax
import jax.numpy as jnp
from jax import lax
from jax.experimental import pallas as pl
from jax.experimental.pallas import tpu as pltpu

F32 = jnp.float32
BF16 = jnp.bfloat16

N_M = 4
HD_M = 128
MW = N_M * HD_M
MLSTM_CHUNK = 128
CONV_W = 5
N_A = 4
HD_A = 64
AW = N_A * 2 * HD_A
REL_BUCKETS = 32
REL_MAX_DIST = 128
EPS = 1e-6
N_GATES = 4 * N_M
C_G = 4 * MW
LANES = 128
HALO = 16
VMEM_LIMIT = 56 * 1024 * 1024
LOG2E = math.log2(math.e)
VT_ROWS = 2 * HD_A + HALO
BIAS_ROWS = 3
LOGIT_SPAN_LIMIT = 96.0
ATTN_WINDOW = 4
ATTN_Q_TILES = 2

def _dot(a, b):
    return jnp.dot(a, b, preferred_element_type=F32)


def _dot_nt(a, b):
    return lax.dot_general(a, b, (((1,), (1,)), ((), ())), preferred_element_type=F32)


def _dot_tn(a, b):
    return lax.dot_general(a, b, (((0,), (0,)), ((), ())), preferred_element_type=F32)


def _dot_exact(a, b):
    return jnp.dot(a, b, preferred_element_type=F32, precision=lax.Precision.HIGHEST)


def _sigmoid(x):
    return 1.0 / (1.0 + jnp.exp(-x))


def _log_sigmoid(x):
    return jnp.minimum(x, 0.0) - jnp.log1p(jnp.exp(-jnp.abs(x)))


def _resident(shape):
    nd = len(shape)
    return pl.BlockSpec(shape, lambda *_: (0,) * nd, pipeline_mode=pl.Buffered(1))


def _resident_layer(shape, layer):
    nd = len(shape)
    return pl.BlockSpec((None,) + tuple(shape), lambda *_: (layer,) + (0,) * nd,
                        pipeline_mode=pl.Buffered(1))


def _inproj_kernel(xp_ref, x_ref, xn_ref, g1_ref, w_ref, wg_ref, gb_ref, cw_ref, cb_ref, qg_ref, kg_ref,
                   bd_ref,
                   qtm_ref, km_ref, vtm_ref, om_ref, gcol_ref, grow_ref, qn_ref, k1_ref, k2_ref, vt_ref,
                   ext_s, *, tiles_per_seq):
    rows = x_ref.shape[0]
    pos = pl.program_id(0) % tiles_per_seq

    def normed(xv):
        ms = jnp.mean(xv * xv, axis=-1, keepdims=True)
        return (xv * lax.rsqrt(ms + EPS) * g1_ref[...]).astype(BF16)

    x = x_ref[...]
    h = normed(x)

    def proj(c0, width):
        return _dot(h, w_ref[:, c0:c0 + width])

    h_ext = jnp.concatenate([normed(xp_ref[...]), h, normed(xn_ref[...])], axis=0)
    rid = lax.broadcasted_iota(jnp.int32, (rows + 2 * HALO, 1), 0)
    inside = jnp.logical_and(jnp.logical_or(rid >= HALO, pos > 0),
                             jnp.logical_or(rid < HALO + rows, pos < tiles_per_seq - 1))
    ext_s[...] = jnp.where(inside, _dot(h_ext, w_ref[:, 0:2 * MW]), 0.0)

    def conv_head(cb):
        cols = slice(cb * LANES, (cb + 1) * LANES)
        conv = cb_ref[:, cols]
        for j in range(CONV_W):
            off = HALO - CONV_W // 2 + j
            conv = conv + ext_s[off:off + rows, cols] * cw_ref[j:j + 1, cols]
        y = conv * _sigmoid(conv)
        if cb < N_M:
            qtm_ref[cols, :] = y.T.astype(BF16)
        else:
            km_ref[:, cb * LANES - MW:(cb + 1) * LANES - MW] = (y * (HD_M ** -0.5)).astype(BF16)

    ones_rows = jnp.ones((HALO, rows), BF16)
    pair = 2 * LANES

    def values_t(c0, out_ref, hp):
        v2 = proj(c0 + hp * pair, pair)
        for sub in range(2):
            hh = 2 * hp + sub
            out_ref[0, hh * VT_ROWS:hh * VT_ROWS + HD_M, :] = (
                v2[:, sub * LANES:(sub + 1) * LANES].T.astype(BF16))
            out_ref[0, hh * VT_ROWS + HD_M:(hh + 1) * VT_ROWS, :] = ones_rows

    def out_gate():
        om_ref[...] = proj(3 * MW, MW).astype(BF16)
        L = MLSTM_CHUNK
        gt = (_dot(h, wg_ref[...]) + gb_ref[...]).T[:N_GATES]
        logf = _log_sigmoid(gt)
        g_under_f = pltpu.roll(gt, N_M, 0)
        upper_f = (lax.broadcasted_iota(jnp.int32, (L, L), 0)
                   <= lax.broadcasted_iota(jnp.int32, (L, L), 1)).astype(F32)
        forward_rows = lax.broadcasted_iota(jnp.int32, (N_GATES, L), 0) < 2 * N_M
        for r in range(rows // L):
            sl = slice(r * L, (r + 1) * L)
            cum = _dot_exact(logf[:, sl], upper_f)
            suf = cum[:, L - 1:L] - cum + logf[:, sl]
            grow_ref[0:N_GATES, sl] = cum
            grow_ref[N_GATES:2 * N_GATES, sl] = gt[:, sl]
            grow_ref[2 * N_GATES:3 * N_GATES, sl] = suf
            src = jnp.where(forward_rows, cum, suf) - g_under_f[:, sl]
            src = jnp.concatenate([src, jnp.zeros((L - N_GATES, L), F32)], axis=0)
            gcol_ref[sl, :] = src.T[:, :N_GATES]

    bd = bd_ref[...]

    def qk_norm(c0, g_ref):
        t = proj(c0, pair)
        msq = _dot((t * t).astype(BF16), bd)
        return t * lax.rsqrt(msq + EPS) * g_ref[...]

    lane = lax.broadcasted_iota(jnp.int32, (rows, pair), 1) % LANES
    ones_cols = jnp.where(lane < HD_A + BIAS_ROWS, 1.0, 0.0)

    def attn_q(hp):
        sl = slice(hp * pair, (hp + 1) * pair)
        qn_ref[:, sl] = (qk_norm(4 * MW + hp * pair, qg_ref) * (HD_A ** -0.5 * LOG2E)).astype(BF16)

    def attn_k(hp):
        sl = slice(hp * pair, (hp + 1) * pair)
        kn = qk_norm(4 * MW + AW + hp * pair, kg_ref)
        k1_ref[:, sl] = jnp.where(lane < HD_A, kn, ones_cols).astype(BF16)
        k2_ref[:, sl] = jnp.where(lane < HD_A, pltpu.roll(kn, pair - HD_A, 1), ones_cols).astype(BF16)

    projections = [
        functools.partial(values_t, 2 * MW, vtm_ref, 0), functools.partial(values_t, 2 * MW, vtm_ref, 1),
        out_gate,
        functools.partial(attn_q, 0), functools.partial(attn_k, 0),
        functools.partial(attn_q, 1), functools.partial(attn_k, 1),
        functools.partial(values_t, 4 * MW + 2 * AW, vt_ref, 0),
    ]
    for cb, projection in enumerate(projections):
        conv_head(cb)
        projection()
    values_t(4 * MW + 2 * AW, vt_ref, 1)


def _inproj(x, g1, w_main, w_gate, layer, gate_b, conv_w, conv_b, qg, kg, bd, seq, tm):
    n, d = x.shape
    wcols = w_main.shape[2]
    hpt = tm // HALO
    nhalo = n // HALO
    row = lambda i: (i, 0)
    vt_spec = pl.BlockSpec((1, N_A * VT_ROWS, tm), lambda i: (i, 0, 0))
    vt_shape = jax.ShapeDtypeStruct((n // tm, N_A * VT_ROWS, tm), BF16)
    return pl.pallas_call(
        functools.partial(_inproj_kernel, tiles_per_seq=seq // tm),
        grid=(n // tm,),
        in_specs=[
            pl.BlockSpec((HALO, d), lambda i: (jnp.maximum(i * hpt - 1, 0), 0)),
            pl.BlockSpec((tm, d), row),
            pl.BlockSpec((HALO, d), lambda i: (jnp.minimum((i + 1) * hpt, nhalo - 1), 0)),
            _resident((1, d)),
            _resident_layer((d, wcols), layer),
            _resident_layer((d, LANES), layer),
            _resident((1, LANES)),
            _resident((CONV_W, 2 * MW)),
            _resident((1, 2 * MW)),
            _resident((1, 2 * LANES)),
            _resident((1, 2 * LANES)),
            _resident((2 * LANES, 2 * LANES)),
        ],
        out_specs=[
            pl.BlockSpec((MW, tm), lambda i: (0, i)),
            pl.BlockSpec((tm, MW), row),
            vt_spec,
            pl.BlockSpec((tm, MW), row),
            pl.BlockSpec((tm, N_GATES), row),
            pl.BlockSpec((3 * N_GATES, tm), lambda i: (0, i)),
            pl.BlockSpec((tm, AW), row),
            pl.BlockSpec((tm, AW), row),
            pl.BlockSpec((tm, AW), row),
            vt_spec,
        ],
        out_shape=[
            jax.ShapeDtypeStruct((MW, n), BF16),
            jax.ShapeDtypeStruct((n, MW), BF16),
            vt_shape,
            jax.ShapeDtypeStruct((n, MW), BF16),
            jax.ShapeDtypeStruct((n, N_GATES), F32),
            jax.ShapeDtypeStruct((3 * N_GATES, n), F32),
            jax.ShapeDtypeStruct((n, AW), BF16),
            jax.ShapeDtypeStruct((n, AW), BF16),
            jax.ShapeDtypeStruct((n, AW), BF16),
            vt_shape,
        ],
        scratch_shapes=[pltpu.VMEM((tm + 2 * HALO, 2 * MW), F32)],
        compiler_params=pltpu.CompilerParams(
            dimension_semantics=("parallel",), vmem_limit_bytes=VMEM_LIMIT),
        name="inproj",
    )(x, x, x, g1, w_main, w_gate, gate_b, conv_w, conv_b, qg, kg, bd)


def _mlstm_kernel(qt_f, k_f, vt_f, gc_f, gr_f,
                  qt_b, k_b, vt_b, gc_b, gr_b,
                  hf_ref, hb_ref,
                  c_s, m_s):
    L = MLSTM_CHUNK
    chunks = k_f.shape[0] // L

    @pl.when(pl.program_id(1) == 0)
    def _():
        c_s[...] = jnp.zeros_like(c_s)
        m_s[...] = jnp.zeros_like(m_s)

    row = lax.broadcasted_iota(jnp.int32, (L, L), 0)
    col = lax.broadcasted_iota(jnp.int32, (L, L), 1)
    lower = col <= row
    upper = row <= col
    dirs = ((qt_f, k_f, vt_f, gc_f, gr_f, hf_ref), (qt_b, k_b, vt_b, gc_b, gr_b, hb_ref))
    state = [(c_s[idx], m_s[idx][:, :1]) for idx in range(2 * N_M)]

    for step in range(chunks):
        chains = []
        for d, (qt_ref, k_ref, vt_ref, gc_ref, gr_ref, out_ref) in enumerate(dirs):
            tok = slice(step * L, (step + 1) * L) if d == 0 else slice((chunks - 1 - step) * L,
                                                                       (chunks - step) * L)
            i_off = 2 * N_M * d
            f_off = i_off + N_M
            src = gc_ref[tok, :]
            cum_rows = gr_ref[0:N_GATES, tok] if d == 0 else gr_ref[2 * N_GATES:3 * N_GATES, tok]
            gate_rows = gr_ref[N_GATES:2 * N_GATES, tok]
            for hh in range(N_M):
                idx = d * N_M + hh
                chains.append(dict(
                    idx=idx, out_ref=out_ref, tok=tok, cols=slice(hh * HD_M, (hh + 1) * HD_M),
                    mask=upper if d == 0 else lower,
                    src_col=src[:, f_off + hh:f_off + hh + 1],
                    b_row=cum_rows[f_off + hh:f_off + hh + 1, :],
                    i_row=gate_rows[i_off + hh:i_off + hh + 1, :],
                    end=slice(L - 1, L) if d == 0 else slice(0, 1),
                    qt=qt_ref[hh * HD_M:(hh + 1) * HD_M, tok],
                    kb=k_ref[tok, hh * HD_M:(hh + 1) * HD_M],
                    vt=vt_ref[0, hh * VT_ROWS:(hh + 1) * VT_ROWS, tok],
                    c_prev=state[idx][0], m_prev=state[idx][1],
                ))

        for ch in chains:
            ch["s"] = _dot(ch["kb"], ch["qt"])
            ch["inter_num"] = _dot(ch["c_prev"].astype(BF16), ch["qt"])
        for ch in chains:
            dmat = jnp.where(ch["mask"], ch["b_row"] - ch["src_col"], -jnp.inf)
            inter = ch["b_row"] + ch["m_prev"]
            ch["m_t"] = jnp.maximum(inter, jnp.max(dmat, axis=0, keepdims=True))
            ch["scale"] = jnp.exp(inter - ch["m_t"])
            ch["w"] = (ch["s"] * jnp.exp(dmat - ch["m_t"])).astype(BF16)
        for ch in chains:
            num = ch["scale"] * ch["inter_num"] + _dot(ch["vt"], ch["w"])
            den = num[HD_M:HD_M + 1]
            ht = num[:HD_M] / jnp.maximum(jnp.abs(den), jnp.exp(-ch["m_t"]))
            ch["out_ref"][ch["tok"], ch["cols"]] = ht.T.astype(ch["out_ref"].dtype)
        for ch in chains:
            b_end = ch["b_row"][:, ch["end"]]
            g_row = b_end - ch["b_row"] + ch["i_row"]
            m_new = jnp.maximum(b_end + ch["m_prev"], jnp.max(g_row, axis=-1, keepdims=True))
            decay = jnp.exp(b_end + ch["m_prev"] - m_new)
            vt_w = (ch["vt"].astype(F32) * jnp.exp(g_row - m_new)).astype(BF16)
            state[ch["idx"]] = (decay * ch["c_prev"] + _dot(vt_w, ch["kb"]), m_new)

    for idx, (c_new, m_new) in enumerate(state):
        c_s[idx] = c_new
        m_s[idx] = jnp.broadcast_to(m_new, (1, LANES))


def _mlstm(qt, k, vt, gate_cols, gate_rows, batch, seq, tm):
    nt = seq // tm

    def tile_f(b, c):
        return b * nt + c

    def tile_b(b, c):
        return b * nt + (nt - 1 - c)

    def specs(tile):
        return [
            pl.BlockSpec((MW, tm), lambda b, c: (0, tile(b, c))),
            pl.BlockSpec((tm, MW), lambda b, c: (tile(b, c), 0)),
            pl.BlockSpec((1, N_M * VT_ROWS, tm), lambda b, c: (tile(b, c), 0, 0)),
            pl.BlockSpec((tm, N_GATES), lambda b, c: (tile(b, c), 0)),
            pl.BlockSpec((3 * N_GATES, tm), lambda b, c: (0, tile(b, c))),
        ]

    n = batch * seq
    return pl.pallas_call(
        _mlstm_kernel,
        grid=(batch, nt),
        in_specs=specs(tile_f) + specs(tile_b),
        out_specs=[
            pl.BlockSpec((tm, MW), lambda b, c: (tile_f(b, c), 0)),
            pl.BlockSpec((tm, MW), lambda b, c: (tile_b(b, c), 0)),
        ],
        out_shape=[jax.ShapeDtypeStruct((n, MW), BF16)] * 2,
        scratch_shapes=[
            pltpu.VMEM((2 * N_M, VT_ROWS, HD_M), F32),
            pltpu.VMEM((2 * N_M, 1, LANES), F32),
        ],
        compiler_params=pltpu.CompilerParams(
            dimension_semantics=("parallel", "arbitrary"), vmem_limit_bytes=VMEM_LIMIT),
        name="mlstm",
    )(qt, k, vt, gate_cols, gate_rows, qt, k, vt, gate_cols, gate_rows)


def _bias_kernel(rb_ref, out_ref, *, tile):
    hh = pl.program_id(0)
    dd = pl.program_id(1)
    kk = lax.broadcasted_iota(jnp.int32, (tile, tile), 0)
    qq = lax.broadcasted_iota(jnp.int32, (tile, tile), 1)
    rel = (dd - 1) * tile + kk - qq
    nb = REL_BUCKETS // 2
    max_exact = nb // 2
    n = jnp.abs(rel)
    nf = jnp.maximum(n, 1).astype(F32)
    large = max_exact + (jnp.log(nf / max_exact) / math.log(REL_MAX_DIST / max_exact)
                         * (nb - max_exact)).astype(jnp.int32)
    large = jnp.minimum(large, nb - 1)
    bucket = jnp.where(rel > 0, nb, 0) + jnp.where(n < max_exact, n, large)
    val = jnp.zeros((tile, tile), F32)
    for bkt in range(REL_BUCKETS):
        val = jnp.where(bucket == bkt, rb_ref[bkt, hh], val)
    out_ref[0, 0] = jnp.where(dd < 3, val * LOG2E, 0.0)


def _bias_tiles(rel_bias, tile):
    return pl.pallas_call(
        functools.partial(_bias_kernel, tile=tile),
        grid=(N_A, 4),
        in_specs=[pl.BlockSpec(memory_space=pltpu.SMEM)],
        out_specs=pl.BlockSpec((1, 1, tile, tile), lambda h, d: (h, d, 0, 0)),
        out_shape=jax.ShapeDtypeStruct((N_A, 4, tile, tile), F32),
        name="rel_bias_tiles",
    )(rel_bias)


def _far_bucket(tile):
    assert tile >= REL_MAX_DIST
    return REL_BUCKETS // 2 - 1


def _attn_kernel(rb_ref, li_ref, q_ref, k1_ref, k2_ref, vt_ref, qg_ref, kg_ref, bias_ref, lam_ref, sg_ref,
                 o_ref,
                 qt_s, m_s, acc_s, p0_s, p1_s, *, far):
    tq = vt_ref.shape[2]
    for sub in range(q_ref.shape[0] // tq):
        rows = pl.ds(sub * tq, tq)
        _attn_query_tile(pl.program_id(2) * (q_ref.shape[0] // tq) + sub,
                         rb_ref, li_ref, q_ref.at[rows], k1_ref, k2_ref, vt_ref, qg_ref, kg_ref,
                         bias_ref, lam_ref, sg_ref, o_ref.at[rows],
                         qt_s.at[sub], m_s.at[sub], acc_s.at[sub], p0_s.at[sub], p1_s.at[sub], far=far)


def _attn_query_tile(qi, rb_ref, li_ref, q_ref, k1_ref, k2_ref, vt_ref, qg_ref, kg_ref, bias_ref,
                     lam_ref, sg_ref, o_ref, qt_s, m_s, acc_s, p0_s, p1_s, *, far):
    hh = pl.program_id(0)
    tq = q_ref.shape[0]
    tk = vt_ref.shape[2]
    nk = vt_ref.shape[0]
    vd = 2 * HD_A
    LEFT, RIGHT, NEAR = 0, 1, 2

    qt = q_ref[...].astype(F32).T
    rid = lax.broadcasted_iota(jnp.int32, qt.shape, 0)
    bases = (qt, jnp.concatenate([qt[HD_A:], qt[:HD_A]], axis=0))
    consts = (rb_ref[far, hh] * LOG2E, rb_ref[REL_BUCKETS // 2 + far, hh] * LOG2E, 0.0)

    def build_queries(shift):
        for i, base in enumerate(bases):
            body = jnp.where(rid < HD_A, base, 0.0).astype(BF16)
            for kind in range(len(consts)):
                qt_s[2 * kind + i] = body
        rid_o = lax.broadcasted_iota(jnp.int32, (HALO, tq), 0)
        for kind, cst in enumerate(consts):
            rest = jnp.full((HALO, tq), cst - shift, F32)
            extra = jnp.zeros((HALO, tq), F32)
            for r in range(BIAS_ROWS):
                part = rest.astype(BF16).astype(F32)
                extra = jnp.where(rid_o == r, part, extra)
                rest = rest - part
            for i in range(2):
                qt_s[2 * kind + i, HD_A:HD_A + HALO, :] = extra.astype(BF16)

    acc_s[...] = jnp.zeros_like(acc_s)

    def key_rows(j):
        return pl.ds(pl.multiple_of(j * tk, tk), tk)

    bucket_vals = [rb_ref[bkt, hh] for bkt in range(REL_BUCKETS)]
    bias_max = functools.reduce(jnp.maximum, bucket_vals) * LOG2E
    bias_min = functools.reduce(jnp.minimum, bucket_vals) * LOG2E
    reach = (HD_A ** 0.5 * LOG2E) * jnp.max(jnp.abs(qg_ref[...])) * jnp.max(jnp.abs(kg_ref[...]))
    shift = reach + bias_max
    span = 2.0 * reach + (bias_max - bias_min)
    bounded = span <= LOGIT_SPAN_LIMIT

    @pl.when(bounded)
    def _():
        build_queries(shift)

        def numerators(j, kind, p_buf, bias=None):
            for i, k_ref in enumerate((k1_ref, k2_ref)):
                s = _dot(k_ref[key_rows(j), :], qt_s[2 * kind + i])
                if bias is not None:
                    s = s + bias
                p_buf[i] = jnp.exp2(s).astype(BF16)

        def accumulate(j, p_buf):
            vt = vt_ref[j]
            for i in range(2):
                acc_s[i] += _dot(vt, p_buf[i])

        w0 = jnp.clip(qi - 1, 0, nk - ATTN_WINDOW)
        far_tile = lambda t: jnp.where(t < w0, t, t + ATTN_WINDOW)
        far_kind = lambda t: jnp.where(t < w0, LEFT, RIGHT)
        p_bufs = (p0_s, p1_s)
        for r in range(ATTN_WINDOW):
            j = w0 + r
            dlt = j - qi
            near = jnp.abs(dlt) <= 1
            kind = jnp.where(near, NEAR, jnp.where(dlt < 0, LEFT, RIGHT))
            numerators(j, kind, p_bufs[r % 2], bias_ref[0, jnp.where(near, dlt + 1, 3)])
            if r > 0:
                accumulate(j - 1, p_bufs[(r - 1) % 2])
        last_window = w0 + ATTN_WINDOW - 1

        def pair(u, carry):
            t0 = 2 * u
            numerators(far_tile(t0), far_kind(t0), p0_s)
            accumulate(jnp.where(u == 0, last_window, far_tile(t0 - 1)), p1_s)
            numerators(far_tile(t0 + 1), far_kind(t0 + 1), p1_s)
            accumulate(far_tile(t0), p0_s)
            return carry

        n_far = nk - ATTN_WINDOW
        for u in range(n_far // 2):
            pair(u, 0)
        accumulate(far_tile(n_far - 1) if n_far else last_window, p1_s)

    @pl.when(jnp.logical_not(bounded))
    def _():
        _online_softmax_tiles(qi, nk, build_queries, key_rows, k1_ref, k2_ref, vt_ref, bias_ref,
                              qt_s, m_s, acc_s)

    lv = lam_ref[...]
    lam = (jnp.exp(jnp.sum(lv[0:1] * lv[1:2], axis=-1, keepdims=True))
           - jnp.exp(jnp.sum(lv[2:3] * lv[3:4], axis=-1, keepdims=True)) + li_ref[0, 0])
    a1 = acc_s[0]
    a2 = acc_s[1]
    o = a1[:vd] / a1[vd:vd + 1] - lam * (a2[:vd] / a2[vd:vd + 1])
    ms = jnp.mean(o * o, axis=0, keepdims=True)
    o = o * lax.rsqrt(ms + EPS) * sg_ref[0] * li_ref[0, 1]
    o_ref[...] = o.T.astype(o_ref.dtype)


def _online_softmax_tiles(qi, nk, build_queries, key_rows, k1_ref, k2_ref, vt_ref, bias_ref,
                          qt_s, m_s, acc_s):
    LEFT, RIGHT, NEAR = 0, 1, 2
    build_queries(0.0)
    m_s[...] = jnp.full_like(m_s, -jnp.inf)

    def one_tile(j, carry):
        dlt = j - qi
        near = jnp.abs(dlt) <= 1
        kind = jnp.where(near, NEAR, jnp.where(dlt < 0, LEFT, RIGHT))
        bias = bias_ref[0, jnp.where(near, dlt + 1, 3)]
        vt = vt_ref[j]
        for i, k_ref in enumerate((k1_ref, k2_ref)):
            s = _dot(k_ref[key_rows(j), :], qt_s[2 * kind + i]) + bias
            m_old = m_s[i]
            m_new = jnp.maximum(m_old, jnp.max(s, axis=0, keepdims=True))
            p = jnp.exp2(s - m_new).astype(BF16)
            acc_s[i] = jnp.exp2(m_old - m_new) * acc_s[i] + _dot(vt, p)
            m_s[i] = m_new
        return carry

    lax.fori_loop(0, nk, one_tile, 0)


def _attention(qn, k1, k2, vt, qg, kg, bias_t, rel_bias, lam_init, lambdas, sub_g_col, batch, seq, tile):
    nt = seq // tile
    nq = nt // ATTN_Q_TILES
    vd = 2 * HD_A
    far = _far_bucket(tile)
    smem = pl.BlockSpec(memory_space=pltpu.SMEM)
    q_rows = ATTN_Q_TILES * tile
    return pl.pallas_call(
        functools.partial(_attn_kernel, far=far),
        grid=(N_A, batch, nq),
        in_specs=[
            smem,
            smem,
            pl.BlockSpec((q_rows, vd), lambda h, b, i: (b * nq + i, h)),
            pl.BlockSpec((seq, vd), lambda h, b, i: (b, h)),
            pl.BlockSpec((seq, vd), lambda h, b, i: (b, h)),
            pl.BlockSpec((nt, VT_ROWS, tile), lambda h, b, i: (b, h, 0)),
            pl.BlockSpec((1, 2 * LANES), lambda h, b, i: (0, 0)),
            pl.BlockSpec((1, 2 * LANES), lambda h, b, i: (0, 0)),
            pl.BlockSpec((1, 4, tile, tile), lambda h, b, i: (h, 0, 0, 0)),
            pl.BlockSpec((4, HD_A), lambda h, b, i: (0, 0)),
            pl.BlockSpec((1, vd, 1), lambda h, b, i: (h, 0, 0)),
        ],
        out_specs=pl.BlockSpec((q_rows, vd), lambda h, b, i: (b * nq + i, h)),
        out_shape=jax.ShapeDtypeStruct((batch * seq, AW), BF16),
        scratch_shapes=[
            pltpu.VMEM((ATTN_Q_TILES, 6, vd, tile), BF16),
            pltpu.VMEM((ATTN_Q_TILES, 2, 1, tile), F32),
            pltpu.VMEM((ATTN_Q_TILES, 2, VT_ROWS, tile), F32),
            pltpu.VMEM((ATTN_Q_TILES, 2, tile, tile), BF16),
            pltpu.VMEM((ATTN_Q_TILES, 2, tile, tile), BF16),
        ],
        compiler_params=pltpu.CompilerParams(
            dimension_semantics=("parallel", "parallel", "arbitrary"),
            vmem_limit_bytes=VMEM_LIMIT),
        name="diff_attention",
    )(rel_bias, lam_init, qn, k1, k2, vt, qg, kg, bias_t, lambdas, sub_g_col)


def _outffn_kernel(x_ref, hf_ref, hb_ref, og_ref, mb_ref, ng_ref, wo_ref, g2_ref, w1_ref, w2_ref,
                   out_ref, *, ff_chunk):
    hs = hf_ref[...].astype(F32) + hb_ref[...].astype(F32)
    x1 = x_ref[...] + _dot(mb_ref[...], wo_ref[MW:, :])
    for hh in range(N_M):
        sl = slice(hh * HD_M, (hh + 1) * HD_M)
        t = hs[:, sl]
        ms = jnp.mean(t * t, axis=-1, keepdims=True)
        mix = t * lax.rsqrt(ms + EPS) * ng_ref[:, sl] * _sigmoid(og_ref[:, sl].astype(F32))
        x1 = x1 + _dot(mix.astype(BF16), wo_ref[sl, :])
    ms = jnp.mean(x1 * x1, axis=-1, keepdims=True)
    h2 = (x1 * lax.rsqrt(ms + EPS) * g2_ref[...]).astype(BF16)
    out_ref[...] = x1
    for j in range(w1_ref.shape[1] // ff_chunk):
        sl = slice(j * ff_chunk, (j + 1) * ff_chunk)
        u = jnp.maximum(_dot(h2, w1_ref[:, sl]), 0.0)
        out_ref[...] += _dot((u * u).astype(BF16), w2_ref[sl, :])


def _outffn(x, hf, hb, om, mixb, ng, wo, g2, w1, w2, layer, tm):
    n, d = x.shape
    dff = w1.shape[2]
    row = lambda i: (i, 0)
    return pl.pallas_call(
        functools.partial(_outffn_kernel, ff_chunk=1024),
        grid=(n // tm,),
        in_specs=[
            pl.BlockSpec((tm, d), row),
            pl.BlockSpec((tm, MW), row),
            pl.BlockSpec((tm, MW), row),
            pl.BlockSpec((tm, MW), row),
            pl.BlockSpec((tm, AW), row),
            _resident((1, MW)),
            _resident_layer((MW + AW, d), layer),
            _resident((1, d)),
            _resident_layer((d, dff), layer),
            _resident_layer((dff, d), layer),
        ],
        out_specs=pl.BlockSpec((tm, d), row),
        out_shape=jax.ShapeDtypeStruct((n, d), F32),
        compiler_params=pltpu.CompilerParams(
            dimension_semantics=("parallel",), vmem_limit_bytes=VMEM_LIMIT),
        name="outproj_ffn",
    )(x, hf, hb, om, mixb, ng, wo, g2, w1, w2)


def kernel(x, norm1_g, w_in, conv_w, conv_b, gate_b, mlstm_norm_g, q_norm_g, k_norm_g, lambdas,
           diff_norm_g, rel_bias, w_out, norm2_g, w_ff1, w_ff2):
    batch, seq, d = x.shape
    depth = w_in.shape[0]
    n = batch * seq
    tm = min(512, n)
    tile = min(512, seq)
    assert seq % MLSTM_CHUNK == 0 and seq % tile == 0 and tm == tile
    assert seq // tile >= ATTN_WINDOW and (seq // tile - ATTN_WINDOW) % 2 == 0
    assert (seq // tile) % ATTN_Q_TILES == 0

    w_main = jnp.concatenate([w_in[:, :, :C_G], w_in[:, :, C_G + N_GATES:]], axis=-1).astype(BF16)
    w_gate = jnp.pad(w_in[:, :, C_G:C_G + N_GATES], ((0, 0), (0, 0), (0, LANES - N_GATES))).astype(BF16)
    w_out_b = w_out.astype(BF16)
    w_ff1_b = w_ff1.astype(BF16)
    w_ff2_b = w_ff2.astype(BF16)
    lane = jnp.arange(2 * LANES)
    bd = jnp.where((lane[:, None] // HD_A) == (lane[None, :] // HD_A), 1.0 / HD_A, 0.0).astype(BF16)

    bias_t = _bias_tiles(rel_bias.astype(F32), tile)

    xf = x.reshape(n, d)
    for l in range(depth):
        lam_init = 0.8 - 0.6 * math.exp(-0.3 * l)
        qg = jnp.tile(q_norm_g[l], 2 * LANES // HD_A)[None, :]
        kg = jnp.tile(k_norm_g[l], 2 * LANES // HD_A)[None, :]
        gb = jnp.pad(gate_b[l].reshape(1, N_GATES).astype(F32), ((0, 0), (0, LANES - N_GATES)))
        qtm, km, vtm, om, gate_cols, gate_rows, qn, k1, k2, vt = _inproj(
            xf, norm1_g[l][None, :], w_main, w_gate, l, gb, conv_w[l], conv_b[l][None, :],
            qg, kg, bd, seq, tm)
        hf, hb = _mlstm(qtm, km, vtm, gate_cols, gate_rows, batch, seq, tm)
        mixb = _attention(qn, k1, k2, vt, qg, kg, bias_t, rel_bias.astype(F32),
                          jnp.array([[lam_init, 1.0 - lam_init]], F32), lambdas[l],
                          diff_norm_g[l].reshape(N_A, 2 * HD_A, 1), batch, seq, tile)
        xf = _outffn(xf, hf, hb, om, mixb, mlstm_norm_g[l][None, :], w_out_b,
                     norm2_g[l][None, :], w_ff1_b, w_ff2_b, l, tm)
    return xf.reshape(batch, seq, d)
```

```python
import functools
import math

import jax
import jax.numpy as jnp
from jax import lax
from jax.experimental import pallas as pl
from jax.experimental.pallas import tpu as pltpu

F32 = jnp.float32
BF16 = jnp.bfloat16

N_M = 4
HD_M = 128
MW = N_M * HD_M
MLSTM_CHUNK = 128
CONV_W = 5
N_A = 4
HD_A = 64
AW = N_A * 2 * HD_A
REL_BUCKETS = 32
REL_MAX_DIST = 128
EPS = 1e-6
N_GATES = 4 * N_M
C_G = 4 * MW
LANES = 128
HALO = 16
VMEM_LIMIT = 56 * 1024 * 1024
LOG2E = math.log2(math.e)
VT_ROWS = 2 * HD_A + HALO
BIAS_ROWS = 3
LOGIT_SPAN_LIMIT = 96.0
ATTN_WINDOW = 4
ATTN_Q_TILES = 2
ROW_TILE = 512
FAR_BUCKET = REL_BUCKETS // 2 - 1


def _dot(a, b):
    return jnp.dot(a, b, preferred_element_type=F32)


def _dot_exact(a, b):
    return jnp.dot(a, b, preferred_element_type=F32, precision=lax.Precision.HIGHEST)


def _sigmoid(x):
    return 1.0 / (1.0 + jnp.exp(-x))


def _log_sigmoid(x):
    return jnp.minimum(x, 0.0) - jnp.log1p(jnp.exp(-jnp.abs(x)))


def _resident(shape):
    nd = len(shape)
    return pl.BlockSpec(shape, lambda *_: (0,) * nd, pipeline_mode=pl.Buffered(1))


def _resident_layer(shape, layer):
    nd = len(shape)
    return pl.BlockSpec((None,) + tuple(shape), lambda *_: (layer,) + (0,) * nd,
                        pipeline_mode=pl.Buffered(1))


def _inproj_kernel(xp_ref, x_ref, xn_ref, g1_ref, w_ref, wg_ref, gb_ref, cw_ref, cb_ref, qg_ref, kg_ref,
                   bd_ref,
                   qtm_ref, km_ref, vtm_ref, om_ref, gcol_ref, grow_ref, qn_ref, k1_ref, k2_ref, vt_ref,
                   ext_s, *, tiles_per_seq):
    rows = x_ref.shape[0]
    pos = pl.program_id(0) % tiles_per_seq

    def normed(xv):
        ms = jnp.mean(xv * xv, axis=-1, keepdims=True)
        return (xv * lax.rsqrt(ms + EPS) * g1_ref[...]).astype(BF16)

    x = x_ref[...]
    h = normed(x)

    def proj(c0, width):
        return _dot(h, w_ref[:, c0:c0 + width])

    h_ext = jnp.concatenate([normed(xp_ref[...]), h, normed(xn_ref[...])], axis=0)
    rid = lax.broadcasted_iota(jnp.int32, (rows + 2 * HALO, 1), 0)
    inside = jnp.logical_and(jnp.logical_or(rid >= HALO, pos > 0),
                             jnp.logical_or(rid < HALO + rows, pos < tiles_per_seq - 1))
    ext_s[...] = jnp.where(inside, _dot(h_ext, w_ref[:, 0:2 * MW]), 0.0)

    def conv_head(cb):
        cols = slice(cb * LANES, (cb + 1) * LANES)
        conv = cb_ref[:, cols]
        for j in range(CONV_W):
            off = HALO - CONV_W // 2 + j
            conv = conv + ext_s[off:off + rows, cols] * cw_ref[j:j + 1, cols]
        y = conv * _sigmoid(conv)
        if cb < N_M:
            qtm_ref[cols, :] = y.T.astype(BF16)
        else:
            km_ref[:, cb * LANES - MW:(cb + 1) * LANES - MW] = (y * (HD_M ** -0.5)).astype(BF16)

    ones_rows = jnp.ones((HALO, rows), BF16)
    pair = 2 * LANES

    def values_t(c0, out_ref, hp):
        v2 = proj(c0 + hp * pair, pair)
        for sub in range(2):
            hh = 2 * hp + sub
            out_ref[0, hh * VT_ROWS:hh * VT_ROWS + HD_M, :] = (
                v2[:, sub * LANES:(sub + 1) * LANES].T.astype(BF16))
            out_ref[0, hh * VT_ROWS + HD_M:(hh + 1) * VT_ROWS, :] = ones_rows

    def out_gate():
        om_ref[...] = proj(3 * MW, MW).astype(BF16)
        L = MLSTM_CHUNK
        gt = (_dot(h, wg_ref[...]) + gb_ref[...]).T[:N_GATES]
        logf = _log_sigmoid(gt)
        g_under_f = pltpu.roll(gt, N_M, 0)
        upper_f = (lax.broadcasted_iota(jnp.int32, (L, L), 0)
                   <= lax.broadcasted_iota(jnp.int32, (L, L), 1)).astype(F32)
        forward_rows = lax.broadcasted_iota(jnp.int32, (N_GATES, L), 0) < 2 * N_M
        for r in range(rows // L):
            sl = slice(r * L, (r + 1) * L)
            cum = _dot_exact(logf[:, sl], upper_f)
            suf = cum[:, L - 1:L] - cum + logf[:, sl]
            grow_ref[0:N_GATES, sl] = cum
            grow_ref[N_GATES:2 * N_GATES, sl] = gt[:, sl]
            grow_ref[2 * N_GATES:3 * N_GATES, sl] = suf
            src = jnp.where(forward_rows, cum, suf) - g_under_f[:, sl]
            src = jnp.concatenate([src, jnp.zeros((L - N_GATES, L), F32)], axis=0)
            gcol_ref[sl, :] = src.T[:, :N_GATES]

    bd = bd_ref[...]

    def qk_norm(c0, g_ref):
        t = proj(c0, pair)
        msq = _dot((t * t).astype(BF16), bd)
        return t * lax.rsqrt(msq + EPS) * g_ref[...]

    lane = lax.broadcasted_iota(jnp.int32, (rows, pair), 1) % LANES
    ones_cols = jnp.where(lane < HD_A + BIAS_ROWS, 1.0, 0.0)

    def attn_q(hp):
        sl = slice(hp * pair, (hp + 1) * pair)
        qn_ref[:, sl] = (qk_norm(4 * MW + hp * pair, qg_ref) * (HD_A ** -0.5 * LOG2E)).astype(BF16)

    def attn_k(hp):
        sl = slice(hp * pair, (hp + 1) * pair)
        kn = qk_norm(4 * MW + AW + hp * pair, kg_ref)
        k1_ref[:, sl] = jnp.where(lane < HD_A, kn, ones_cols).astype(BF16)
        k2_ref[:, sl] = jnp.where(lane < HD_A, pltpu.roll(kn, pair - HD_A, 1), ones_cols).astype(BF16)

    projections = [
        functools.partial(values_t, 2 * MW, vtm_ref, 0), functools.partial(values_t, 2 * MW, vtm_ref, 1),
        out_gate,
        functools.partial(attn_q, 0), functools.partial(attn_k, 0),
        functools.partial(attn_q, 1), functools.partial(attn_k, 1),
        functools.partial(values_t, 4 * MW + 2 * AW, vt_ref, 0),
    ]
    for cb, projection in enumerate(projections):
        conv_head(cb)
        projection()
    values_t(4 * MW + 2 * AW, vt_ref, 1)


def _inproj(x, g1, w_main, w_gate, layer, gate_b, conv_w, conv_b, qg, kg, bd, seq, tm):
    n, d = x.shape
    wcols = w_main.shape[2]
    hpt = tm // HALO
    nhalo = n // HALO
    row = lambda i: (i, 0)
    vt_spec = pl.BlockSpec((1, N_A * VT_ROWS, tm), lambda i: (i, 0, 0))
    vt_shape = jax.ShapeDtypeStruct((n // tm, N_A * VT_ROWS, tm), BF16)
    return pl.pallas_call(
        functools.partial(_inproj_kernel, tiles_per_seq=seq // tm),
        grid=(n // tm,),
        in_specs=[
            pl.BlockSpec((HALO, d), lambda i: (jnp.maximum(i * hpt - 1, 0), 0)),
            pl.BlockSpec((tm, d), row),
            pl.BlockSpec((HALO, d), lambda i: (jnp.minimum((i + 1) * hpt, nhalo - 1), 0)),
            _resident((1, d)),
            _resident_layer((d, wcols), layer),
            _resident_layer((d, LANES), layer),
            _resident((1, LANES)),
            _resident((CONV_W, 2 * MW)),
            _resident((1, 2 * MW)),
            _resident((1, 2 * LANES)),
            _resident((1, 2 * LANES)),
            _resident((2 * LANES, 2 * LANES)),
        ],
        out_specs=[
            pl.BlockSpec((MW, tm), lambda i: (0, i)),
            pl.BlockSpec((tm, MW), row),
            vt_spec,
            pl.BlockSpec((tm, MW), row),
            pl.BlockSpec((tm, N_GATES), row),
            pl.BlockSpec((3 * N_GATES, tm), lambda i: (0, i)),
            pl.BlockSpec((tm, AW), row),
            pl.BlockSpec((tm, AW), row),
            pl.BlockSpec((tm, AW), row),
            vt_spec,
        ],
        out_shape=[
            jax.ShapeDtypeStruct((MW, n), BF16),
            jax.ShapeDtypeStruct((n, MW), BF16),
            vt_shape,
            jax.ShapeDtypeStruct((n, MW), BF16),
            jax.ShapeDtypeStruct((n, N_GATES), F32),
            jax.ShapeDtypeStruct((3 * N_GATES, n), F32),
            jax.ShapeDtypeStruct((n, AW), BF16),
            jax.ShapeDtypeStruct((n, AW), BF16),
            jax.ShapeDtypeStruct((n, AW), BF16),
            vt_shape,
        ],
        scratch_shapes=[pltpu.VMEM((tm + 2 * HALO, 2 * MW), F32)],
        compiler_params=pltpu.CompilerParams(
            dimension_semantics=("parallel",), vmem_limit_bytes=VMEM_LIMIT),
        name="inproj",
    )(x, x, x, g1, w_main, w_gate, gate_b, conv_w, conv_b, qg, kg, bd)


def _mlstm_kernel(qt_f, k_f, vt_f, gc_f, gr_f,
                  qt_b, k_b, vt_b, gc_b, gr_b,
                  hf_ref, hb_ref,
                  c_s, m_s):
    L = MLSTM_CHUNK
    chunks = k_f.shape[0] // L

    @pl.when(pl.program_id(1) == 0)
    def _():
        c_s[...] = jnp.zeros_like(c_s)
        m_s[...] = jnp.zeros_like(m_s)

    row = lax.broadcasted_iota(jnp.int32, (L, L), 0)
    col = lax.broadcasted_iota(jnp.int32, (L, L), 1)
    lower = col <= row
    upper = row <= col
    dirs = ((qt_f, k_f, vt_f, gc_f, gr_f, hf_ref), (qt_b, k_b, vt_b, gc_b, gr_b, hb_ref))
    state = [(c_s[idx], m_s[idx][:, :1]) for idx in range(2 * N_M)]

    for step in range(chunks):
        chains = []
        for d, (qt_ref, k_ref, vt_ref, gc_ref, gr_ref, out_ref) in enumerate(dirs):
            tok = slice(step * L, (step + 1) * L) if d == 0 else slice((chunks - 1 - step) * L,
                                                                       (chunks - step) * L)
            i_off = 2 * N_M * d
            f_off = i_off + N_M
            src = gc_ref[tok, :]
            cum_rows = gr_ref[0:N_GATES, tok] if d == 0 else gr_ref[2 * N_GATES:3 * N_GATES, tok]
            gate_rows = gr_ref[N_GATES:2 * N_GATES, tok]
            for hh in range(N_M):
                idx = d * N_M + hh
                chains.append(dict(
                    idx=idx, out_ref=out_ref, tok=tok, cols=slice(hh * HD_M, (hh + 1) * HD_M),
                    mask=upper if d == 0 else lower,
                    src_col=src[:, f_off + hh:f_off + hh + 1],
                    b_row=cum_rows[f_off + hh:f_off + hh + 1, :],
                    i_row=gate_rows[i_off + hh:i_off + hh + 1, :],
                    end=slice(L - 1, L) if d == 0 else slice(0, 1),
                    qt=qt_ref[hh * HD_M:(hh + 1) * HD_M, tok],
                    kb=k_ref[tok, hh * HD_M:(hh + 1) * HD_M],
                    vt=vt_ref[0, hh * VT_ROWS:(hh + 1) * VT_ROWS, tok],
                    c_prev=state[idx][0], m_prev=state[idx][1],
                ))

        for ch in chains:
            ch["s"] = _dot(ch["kb"], ch["qt"])
            ch["inter_num"] = _dot(ch["c_prev"].astype(BF16), ch["qt"])
        for ch in chains:
            dmat = jnp.where(ch["mask"], ch["b_row"] - ch["src_col"], -jnp.inf)
            inter = ch["b_row"] + ch["m_prev"]
            ch["m_t"] = jnp.maximum(inter, jnp.max(dmat, axis=0, keepdims=True))
            ch["scale"] = jnp.exp(inter - ch["m_t"])
            ch["w"] = (ch["s"] * jnp.exp(dmat - ch["m_t"])).astype(BF16)
        for ch in chains:
            num = ch["scale"] * ch["inter_num"] + _dot(ch["vt"], ch["w"])
            den = num[HD_M:HD_M + 1]
            ht = num[:HD_M] / jnp.maximum(jnp.abs(den), jnp.exp(-ch["m_t"]))
            ch["out_ref"][ch["tok"], ch["cols"]] = ht.T.astype(ch["out_ref"].dtype)
        for ch in chains:
            b_end = ch["b_row"][:, ch["end"]]
            g_row = b_end - ch["b_row"] + ch["i_row"]
            m_new = jnp.maximum(b_end + ch["m_prev"], jnp.max(g_row, axis=-1, keepdims=True))
            decay = jnp.exp(b_end + ch["m_prev"] - m_new)
            vt_w = (ch["vt"].astype(F32) * jnp.exp(g_row - m_new)).astype(BF16)
            state[ch["idx"]] = (decay * ch["c_prev"] + _dot(vt_w, ch["kb"]), m_new)

    for idx, (c_new, m_new) in enumerate(state):
        c_s[idx] = c_new
        m_s[idx] = jnp.broadcast_to(m_new, (1, LANES))


def _mlstm(qt, k, vt, gate_cols, gate_rows, batch, seq, tm):
    nt = seq // tm

    def tile_f(b, c):
        return b * nt + c

    def tile_b(b, c):
        return b * nt + (nt - 1 - c)

    def specs(tile):
        return [
            pl.BlockSpec((MW, tm), lambda b, c: (0, tile(b, c))),
            pl.BlockSpec((tm, MW), lambda b, c: (tile(b, c), 0)),
            pl.BlockSpec((1, N_M * VT_ROWS, tm), lambda b, c: (tile(b, c), 0, 0)),
            pl.BlockSpec((tm, N_GATES), lambda b, c: (tile(b, c), 0)),
            pl.BlockSpec((3 * N_GATES, tm), lambda b, c: (0, tile(b, c))),
        ]

    n = batch * seq
    return pl.pallas_call(
        _mlstm_kernel,
        grid=(batch, nt),
        in_specs=specs(tile_f) + specs(tile_b),
        out_specs=[
            pl.BlockSpec((tm, MW), lambda b, c: (tile_f(b, c), 0)),
            pl.BlockSpec((tm, MW), lambda b, c: (tile_b(b, c), 0)),
        ],
        out_shape=[jax.ShapeDtypeStruct((n, MW), BF16)] * 2,
        scratch_shapes=[
            pltpu.VMEM((2 * N_M, VT_ROWS, HD_M), F32),
            pltpu.VMEM((2 * N_M, 1, LANES), F32),
        ],
        compiler_params=pltpu.CompilerParams(
            dimension_semantics=("parallel", "arbitrary"), vmem_limit_bytes=VMEM_LIMIT),
        name="mlstm",
    )(qt, k, vt, gate_cols, gate_rows, qt, k, vt, gate_cols, gate_rows)


def _bias_kernel(rb_ref, out_ref, *, tile):
    hh = pl.program_id(0)
    dd = pl.program_id(1)
    kk = lax.broadcasted_iota(jnp.int32, (tile, tile), 0)
    qq = lax.broadcasted_iota(jnp.int32, (tile, tile), 1)
    rel = (dd - 1) * tile + kk - qq
    nb = REL_BUCKETS // 2
    max_exact = nb // 2
    n = jnp.abs(rel)
    nf = jnp.maximum(n, 1).astype(F32)
    large = max_exact + (jnp.log(nf / max_exact) / math.log(REL_MAX_DIST / max_exact)
                         * (nb - max_exact)).astype(jnp.int32)
    large = jnp.minimum(large, nb - 1)
    bucket = jnp.where(rel > 0, nb, 0) + jnp.where(n < max_exact, n, large)
    val = jnp.zeros((tile, tile), F32)
    for bkt in range(REL_BUCKETS):
        val = jnp.where(bucket == bkt, rb_ref[bkt, hh], val)
    out_ref[0, 0] = jnp.where(dd < 3, val * LOG2E, 0.0)


def _bias_tiles(rel_bias, tile):
    return pl.pallas_call(
        functools.partial(_bias_kernel, tile=tile),
        grid=(N_A, 4),
        in_specs=[pl.BlockSpec(memory_space=pltpu.SMEM)],
        out_specs=pl.BlockSpec((1, 1, tile, tile), lambda h, d: (h, d, 0, 0)),
        out_shape=jax.ShapeDtypeStruct((N_A, 4, tile, tile), F32),
        name="rel_bias_tiles",
    )(rel_bias)


def _attn_kernel(rb_ref, li_ref, q_ref, k1_ref, k2_ref, vt_ref, qg_ref, kg_ref, bias_ref, lam_ref, sg_ref,
                 o_ref,
                 qt_s, m_s, acc_s, p0_s, p1_s):
    tq = vt_ref.shape[2]
    for sub in range(q_ref.shape[0] // tq):
        rows = pl.ds(sub * tq, tq)
        _attn_query_tile(pl.program_id(2) * (q_ref.shape[0] // tq) + sub,
                         rb_ref, li_ref, q_ref.at[rows], k1_ref, k2_ref, vt_ref, qg_ref, kg_ref,
                         bias_ref, lam_ref, sg_ref, o_ref.at[rows],
                         qt_s.at[sub], m_s.at[sub], acc_s.at[sub], p0_s.at[sub], p1_s.at[sub])


def _attn_query_tile(qi, rb_ref, li_ref, q_ref, k1_ref, k2_ref, vt_ref, qg_ref, kg_ref, bias_ref,
                     lam_ref, sg_ref, o_ref, qt_s, m_s, acc_s, p0_s, p1_s):
    hh = pl.program_id(0)
    tq = q_ref.shape[0]
    tk = vt_ref.shape[2]
    nk = vt_ref.shape[0]
    vd = 2 * HD_A
    LEFT, RIGHT, NEAR = 0, 1, 2

    qt = q_ref[...].astype(F32).T
    rid = lax.broadcasted_iota(jnp.int32, qt.shape, 0)
    bases = (qt, jnp.concatenate([qt[HD_A:], qt[:HD_A]], axis=0))
    consts = (rb_ref[FAR_BUCKET, hh] * LOG2E, rb_ref[REL_BUCKETS // 2 + FAR_BUCKET, hh] * LOG2E, 0.0)

    def build_queries(shift):
        for i, base in enumerate(bases):
            body = jnp.where(rid < HD_A, base, 0.0).astype(BF16)
            for kind in range(len(consts)):
                qt_s[2 * kind + i] = body
        rid_o = lax.broadcasted_iota(jnp.int32, (HALO, tq), 0)
        for kind, cst in enumerate(consts):
            rest = jnp.full((HALO, tq), cst - shift, F32)
            extra = jnp.zeros((HALO, tq), F32)
            for r in range(BIAS_ROWS):
                part = rest.astype(BF16).astype(F32)
                extra = jnp.where(rid_o == r, part, extra)
                rest = rest - part
            for i in range(2):
                qt_s[2 * kind + i, HD_A:HD_A + HALO, :] = extra.astype(BF16)

    acc_s[...] = jnp.zeros_like(acc_s)

    def key_rows(j):
        return pl.ds(pl.multiple_of(j * tk, tk), tk)

    bucket_vals = [rb_ref[bkt, hh] for bkt in range(REL_BUCKETS)]
    bias_max = functools.reduce(jnp.maximum, bucket_vals) * LOG2E
    bias_min = functools.reduce(jnp.minimum, bucket_vals) * LOG2E
    reach = (HD_A ** 0.5 * LOG2E) * jnp.max(jnp.abs(qg_ref[...])) * jnp.max(jnp.abs(kg_ref[...]))
    shift = reach + bias_max
    span = 2.0 * reach + (bias_max - bias_min)
    bounded = span <= LOGIT_SPAN_LIMIT

    @pl.when(bounded)
    def _():
        build_queries(shift)

        def numerators(j, kind, p_buf, bias=None):
            for i, k_ref in enumerate((k1_ref, k2_ref)):
                s = _dot(k_ref[key_rows(j), :], qt_s[2 * kind + i])
                if bias is not None:
                    s = s + bias
                p_buf[i] = jnp.exp2(s).astype(BF16)

        def accumulate(j, p_buf):
            vt = vt_ref[j]
            for i in range(2):
                acc_s[i] += _dot(vt, p_buf[i])

        w0 = jnp.clip(qi - 1, 0, nk - ATTN_WINDOW)
        far_tile = lambda t: jnp.where(t < w0, t, t + ATTN_WINDOW)
        far_kind = lambda t: jnp.where(t < w0, LEFT, RIGHT)
        p_bufs = (p0_s, p1_s)
        for r in range(ATTN_WINDOW):
            j = w0 + r
            dlt = j - qi
            near = jnp.abs(dlt) <= 1
            kind = jnp.where(near, NEAR, jnp.where(dlt < 0, LEFT, RIGHT))
            numerators(j, kind, p_bufs[r % 2], bias_ref[0, jnp.where(near, dlt + 1, 3)])
            if r > 0:
                accumulate(j - 1, p_bufs[(r - 1) % 2])
        last_window = w0 + ATTN_WINDOW - 1

        n_far = nk - ATTN_WINDOW
        for t0 in range(0, n_far, 2):
            numerators(far_tile(t0), far_kind(t0), p0_s)
            accumulate(far_tile(t0 - 1) if t0 else last_window, p1_s)
            numerators(far_tile(t0 + 1), far_kind(t0 + 1), p1_s)
            accumulate(far_tile(t0), p0_s)
        accumulate(far_tile(n_far - 1) if n_far else last_window, p1_s)

    @pl.when(jnp.logical_not(bounded))
    def _():
        _online_softmax_tiles(qi, nk, build_queries, key_rows, k1_ref, k2_ref, vt_ref, bias_ref,
                              qt_s, m_s, acc_s)

    lv = lam_ref[...]
    lam = (jnp.exp(jnp.sum(lv[0:1] * lv[1:2], axis=-1, keepdims=True))
           - jnp.exp(jnp.sum(lv[2:3] * lv[3:4], axis=-1, keepdims=True)) + li_ref[0, 0])
    a1 = acc_s[0]
    a2 = acc_s[1]
    o = a1[:vd] / a1[vd:vd + 1] - lam * (a2[:vd] / a2[vd:vd + 1])
    ms = jnp.mean(o * o, axis=0, keepdims=True)
    o = o * lax.rsqrt(ms + EPS) * sg_ref[0] * li_ref[0, 1]
    o_ref[...] = o.T.astype(o_ref.dtype)


def _online_softmax_tiles(qi, nk, build_queries, key_rows, k1_ref, k2_ref, vt_ref, bias_ref,
                          qt_s, m_s, acc_s):
    LEFT, RIGHT, NEAR = 0, 1, 2
    build_queries(0.0)
    m_s[...] = jnp.full_like(m_s, -jnp.inf)

    def one_tile(j, carry):
        dlt = j - qi
        near = jnp.abs(dlt) <= 1
        kind = jnp.where(near, NEAR, jnp.where(dlt < 0, LEFT, RIGHT))
        bias = bias_ref[0, jnp.where(near, dlt + 1, 3)]
        vt = vt_ref[j]
        for i, k_ref in enumerate((k1_ref, k2_ref)):
            s = _dot(k_ref[key_rows(j), :], qt_s[2 * kind + i]) + bias
            m_old = m_s[i]
            m_new = jnp.maximum(m_old, jnp.max(s, axis=0, keepdims=True))
            p = jnp.exp2(s - m_new).astype(BF16)
            acc_s[i] = jnp.exp2(m_old - m_new) * acc_s[i] + _dot(vt, p)
            m_s[i] = m_new
        return carry

    lax.fori_loop(0, nk, one_tile, 0)


def _attention(qn, k1, k2, vt, qg, kg, bias_t, rel_bias, lam_init, lambdas, sub_g_col, batch, seq, tile):
    nt = seq // tile
    nq = nt // ATTN_Q_TILES
    vd = 2 * HD_A
    smem = pl.BlockSpec(memory_space=pltpu.SMEM)
    q_rows = ATTN_Q_TILES * tile
    return pl.pallas_call(
        _attn_kernel,
        grid=(N_A, batch, nq),
        in_specs=[
            smem,
            smem,
            pl.BlockSpec((q_rows, vd), lambda h, b, i: (b * nq + i, h)),
            pl.BlockSpec((seq, vd), lambda h, b, i: (b, h)),
            pl.BlockSpec((seq, vd), lambda h, b, i: (b, h)),
            pl.BlockSpec((nt, VT_ROWS, tile), lambda h, b, i: (b, h, 0)),
            pl.BlockSpec((1, 2 * LANES), lambda h, b, i: (0, 0)),
            pl.BlockSpec((1, 2 * LANES), lambda h, b, i: (0, 0)),
            pl.BlockSpec((1, 4, tile, tile), lambda h, b, i: (h, 0, 0, 0)),
            pl.BlockSpec((4, HD_A), lambda h, b, i: (0, 0)),
            pl.BlockSpec((1, vd, 1), lambda h, b, i: (h, 0, 0)),
        ],
        out_specs=pl.BlockSpec((q_rows, vd), lambda h, b, i: (b * nq + i, h)),
        out_shape=jax.ShapeDtypeStruct((batch * seq, AW), BF16),
        scratch_shapes=[
            pltpu.VMEM((ATTN_Q_TILES, 6, vd, tile), BF16),
            pltpu.VMEM((ATTN_Q_TILES, 2, 1, tile), F32),
            pltpu.VMEM((ATTN_Q_TILES, 2, VT_ROWS, tile), F32),
            pltpu.VMEM((ATTN_Q_TILES, 2, tile, tile), BF16),
            pltpu.VMEM((ATTN_Q_TILES, 2, tile, tile), BF16),
        ],
        compiler_params=pltpu.CompilerParams(
            dimension_semantics=("parallel", "parallel", "arbitrary"),
            vmem_limit_bytes=VMEM_LIMIT),
        name="diff_attention",
    )(rel_bias, lam_init, qn, k1, k2, vt, qg, kg, bias_t, lambdas, sub_g_col)


def _outffn_kernel(x_ref, hf_ref, hb_ref, og_ref, mb_ref, ng_ref, wo_ref, g2_ref, w1_ref, w2_ref,
                   out_ref, *, ff_chunk):
    hs = hf_ref[...].astype(F32) + hb_ref[...].astype(F32)
    x1 = x_ref[...] + _dot(mb_ref[...], wo_ref[MW:, :])
    for hh in range(N_M):
        sl = slice(hh * HD_M, (hh + 1) * HD_M)
        t = hs[:, sl]
        ms = jnp.mean(t * t, axis=-1, keepdims=True)
        mix = t * lax.rsqrt(ms + EPS) * ng_ref[:, sl] * _sigmoid(og_ref[:, sl].astype(F32))
        x1 = x1 + _dot(mix.astype(BF16), wo_ref[sl, :])
    ms = jnp.mean(x1 * x1, axis=-1, keepdims=True)
    h2 = (x1 * lax.rsqrt(ms + EPS) * g2_ref[...]).astype(BF16)
    out_ref[...] = x1
    for j in range(w1_ref.shape[1] // ff_chunk):
        sl = slice(j * ff_chunk, (j + 1) * ff_chunk)
        u = jnp.maximum(_dot(h2, w1_ref[:, sl]), 0.0)
        out_ref[...] += _dot((u * u).astype(BF16), w2_ref[sl, :])


def _outffn(x, hf, hb, om, mixb, ng, wo, g2, w1, w2, layer, tm):
    n, d = x.shape
    dff = w1.shape[2]
    row = lambda i: (i, 0)
    return pl.pallas_call(
        functools.partial(_outffn_kernel, ff_chunk=1024),
        grid=(n // tm,),
        in_specs=[
            pl.BlockSpec((tm, d), row),
            pl.BlockSpec((tm, MW), row),
            pl.BlockSpec((tm, MW), row),
            pl.BlockSpec((tm, MW), row),
            pl.BlockSpec((tm, AW), row),
            _resident((1, MW)),
            _resident_layer((MW + AW, d), layer),
            _resident((1, d)),
            _resident_layer((d, dff), layer),
            _resident_layer((dff, d), layer),
        ],
        out_specs=pl.BlockSpec((tm, d), row),
        out_shape=jax.ShapeDtypeStruct((n, d), F32),
        compiler_params=pltpu.CompilerParams(
            dimension_semantics=("parallel",), vmem_limit_bytes=VMEM_LIMIT),
        name="outproj_ffn",
    )(x, hf, hb, om, mixb, ng, wo, g2, w1, w2)


def kernel(x, norm1_g, w_in, conv_w, conv_b, gate_b, mlstm_norm_g, q_norm_g, k_norm_g, lambdas,
           diff_norm_g, rel_bias, w_out, norm2_g, w_ff1, w_ff2):
    batch, seq, d = x.shape
    depth = w_in.shape[0]
    n = batch * seq
    tm = tile = ROW_TILE
    assert d == w_in.shape[1] and w_in.shape[2] == C_G + N_GATES + 3 * AW
    assert seq % tile == 0 and tile % MLSTM_CHUNK == 0 and tile >= REL_MAX_DIST
    assert seq // tile >= ATTN_WINDOW and (seq // tile - ATTN_WINDOW) % 2 == 0
    assert (seq // tile) % ATTN_Q_TILES == 0

    w_main = jnp.concatenate([w_in[:, :, :C_G], w_in[:, :, C_G + N_GATES:]], axis=-1).astype(BF16)
    w_gate = jnp.pad(w_in[:, :, C_G:C_G + N_GATES], ((0, 0), (0, 0), (0, LANES - N_GATES))).astype(BF16)
    w_out_b = w_out.astype(BF16)
    w_ff1_b = w_ff1.astype(BF16)
    w_ff2_b = w_ff2.astype(BF16)
    lane = jnp.arange(2 * LANES)
    bd = jnp.where((lane[:, None] // HD_A) == (lane[None, :] // HD_A), 1.0 / HD_A, 0.0).astype(BF16)

    bias_t = _bias_tiles(rel_bias.astype(F32), tile)

    xf = x.reshape(n, d)
    for l in range(depth):
        lam_init = 0.8 - 0.6 * math.exp(-0.3 * l)
        qg = jnp.tile(q_norm_g[l], 2 * LANES // HD_A)[None, :]
        kg = jnp.tile(k_norm_g[l], 2 * LANES // HD_A)[None, :]
        gb = jnp.pad(gate_b[l].reshape(1, N_GATES).astype(F32), ((0, 0), (0, LANES - N_GATES)))
        qtm, km, vtm, om, gate_cols, gate_rows, qn, k1, k2, vt = _inproj(
            xf, norm1_g[l][None, :], w_main, w_gate, l, gb, conv_w[l], conv_b[l][None, :],
            qg, kg, bd, seq, tm)
        hf, hb = _mlstm(qtm, km, vtm, gate_cols, gate_rows, batch, seq, tm)
        mixb = _attention(qn, k1, k2, vt, qg, kg, bias_t, rel_bias.astype(F32),
                          jnp.array([[lam_init, 1.0 - lam_init]], F32), lambdas[l],
                          diff_norm_g[l].reshape(N_A, 2 * HD_A, 1), batch, seq, tile)
        xf = _outffn(xf, hf, hb, om, mixb, mlstm_norm_g[l][None, :], w_out_b,
                     norm2_g[l][None, :], w_ff1_b, w_ff2_b, l, tm)
    return xf.reshape(batch, seq, d)
```

```python
import functools
import math

import jax
import jax.numpy as jnp
from jax import lax
from jax.experimental import pallas as pl
from jax.experimental.pallas import tpu as pltpu

F32 = jnp.float32
BF16 = jnp.bfloat16

N_M = 4
HD_M = 128
MW = N_M * HD_M
MLSTM_CHUNK = 128
CONV_W = 5
N_A = 4
HD_A = 64
AW = N_A * 2 * HD_A
REL_BUCKETS = 32
REL_MAX_DIST = 128
EPS = 1e-6
N_GATES = 4 * N_M
C_G = 4 * MW
LANES = 128
HALO = 16
VMEM_LIMIT = 56 * 1024 * 1024
LOG2E = math.log2(math.e)
VT_ROWS = 2 * HD_A + HALO
BIAS_ROWS = 3
LOGIT_SPAN_LIMIT = 96.0
ATTN_WINDOW = 4
ATTN_Q_TILES = 2
ROW_TILE = 512
FAR_BUCKET = REL_BUCKETS // 2 - 1


def _dot(a, b):
    return jnp.dot(a, b, preferred_element_type=F32)


def _dot_exact(a, b):
    return jnp.dot(a, b, preferred_element_type=F32, precision=lax.Precision.HIGHEST)


def _sigmoid(x):
    return 1.0 / (1.0 + jnp.exp(-x))


def _log_sigmoid(x):
    return jnp.minimum(x, 0.0) - jnp.log1p(jnp.exp(-jnp.abs(x)))


def _resident(shape):
    nd = len(shape)
    return pl.BlockSpec(shape, lambda *_: (0,) * nd, pipeline_mode=pl.Buffered(1))


def _resident_layer(shape, layer):
    nd = len(shape)
    return pl.BlockSpec((None,) + tuple(shape), lambda *_: (layer,) + (0,) * nd,
                        pipeline_mode=pl.Buffered(1))


def _inproj_kernel(xp_ref, x_ref, xn_ref, g1_ref, w_ref, wg_ref, gb_ref, cw_ref, cb_ref, qg_ref, kg_ref,
                   bd_ref,
                   qtm_ref, km_ref, vtm_ref, om_ref, gcol_ref, grow_ref, qn_ref, k1_ref, k2_ref, vt_ref,
                   ext_s, *, tiles_per_seq):
    rows = x_ref.shape[0]
    pos = pl.program_id(0) % tiles_per_seq

    def normed(xv):
        ms = jnp.mean(xv * xv, axis=-1, keepdims=True)
        return (xv * lax.rsqrt(ms + EPS) * g1_ref[...]).astype(BF16)

    x = x_ref[...]
    h = normed(x)

    def proj(c0, width):
        return _dot(h, w_ref[:, c0:c0 + width])

    h_prev = jnp.where(pos > 0, normed(xp_ref[...]), jnp.zeros((), BF16))
    h_next = jnp.where(pos < tiles_per_seq - 1, normed(xn_ref[...]), jnp.zeros((), BF16))
    ext_s[...] = _dot(jnp.concatenate([h_prev, h, h_next], axis=0), w_ref[:, 0:2 * MW])

    def conv_head(cb):
        cols = slice(cb * LANES, (cb + 1) * LANES)
        conv = cb_ref[:, cols]
        for j in range(CONV_W):
            off = HALO - CONV_W // 2 + j
            conv = conv + ext_s[off:off + rows, cols] * cw_ref[j:j + 1, cols]
        y = conv * _sigmoid(conv)
        if cb < N_M:
            qtm_ref[cols, :] = y.T.astype(BF16)
        else:
            km_ref[:, cb * LANES - MW:(cb + 1) * LANES - MW] = (y * (HD_M ** -0.5)).astype(BF16)

    ones_rows = jnp.ones((HALO, rows), BF16)
    pair = 2 * LANES

    def values_t(c0, out_ref, hp):
        v2 = proj(c0 + hp * pair, pair)
        for sub in range(2):
            hh = 2 * hp + sub
            out_ref[0, hh * VT_ROWS:hh * VT_ROWS + HD_M, :] = (
                v2[:, sub * LANES:(sub + 1) * LANES].T.astype(BF16))
            out_ref[0, hh * VT_ROWS + HD_M:(hh + 1) * VT_ROWS, :] = ones_rows

    def out_gate():
        om_ref[...] = proj(3 * MW, MW).astype(BF16)
        L = MLSTM_CHUNK
        gt = (_dot(h, wg_ref[...]) + gb_ref[...]).T[:N_GATES]
        logf = _log_sigmoid(gt)
        g_under_f = pltpu.roll(gt, N_M, 0)
        upper_f = (lax.broadcasted_iota(jnp.int32, (L, L), 0)
                   <= lax.broadcasted_iota(jnp.int32, (L, L), 1)).astype(F32)
        forward_rows = lax.broadcasted_iota(jnp.int32, (N_GATES, L), 0) < 2 * N_M
        for r in range(rows // L):
            sl = slice(r * L, (r + 1) * L)
            cum = _dot_exact(logf[:, sl], upper_f)
            suf = cum[:, L - 1:L] - cum + logf[:, sl]
            grow_ref[0:N_GATES, sl] = cum
            grow_ref[N_GATES:2 * N_GATES, sl] = gt[:, sl]
            grow_ref[2 * N_GATES:3 * N_GATES, sl] = suf
            src = jnp.where(forward_rows, cum, suf) - g_under_f[:, sl]
            src = jnp.concatenate([src, jnp.zeros((L - N_GATES, L), F32)], axis=0)
            gcol_ref[sl, :] = src.T[:, :N_GATES]

    bd = bd_ref[...]

    def qk_norm(c0, g_ref):
        t = proj(c0, pair)
        msq = _dot((t * t).astype(BF16), bd)
        return t * lax.rsqrt(msq + EPS) * g_ref[...]

    lane = lax.broadcasted_iota(jnp.int32, (rows, pair), 1) % LANES
    ones_cols = jnp.where(lane < HD_A + BIAS_ROWS, 1.0, 0.0)

    def attn_q(hp):
        sl = slice(hp * pair, (hp + 1) * pair)
        qn_ref[:, sl] = (qk_norm(4 * MW + hp * pair, qg_ref) * (HD_A ** -0.5 * LOG2E)).astype(BF16)

    def attn_k(hp):
        sl = slice(hp * pair, (hp + 1) * pair)
        kn = qk_norm(4 * MW + AW + hp * pair, kg_ref)
        k1_ref[:, sl] = jnp.where(lane < HD_A, kn, ones_cols).astype(BF16)
        k2_ref[:, sl] = jnp.where(lane < HD_A, pltpu.roll(kn, pair - HD_A, 1), ones_cols).astype(BF16)

    projections = [
        functools.partial(values_t, 2 * MW, vtm_ref, 0), functools.partial(values_t, 2 * MW, vtm_ref, 1),
        out_gate,
        functools.partial(attn_q, 0), functools.partial(attn_k, 0),
        functools.partial(attn_q, 1), functools.partial(attn_k, 1),
        functools.partial(values_t, 4 * MW + 2 * AW, vt_ref, 0),
    ]
    for cb, projection in enumerate(projections):
        conv_head(cb)
        projection()
    values_t(4 * MW + 2 * AW, vt_ref, 1)


def _inproj(x, g1, w_main, w_gate, layer, gate_b, conv_w, conv_b, qg, kg, bd, seq, tm):
    n, d = x.shape
    wcols = w_main.shape[2]
    hpt = tm // HALO
    nhalo = n // HALO
    row = lambda i: (i, 0)
    vt_spec = pl.BlockSpec((1, N_A * VT_ROWS, tm), lambda i: (i, 0, 0))
    vt_shape = jax.ShapeDtypeStruct((n // tm, N_A * VT_ROWS, tm), BF16)
    return pl.pallas_call(
        functools.partial(_inproj_kernel, tiles_per_seq=seq // tm),
        grid=(n // tm,),
        in_specs=[
            pl.BlockSpec((HALO, d), lambda i: (jnp.maximum(i * hpt - 1, 0), 0)),
            pl.BlockSpec((tm, d), row),
            pl.BlockSpec((HALO, d), lambda i: (jnp.minimum((i + 1) * hpt, nhalo - 1), 0)),
            _resident((1, d)),
            _resident_layer((d, wcols), layer),
            _resident_layer((d, LANES), layer),
            _resident((1, LANES)),
            _resident((CONV_W, 2 * MW)),
            _resident((1, 2 * MW)),
            _resident((1, 2 * LANES)),
            _resident((1, 2 * LANES)),
            _resident((2 * LANES, 2 * LANES)),
        ],
        out_specs=[
            pl.BlockSpec((MW, tm), lambda i: (0, i)),
            pl.BlockSpec((tm, MW), row),
            vt_spec,
            pl.BlockSpec((tm, MW), row),
            pl.BlockSpec((tm, N_GATES), row),
            pl.BlockSpec((3 * N_GATES, tm), lambda i: (0, i)),
            pl.BlockSpec((tm, AW), row),
            pl.BlockSpec((tm, AW), row),
            pl.BlockSpec((tm, AW), row),
            vt_spec,
        ],
        out_shape=[
            jax.ShapeDtypeStruct((MW, n), BF16),
            jax.ShapeDtypeStruct((n, MW), BF16),
            vt_shape,
            jax.ShapeDtypeStruct((n, MW), BF16),
            jax.ShapeDtypeStruct((n, N_GATES), F32),
            jax.ShapeDtypeStruct((3 * N_GATES, n), F32),
            jax.ShapeDtypeStruct((n, AW), BF16),
            jax.ShapeDtypeStruct((n, AW), BF16),
            jax.ShapeDtypeStruct((n, AW), BF16),
            vt_shape,
        ],
        scratch_shapes=[pltpu.VMEM((tm + 2 * HALO, 2 * MW), F32)],
        compiler_params=pltpu.CompilerParams(
            dimension_semantics=("parallel",), vmem_limit_bytes=VMEM_LIMIT),
        name="inproj",
    )(x, x, x, g1, w_main, w_gate, gate_b, conv_w, conv_b, qg, kg, bd)


def _mlstm_kernel(qt_f, k_f, vt_f, gc_f, gr_f,
                  qt_b, k_b, vt_b, gc_b, gr_b,
                  hf_ref, hb_ref,
                  c_s, m_s):
    L = MLSTM_CHUNK
    chunks = k_f.shape[0] // L

    @pl.when(pl.program_id(1) == 0)
    def _():
        c_s[...] = jnp.zeros_like(c_s)
        m_s[...] = jnp.zeros_like(m_s)

    row = lax.broadcasted_iota(jnp.int32, (L, L), 0)
    col = lax.broadcasted_iota(jnp.int32, (L, L), 1)
    lower = col <= row
    upper = row <= col
    dirs = ((qt_f, k_f, vt_f, gc_f, gr_f, hf_ref), (qt_b, k_b, vt_b, gc_b, gr_b, hb_ref))
    state = [(c_s[idx], m_s[idx][:, :1]) for idx in range(2 * N_M)]

    for step in range(chunks):
        chains = []
        for d, (qt_ref, k_ref, vt_ref, gc_ref, gr_ref, out_ref) in enumerate(dirs):
            tok = slice(step * L, (step + 1) * L) if d == 0 else slice((chunks - 1 - step) * L,
                                                                       (chunks - step) * L)
            i_off = 2 * N_M * d
            f_off = i_off + N_M
            src = gc_ref[tok, :]
            cum_rows = gr_ref[0:N_GATES, tok] if d == 0 else gr_ref[2 * N_GATES:3 * N_GATES, tok]
            gate_rows = gr_ref[N_GATES:2 * N_GATES, tok]
            for hh in range(N_M):
                idx = d * N_M + hh
                chains.append(dict(
                    idx=idx, out_ref=out_ref, tok=tok, cols=slice(hh * HD_M, (hh + 1) * HD_M),
                    mask=upper if d == 0 else lower,
                    src_col=src[:, f_off + hh:f_off + hh + 1],
                    b_row=cum_rows[f_off + hh:f_off + hh + 1, :],
                    i_row=gate_rows[i_off + hh:i_off + hh + 1, :],
                    end=slice(L - 1, L) if d == 0 else slice(0, 1),
                    qt=qt_ref[hh * HD_M:(hh + 1) * HD_M, tok],
                    kb=k_ref[tok, hh * HD_M:(hh + 1) * HD_M],
                    vt=vt_ref[0, hh * VT_ROWS:(hh + 1) * VT_ROWS, tok],
                    c_prev=state[idx][0], m_prev=state[idx][1],
                ))

        for ch in chains:
            ch["s"] = _dot(ch["kb"], ch["qt"])
            ch["inter_num"] = _dot(ch["c_prev"].astype(BF16), ch["qt"])
        for ch in chains:
            dmat = jnp.where(ch["mask"], ch["b_row"] - ch["src_col"], -jnp.inf)
            inter = ch["b_row"] + ch["m_prev"]
            ch["m_t"] = jnp.maximum(inter, jnp.max(dmat, axis=0, keepdims=True))
            ch["scale"] = jnp.exp(inter - ch["m_t"])
            ch["w"] = (ch["s"] * jnp.exp(dmat - ch["m_t"])).astype(BF16)
        for ch in chains:
            num = ch["scale"] * ch["inter_num"] + _dot(ch["vt"], ch["w"])
            den = num[HD_M:HD_M + 1]
            ht = num[:HD_M] / jnp.maximum(jnp.abs(den), jnp.exp(-ch["m_t"]))
            ch["out_ref"][ch["tok"], ch["cols"]] = ht.T.astype(ch["out_ref"].dtype)
        for ch in chains:
            b_end = ch["b_row"][:, ch["end"]]
            g_row = b_end - ch["b_row"] + ch["i_row"]
            m_new = jnp.maximum(b_end + ch["m_prev"], jnp.max(g_row, axis=-1, keepdims=True))
            decay = jnp.exp(b_end + ch["m_prev"] - m_new)
            vt_w = (ch["vt"].astype(F32) * jnp.exp(g_row - m_new)).astype(BF16)
            state[ch["idx"]] = (decay * ch["c_prev"] + _dot(vt_w, ch["kb"]), m_new)

    for idx, (c_new, m_new) in enumerate(state):
        c_s[idx] = c_new
        m_s[idx] = jnp.broadcast_to(m_new, (1, LANES))


def _mlstm(qt, k, vt, gate_cols, gate_rows, batch, seq, tm):
    nt = seq // tm

    def tile_f(b, c):
        return b * nt + c

    def tile_b(b, c):
        return b * nt + (nt - 1 - c)

    def specs(tile):
        return [
            pl.BlockSpec((MW, tm), lambda b, c: (0, tile(b, c))),
            pl.BlockSpec((tm, MW), lambda b, c: (tile(b, c), 0)),
            pl.BlockSpec((1, N_M * VT_ROWS, tm), lambda b, c: (tile(b, c), 0, 0)),
            pl.BlockSpec((tm, N_GATES), lambda b, c: (tile(b, c), 0)),
            pl.BlockSpec((3 * N_GATES, tm), lambda b, c: (0, tile(b, c))),
        ]

    n = batch * seq
    return pl.pallas_call(
        _mlstm_kernel,
        grid=(batch, nt),
        in_specs=specs(tile_f) + specs(tile_b),
        out_specs=[
            pl.BlockSpec((tm, MW), lambda b, c: (tile_f(b, c), 0)),
            pl.BlockSpec((tm, MW), lambda b, c: (tile_b(b, c), 0)),
        ],
        out_shape=[jax.ShapeDtypeStruct((n, MW), BF16)] * 2,
        scratch_shapes=[
            pltpu.VMEM((2 * N_M, VT_ROWS, HD_M), F32),
            pltpu.VMEM((2 * N_M, 1, LANES), F32),
        ],
        compiler_params=pltpu.CompilerParams(
            dimension_semantics=("parallel", "arbitrary"), vmem_limit_bytes=VMEM_LIMIT),
        name="mlstm",
    )(qt, k, vt, gate_cols, gate_rows, qt, k, vt, gate_cols, gate_rows)


def _bias_kernel(rb_ref, out_ref, *, tile):
    hh = pl.program_id(0)
    dd = pl.program_id(1)
    kk = lax.broadcasted_iota(jnp.int32, (tile, tile), 0)
    qq = lax.broadcasted_iota(jnp.int32, (tile, tile), 1)
    rel = (dd - 1) * tile + kk - qq
    nb = REL_BUCKETS // 2
    max_exact = nb // 2
    n = jnp.abs(rel)
    nf = jnp.maximum(n, 1).astype(F32)
    large = max_exact + (jnp.log(nf / max_exact) / math.log(REL_MAX_DIST / max_exact)
                         * (nb - max_exact)).astype(jnp.int32)
    large = jnp.minimum(large, nb - 1)
    bucket = jnp.where(rel > 0, nb, 0) + jnp.where(n < max_exact, n, large)
    val = jnp.zeros((tile, tile), F32)
    for bkt in range(REL_BUCKETS):
        val = jnp.where(bucket == bkt, rb_ref[bkt, hh], val)
    out_ref[0, 0] = jnp.where(dd < 3, val * LOG2E, 0.0)


def _bias_tiles(rel_bias, tile):
    return pl.pallas_call(
        functools.partial(_bias_kernel, tile=tile),
        grid=(N_A, 4),
        in_specs=[pl.BlockSpec(memory_space=pltpu.SMEM)],
        out_specs=pl.BlockSpec((1, 1, tile, tile), lambda h, d: (h, d, 0, 0)),
        out_shape=jax.ShapeDtypeStruct((N_A, 4, tile, tile), F32),
        name="rel_bias_tiles",
    )(rel_bias)


def _sublane_partial_sums(e):
    return jnp.sum(e.reshape(e.shape[0] // 8, 8, e.shape[1]), axis=0)


def _attn_kernel(rb_ref, li_ref, q_ref, k1_ref, k2_ref, vt_ref, qg_ref, kg_ref, bias_ref, lam_ref, sg_ref,
                 o_ref,
                 qt_s, m_s, l_s, acc_s, p0_s, p1_s):
    hh = pl.program_id(0)
    tq = vt_ref.shape[2]
    n_sub = q_ref.shape[0] // tq

    bucket_vals = [rb_ref[bkt, hh] for bkt in range(REL_BUCKETS)]
    bias_max = functools.reduce(jnp.maximum, bucket_vals) * LOG2E
    bias_min = functools.reduce(jnp.minimum, bucket_vals) * LOG2E
    reach = (HD_A ** 0.5 * LOG2E) * jnp.max(jnp.abs(qg_ref[...])) * jnp.max(jnp.abs(kg_ref[...]))
    shift = reach + bias_max
    span = 2.0 * reach + (bias_max - bias_min)
    bounded = span <= LOGIT_SPAN_LIMIT

    def query_tiles(fixed_shift):
        for sub in range(n_sub):
            rows = pl.ds(sub * tq, tq)
            _attn_query_tile(pl.program_id(2) * n_sub + sub, fixed_shift,
                             rb_ref, li_ref, q_ref.at[rows], k1_ref, k2_ref, vt_ref, bias_ref, lam_ref,
                             sg_ref, o_ref.at[rows], qt_s.at[sub], m_s.at[sub], l_s.at[sub],
                             acc_s.at[sub], p0_s.at[sub], p1_s.at[sub])

    pl.when(bounded)(lambda: query_tiles(shift))
    pl.when(jnp.logical_not(bounded))(lambda: query_tiles(None))


def _attn_query_tile(qi, fixed_shift, rb_ref, li_ref, q_ref, k1_ref, k2_ref, vt_ref, bias_ref,
                     lam_ref, sg_ref, o_ref, qt_s, m_s, l_s, acc_s, p0_s, p1_s):
    hh = pl.program_id(0)
    tq = q_ref.shape[0]
    tk = vt_ref.shape[2]
    nk = vt_ref.shape[0]
    vd = 2 * HD_A
    LEFT, RIGHT, NEAR = 0, 1, 2

    qt = q_ref[...].astype(F32).T
    rid = lax.broadcasted_iota(jnp.int32, qt.shape, 0)
    bases = (qt, jnp.concatenate([qt[HD_A:], qt[:HD_A]], axis=0))
    consts = (rb_ref[FAR_BUCKET, hh] * LOG2E, rb_ref[REL_BUCKETS // 2 + FAR_BUCKET, hh] * LOG2E, 0.0)

    def build_queries(shift):
        for i, base in enumerate(bases):
            body = jnp.where(rid < HD_A, base, 0.0).astype(BF16)
            for kind in range(len(consts)):
                qt_s[2 * kind + i] = body
        rid_o = lax.broadcasted_iota(jnp.int32, (HALO, tq), 0)
        for kind, cst in enumerate(consts):
            rest = jnp.full((HALO, tq), cst - shift, F32)
            extra = jnp.zeros((HALO, tq), F32)
            for r in range(BIAS_ROWS):
                part = rest.astype(BF16).astype(F32)
                extra = jnp.where(rid_o == r, part, extra)
                rest = rest - part
            for i in range(2):
                qt_s[2 * kind + i, HD_A:HD_A + HALO, :] = extra.astype(BF16)

    acc_s[...] = jnp.zeros_like(acc_s)
    l_s[...] = jnp.zeros_like(l_s)

    def key_rows(j):
        return pl.ds(pl.multiple_of(j * tk, tk), tk)

    if fixed_shift is None:
        _online_softmax_tiles(qi, nk, build_queries, key_rows, k1_ref, k2_ref, vt_ref, bias_ref,
                              qt_s, m_s, l_s, acc_s)
    else:
        build_queries(fixed_shift)

        def numerators(j, kind, p_buf, bias=None):
            for i, k_ref in enumerate((k1_ref, k2_ref)):
                s = _dot(k_ref[key_rows(j), :], qt_s[2 * kind + i])
                if bias is not None:
                    s = s + bias
                e = jnp.exp2(s)
                l_s[i] += _sublane_partial_sums(e)
                p_buf[i] = e.astype(BF16)

        def accumulate(j, p_buf):
            vt = vt_ref[j, :vd, :]
            for i in range(2):
                acc_s[i] += _dot(vt, p_buf[i])

        w0 = jnp.clip(qi - 1, 0, nk - ATTN_WINDOW)
        far_tile = lambda t: jnp.where(t < w0, t, t + ATTN_WINDOW)
        far_kind = lambda t: jnp.where(t < w0, LEFT, RIGHT)
        p_bufs = (p0_s, p1_s)
        for r in range(ATTN_WINDOW):
            j = w0 + r
            dlt = j - qi
            near = jnp.abs(dlt) <= 1
            kind = jnp.where(near, NEAR, jnp.where(dlt < 0, LEFT, RIGHT))
            numerators(j, kind, p_bufs[r % 2], bias_ref[0, jnp.where(near, dlt + 1, 3)])
            if r > 0:
                accumulate(j - 1, p_bufs[(r - 1) % 2])
        last_window = w0 + ATTN_WINDOW - 1

        n_far = nk - ATTN_WINDOW
        for t0 in range(0, n_far, 2):
            numerators(far_tile(t0), far_kind(t0), p0_s)
            accumulate(far_tile(t0 - 1) if t0 else last_window, p1_s)
            numerators(far_tile(t0 + 1), far_kind(t0 + 1), p1_s)
            accumulate(far_tile(t0), p0_s)
        accumulate(far_tile(n_far - 1) if n_far else last_window, p1_s)

    lv = lam_ref[...]
    lam = (jnp.exp(jnp.sum(lv[0:1] * lv[1:2], axis=-1, keepdims=True))
           - jnp.exp(jnp.sum(lv[2:3] * lv[3:4], axis=-1, keepdims=True)) + li_ref[0, 0])
    sums = [jnp.sum(l_s[i], axis=0, keepdims=True) for i in range(2)]
    o = acc_s[0] / sums[0] - lam * (acc_s[1] / sums[1])
    ms = jnp.mean(o * o, axis=0, keepdims=True)
    o = o * lax.rsqrt(ms + EPS) * sg_ref[0] * li_ref[0, 1]
    o_ref[...] = o.T.astype(o_ref.dtype)


def _online_softmax_tiles(qi, nk, build_queries, key_rows, k1_ref, k2_ref, vt_ref, bias_ref,
                          qt_s, m_s, l_s, acc_s):
    LEFT, RIGHT, NEAR = 0, 1, 2
    build_queries(0.0)
    m_s[...] = jnp.full_like(m_s, -jnp.inf)

    def one_tile(j, carry):
        dlt = j - qi
        near = jnp.abs(dlt) <= 1
        kind = jnp.where(near, NEAR, jnp.where(dlt < 0, LEFT, RIGHT))
        bias = bias_ref[0, jnp.where(near, dlt + 1, 3)]
        vt = vt_ref[j, :2 * HD_A, :]
        for i, k_ref in enumerate((k1_ref, k2_ref)):
            s = _dot(k_ref[key_rows(j), :], qt_s[2 * kind + i]) + bias
            m_old = m_s[i]
            m_new = jnp.maximum(m_old, jnp.max(s, axis=0, keepdims=True))
            alpha = jnp.exp2(m_old - m_new)
            e = jnp.exp2(s - m_new)
            l_s[i] = alpha * l_s[i] + _sublane_partial_sums(e)
            acc_s[i] = alpha * acc_s[i] + _dot(vt, e.astype(BF16))
            m_s[i] = m_new
        return carry

    lax.fori_loop(0, nk, one_tile, 0)


def _attention(qn, k1, k2, vt, qg, kg, bias_t, rel_bias, lam_init, lambdas, sub_g_col, batch, seq, tile):
    nt = seq // tile
    nq = nt // ATTN_Q_TILES
    vd = 2 * HD_A
    smem = pl.BlockSpec(memory_space=pltpu.SMEM)
    q_rows = ATTN_Q_TILES * tile
    return pl.pallas_call(
        _attn_kernel,
        grid=(N_A, batch, nq),
        in_specs=[
            smem,
            smem,
            pl.BlockSpec((q_rows, vd), lambda h, b, i: (b * nq + i, h)),
            pl.BlockSpec((seq, vd), lambda h, b, i: (b, h)),
            pl.BlockSpec((seq, vd), lambda h, b, i: (b, h)),
            pl.BlockSpec((nt, VT_ROWS, tile), lambda h, b, i: (b, h, 0)),
            pl.BlockSpec((1, 2 * LANES), lambda h, b, i: (0, 0)),
            pl.BlockSpec((1, 2 * LANES), lambda h, b, i: (0, 0)),
            pl.BlockSpec((1, 4, tile, tile), lambda h, b, i: (h, 0, 0, 0)),
            pl.BlockSpec((4, HD_A), lambda h, b, i: (0, 0)),
            pl.BlockSpec((1, vd, 1), lambda h, b, i: (h, 0, 0)),
        ],
        out_specs=pl.BlockSpec((q_rows, vd), lambda h, b, i: (b * nq + i, h)),
        out_shape=jax.ShapeDtypeStruct((batch * seq, AW), BF16),
        scratch_shapes=[
            pltpu.VMEM((ATTN_Q_TILES, 6, vd, tile), BF16),
            pltpu.VMEM((ATTN_Q_TILES, 2, 1, tile), F32),
            pltpu.VMEM((ATTN_Q_TILES, 2, 8, tile), F32),
            pltpu.VMEM((ATTN_Q_TILES, 2, vd, tile), F32),
            pltpu.VMEM((ATTN_Q_TILES, 2, tile, tile), BF16),
            pltpu.VMEM((ATTN_Q_TILES, 2, tile, tile), BF16),
        ],
        compiler_params=pltpu.CompilerParams(
            dimension_semantics=("parallel", "parallel", "arbitrary"),
            vmem_limit_bytes=VMEM_LIMIT),
        name="diff_attention",
    )(rel_bias, lam_init, qn, k1, k2, vt, qg, kg, bias_t, lambdas, sub_g_col)


def _outffn_kernel(x_ref, hf_ref, hb_ref, og_ref, mb_ref, ng_ref, wo_ref, g2_ref, w1_ref, w2_ref,
                   out_ref, *, ff_chunk):
    hs = hf_ref[...].astype(F32) + hb_ref[...].astype(F32)
    x1 = x_ref[...] + _dot(mb_ref[...], wo_ref[MW:, :])
    for hh in range(N_M):
        sl = slice(hh * HD_M, (hh + 1) * HD_M)
        t = hs[:, sl]
        ms = jnp.mean(t * t, axis=-1, keepdims=True)
        mix = t * lax.rsqrt(ms + EPS) * ng_ref[:, sl] * _sigmoid(og_ref[:, sl].astype(F32))
        x1 = x1 + _dot(mix.astype(BF16), wo_ref[sl, :])
    ms = jnp.mean(x1 * x1, axis=-1, keepdims=True)
    h2 = (x1 * lax.rsqrt(ms + EPS) * g2_ref[...]).astype(BF16)
    out_ref[...] = x1
    for j in range(w1_ref.shape[1] // ff_chunk):
        sl = slice(j * ff_chunk, (j + 1) * ff_chunk)
        u = jnp.maximum(_dot(h2, w1_ref[:, sl]), 0.0)
        out_ref[...] += _dot((u * u).astype(BF16), w2_ref[sl, :])


def _outffn(x, hf, hb, om, mixb, ng, wo, g2, w1, w2, layer, tm):
    n, d = x.shape
    dff = w1.shape[2]
    row = lambda i: (i, 0)
    return pl.pallas_call(
        functools.partial(_outffn_kernel, ff_chunk=1024),
        grid=(n // tm,),
        in_specs=[
            pl.BlockSpec((tm, d), row),
            pl.BlockSpec((tm, MW), row),
            pl.BlockSpec((tm, MW), row),
            pl.BlockSpec((tm, MW), row),
            pl.BlockSpec((tm, AW), row),
            _resident((1, MW)),
            _resident_layer((MW + AW, d), layer),
            _resident((1, d)),
            _resident_layer((d, dff), layer),
            _resident_layer((dff, d), layer),
        ],
        out_specs=pl.BlockSpec((tm, d), row),
        out_shape=jax.ShapeDtypeStruct((n, d), F32),
        compiler_params=pltpu.CompilerParams(
            dimension_semantics=("parallel",), vmem_limit_bytes=VMEM_LIMIT),
        name="outproj_ffn",
    )(x, hf, hb, om, mixb, ng, wo, g2, w1, w2)


def kernel(x, norm1_g, w_in, conv_w, conv_b, gate_b, mlstm_norm_g, q_norm_g, k_norm_g, lambdas,
           diff_norm_g, rel_bias, w_out, norm2_g, w_ff1, w_ff2):
    batch, seq, d = x.shape
    depth = w_in.shape[0]
    n = batch * seq
    tm = tile = ROW_TILE
    assert d == w_in.shape[1] and w_in.shape[2] == C_G + N_GATES + 3 * AW
    assert seq % tile == 0 and tile % MLSTM_CHUNK == 0 and tile >= REL_MAX_DIST
    assert seq // tile >= ATTN_WINDOW and (seq // tile - ATTN_WINDOW) % 2 == 0
    assert (seq // tile) % ATTN_Q_TILES == 0

    w_main = jnp.concatenate([w_in[:, :, :C_G], w_in[:, :, C_G + N_GATES:]], axis=-1).astype(BF16)
    w_gate = jnp.pad(w_in[:, :, C_G:C_G + N_GATES], ((0, 0), (0, 0), (0, LANES - N_GATES))).astype(BF16)
    w_out_b = w_out.astype(BF16)
    w_ff1_b = w_ff1.astype(BF16)
    w_ff2_b = w_ff2.astype(BF16)
    lane = jnp.arange(2 * LANES)
    bd = jnp.where((lane[:, None] // HD_A) == (lane[None, :] // HD_A), 1.0 / HD_A, 0.0).astype(BF16)

    bias_t = _bias_tiles(rel_bias.astype(F32), tile)

    xf = x.reshape(n, d)
    for l in range(depth):
        lam_init = 0.8 - 0.6 * math.exp(-0.3 * l)
        qg = jnp.tile(q_norm_g[l], 2 * LANES // HD_A)[None, :]
        kg = jnp.tile(k_norm_g[l], 2 * LANES // HD_A)[None, :]
        gb = jnp.pad(gate_b[l].reshape(1, N_GATES).astype(F32), ((0, 0), (0, LANES - N_GATES)))
        qtm, km, vtm, om, gate_cols, gate_rows, qn, k1, k2, vt = _inproj(
            xf, norm1_g[l][None, :], w_main, w_gate, l, gb, conv_w[l], conv_b[l][None, :],
            qg, kg, bd, seq, tm)
        hf, hb = _mlstm(qtm, km, vtm, gate_cols, gate_rows, batch, seq, tm)
        mixb = _attention(qn, k1, k2, vt, qg, kg, bias_t, rel_bias.astype(F32),
                          jnp.array([[lam_init, 1.0 - lam_init]], F32), lambdas[l],
                          diff_norm_g[l].reshape(N_A, 2 * HD_A, 1), batch, seq, tile)
        xf = _outffn(xf, hf, hb, om, mixb, mlstm_norm_g[l][None, :], w_out_b,
                     norm2_g[l][None, :], w_ff1_b, w_ff2_b, l, tm)
    return xf.reshape(batch, seq, d)
```

```python
import functools
import math

import jax
import jax.numpy as jnp
from jax import lax
from jax.experimental import pallas as pl
from jax.experimental.pallas import tpu as pltpu

F32 = jnp.float32
BF16 = jnp.bfloat16

N_M = 4
HD_M = 128
MW = N_M * HD_M
MLSTM_CHUNK = 128
CONV_W = 5
N_A = 4
HD_A = 64
AW = N_A * 2 * HD_A
REL_BUCKETS = 32
REL_MAX_DIST = 128
EPS = 1e-6
N_GATES = 4 * N_M
C_G = 4 * MW
LANES = 128
HALO = 16
VMEM_LIMIT = 56 * 1024 * 1024
LOG2E = math.log2(math.e)
VT_ROWS = HD_M + HALO
BIAS_ROWS = 3
LOGIT_SPAN_LIMIT = 96.0
ATTN_WINDOW = 4
ATTN_Q_TILES = 2
ROW_TILE = 512
FAR_BUCKET = REL_BUCKETS // 2 - 1


def _dot(a, b):
    return jnp.dot(a, b, preferred_element_type=F32)


def _dot_exact(a, b):
    return jnp.dot(a, b, preferred_element_type=F32, precision=lax.Precision.HIGHEST)


def _sigmoid(x):
    return 1.0 / (1.0 + jnp.exp(-x))


def _log_sigmoid(x):
    return jnp.minimum(x, 0.0) - jnp.log1p(jnp.exp(-jnp.abs(x)))


def _resident(shape):
    nd = len(shape)
    return pl.BlockSpec(shape, lambda *_: (0,) * nd, pipeline_mode=pl.Buffered(1))


def _resident_layer(shape, layer):
    nd = len(shape)
    return pl.BlockSpec((None,) + tuple(shape), lambda *_: (layer,) + (0,) * nd,
                        pipeline_mode=pl.Buffered(1))


def _inproj_kernel(xp_ref, x_ref, xn_ref, g1_ref, w_ref, wg_ref, gb_ref, cw_ref, cb_ref, qg_ref, kg_ref,
                   bd_ref,
                   qtm_ref, km_ref, vtm_ref, om_ref, gcol_ref, grow_ref, qn_ref, k1_ref, k2_ref, vt_ref,
                   ext_s, *, tiles_per_seq):
    rows = x_ref.shape[0]
    pos = pl.program_id(0) % tiles_per_seq

    def normed(xv):
        ms = jnp.mean(xv * xv, axis=-1, keepdims=True)
        return (xv * lax.rsqrt(ms + EPS) * g1_ref[...]).astype(BF16)

    x = x_ref[...]
    h = normed(x)

    def proj(c0, width):
        return _dot(h, w_ref[:, c0:c0 + width])

    h_prev = jnp.where(pos > 0, normed(xp_ref[...]), jnp.zeros((), BF16))
    h_next = jnp.where(pos < tiles_per_seq - 1, normed(xn_ref[...]), jnp.zeros((), BF16))
    ext_s[...] = _dot(jnp.concatenate([h_prev, h, h_next], axis=0), w_ref[:, 0:2 * MW])

    def conv_head(cb):
        cols = slice(cb * LANES, (cb + 1) * LANES)
        conv = cb_ref[:, cols]
        for j in range(CONV_W):
            off = HALO - CONV_W // 2 + j
            conv = conv + ext_s[off:off + rows, cols] * cw_ref[j:j + 1, cols]
        y = conv * _sigmoid(conv)
        if cb < N_M:
            qtm_ref[cols, :] = y.T.astype(BF16)
        else:
            km_ref[:, cb * LANES - MW:(cb + 1) * LANES - MW] = (y * (HD_M ** -0.5)).astype(BF16)

    ones_rows = jnp.ones((HALO, rows), BF16)
    pair = 2 * LANES

    def values_t(c0, out_ref, hp):
        v2 = proj(c0 + hp * pair, pair)
        for sub in range(2):
            hh = 2 * hp + sub
            out_ref[0, hh * VT_ROWS:hh * VT_ROWS + HD_M, :] = (
                v2[:, sub * LANES:(sub + 1) * LANES].T.astype(BF16))
            out_ref[0, hh * VT_ROWS + HD_M:(hh + 1) * VT_ROWS, :] = ones_rows

    def out_gate():
        om_ref[...] = proj(3 * MW, MW).astype(BF16)
        L = MLSTM_CHUNK
        gt = (_dot(h, wg_ref[...]) + gb_ref[...]).T[:N_GATES]
        logf = _log_sigmoid(gt)
        g_under_f = pltpu.roll(gt, N_M, 0)
        upper_f = (lax.broadcasted_iota(jnp.int32, (L, L), 0)
                   <= lax.broadcasted_iota(jnp.int32, (L, L), 1)).astype(F32)
        forward_rows = lax.broadcasted_iota(jnp.int32, (N_GATES, L), 0) < 2 * N_M
        for r in range(rows // L):
            sl = slice(r * L, (r + 1) * L)
            cum = _dot_exact(logf[:, sl], upper_f)
            suf = cum[:, L - 1:L] - cum + logf[:, sl]
            grow_ref[0:N_GATES, sl] = cum
            grow_ref[N_GATES:2 * N_GATES, sl] = gt[:, sl]
            grow_ref[2 * N_GATES:3 * N_GATES, sl] = suf
            src = jnp.where(forward_rows, cum, suf) - g_under_f[:, sl]
            src = jnp.concatenate([src, jnp.zeros((L - N_GATES, L), F32)], axis=0)
            gcol_ref[sl, :] = src.T[:, :N_GATES]

    bd = bd_ref[...]

    def qk_norm(c0, g_ref):
        t = proj(c0, pair)
        msq = _dot((t * t).astype(BF16), bd)
        return t * lax.rsqrt(msq + EPS) * g_ref[...]

    lane = lax.broadcasted_iota(jnp.int32, (rows, pair), 1) % LANES
    ones_cols = jnp.where(lane < HD_A + BIAS_ROWS, 1.0, 0.0)

    def attn_q(hp):
        sl = slice(hp * pair, (hp + 1) * pair)
        qn_ref[:, sl] = (qk_norm(4 * MW + hp * pair, qg_ref) * (HD_A ** -0.5 * LOG2E)).astype(BF16)

    def attn_k(hp):
        sl = slice(hp * pair, (hp + 1) * pair)
        kn = qk_norm(4 * MW + AW + hp * pair, kg_ref)
        k1_ref[:, sl] = jnp.where(lane < HD_A, kn, ones_cols).astype(BF16)
        k2_ref[:, sl] = jnp.where(lane < HD_A, pltpu.roll(kn, pair - HD_A, 1), ones_cols).astype(BF16)

    projections = [
        functools.partial(values_t, 2 * MW, vtm_ref, 0), functools.partial(values_t, 2 * MW, vtm_ref, 1),
        out_gate,
        functools.partial(attn_q, 0), functools.partial(attn_k, 0),
        functools.partial(attn_q, 1), functools.partial(attn_k, 1),
        functools.partial(values_t, 4 * MW + 2 * AW, vt_ref, 0),
    ]
    for cb, projection in enumerate(projections):
        conv_head(cb)
        projection()
    values_t(4 * MW + 2 * AW, vt_ref, 1)


def _inproj(x, g1, w_main, w_gate, layer, gate_b, conv_w, conv_b, qg, kg, bd, seq, tm):
    n, d = x.shape
    wcols = w_main.shape[2]
    hpt = tm // HALO
    nhalo = n // HALO
    row = lambda i: (i, 0)
    vt_spec = pl.BlockSpec((1, N_A * VT_ROWS, tm), lambda i: (i, 0, 0))
    vt_shape = jax.ShapeDtypeStruct((n // tm, N_A * VT_ROWS, tm), BF16)
    return pl.pallas_call(
        functools.partial(_inproj_kernel, tiles_per_seq=seq // tm),
        grid=(n // tm,),
        in_specs=[
            pl.BlockSpec((HALO, d), lambda i: (jnp.maximum(i * hpt - 1, 0), 0)),
            pl.BlockSpec((tm, d), row),
            pl.BlockSpec((HALO, d), lambda i: (jnp.minimum((i + 1) * hpt, nhalo - 1), 0)),
            _resident((1, d)),
            _resident_layer((d, wcols), layer),
            _resident_layer((d, LANES), layer),
            _resident((1, LANES)),
            _resident((CONV_W, 2 * MW)),
            _resident((1, 2 * MW)),
            _resident((1, 2 * LANES)),
            _resident((1, 2 * LANES)),
            _resident((2 * LANES, 2 * LANES)),
        ],
        out_specs=[
            pl.BlockSpec((MW, tm), lambda i: (0, i)),
            pl.BlockSpec((tm, MW), row),
            vt_spec,
            pl.BlockSpec((tm, MW), row),
            pl.BlockSpec((tm, N_GATES), row),
            pl.BlockSpec((3 * N_GATES, tm), lambda i: (0, i)),
            pl.BlockSpec((tm, AW), row),
            pl.BlockSpec((tm, AW), row),
            pl.BlockSpec((tm, AW), row),
            vt_spec,
        ],
        out_shape=[
            jax.ShapeDtypeStruct((MW, n), BF16),
            jax.ShapeDtypeStruct((n, MW), BF16),
            vt_shape,
            jax.ShapeDtypeStruct((n, MW), BF16),
            jax.ShapeDtypeStruct((n, N_GATES), F32),
            jax.ShapeDtypeStruct((3 * N_GATES, n), F32),
            jax.ShapeDtypeStruct((n, AW), BF16),
            jax.ShapeDtypeStruct((n, AW), BF16),
            jax.ShapeDtypeStruct((n, AW), BF16),
            vt_shape,
        ],
        scratch_shapes=[pltpu.VMEM((tm + 2 * HALO, 2 * MW), F32)],
        compiler_params=pltpu.CompilerParams(
            dimension_semantics=("parallel",), vmem_limit_bytes=VMEM_LIMIT),
        name="inproj",
    )(x, x, x, g1, w_main, w_gate, gate_b, conv_w, conv_b, qg, kg, bd)


def _mlstm_kernel(qt_f, k_f, vt_f, gc_f, gr_f,
                  qt_b, k_b, vt_b, gc_b, gr_b,
                  hf_ref, hb_ref,
                  c_s, m_s):
    L = MLSTM_CHUNK
    chunks = k_f.shape[0] // L

    @pl.when(pl.program_id(1) == 0)
    def _():
        c_s[...] = jnp.zeros_like(c_s)
        m_s[...] = jnp.zeros_like(m_s)

    row = lax.broadcasted_iota(jnp.int32, (L, L), 0)
    col = lax.broadcasted_iota(jnp.int32, (L, L), 1)
    lower = col <= row
    upper = row <= col
    dirs = ((qt_f, k_f, vt_f, gc_f, gr_f, hf_ref), (qt_b, k_b, vt_b, gc_b, gr_b, hb_ref))
    state = [(c_s[idx], m_s[idx][:, :1]) for idx in range(2 * N_M)]

    for step in range(chunks):
        chains = []
        for d, (qt_ref, k_ref, vt_ref, gc_ref, gr_ref, out_ref) in enumerate(dirs):
            tok = slice(step * L, (step + 1) * L) if d == 0 else slice((chunks - 1 - step) * L,
                                                                       (chunks - step) * L)
            i_off = 2 * N_M * d
            f_off = i_off + N_M
            src = gc_ref[tok, :]
            cum_rows = gr_ref[0:N_GATES, tok] if d == 0 else gr_ref[2 * N_GATES:3 * N_GATES, tok]
            gate_rows = gr_ref[N_GATES:2 * N_GATES, tok]
            for hh in range(N_M):
                idx = d * N_M + hh
                chains.append(dict(
                    idx=idx, out_ref=out_ref, tok=tok, cols=slice(hh * HD_M, (hh + 1) * HD_M),
                    mask=upper if d == 0 else lower,
                    src_col=src[:, f_off + hh:f_off + hh + 1],
                    b_row=cum_rows[f_off + hh:f_off + hh + 1, :],
                    i_row=gate_rows[i_off + hh:i_off + hh + 1, :],
                    end=slice(L - 1, L) if d == 0 else slice(0, 1),
                    qt=qt_ref[hh * HD_M:(hh + 1) * HD_M, tok],
                    kb=k_ref[tok, hh * HD_M:(hh + 1) * HD_M],
                    vt=vt_ref[0, hh * VT_ROWS:(hh + 1) * VT_ROWS, tok],
                    c_prev=state[idx][0], m_prev=state[idx][1],
                ))

        for ch in chains:
            ch["s"] = _dot(ch["kb"], ch["qt"])
            ch["inter_num"] = _dot(ch["c_prev"].astype(BF16), ch["qt"])
        for ch in chains:
            dmat = jnp.where(ch["mask"], ch["b_row"] - ch["src_col"], -jnp.inf)
            inter = ch["b_row"] + ch["m_prev"]
            ch["m_t"] = jnp.maximum(inter, jnp.max(dmat, axis=0, keepdims=True))
            ch["scale"] = jnp.exp(inter - ch["m_t"])
            ch["w"] = (ch["s"] * jnp.exp(dmat - ch["m_t"])).astype(BF16)
        for ch in chains:
            num = ch["scale"] * ch["inter_num"] + _dot(ch["vt"], ch["w"])
            den = num[HD_M:HD_M + 1]
            ht = num[:HD_M] / jnp.maximum(jnp.abs(den), jnp.exp(-ch["m_t"]))
            ch["out_ref"][ch["tok"], ch["cols"]] = ht.T.astype(ch["out_ref"].dtype)
        for ch in chains:
            b_end = ch["b_row"][:, ch["end"]]
            g_row = b_end - ch["b_row"] + ch["i_row"]
            m_new = jnp.maximum(b_end + ch["m_prev"], jnp.max(g_row, axis=-1, keepdims=True))
            decay = jnp.exp(b_end + ch["m_prev"] - m_new)
            vt_w = (ch["vt"].astype(F32) * jnp.exp(g_row - m_new)).astype(BF16)
            state[ch["idx"]] = (decay * ch["c_prev"] + _dot(vt_w, ch["kb"]), m_new)

    for idx, (c_new, m_new) in enumerate(state):
        c_s[idx] = c_new
        m_s[idx] = jnp.broadcast_to(m_new, (1, LANES))


def _mlstm(qt, k, vt, gate_cols, gate_rows, batch, seq, tm):
    nt = seq // tm

    def tile_f(b, c):
        return b * nt + c

    def tile_b(b, c):
        return b * nt + (nt - 1 - c)

    def specs(tile):
        return [
            pl.BlockSpec((MW, tm), lambda b, c: (0, tile(b, c))),
            pl.BlockSpec((tm, MW), lambda b, c: (tile(b, c), 0)),
            pl.BlockSpec((1, N_M * VT_ROWS, tm), lambda b, c: (tile(b, c), 0, 0)),
            pl.BlockSpec((tm, N_GATES), lambda b, c: (tile(b, c), 0)),
            pl.BlockSpec((3 * N_GATES, tm), lambda b, c: (0, tile(b, c))),
        ]

    n = batch * seq
    return pl.pallas_call(
        _mlstm_kernel,
        grid=(batch, nt),
        in_specs=specs(tile_f) + specs(tile_b),
        out_specs=[
            pl.BlockSpec((tm, MW), lambda b, c: (tile_f(b, c), 0)),
            pl.BlockSpec((tm, MW), lambda b, c: (tile_b(b, c), 0)),
        ],
        out_shape=[jax.ShapeDtypeStruct((n, MW), BF16)] * 2,
        scratch_shapes=[
            pltpu.VMEM((2 * N_M, VT_ROWS, HD_M), F32),
            pltpu.VMEM((2 * N_M, 1, LANES), F32),
        ],
        compiler_params=pltpu.CompilerParams(
            dimension_semantics=("parallel", "arbitrary"), vmem_limit_bytes=VMEM_LIMIT),
        name="mlstm",
    )(qt, k, vt, gate_cols, gate_rows, qt, k, vt, gate_cols, gate_rows)


def _bias_kernel(rb_ref, out_ref, *, tile):
    hh = pl.program_id(0)
    dd = pl.program_id(1)
    kk = lax.broadcasted_iota(jnp.int32, (tile, tile), 0)
    qq = lax.broadcasted_iota(jnp.int32, (tile, tile), 1)
    rel = (dd - 1) * tile + kk - qq
    nb = REL_BUCKETS // 2
    max_exact = nb // 2
    n = jnp.abs(rel)
    nf = jnp.maximum(n, 1).astype(F32)
    large = max_exact + (jnp.log(nf / max_exact) / math.log(REL_MAX_DIST / max_exact)
                         * (nb - max_exact)).astype(jnp.int32)
    large = jnp.minimum(large, nb - 1)
    bucket = jnp.where(rel > 0, nb, 0) + jnp.where(n < max_exact, n, large)
    val = jnp.zeros((tile, tile), F32)
    for bkt in range(REL_BUCKETS):
        val = jnp.where(bucket == bkt, rb_ref[bkt, hh], val)
    out_ref[0, 0] = jnp.where(dd < 3, val * LOG2E, 0.0)


def _bias_tiles(rel_bias, tile):
    return pl.pallas_call(
        functools.partial(_bias_kernel, tile=tile),
        grid=(N_A, 4),
        in_specs=[pl.BlockSpec(memory_space=pltpu.SMEM)],
        out_specs=pl.BlockSpec((1, 1, tile, tile), lambda h, d: (h, d, 0, 0)),
        out_shape=jax.ShapeDtypeStruct((N_A, 4, tile, tile), F32),
        name="rel_bias_tiles",
    )(rel_bias)


def _sublane_partial_sums(e):
    return jnp.sum(e.reshape(e.shape[0] // 8, 8, e.shape[1]), axis=0)


def _attn_kernel(rb_ref, li_ref, q_ref, k1_ref, k2_ref, vt_ref, qg_ref, kg_ref, bias_ref, lam_ref, sg_ref,
                 o_ref,
                 qt_s, m_s, l_s, acc_s, p0_s, p1_s):
    tq = vt_ref.shape[2]
    for sub in range(q_ref.shape[0] // tq):
        rows = pl.ds(sub * tq, tq)
        _attn_query_tile(pl.program_id(2) * (q_ref.shape[0] // tq) + sub,
                         rb_ref, li_ref, q_ref.at[rows], k1_ref, k2_ref, vt_ref, qg_ref, kg_ref,
                         bias_ref, lam_ref, sg_ref, o_ref.at[rows],
                         qt_s.at[sub], m_s.at[sub], l_s.at[sub], acc_s.at[sub], p0_s.at[sub], p1_s.at[sub])


def _attn_query_tile(qi, rb_ref, li_ref, q_ref, k1_ref, k2_ref, vt_ref, qg_ref, kg_ref, bias_ref,
                     lam_ref, sg_ref, o_ref, qt_s, m_s, l_s, acc_s, p0_s, p1_s):
    hh = pl.program_id(0)
    tq = q_ref.shape[0]
    tk = vt_ref.shape[2]
    nk = vt_ref.shape[0]
    vd = 2 * HD_A
    LEFT, RIGHT, NEAR = 0, 1, 2

    qt = q_ref[...].astype(F32).T
    rid = lax.broadcasted_iota(jnp.int32, qt.shape, 0)
    bases = (qt, jnp.concatenate([qt[HD_A:], qt[:HD_A]], axis=0))
    consts = (rb_ref[FAR_BUCKET, hh] * LOG2E, rb_ref[REL_BUCKETS // 2 + FAR_BUCKET, hh] * LOG2E, 0.0)

    def build_queries(shift):
        for i, base in enumerate(bases):
            body = jnp.where(rid < HD_A, base, 0.0).astype(BF16)
            for kind in range(len(consts)):
                qt_s[2 * kind + i] = body
        rid_o = lax.broadcasted_iota(jnp.int32, (HALO, tq), 0)
        for kind, cst in enumerate(consts):
            rest = jnp.full((HALO, tq), cst - shift, F32)
            extra = jnp.zeros((HALO, tq), F32)
            for r in range(BIAS_ROWS):
                part = rest.astype(BF16).astype(F32)
                extra = jnp.where(rid_o == r, part, extra)
                rest = rest - part
            for i in range(2):
                qt_s[2 * kind + i, HD_A:HD_A + HALO, :] = extra.astype(BF16)

    acc_s[...] = jnp.zeros_like(acc_s)
    l_s[...] = jnp.zeros_like(l_s)

    def key_rows(j):
        return pl.ds(pl.multiple_of(j * tk, tk), tk)

    bucket_vals = [rb_ref[bkt, hh] for bkt in range(REL_BUCKETS)]
    bias_max = functools.reduce(jnp.maximum, bucket_vals) * LOG2E
    bias_min = functools.reduce(jnp.minimum, bucket_vals) * LOG2E
    reach = (HD_A ** 0.5 * LOG2E) * jnp.max(jnp.abs(qg_ref[...])) * jnp.max(jnp.abs(kg_ref[...]))
    shift = reach + bias_max
    span = 2.0 * reach + (bias_max - bias_min)
    bounded = span <= LOGIT_SPAN_LIMIT

    @pl.when(bounded)
    def _():
        build_queries(shift)

        def numerators(j, kind, p_buf, bias=None):
            for i, k_ref in enumerate((k1_ref, k2_ref)):
                s = _dot(k_ref[key_rows(j), :], qt_s[2 * kind + i])
                if bias is not None:
                    s = s + bias
                e = jnp.exp2(s)
                l_s[i] += _sublane_partial_sums(e)
                p_buf[i] = e.astype(BF16)

        def accumulate(j, p_buf):
            vt = vt_ref[j, :vd, :]
            for i in range(2):
                acc_s[i] += _dot(vt, p_buf[i])

        w0 = jnp.clip(qi - 1, 0, nk - ATTN_WINDOW)
        far_tile = lambda t: jnp.where(t < w0, t, t + ATTN_WINDOW)
        far_kind = lambda t: jnp.where(t < w0, LEFT, RIGHT)
        p_bufs = (p0_s, p1_s)
        for r in range(ATTN_WINDOW):
            j = w0 + r
            dlt = j - qi
            near = jnp.abs(dlt) <= 1
            kind = jnp.where(near, NEAR, jnp.where(dlt < 0, LEFT, RIGHT))
            numerators(j, kind, p_bufs[r % 2], bias_ref[0, jnp.where(near, dlt + 1, 3)])
            if r > 0:
                accumulate(j - 1, p_bufs[(r - 1) % 2])
        last_window = w0 + ATTN_WINDOW - 1

        n_far = nk - ATTN_WINDOW
        for t0 in range(0, n_far, 2):
            numerators(far_tile(t0), far_kind(t0), p0_s)
            accumulate(far_tile(t0 - 1) if t0 else last_window, p1_s)
            numerators(far_tile(t0 + 1), far_kind(t0 + 1), p1_s)
            accumulate(far_tile(t0), p0_s)
        accumulate(far_tile(n_far - 1) if n_far else last_window, p1_s)

    @pl.when(jnp.logical_not(bounded))
    def _():
        _online_softmax_tiles(qi, nk, build_queries, key_rows, k1_ref, k2_ref, vt_ref, bias_ref,
                              qt_s, m_s, l_s, acc_s)

    lv = lam_ref[...]
    lam = (jnp.exp(jnp.sum(lv[0:1] * lv[1:2], axis=-1, keepdims=True))
           - jnp.exp(jnp.sum(lv[2:3] * lv[3:4], axis=-1, keepdims=True)) + li_ref[0, 0])
    sums = [jnp.sum(l_s[i], axis=0, keepdims=True) for i in range(2)]
    o = acc_s[0] / sums[0] - lam * (acc_s[1] / sums[1])
    ms = jnp.mean(o * o, axis=0, keepdims=True)
    o = o * lax.rsqrt(ms + EPS) * sg_ref[0] * li_ref[0, 1]
    o_ref[...] = o.T.astype(o_ref.dtype)


def _online_softmax_tiles(qi, nk, build_queries, key_rows, k1_ref, k2_ref, vt_ref, bias_ref,
                          qt_s, m_s, l_s, acc_s):
    LEFT, RIGHT, NEAR = 0, 1, 2
    build_queries(0.0)
    m_s[...] = jnp.full_like(m_s, -jnp.inf)

    def one_tile(j, carry):
        dlt = j - qi
        near = jnp.abs(dlt) <= 1
        kind = jnp.where(near, NEAR, jnp.where(dlt < 0, LEFT, RIGHT))
        bias = bias_ref[0, jnp.where(near, dlt + 1, 3)]
        vt = vt_ref[j, :2 * HD_A, :]
        for i, k_ref in enumerate((k1_ref, k2_ref)):
            s = _dot(k_ref[key_rows(j), :], qt_s[2 * kind + i]) + bias
            m_old = m_s[i]
            m_new = jnp.maximum(m_old, jnp.max(s, axis=0, keepdims=True))
            alpha = jnp.exp2(m_old - m_new)
            e = jnp.exp2(s - m_new)
            l_s[i] = alpha * l_s[i] + _sublane_partial_sums(e)
            acc_s[i] = alpha * acc_s[i] + _dot(vt, e.astype(BF16))
            m_s[i] = m_new
        return carry

    lax.fori_loop(0, nk, one_tile, 0)


def _attention(qn, k1, k2, vt, qg, kg, bias_t, rel_bias, lam_init, lambdas, sub_g_col, batch, seq, tile):
    nt = seq // tile
    nq = nt // ATTN_Q_TILES
    vd = 2 * HD_A
    smem = pl.BlockSpec(memory_space=pltpu.SMEM)
    q_rows = ATTN_Q_TILES * tile
    return pl.pallas_call(
        _attn_kernel,
        grid=(N_A, batch, nq),
        in_specs=[
            smem,
            smem,
            pl.BlockSpec((q_rows, vd), lambda h, b, i: (b * nq + i, h)),
            pl.BlockSpec((seq, vd), lambda h, b, i: (b, h)),
            pl.BlockSpec((seq, vd), lambda h, b, i: (b, h)),
            pl.BlockSpec((nt, VT_ROWS, tile), lambda h, b, i: (b, h, 0)),
            pl.BlockSpec((1, 2 * LANES), lambda h, b, i: (0, 0)),
            pl.BlockSpec((1, 2 * LANES), lambda h, b, i: (0, 0)),
            pl.BlockSpec((1, 4, tile, tile), lambda h, b, i: (h, 0, 0, 0)),
            pl.BlockSpec((4, HD_A), lambda h, b, i: (0, 0)),
            pl.BlockSpec((1, vd, 1), lambda h, b, i: (h, 0, 0)),
        ],
        out_specs=pl.BlockSpec((q_rows, vd), lambda h, b, i: (b * nq + i, h)),
        out_shape=jax.ShapeDtypeStruct((batch * seq, AW), BF16),
        scratch_shapes=[
            pltpu.VMEM((ATTN_Q_TILES, 6, vd, tile), BF16),
            pltpu.VMEM((ATTN_Q_TILES, 2, 1, tile), F32),
            pltpu.VMEM((ATTN_Q_TILES, 2, 8, tile), F32),
            pltpu.VMEM((ATTN_Q_TILES, 2, vd, tile), F32),
            pltpu.VMEM((ATTN_Q_TILES, 2, tile, tile), BF16),
            pltpu.VMEM((ATTN_Q_TILES, 2, tile, tile), BF16),
        ],
        compiler_params=pltpu.CompilerParams(
            dimension_semantics=("parallel", "parallel", "arbitrary"),
            vmem_limit_bytes=VMEM_LIMIT),
        name="diff_attention",
    )(rel_bias, lam_init, qn, k1, k2, vt, qg, kg, bias_t, lambdas, sub_g_col)


def _outffn_kernel(x_ref, hf_ref, hb_ref, og_ref, mb_ref, ng_ref, wo_ref, g2_ref, w1_ref, w2_ref,
                   out_ref, *, ff_chunk):
    hs = hf_ref[...].astype(F32) + hb_ref[...].astype(F32)
    x1 = x_ref[...] + _dot(mb_ref[...], wo_ref[MW:, :])
    for hh in range(N_M):
        sl = slice(hh * HD_M, (hh + 1) * HD_M)
        t = hs[:, sl]
        ms = jnp.mean(t * t, axis=-1, keepdims=True)
        mix = t * lax.rsqrt(ms + EPS) * ng_ref[:, sl] * _sigmoid(og_ref[:, sl].astype(F32))
        x1 = x1 + _dot(mix.astype(BF16), wo_ref[sl, :])
    ms = jnp.mean(x1 * x1, axis=-1, keepdims=True)
    h2 = (x1 * lax.rsqrt(ms + EPS) * g2_ref[...]).astype(BF16)
    out_ref[...] = x1
    for j in range(w1_ref.shape[1] // ff_chunk):
        sl = slice(j * ff_chunk, (j + 1) * ff_chunk)
        u = jnp.maximum(_dot(h2, w1_ref[:, sl]), 0.0)
        out_ref[...] += _dot((u * u).astype(BF16), w2_ref[sl, :])


def _outffn(x, hf, hb, om, mixb, ng, wo, g2, w1, w2, layer, tm):
    n, d = x.shape
    dff = w1.shape[2]
    row = lambda i: (i, 0)
    return pl.pallas_call(
        functools.partial(_outffn_kernel, ff_chunk=1024),
        grid=(n // tm,),
        in_specs=[
            pl.BlockSpec((tm, d), row),
            pl.BlockSpec((tm, MW), row),
            pl.BlockSpec((tm, MW), row),
            pl.BlockSpec((tm, MW), row),
            pl.BlockSpec((tm, AW), row),
            _resident((1, MW)),
            _resident_layer((MW + AW, d), layer),
            _resident((1, d)),
            _resident_layer((d, dff), layer),
            _resident_layer((dff, d), layer),
        ],
        out_specs=pl.BlockSpec((tm, d), row),
        out_shape=jax.ShapeDtypeStruct((n, d), F32),
        compiler_params=pltpu.CompilerParams(
            dimension_semantics=("parallel",), vmem_limit_bytes=VMEM_LIMIT),
        name="outproj_ffn",
    )(x, hf, hb, om, mixb, ng, wo, g2, w1, w2)


def kernel(x, norm1_g, w_in, conv_w, conv_b, gate_b, mlstm_norm_g, q_norm_g, k_norm_g, lambdas,
           diff_norm_g, rel_bias, w_out, norm2_g, w_ff1, w_ff2):
    batch, seq, d = x.shape
    depth = w_in.shape[0]
    n = batch * seq
    tm = tile = ROW_TILE
    assert d == w_in.shape[1] and w_in.shape[2] == C_G + N_GATES + 3 * AW
    assert seq % tile == 0 and tile % MLSTM_CHUNK == 0 and tile >= REL_MAX_DIST
    assert seq // tile >= ATTN_WINDOW and (seq // tile - ATTN_WINDOW) % 2 == 0
    assert (seq // tile) % ATTN_Q_TILES == 0

    w_main = jnp.concatenate([w_in[:, :, :C_G], w_in[:, :, C_G + N_GATES:]], axis=-1).astype(BF16)
    w_gate = jnp.pad(w_in[:, :, C_G:C_G + N_GATES], ((0, 0), (0, 0), (0, LANES - N_GATES))).astype(BF16)
    w_out_b = w_out.astype(BF16)
    w_ff1_b = w_ff1.astype(BF16)
    w_ff2_b = w_ff2.astype(BF16)
    lane = jnp.arange(2 * LANES)
    bd = jnp.where((lane[:, None] // HD_A) == (lane[None, :] // HD_A), 1.0 / HD_A, 0.0).astype(BF16)

    bias_t = _bias_tiles(rel_bias.astype(F32), tile)

    xf = x.reshape(n, d)
    for l in range(depth):
        lam_init = 0.8 - 0.6 * math.exp(-0.3 * l)
        qg = jnp.tile(q_norm_g[l], 2 * LANES // HD_A)[None, :]
        kg = jnp.tile(k_norm_g[l], 2 * LANES // HD_A)[None, :]
        gb = jnp.pad(gate_b[l].reshape(1, N_GATES).astype(F32), ((0, 0), (0, LANES - N_GATES)))
        qtm, km, vtm, om, gate_cols, gate_rows, qn, k1, k2, vt = _inproj(
            xf, norm1_g[l][None, :], w_main, w_gate, l, gb, conv_w[l], conv_b[l][None, :],
            qg, kg, bd, seq, tm)
        hf, hb = _mlstm(qtm, km, vtm, gate_cols, gate_rows, batch, seq, tm)
        mixb = _attention(qn, k1, k2, vt, qg, kg, bias_t, rel_bias.astype(F32),
                          jnp.array([[lam_init, 1.0 - lam_init]], F32), lambdas[l],
                          diff_norm_g[l].reshape(N_A, 2 * HD_A, 1), batch, seq, tile)
        xf = _outffn(xf, hf, hb, om, mixb, mlstm_norm_g[l][None, :], w_out_b,
                     norm2_g[l][None, :], w_ff1_b, w_ff2_b, l, tm)
    return xf.reshape(batch, seq, d)
```

```python
import functools
import math

import jax
import jax.numpy as jnp
from jax import lax
from jax.experimental import pallas as pl
from jax.experimental.pallas import tpu as pltpu

F32 = jnp.float32
BF16 = jnp.bfloat16

N_M = 4
HD_M = 128
MW = N_M * HD_M
MLSTM_CHUNK = 128
CONV_W = 5
N_A = 4
HD_A = 64
AW = N_A * 2 * HD_A
REL_BUCKETS = 32
REL_MAX_DIST = 128
EPS = 1e-6
N_GATES = 4 * N_M
C_G = 4 * MW
LANES = 128
HALO = 16
VMEM_LIMIT = 56 * 1024 * 1024
LOG2E = math.log2(math.e)
VT_ROWS = HD_M + HALO
BIAS_ROWS = 3
LOGIT_SPAN_LIMIT = 96.0
ATTN_WINDOW = 4
ATTN_Q_TILES = 2
ROW_TILE = 512
FAR_BUCKET = REL_BUCKETS // 2 - 1


def _dot(a, b):
    return jnp.dot(a, b, preferred_element_type=F32)


def _dot_exact(a, b):
    return jnp.dot(a, b, preferred_element_type=F32, precision=lax.Precision.HIGHEST)


def _sigmoid(x):
    return 1.0 / (1.0 + jnp.exp(-x))


def _log_sigmoid(x):
    return jnp.minimum(x, 0.0) - jnp.log1p(jnp.exp(-jnp.abs(x)))


def _resident(shape):
    nd = len(shape)
    return pl.BlockSpec(shape, lambda *_: (0,) * nd, pipeline_mode=pl.Buffered(1))


def _resident_layer(shape, layer):
    nd = len(shape)
    return pl.BlockSpec((None,) + tuple(shape), lambda *_: (layer,) + (0,) * nd,
                        pipeline_mode=pl.Buffered(1))


def _inproj_kernel(xp_ref, x_ref, xn_ref, g1_ref, w_ref, wg_ref, gb_ref, cw_ref, cb_ref, qg_ref, kg_ref,
                   bd_ref,
                   qtm_ref, km_ref, vtm_ref, om_ref, gcol_ref, grow_ref, qn_ref, k1_ref, k2_ref, vt_ref,
                   ext_s, *, tiles_per_seq):
    rows = x_ref.shape[0]
    pos = pl.program_id(0) % tiles_per_seq

    def normed(xv):
        ms = jnp.mean(xv * xv, axis=-1, keepdims=True)
        return (xv * lax.rsqrt(ms + EPS) * g1_ref[...]).astype(BF16)

    x = x_ref[...]
    h = normed(x)

    def proj(c0, width):
        return _dot(h, w_ref[:, c0:c0 + width])

    h_prev = jnp.where(pos > 0, normed(xp_ref[...]), jnp.zeros((), BF16))
    h_next = jnp.where(pos < tiles_per_seq - 1, normed(xn_ref[...]), jnp.zeros((), BF16))
    ext_s[...] = _dot(jnp.concatenate([h_prev, h, h_next], axis=0), w_ref[:, 0:2 * MW])

    def conv_head(cb):
        cols = slice(cb * LANES, (cb + 1) * LANES)
        conv = cb_ref[:, cols]
        for j in range(CONV_W):
            off = HALO - CONV_W // 2 + j
            conv = conv + ext_s[off:off + rows, cols] * cw_ref[j:j + 1, cols]
        y = conv * _sigmoid(conv)
        if cb < N_M:
            qtm_ref[cols, :] = y.T.astype(BF16)
        else:
            km_ref[:, cb * LANES - MW:(cb + 1) * LANES - MW] = (y * (HD_M ** -0.5)).astype(BF16)

    ones_rows = jnp.ones((HALO, rows), BF16)
    pair = 2 * LANES

    def values_t(c0, out_ref):
        v4 = proj(c0, N_M * HD_M)
        for hh in range(N_M):
            out_ref[0, hh * VT_ROWS:hh * VT_ROWS + HD_M, :] = (
                v4[:, hh * LANES:(hh + 1) * LANES].T.astype(BF16))
            out_ref[0, hh * VT_ROWS + HD_M:(hh + 1) * VT_ROWS, :] = ones_rows

    def out_gate():
        om_ref[...] = proj(3 * MW, MW).astype(BF16)
        L = MLSTM_CHUNK
        gt = (_dot(h, wg_ref[...]) + gb_ref[...]).T[:N_GATES]
        logf = _log_sigmoid(gt)
        g_under_f = pltpu.roll(gt, N_M, 0)
        upper_f = (lax.broadcasted_iota(jnp.int32, (L, L), 0)
                   <= lax.broadcasted_iota(jnp.int32, (L, L), 1)).astype(F32)
        forward_rows = lax.broadcasted_iota(jnp.int32, (N_GATES, L), 0) < 2 * N_M
        for r in range(rows // L):
            sl = slice(r * L, (r + 1) * L)
            cum = _dot_exact(logf[:, sl], upper_f)
            suf = cum[:, L - 1:L] - cum + logf[:, sl]
            grow_ref[0:N_GATES, sl] = cum
            grow_ref[N_GATES:2 * N_GATES, sl] = gt[:, sl]
            grow_ref[2 * N_GATES:3 * N_GATES, sl] = suf
            src = jnp.where(forward_rows, cum, suf) - g_under_f[:, sl]
            src = jnp.concatenate([src, jnp.zeros((L - N_GATES, L), F32)], axis=0)
            gcol_ref[sl, :] = src.T[:, :N_GATES]

    bd = bd_ref[...]

    def qk_norm(c0, g_ref):
        t = proj(c0, AW)
        halves = []
        for hp in range(AW // pair):
            th = t[:, hp * pair:(hp + 1) * pair]
            msq = _dot((th * th).astype(BF16), bd)
            halves.append(th * lax.rsqrt(msq + EPS) * g_ref[...])
        return halves

    lane = lax.broadcasted_iota(jnp.int32, (rows, pair), 1) % LANES
    ones_cols = jnp.where(lane < HD_A + BIAS_ROWS, 1.0, 0.0)

    def attn_q():
        for hp, qn in enumerate(qk_norm(4 * MW, qg_ref)):
            qn_ref[:, hp * pair:(hp + 1) * pair] = (qn * (HD_A ** -0.5 * LOG2E)).astype(BF16)

    def attn_k():
        for hp, kn in enumerate(qk_norm(4 * MW + AW, kg_ref)):
            sl = slice(hp * pair, (hp + 1) * pair)
            k1_ref[:, sl] = jnp.where(lane < HD_A, kn, ones_cols).astype(BF16)
            k2_ref[:, sl] = jnp.where(lane < HD_A, pltpu.roll(kn, pair - HD_A, 1), ones_cols).astype(BF16)

    for cb in range(2 * MW // LANES):
        conv_head(cb)
    values_t(2 * MW, vtm_ref)
    out_gate()
    attn_q()
    attn_k()
    values_t(4 * MW + 2 * AW, vt_ref)


def _inproj(x, g1, w_main, w_gate, layer, gate_b, conv_w, conv_b, qg, kg, bd, seq, tm):
    n, d = x.shape
    wcols = w_main.shape[2]
    hpt = tm // HALO
    nhalo = n // HALO
    row = lambda i: (i, 0)
    vt_spec = pl.BlockSpec((1, N_A * VT_ROWS, tm), lambda i: (i, 0, 0))
    vt_shape = jax.ShapeDtypeStruct((n // tm, N_A * VT_ROWS, tm), BF16)
    return pl.pallas_call(
        functools.partial(_inproj_kernel, tiles_per_seq=seq // tm),
        grid=(n // tm,),
        in_specs=[
            pl.BlockSpec((HALO, d), lambda i: (jnp.maximum(i * hpt - 1, 0), 0)),
            pl.BlockSpec((tm, d), row),
            pl.BlockSpec((HALO, d), lambda i: (jnp.minimum((i + 1) * hpt, nhalo - 1), 0)),
            _resident((1, d)),
            _resident_layer((d, wcols), layer),
            _resident_layer((d, LANES), layer),
            _resident((1, LANES)),
            _resident((CONV_W, 2 * MW)),
            _resident((1, 2 * MW)),
            _resident((1, 2 * LANES)),
            _resident((1, 2 * LANES)),
            _resident((2 * LANES, 2 * LANES)),
        ],
        out_specs=[
            pl.BlockSpec((MW, tm), lambda i: (0, i)),
            pl.BlockSpec((tm, MW), row),
            vt_spec,
            pl.BlockSpec((tm, MW), row),
            pl.BlockSpec((tm, N_GATES), row),
            pl.BlockSpec((3 * N_GATES, tm), lambda i: (0, i)),
            pl.BlockSpec((tm, AW), row),
            pl.BlockSpec((tm, AW), row),
            pl.BlockSpec((tm, AW), row),
            vt_spec,
        ],
        out_shape=[
            jax.ShapeDtypeStruct((MW, n), BF16),
            jax.ShapeDtypeStruct((n, MW), BF16),
            vt_shape,
            jax.ShapeDtypeStruct((n, MW), BF16),
            jax.ShapeDtypeStruct((n, N_GATES), F32),
            jax.ShapeDtypeStruct((3 * N_GATES, n), F32),
            jax.ShapeDtypeStruct((n, AW), BF16),
            jax.ShapeDtypeStruct((n, AW), BF16),
            jax.ShapeDtypeStruct((n, AW), BF16),
            vt_shape,
        ],
        scratch_shapes=[pltpu.VMEM((tm + 2 * HALO, 2 * MW), F32)],
        compiler_params=pltpu.CompilerParams(
            dimension_semantics=("parallel",), vmem_limit_bytes=VMEM_LIMIT),
        name="inproj",
    )(x, x, x, g1, w_main, w_gate, gate_b, conv_w, conv_b, qg, kg, bd)


def _mlstm_kernel(qt_f, k_f, vt_f, gc_f, gr_f,
                  qt_b, k_b, vt_b, gc_b, gr_b,
                  hf_ref, hb_ref,
                  c_s, m_s):
    L = MLSTM_CHUNK
    chunks = k_f.shape[0] // L

    @pl.when(pl.program_id(1) == 0)
    def _():
        c_s[...] = jnp.zeros_like(c_s)
        m_s[...] = jnp.zeros_like(m_s)

    row = lax.broadcasted_iota(jnp.int32, (L, L), 0)
    col = lax.broadcasted_iota(jnp.int32, (L, L), 1)
    lower = col <= row
    upper = row <= col
    dirs = ((qt_f, k_f, vt_f, gc_f, gr_f, hf_ref), (qt_b, k_b, vt_b, gc_b, gr_b, hb_ref))
    state = [(c_s[idx], m_s[idx][:, :1]) for idx in range(2 * N_M)]

    for step in range(chunks):
        chains = []
        for d, (qt_ref, k_ref, vt_ref, gc_ref, gr_ref, out_ref) in enumerate(dirs):
            tok = slice(step * L, (step + 1) * L) if d == 0 else slice((chunks - 1 - step) * L,
                                                                       (chunks - step) * L)
            i_off = 2 * N_M * d
            f_off = i_off + N_M
            src = gc_ref[tok, :]
            cum_rows = gr_ref[0:N_GATES, tok] if d == 0 else gr_ref[2 * N_GATES:3 * N_GATES, tok]
            gate_rows = gr_ref[N_GATES:2 * N_GATES, tok]
            for hh in range(N_M):
                idx = d * N_M + hh
                chains.append(dict(
                    idx=idx, out_ref=out_ref, tok=tok, cols=slice(hh * HD_M, (hh + 1) * HD_M),
                    mask=upper if d == 0 else lower,
                    src_col=src[:, f_off + hh:f_off + hh + 1],
                    b_row=cum_rows[f_off + hh:f_off + hh + 1, :],
                    i_row=gate_rows[i_off + hh:i_off + hh + 1, :],
                    end=slice(L - 1, L) if d == 0 else slice(0, 1),
                    qt=qt_ref[hh * HD_M:(hh + 1) * HD_M, tok],
                    kb=k_ref[tok, hh * HD_M:(hh + 1) * HD_M],
                    vt=vt_ref[0, hh * VT_ROWS:(hh + 1) * VT_ROWS, tok],
                    c_prev=state[idx][0], m_prev=state[idx][1],
                ))

        for ch in chains:
            ch["s"] = _dot(ch["kb"], ch["qt"])
            ch["inter_num"] = _dot(ch["c_prev"].astype(BF16), ch["qt"])
        for ch in chains:
            dmat = jnp.where(ch["mask"], ch["b_row"] - ch["src_col"], -jnp.inf)
            inter = ch["b_row"] + ch["m_prev"]
            ch["m_t"] = jnp.maximum(inter, jnp.max(dmat, axis=0, keepdims=True))
            ch["scale"] = jnp.exp(inter - ch["m_t"])
            ch["w"] = (ch["s"] * jnp.exp(dmat - ch["m_t"])).astype(BF16)
        for ch in chains:
            num = ch["scale"] * ch["inter_num"] + _dot(ch["vt"], ch["w"])
            den = num[HD_M:HD_M + 1]
            ht = num[:HD_M] / jnp.maximum(jnp.abs(den), jnp.exp(-ch["m_t"]))
            ch["out_ref"][ch["tok"], ch["cols"]] = ht.T.astype(ch["out_ref"].dtype)
        for ch in chains:
            b_end = ch["b_row"][:, ch["end"]]
            g_row = b_end - ch["b_row"] + ch["i_row"]
            m_new = jnp.maximum(b_end + ch["m_prev"], jnp.max(g_row, axis=-1, keepdims=True))
            decay = jnp.exp(b_end + ch["m_prev"] - m_new)
            vt_w = (ch["vt"].astype(F32) * jnp.exp(g_row - m_new)).astype(BF16)
            state[ch["idx"]] = (decay * ch["c_prev"] + _dot(vt_w, ch["kb"]), m_new)

    for idx, (c_new, m_new) in enumerate(state):
        c_s[idx] = c_new
        m_s[idx] = jnp.broadcast_to(m_new, (1, LANES))


def _mlstm(qt, k, vt, gate_cols, gate_rows, batch, seq, tm):
    nt = seq // tm

    def tile_f(b, c):
        return b * nt + c

    def tile_b(b, c):
        return b * nt + (nt - 1 - c)

    def specs(tile):
        return [
            pl.BlockSpec((MW, tm), lambda b, c: (0, tile(b, c))),
            pl.BlockSpec((tm, MW), lambda b, c: (tile(b, c), 0)),
            pl.BlockSpec((1, N_M * VT_ROWS, tm), lambda b, c: (tile(b, c), 0, 0)),
            pl.BlockSpec((tm, N_GATES), lambda b, c: (tile(b, c), 0)),
            pl.BlockSpec((3 * N_GATES, tm), lambda b, c: (0, tile(b, c))),
        ]

    n = batch * seq
    return pl.pallas_call(
        _mlstm_kernel,
        grid=(batch, nt),
        in_specs=specs(tile_f) + specs(tile_b),
        out_specs=[
            pl.BlockSpec((tm, MW), lambda b, c: (tile_f(b, c), 0)),
            pl.BlockSpec((tm, MW), lambda b, c: (tile_b(b, c), 0)),
        ],
        out_shape=[jax.ShapeDtypeStruct((n, MW), BF16)] * 2,
        scratch_shapes=[
            pltpu.VMEM((2 * N_M, VT_ROWS, HD_M), F32),
            pltpu.VMEM((2 * N_M, 1, LANES), F32),
        ],
        compiler_params=pltpu.CompilerParams(
            dimension_semantics=("parallel", "arbitrary"), vmem_limit_bytes=VMEM_LIMIT),
        name="mlstm",
    )(qt, k, vt, gate_cols, gate_rows, qt, k, vt, gate_cols, gate_rows)


def _bias_kernel(rb_ref, out_ref, *, tile):
    hh = pl.program_id(0)
    dd = pl.program_id(1)
    kk = lax.broadcasted_iota(jnp.int32, (tile, tile), 0)
    qq = lax.broadcasted_iota(jnp.int32, (tile, tile), 1)
    rel = (dd - 1) * tile + kk - qq
    nb = REL_BUCKETS // 2
    max_exact = nb // 2
    n = jnp.abs(rel)
    nf = jnp.maximum(n, 1).astype(F32)
    large = max_exact + (jnp.log(nf / max_exact) / math.log(REL_MAX_DIST / max_exact)
                         * (nb - max_exact)).astype(jnp.int32)
    large = jnp.minimum(large, nb - 1)
    bucket = jnp.where(rel > 0, nb, 0) + jnp.where(n < max_exact, n, large)
    val = jnp.zeros((tile, tile), F32)
    for bkt in range(REL_BUCKETS):
        val = jnp.where(bucket == bkt, rb_ref[bkt, hh], val)
    out_ref[0, 0] = jnp.where(dd < 3, val * LOG2E, 0.0)


def _bias_tiles(rel_bias, tile):
    return pl.pallas_call(
        functools.partial(_bias_kernel, tile=tile),
        grid=(N_A, 4),
        in_specs=[pl.BlockSpec(memory_space=pltpu.SMEM)],
        out_specs=pl.BlockSpec((1, 1, tile, tile), lambda h, d: (h, d, 0, 0)),
        out_shape=jax.ShapeDtypeStruct((N_A, 4, tile, tile), F32),
        name="rel_bias_tiles",
    )(rel_bias)


def _sublane_partial_sums(e):
    return jnp.sum(e.reshape(e.shape[0] // 8, 8, e.shape[1]), axis=0)


def _attn_kernel(rb_ref, li_ref, q_ref, k1_ref, k2_ref, vt_ref, qg_ref, kg_ref, bias_ref, lam_ref, sg_ref,
                 o_ref,
                 qt_s, m_s, l_s, acc_s, p0_s, p1_s):
    tq = vt_ref.shape[2]
    for sub in range(q_ref.shape[0] // tq):
        rows = pl.ds(sub * tq, tq)
        _attn_query_tile(pl.program_id(2) * (q_ref.shape[0] // tq) + sub,
                         rb_ref, li_ref, q_ref.at[rows], k1_ref, k2_ref, vt_ref, qg_ref, kg_ref,
                         bias_ref, lam_ref, sg_ref, o_ref.at[rows],
                         qt_s.at[sub], m_s.at[sub], l_s.at[sub], acc_s.at[sub], p0_s.at[sub], p1_s.at[sub])


def _attn_query_tile(qi, rb_ref, li_ref, q_ref, k1_ref, k2_ref, vt_ref, qg_ref, kg_ref, bias_ref,
                     lam_ref, sg_ref, o_ref, qt_s, m_s, l_s, acc_s, p0_s, p1_s):
    hh = pl.program_id(0)
    tq = q_ref.shape[0]
    tk = vt_ref.shape[2]
    nk = vt_ref.shape[0]
    vd = 2 * HD_A
    LEFT, RIGHT, NEAR = 0, 1, 2

    qt = q_ref[...].astype(F32).T
    rid = lax.broadcasted_iota(jnp.int32, qt.shape, 0)
    bases = (qt, jnp.concatenate([qt[HD_A:], qt[:HD_A]], axis=0))
    consts = (rb_ref[FAR_BUCKET, hh] * LOG2E, rb_ref[REL_BUCKETS // 2 + FAR_BUCKET, hh] * LOG2E, 0.0)

    def build_queries(shift):
        for i, base in enumerate(bases):
            body = jnp.where(rid < HD_A, base, 0.0).astype(BF16)
            for kind in range(len(consts)):
                qt_s[2 * kind + i] = body
        rid_o = lax.broadcasted_iota(jnp.int32, (HALO, tq), 0)
        for kind, cst in enumerate(consts):
            rest = jnp.full((HALO, tq), cst - shift, F32)
            extra = jnp.zeros((HALO, tq), F32)
            for r in range(BIAS_ROWS):
                part = rest.astype(BF16).astype(F32)
                extra = jnp.where(rid_o == r, part, extra)
                rest = rest - part
            for i in range(2):
                qt_s[2 * kind + i, HD_A:HD_A + HALO, :] = extra.astype(BF16)

    acc_s[...] = jnp.zeros_like(acc_s)
    l_s[...] = jnp.zeros_like(l_s)

    def key_rows(j):
        return pl.ds(pl.multiple_of(j * tk, tk), tk)

    bucket_vals = [rb_ref[bkt, hh] for bkt in range(REL_BUCKETS)]
    bias_max = functools.reduce(jnp.maximum, bucket_vals) * LOG2E
    bias_min = functools.reduce(jnp.minimum, bucket_vals) * LOG2E
    reach = (HD_A ** 0.5 * LOG2E) * jnp.max(jnp.abs(qg_ref[...])) * jnp.max(jnp.abs(kg_ref[...]))
    shift = reach + bias_max
    span = 2.0 * reach + (bias_max - bias_min)
    bounded = span <= LOGIT_SPAN_LIMIT

    @pl.when(bounded)
    def _():
        build_queries(shift)

        def numerators(j, kind, p_buf, bias=None):
            for i, k_ref in enumerate((k1_ref, k2_ref)):
                s = _dot(k_ref[key_rows(j), :], qt_s[2 * kind + i])
                if bias is not None:
                    s = s + bias
                e = jnp.exp2(s)
                l_s[i] += _sublane_partial_sums(e)
                p_buf[i] = e.astype(BF16)

        def accumulate(j, p_buf):
            vt = vt_ref[j, :vd, :]
            for i in range(2):
                acc_s[i] += _dot(vt, p_buf[i])

        w0 = jnp.clip(qi - 1, 0, nk - ATTN_WINDOW)
        far_tile = lambda t: jnp.where(t < w0, t, t + ATTN_WINDOW)
        far_kind = lambda t: jnp.where(t < w0, LEFT, RIGHT)
        p_bufs = (p0_s, p1_s)
        for r in range(ATTN_WINDOW):
            j = w0 + r
            dlt = j - qi
            near = jnp.abs(dlt) <= 1
            kind = jnp.where(near, NEAR, jnp.where(dlt < 0, LEFT, RIGHT))
            numerators(j, kind, p_bufs[r % 2], bias_ref[0, jnp.where(near, dlt + 1, 3)])
            if r > 0:
                accumulate(j - 1, p_bufs[(r - 1) % 2])
        last_window = w0 + ATTN_WINDOW - 1

        n_far = nk - ATTN_WINDOW
        for t0 in range(0, n_far, 2):
            numerators(far_tile(t0), far_kind(t0), p0_s)
            accumulate(far_tile(t0 - 1) if t0 else last_window, p1_s)
            numerators(far_tile(t0 + 1), far_kind(t0 + 1), p1_s)
            accumulate(far_tile(t0), p0_s)
        accumulate(far_tile(n_far - 1) if n_far else last_window, p1_s)

    @pl.when(jnp.logical_not(bounded))
    def _():
        _online_softmax_tiles(qi, nk, build_queries, key_rows, k1_ref, k2_ref, vt_ref, bias_ref,
                              qt_s, m_s, l_s, acc_s)

    lv = lam_ref[...]
    lam = (jnp.exp(jnp.sum(lv[0:1] * lv[1:2], axis=-1, keepdims=True))
           - jnp.exp(jnp.sum(lv[2:3] * lv[3:4], axis=-1, keepdims=True)) + li_ref[0, 0])
    sums = [jnp.sum(l_s[i], axis=0, keepdims=True) for i in range(2)]
    o = acc_s[0] / sums[0] - lam * (acc_s[1] / sums[1])
    ms = jnp.mean(o * o, axis=0, keepdims=True)
    o = o * lax.rsqrt(ms + EPS) * sg_ref[0] * li_ref[0, 1]
    o_ref[...] = o.T.astype(o_ref.dtype)


def _online_softmax_tiles(qi, nk, build_queries, key_rows, k1_ref, k2_ref, vt_ref, bias_ref,
                          qt_s, m_s, l_s, acc_s):
    LEFT, RIGHT, NEAR = 0, 1, 2
    build_queries(0.0)
    m_s[...] = jnp.full_like(m_s, -jnp.inf)

    def one_tile(j, carry):
        dlt = j - qi
        near = jnp.abs(dlt) <= 1
        kind = jnp.where(near, NEAR, jnp.where(dlt < 0, LEFT, RIGHT))
        bias = bias_ref[0, jnp.where(near, dlt + 1, 3)]
        vt = vt_ref[j, :2 * HD_A, :]
        for i, k_ref in enumerate((k1_ref, k2_ref)):
            s = _dot(k_ref[key_rows(j), :], qt_s[2 * kind + i]) + bias
            m_old = m_s[i]
            m_new = jnp.maximum(m_old, jnp.max(s, axis=0, keepdims=True))
            alpha = jnp.exp2(m_old - m_new)
            e = jnp.exp2(s - m_new)
            l_s[i] = alpha * l_s[i] + _sublane_partial_sums(e)
            acc_s[i] = alpha * acc_s[i] + _dot(vt, e.astype(BF16))
            m_s[i] = m_new
        return carry

    lax.fori_loop(0, nk, one_tile, 0)


def _attention(qn, k1, k2, vt, qg, kg, bias_t, rel_bias, lam_init, lambdas, sub_g_col, batch, seq, tile):
    nt = seq // tile
    nq = nt // ATTN_Q_TILES
    vd = 2 * HD_A
    smem = pl.BlockSpec(memory_space=pltpu.SMEM)
    q_rows = ATTN_Q_TILES * tile
    return pl.pallas_call(
        _attn_kernel,
        grid=(N_A, batch, nq),
        in_specs=[
            smem,
            smem,
            pl.BlockSpec((q_rows, vd), lambda h, b, i: (b * nq + i, h)),
            pl.BlockSpec((seq, vd), lambda h, b, i: (b, h)),
            pl.BlockSpec((seq, vd), lambda h, b, i: (b, h)),
            pl.BlockSpec((nt, VT_ROWS, tile), lambda h, b, i: (b, h, 0)),
            pl.BlockSpec((1, 2 * LANES), lambda h, b, i: (0, 0)),
            pl.BlockSpec((1, 2 * LANES), lambda h, b, i: (0, 0)),
            pl.BlockSpec((1, 4, tile, tile), lambda h, b, i: (h, 0, 0, 0)),
            pl.BlockSpec((4, HD_A), lambda h, b, i: (0, 0)),
            pl.BlockSpec((1, vd, 1), lambda h, b, i: (h, 0, 0)),
        ],
        out_specs=pl.BlockSpec((q_rows, vd), lambda h, b, i: (b * nq + i, h)),
        out_shape=jax.ShapeDtypeStruct((batch * seq, AW), BF16),
        scratch_shapes=[
            pltpu.VMEM((ATTN_Q_TILES, 6, vd, tile), BF16),
            pltpu.VMEM((ATTN_Q_TILES, 2, 1, tile), F32),
            pltpu.VMEM((ATTN_Q_TILES, 2, 8, tile), F32),
            pltpu.VMEM((ATTN_Q_TILES, 2, vd, tile), F32),
            pltpu.VMEM((ATTN_Q_TILES, 2, tile, tile), BF16),
            pltpu.VMEM((ATTN_Q_TILES, 2, tile, tile), BF16),
        ],
        compiler_params=pltpu.CompilerParams(
            dimension_semantics=("parallel", "parallel", "arbitrary"),
            vmem_limit_bytes=VMEM_LIMIT),
        name="diff_attention",
    )(rel_bias, lam_init, qn, k1, k2, vt, qg, kg, bias_t, lambdas, sub_g_col)


def _outffn_kernel(x_ref, hf_ref, hb_ref, og_ref, mb_ref, ng_ref, wo_ref, g2_ref, w1_ref, w2_ref,
                   out_ref, *, ff_chunk):
    hs = hf_ref[...].astype(F32) + hb_ref[...].astype(F32)
    x1 = x_ref[...] + _dot(mb_ref[...], wo_ref[MW:, :])
    for hh in range(N_M):
        sl = slice(hh * HD_M, (hh + 1) * HD_M)
        t = hs[:, sl]
        ms = jnp.mean(t * t, axis=-1, keepdims=True)
        mix = t * lax.rsqrt(ms + EPS) * ng_ref[:, sl] * _sigmoid(og_ref[:, sl].astype(F32))
        x1 = x1 + _dot(mix.astype(BF16), wo_ref[sl, :])
    ms = jnp.mean(x1 * x1, axis=-1, keepdims=True)
    h2 = (x1 * lax.rsqrt(ms + EPS) * g2_ref[...]).astype(BF16)
    out_ref[...] = x1
    for j in range(w1_ref.shape[1] // ff_chunk):
        sl = slice(j * ff_chunk, (j + 1) * ff_chunk)
        u = jnp.maximum(_dot(h2, w1_ref[:, sl]), 0.0)
        out_ref[...] += _dot((u * u).astype(BF16), w2_ref[sl, :])


def _outffn(x, hf, hb, om, mixb, ng, wo, g2, w1, w2, layer, tm):
    n, d = x.shape
    dff = w1.shape[2]
    row = lambda i: (i, 0)
    return pl.pallas_call(
        functools.partial(_outffn_kernel, ff_chunk=1024),
        grid=(n // tm,),
        in_specs=[
            pl.BlockSpec((tm, d), row),
            pl.BlockSpec((tm, MW), row),
            pl.BlockSpec((tm, MW), row),
            pl.BlockSpec((tm, MW), row),
            pl.BlockSpec((tm, AW), row),
            _resident((1, MW)),
            _resident_layer((MW + AW, d), layer),
            _resident((1, d)),
            _resident_layer((d, dff), layer),
            _resident_layer((dff, d), layer),
        ],
        out_specs=pl.BlockSpec((tm, d), row),
        out_shape=jax.ShapeDtypeStruct((n, d), F32),
        compiler_params=pltpu.CompilerParams(
            dimension_semantics=("parallel",), vmem_limit_bytes=VMEM_LIMIT),
        name="outproj_ffn",
    )(x, hf, hb, om, mixb, ng, wo, g2, w1, w2)


def kernel(x, norm1_g, w_in, conv_w, conv_b, gate_b, mlstm_norm_g, q_norm_g, k_norm_g, lambdas,
           diff_norm_g, rel_bias, w_out, norm2_g, w_ff1, w_ff2):
    batch, seq, d = x.shape
    depth = w_in.shape[0]
    n = batch * seq
    tm = tile = ROW_TILE
    assert d == w_in.shape[1] and w_in.shape[2] == C_G + N_GATES + 3 * AW
    assert seq % tile == 0 and tile % MLSTM_CHUNK == 0 and tile >= REL_MAX_DIST
    assert seq // tile >= ATTN_WINDOW and (seq // tile - ATTN_WINDOW) % 2 == 0
    assert (seq // tile) % ATTN_Q_TILES == 0

    w_main = jnp.concatenate([w_in[:, :, :C_G], w_in[:, :, C_G + N_GATES:]], axis=-1).astype(BF16)
    w_gate = jnp.pad(w_in[:, :, C_G:C_G + N_GATES], ((0, 0), (0, 0), (0, LANES - N_GATES))).astype(BF16)
    w_out_b = w_out.astype(BF16)
    w_ff1_b = w_ff1.astype(BF16)
    w_ff2_b = w_ff2.astype(BF16)
    lane = jnp.arange(2 * LANES)
    bd = jnp.where((lane[:, None] // HD_A) == (lane[None, :] // HD_A), 1.0 / HD_A, 0.0).astype(BF16)

    bias_t = _bias_tiles(rel_bias.astype(F32), tile)

    xf = x.reshape(n, d)
    for l in range(depth):
        lam_init = 0.8 - 0.6 * math.exp(-0.3 * l)
        qg = jnp.tile(q_norm_g[l], 2 * LANES // HD_A)[None, :]
        kg = jnp.tile(k_norm_g[l], 2 * LANES // HD_A)[None, :]
        gb = jnp.pad(gate_b[l].reshape(1, N_GATES).astype(F32), ((0, 0), (0, LANES - N_GATES)))
        qtm, km, vtm, om, gate_cols, gate_rows, qn, k1, k2, vt = _inproj(
            xf, norm1_g[l][None, :], w_main, w_gate, l, gb, conv_w[l], conv_b[l][None, :],
            qg, kg, bd, seq, tm)
        hf, hb = _mlstm(qtm, km, vtm, gate_cols, gate_rows, batch, seq, tm)
        mixb = _attention(qn, k1, k2, vt, qg, kg, bias_t, rel_bias.astype(F32),
                          jnp.array([[lam_init, 1.0 - lam_init]], F32), lambdas[l],
                          diff_norm_g[l].reshape(N_A, 2 * HD_A, 1), batch, seq, tile)
        xf = _outffn(xf, hf, hb, om, mixb, mlstm_norm_g[l][None, :], w_out_b,
                     norm2_g[l][None, :], w_ff1_b, w_ff2_b, l, tm)
    return xf.reshape(batch, seq, d)
```

```python
import functools
import math

import jax
import jax.numpy as jnp
from jax import lax
from jax.experimental import pallas as pl
from jax.experimental.pallas import tpu as pltpu

F32 = jnp.float32
BF16 = jnp.bfloat16

N_M = 4
HD_M = 128
MW = N_M * HD_M
MLSTM_CHUNK = 128
CONV_W = 5
N_A = 4
HD_A = 64
AW = N_A * 2 * HD_A
REL_BUCKETS = 32
REL_MAX_DIST = 128
EPS = 1e-6
N_GATES = 4 * N_M
C_G = 4 * MW
LANES = 128
HALO = 16
VMEM_LIMIT = 56 * 1024 * 1024
LOG2E = math.log2(math.e)
VT_ROWS = HD_M + HALO
BIAS_ROWS = 3
LOGIT_SPAN_LIMIT = 96.0
ATTN_WINDOW = 4
ATTN_Q_TILES = 2
ROW_TILE = 512
FAR_BUCKET = REL_BUCKETS // 2 - 1


def _dot(a, b):
    return jnp.dot(a, b, preferred_element_type=F32)


def _dot_exact(a, b):
    return jnp.dot(a, b, preferred_element_type=F32, precision=lax.Precision.HIGHEST)


def _sigmoid(x):
    return 1.0 / (1.0 + jnp.exp(-x))


def _log_sigmoid(x):
    return jnp.minimum(x, 0.0) - jnp.log1p(jnp.exp(-jnp.abs(x)))


def _resident(shape):
    nd = len(shape)
    return pl.BlockSpec(shape, lambda *_: (0,) * nd, pipeline_mode=pl.Buffered(1))


def _resident_layer(shape, layer):
    nd = len(shape)
    return pl.BlockSpec((None,) + tuple(shape), lambda *_: (layer,) + (0,) * nd,
                        pipeline_mode=pl.Buffered(1))


def _inproj_kernel(xp_ref, x_ref, xn_ref, g1_ref, w_ref, wg_ref, gb_ref, cw_ref, cb_ref, qg_ref, kg_ref,
                   bd_ref,
                   qtm_ref, km_ref, vtm_ref, om_ref, gcol_ref, grow_ref, qn_ref, k1_ref, k2_ref, vt_ref,
                   ext_s, *, tiles_per_seq):
    rows = x_ref.shape[0]
    pos = pl.program_id(0) % tiles_per_seq

    def normed(xv):
        ms = jnp.mean(xv * xv, axis=-1, keepdims=True)
        return (xv * lax.rsqrt(ms + EPS) * g1_ref[...]).astype(BF16)

    x = x_ref[...]
    h = normed(x)

    def proj(c0, width):
        return _dot(h, w_ref[:, c0:c0 + width])

    h_prev = jnp.where(pos > 0, normed(xp_ref[...]), jnp.zeros((), BF16))
    h_next = jnp.where(pos < tiles_per_seq - 1, normed(xn_ref[...]), jnp.zeros((), BF16))
    ext_s[...] = _dot(jnp.concatenate([h_prev, h, h_next], axis=0), w_ref[:, 0:2 * MW])

    def conv_head(cb):
        cols = slice(cb * LANES, (cb + 1) * LANES)
        conv = cb_ref[:, cols]
        for j in range(CONV_W):
            off = HALO - CONV_W // 2 + j
            conv = conv + ext_s[off:off + rows, cols] * cw_ref[j:j + 1, cols]
        y = conv * _sigmoid(conv)
        if cb < N_M:
            qtm_ref[cols, :] = y.T.astype(BF16)
        else:
            km_ref[:, cb * LANES - MW:(cb + 1) * LANES - MW] = (y * (HD_M ** -0.5)).astype(BF16)

    ones_rows = jnp.ones((HALO, rows), BF16)
    pair = 2 * LANES

    def values_t(c0, out_ref):
        v4 = proj(c0, N_M * HD_M)
        for hh in range(N_M):
            out_ref[0, hh * VT_ROWS:hh * VT_ROWS + HD_M, :] = (
                v4[:, hh * LANES:(hh + 1) * LANES].T.astype(BF16))
            out_ref[0, hh * VT_ROWS + HD_M:(hh + 1) * VT_ROWS, :] = ones_rows

    def out_gate():
        om_ref[...] = proj(3 * MW, MW).astype(BF16)
        L = MLSTM_CHUNK
        gt = (_dot(h, wg_ref[...]) + gb_ref[...]).T[:N_GATES]
        logf = _log_sigmoid(gt)
        g_under_f = pltpu.roll(gt, N_M, 0)
        upper_f = (lax.broadcasted_iota(jnp.int32, (L, L), 0)
                   <= lax.broadcasted_iota(jnp.int32, (L, L), 1)).astype(F32)
        forward_rows = lax.broadcasted_iota(jnp.int32, (N_GATES, L), 0) < 2 * N_M
        for r in range(rows // L):
            sl = slice(r * L, (r + 1) * L)
            cum = _dot_exact(logf[:, sl], upper_f)
            suf = cum[:, L - 1:L] - cum + logf[:, sl]
            grow_ref[0:N_GATES, sl] = cum
            grow_ref[N_GATES:2 * N_GATES, sl] = gt[:, sl]
            grow_ref[2 * N_GATES:3 * N_GATES, sl] = suf
            src = jnp.where(forward_rows, cum, suf) - g_under_f[:, sl]
            src = jnp.concatenate([src, jnp.zeros((L - N_GATES, L), F32)], axis=0)
            gcol_ref[sl, :] = src.T[:, :N_GATES]

    bd = bd_ref[...]

    def qk_norm(c0, g_ref):
        t = proj(c0, AW)
        halves = []
        for hp in range(AW // pair):
            th = t[:, hp * pair:(hp + 1) * pair]
            msq = _dot((th * th).astype(BF16), bd)
            halves.append(th * lax.rsqrt(msq + EPS) * g_ref[...])
        return halves

    lane = lax.broadcasted_iota(jnp.int32, (rows, pair), 1) % LANES
    ones_cols = jnp.where(lane < HD_A + BIAS_ROWS, 1.0, 0.0)

    def attn_q():
        for hp, qn in enumerate(qk_norm(4 * MW, qg_ref)):
            qn_ref[:, hp * pair:(hp + 1) * pair] = (qn * (HD_A ** -0.5 * LOG2E)).astype(BF16)

    def attn_k():
        for hp, kn in enumerate(qk_norm(4 * MW + AW, kg_ref)):
            sl = slice(hp * pair, (hp + 1) * pair)
            k1_ref[:, sl] = jnp.where(lane < HD_A, kn, ones_cols).astype(BF16)
            k2_ref[:, sl] = jnp.where(lane < HD_A, pltpu.roll(kn, pair - HD_A, 1), ones_cols).astype(BF16)

    for cb in range(2 * MW // LANES):
        conv_head(cb)
    values_t(2 * MW, vtm_ref)
    out_gate()
    attn_q()
    attn_k()
    values_t(4 * MW + 2 * AW, vt_ref)


def _inproj(x, g1, w_main, w_gate, layer, gate_b, conv_w, conv_b, qg, kg, bd, seq, tm):
    n, d = x.shape
    wcols = w_main.shape[2]
    hpt = tm // HALO
    nhalo = n // HALO
    row = lambda i: (i, 0)
    vt_spec = pl.BlockSpec((1, N_A * VT_ROWS, tm), lambda i: (i, 0, 0))
    vt_shape = jax.ShapeDtypeStruct((n // tm, N_A * VT_ROWS, tm), BF16)
    return pl.pallas_call(
        functools.partial(_inproj_kernel, tiles_per_seq=seq // tm),
        grid=(n // tm,),
        in_specs=[
            pl.BlockSpec((HALO, d), lambda i: (jnp.maximum(i * hpt - 1, 0), 0)),
            pl.BlockSpec((tm, d), row),
            pl.BlockSpec((HALO, d), lambda i: (jnp.minimum((i + 1) * hpt, nhalo - 1), 0)),
            _resident((1, d)),
            _resident_layer((d, wcols), layer),
            _resident_layer((d, LANES), layer),
            _resident((1, LANES)),
            _resident((CONV_W, 2 * MW)),
            _resident((1, 2 * MW)),
            _resident((1, 2 * LANES)),
            _resident((1, 2 * LANES)),
            _resident((2 * LANES, 2 * LANES)),
        ],
        out_specs=[
            pl.BlockSpec((MW, tm), lambda i: (0, i)),
            pl.BlockSpec((tm, MW), row),
            vt_spec,
            pl.BlockSpec((tm, MW), row),
            pl.BlockSpec((tm, N_GATES), row),
            pl.BlockSpec((3 * N_GATES, tm), lambda i: (0, i)),
            pl.BlockSpec((tm, AW), row),
            pl.BlockSpec((tm, AW), row),
            pl.BlockSpec((tm, AW), row),
            vt_spec,
        ],
        out_shape=[
            jax.ShapeDtypeStruct((MW, n), BF16),
            jax.ShapeDtypeStruct((n, MW), BF16),
            vt_shape,
            jax.ShapeDtypeStruct((n, MW), BF16),
            jax.ShapeDtypeStruct((n, N_GATES), F32),
            jax.ShapeDtypeStruct((3 * N_GATES, n), F32),
            jax.ShapeDtypeStruct((n, AW), BF16),
            jax.ShapeDtypeStruct((n, AW), BF16),
            jax.ShapeDtypeStruct((n, AW), BF16),
            vt_shape,
        ],
        scratch_shapes=[pltpu.VMEM((tm + 2 * HALO, 2 * MW), F32)],
        compiler_params=pltpu.CompilerParams(
            dimension_semantics=("parallel",), vmem_limit_bytes=VMEM_LIMIT),
        name="inproj",
    )(x, x, x, g1, w_main, w_gate, gate_b, conv_w, conv_b, qg, kg, bd)


def _mlstm_kernel(qt_f, k_f, vt_f, gc_f, gr_f,
                  qt_b, k_b, vt_b, gc_b, gr_b,
                  hf_ref, hb_ref,
                  c_s, m_s):
    L = MLSTM_CHUNK
    chunks = k_f.shape[0] // L

    @pl.when(pl.program_id(1) == 0)
    def _():
        c_s[...] = jnp.zeros_like(c_s)
        m_s[...] = jnp.zeros_like(m_s)

    row = lax.broadcasted_iota(jnp.int32, (L, L), 0)
    col = lax.broadcasted_iota(jnp.int32, (L, L), 1)
    lower = col <= row
    upper = row <= col
    dirs = ((qt_f, k_f, vt_f, gc_f, gr_f, hf_ref), (qt_b, k_b, vt_b, gc_b, gr_b, hb_ref))
    state = [(c_s[idx], m_s[idx][:, :1]) for idx in range(2 * N_M)]

    for step in range(chunks):
        chains = []
        for d, (qt_ref, k_ref, vt_ref, gc_ref, gr_ref, out_ref) in enumerate(dirs):
            tok = slice(step * L, (step + 1) * L) if d == 0 else slice((chunks - 1 - step) * L,
                                                                       (chunks - step) * L)
            i_off = 2 * N_M * d
            f_off = i_off + N_M
            src = gc_ref[tok, :]
            cum_rows = gr_ref[0:N_GATES, tok] if d == 0 else gr_ref[2 * N_GATES:3 * N_GATES, tok]
            gate_rows = gr_ref[N_GATES:2 * N_GATES, tok]
            for hh in range(N_M):
                idx = d * N_M + hh
                chains.append(dict(
                    idx=idx, out_ref=out_ref, tok=tok, cols=slice(hh * HD_M, (hh + 1) * HD_M),
                    mask=upper if d == 0 else lower,
                    src_col=src[:, f_off + hh:f_off + hh + 1],
                    b_row=cum_rows[f_off + hh:f_off + hh + 1, :],
                    i_row=gate_rows[i_off + hh:i_off + hh + 1, :],
                    end=slice(L - 1, L) if d == 0 else slice(0, 1),
                    qt=qt_ref[hh * HD_M:(hh + 1) * HD_M, tok],
                    kb=k_ref[tok, hh * HD_M:(hh + 1) * HD_M],
                    vt=vt_ref[0, hh * VT_ROWS:(hh + 1) * VT_ROWS, tok],
                    c_prev=state[idx][0], m_prev=state[idx][1],
                ))

        for ch in chains:
            ch["s"] = _dot(ch["kb"], ch["qt"])
            ch["inter_num"] = _dot(ch["c_prev"].astype(BF16), ch["qt"])
        for ch in chains:
            dmat = jnp.where(ch["mask"], ch["b_row"] - ch["src_col"], -jnp.inf)
            inter = ch["b_row"] + ch["m_prev"]
            ch["m_t"] = jnp.maximum(inter, jnp.max(dmat, axis=0, keepdims=True))
            ch["scale"] = jnp.exp(inter - ch["m_t"])
            ch["w"] = (ch["s"] * jnp.exp(dmat - ch["m_t"])).astype(BF16)
        for ch in chains:
            num = ch["scale"] * ch["inter_num"] + _dot(ch["vt"], ch["w"])
            den = num[HD_M:HD_M + 1]
            ht = num[:HD_M] / jnp.maximum(jnp.abs(den), jnp.exp(-ch["m_t"]))
            ch["out_ref"][ch["tok"], ch["cols"]] = ht.T.astype(ch["out_ref"].dtype)
        for ch in chains:
            b_end = ch["b_row"][:, ch["end"]]
            g_row = b_end - ch["b_row"] + ch["i_row"]
            m_new = jnp.maximum(b_end + ch["m_prev"], jnp.max(g_row, axis=-1, keepdims=True))
            decay = jnp.exp(b_end + ch["m_prev"] - m_new)
            vt_w = (ch["vt"].astype(F32) * jnp.exp(g_row - m_new)).astype(BF16)
            state[ch["idx"]] = (decay * ch["c_prev"] + _dot(vt_w, ch["kb"]), m_new)

    for idx, (c_new, m_new) in enumerate(state):
        c_s[idx] = c_new
        m_s[idx] = jnp.broadcast_to(m_new, (1, LANES))


def _mlstm(qt, k, vt, gate_cols, gate_rows, batch, seq, tm):
    nt = seq // tm

    def tile_f(b, c):
        return b * nt + c

    def tile_b(b, c):
        return b * nt + (nt - 1 - c)

    def specs(tile):
        return [
            pl.BlockSpec((MW, tm), lambda b, c: (0, tile(b, c))),
            pl.BlockSpec((tm, MW), lambda b, c: (tile(b, c), 0)),
            pl.BlockSpec((1, N_M * VT_ROWS, tm), lambda b, c: (tile(b, c), 0, 0)),
            pl.BlockSpec((tm, N_GATES), lambda b, c: (tile(b, c), 0)),
            pl.BlockSpec((3 * N_GATES, tm), lambda b, c: (0, tile(b, c))),
        ]

    n = batch * seq
    return pl.pallas_call(
        _mlstm_kernel,
        grid=(batch, nt),
        in_specs=specs(tile_f) + specs(tile_b),
        out_specs=[
            pl.BlockSpec((tm, MW), lambda b, c: (tile_f(b, c), 0)),
            pl.BlockSpec((tm, MW), lambda b, c: (tile_b(b, c), 0)),
        ],
        out_shape=[jax.ShapeDtypeStruct((n, MW), BF16)] * 2,
        scratch_shapes=[
            pltpu.VMEM((2 * N_M, VT_ROWS, HD_M), F32),
            pltpu.VMEM((2 * N_M, 1, LANES), F32),
        ],
        compiler_params=pltpu.CompilerParams(
            dimension_semantics=("parallel", "arbitrary"), vmem_limit_bytes=VMEM_LIMIT),
        name="mlstm",
    )(qt, k, vt, gate_cols, gate_rows, qt, k, vt, gate_cols, gate_rows)


def _bias_kernel(rb_ref, out_ref, *, tile):
    hh = pl.program_id(0)
    dd = pl.program_id(1)
    kk = lax.broadcasted_iota(jnp.int32, (tile, tile), 0)
    qq = lax.broadcasted_iota(jnp.int32, (tile, tile), 1)
    rel = (dd - 1) * tile + kk - qq
    nb = REL_BUCKETS // 2
    max_exact = nb // 2
    n = jnp.abs(rel)
    nf = jnp.maximum(n, 1).astype(F32)
    large = max_exact + (jnp.log(nf / max_exact) / math.log(REL_MAX_DIST / max_exact)
                         * (nb - max_exact)).astype(jnp.int32)
    large = jnp.minimum(large, nb - 1)
    bucket = jnp.where(rel > 0, nb, 0) + jnp.where(n < max_exact, n, large)
    val = jnp.zeros((tile, tile), F32)
    for bkt in range(REL_BUCKETS):
        val = jnp.where(bucket == bkt, rb_ref[bkt, hh], val)
    out_ref[0, 0] = jnp.where(dd < 3, val * LOG2E, 0.0)


def _bias_tiles(rel_bias, tile):
    return pl.pallas_call(
        functools.partial(_bias_kernel, tile=tile),
        grid=(N_A, 4),
        in_specs=[pl.BlockSpec(memory_space=pltpu.SMEM)],
        out_specs=pl.BlockSpec((1, 1, tile, tile), lambda h, d: (h, d, 0, 0)),
        out_shape=jax.ShapeDtypeStruct((N_A, 4, tile, tile), F32),
        name="rel_bias_tiles",
    )(rel_bias)


def _sublane_partial_sums(e):
    return jnp.sum(e.reshape(e.shape[0] // 8, 8, e.shape[1]), axis=0)


def _attn_kernel(rb_ref, li_ref, q_ref, k1_ref, k2_ref, vt_ref, qg_ref, kg_ref, bias_ref, lam_ref, sg_ref,
                 o_ref,
                 qt_s, m_s, l_s, acc_s, p0_s, p1_s):
    tq = vt_ref.shape[2]
    for sub in range(q_ref.shape[0] // tq):
        rows = pl.ds(sub * tq, tq)
        _attn_query_tile(pl.program_id(2) * (q_ref.shape[0] // tq) + sub,
                         rb_ref, li_ref, q_ref.at[rows], k1_ref, k2_ref, vt_ref, qg_ref, kg_ref,
                         bias_ref, lam_ref, sg_ref, o_ref.at[rows],
                         qt_s.at[sub], m_s.at[sub], l_s.at[sub], acc_s.at[sub], p0_s.at[sub], p1_s.at[sub])


def _attn_query_tile(qi, rb_ref, li_ref, q_ref, k1_ref, k2_ref, vt_ref, qg_ref, kg_ref, bias_ref,
                     lam_ref, sg_ref, o_ref, qt_s, m_s, l_s, acc_s, p0_s, p1_s):
    hh = pl.program_id(0)
    tq = q_ref.shape[0]
    tk = vt_ref.shape[2]
    nk = vt_ref.shape[0]
    vd = 2 * HD_A
    LEFT, RIGHT, NEAR = 0, 1, 2

    qt = q_ref[...].astype(F32).T
    rid = lax.broadcasted_iota(jnp.int32, qt.shape, 0)
    bases = (qt, jnp.concatenate([qt[HD_A:], qt[:HD_A]], axis=0))
    consts = (rb_ref[FAR_BUCKET, hh] * LOG2E, rb_ref[REL_BUCKETS // 2 + FAR_BUCKET, hh] * LOG2E, 0.0)

    def build_queries(shift):
        for i, base in enumerate(bases):
            body = jnp.where(rid < HD_A, base, 0.0).astype(BF16)
            for kind in range(len(consts)):
                qt_s[2 * kind + i] = body
        rid_o = lax.broadcasted_iota(jnp.int32, (HALO, tq), 0)
        for kind, cst in enumerate(consts):
            rest = jnp.full((HALO, tq), cst - shift, F32)
            extra = jnp.zeros((HALO, tq), F32)
            for r in range(BIAS_ROWS):
                part = rest.astype(BF16).astype(F32)
                extra = jnp.where(rid_o == r, part, extra)
                rest = rest - part
            for i in range(2):
                qt_s[2 * kind + i, HD_A:HD_A + HALO, :] = extra.astype(BF16)

    acc_s[...] = jnp.zeros_like(acc_s)
    l_s[...] = jnp.zeros_like(l_s)

    def key_rows(j):
        return pl.ds(pl.multiple_of(j * tk, tk), tk)

    bucket_vals = [rb_ref[bkt, hh] for bkt in range(REL_BUCKETS)]
    bias_max = functools.reduce(jnp.maximum, bucket_vals) * LOG2E
    bias_min = functools.reduce(jnp.minimum, bucket_vals) * LOG2E
    reach = (HD_A ** 0.5 * LOG2E) * jnp.max(jnp.abs(qg_ref[...])) * jnp.max(jnp.abs(kg_ref[...]))
    shift = reach + bias_max
    span = 2.0 * reach + (bias_max - bias_min)
    bounded = span <= LOGIT_SPAN_LIMIT

    @pl.when(bounded)
    def _():
        build_queries(shift)

        def numerators(j, kind, p_buf, bias=None):
            for i, k_ref in enumerate((k1_ref, k2_ref)):
                s = _dot(k_ref[key_rows(j), :], qt_s[2 * kind + i])
                if bias is not None:
                    s = s + bias
                e = jnp.exp2(s)
                l_s[i] += _sublane_partial_sums(e)
                p_buf[i] = e.astype(BF16)

        def accumulate(j, p_buf):
            vt = vt_ref[j, :vd, :]
            for i in range(2):
                acc_s[i] += _dot(vt, p_buf[i])

        w0 = jnp.clip(qi - 1, 0, nk - ATTN_WINDOW)
        far_tile = lambda t: jnp.where(t < w0, t, t + ATTN_WINDOW)
        far_kind = lambda t: jnp.where(t < w0, LEFT, RIGHT)
        p_bufs = (p0_s, p1_s)
        for r in range(ATTN_WINDOW):
            j = w0 + r
            dlt = j - qi
            near = jnp.abs(dlt) <= 1
            kind = jnp.where(near, NEAR, jnp.where(dlt < 0, LEFT, RIGHT))
            numerators(j, kind, p_bufs[r % 2], bias_ref[0, jnp.where(near, dlt + 1, 3)])
            if r > 0:
                accumulate(j - 1, p_bufs[(r - 1) % 2])
        last_window = w0 + ATTN_WINDOW - 1

        n_far = nk - ATTN_WINDOW
        for t0 in range(0, n_far, 2):
            numerators(far_tile(t0), far_kind(t0), p0_s)
            accumulate(far_tile(t0 - 1) if t0 else last_window, p1_s)
            numerators(far_tile(t0 + 1), far_kind(t0 + 1), p1_s)
            accumulate(far_tile(t0), p0_s)
        accumulate(far_tile(n_far - 1) if n_far else last_window, p1_s)

    @pl.when(jnp.logical_not(bounded))
    def _():
        _online_softmax_tiles(qi, nk, build_queries, key_rows, k1_ref, k2_ref, vt_ref, bias_ref,
                              qt_s, m_s, l_s, acc_s)

    lv = lam_ref[...]
    lam = (jnp.exp(jnp.sum(lv[0:1] * lv[1:2], axis=-1, keepdims=True))
           - jnp.exp(jnp.sum(lv[2:3] * lv[3:4], axis=-1, keepdims=True)) + li_ref[0, 0])
    sums = [jnp.sum(l_s[i], axis=0, keepdims=True) for i in range(2)]
    o = acc_s[0] / sums[0] - lam * (acc_s[1] / sums[1])
    ms = jnp.mean(o * o, axis=0, keepdims=True)
    o = o * lax.rsqrt(ms + EPS) * sg_ref[0] * li_ref[0, 1]
    o_ref[...] = o.T.astype(o_ref.dtype)


def _online_softmax_tiles(qi, nk, build_queries, key_rows, k1_ref, k2_ref, vt_ref, bias_ref,
                          qt_s, m_s, l_s, acc_s):
    LEFT, RIGHT, NEAR = 0, 1, 2
    build_queries(0.0)
    m_s[...] = jnp.full_like(m_s, -jnp.inf)

    def one_tile(j, carry):
        dlt = j - qi
        near = jnp.abs(dlt) <= 1
        kind = jnp.where(near, NEAR, jnp.where(dlt < 0, LEFT, RIGHT))
        bias = bias_ref[0, jnp.where(near, dlt + 1, 3)]
        vt = vt_ref[j, :2 * HD_A, :]
        for i, k_ref in enumerate((k1_ref, k2_ref)):
            s = _dot(k_ref[key_rows(j), :], qt_s[2 * kind + i]) + bias
            m_old = m_s[i]
            m_new = jnp.maximum(m_old, jnp.max(s, axis=0, keepdims=True))
            alpha = jnp.exp2(m_old - m_new)
            e = jnp.exp2(s - m_new)
            l_s[i] = alpha * l_s[i] + _sublane_partial_sums(e)
            acc_s[i] = alpha * acc_s[i] + _dot(vt, e.astype(BF16))
            m_s[i] = m_new
        return carry

    lax.fori_loop(0, nk, one_tile, 0)


def _attention(qn, k1, k2, vt, qg, kg, bias_t, rel_bias, lam_init, lambdas, sub_g_col, batch, seq, tile):
    nt = seq // tile
    nq = nt // ATTN_Q_TILES
    vd = 2 * HD_A
    smem = pl.BlockSpec(memory_space=pltpu.SMEM)
    q_rows = ATTN_Q_TILES * tile
    return pl.pallas_call(
        _attn_kernel,
        grid=(N_A, batch, nq),
        in_specs=[
            smem,
            smem,
            pl.BlockSpec((q_rows, vd), lambda h, b, i: (b * nq + i, h)),
            pl.BlockSpec((seq, vd), lambda h, b, i: (b, h)),
            pl.BlockSpec((seq, vd), lambda h, b, i: (b, h)),
            pl.BlockSpec((nt, VT_ROWS, tile), lambda h, b, i: (b, h, 0)),
            pl.BlockSpec((1, 2 * LANES), lambda h, b, i: (0, 0)),
            pl.BlockSpec((1, 2 * LANES), lambda h, b, i: (0, 0)),
            pl.BlockSpec((1, 4, tile, tile), lambda h, b, i: (h, 0, 0, 0)),
            pl.BlockSpec((4, HD_A), lambda h, b, i: (0, 0)),
            pl.BlockSpec((1, vd, 1), lambda h, b, i: (h, 0, 0)),
        ],
        out_specs=pl.BlockSpec((q_rows, vd), lambda h, b, i: (b * nq + i, h)),
        out_shape=jax.ShapeDtypeStruct((batch * seq, AW), BF16),
        scratch_shapes=[
            pltpu.VMEM((ATTN_Q_TILES, 6, vd, tile), BF16),
            pltpu.VMEM((ATTN_Q_TILES, 2, 1, tile), F32),
            pltpu.VMEM((ATTN_Q_TILES, 2, 8, tile), F32),
            pltpu.VMEM((ATTN_Q_TILES, 2, vd, tile), F32),
            pltpu.VMEM((ATTN_Q_TILES, 2, tile, tile), BF16),
            pltpu.VMEM((ATTN_Q_TILES, 2, tile, tile), BF16),
        ],
        compiler_params=pltpu.CompilerParams(
            dimension_semantics=("parallel", "parallel", "arbitrary"),
            vmem_limit_bytes=VMEM_LIMIT),
        name="diff_attention",
    )(rel_bias, lam_init, qn, k1, k2, vt, qg, kg, bias_t, lambdas, sub_g_col)


def _outffn_kernel(x_ref, hf_ref, hb_ref, og_ref, mb_ref, ng_ref, wo_ref, g2_ref, w1_ref, w2_ref,
                   out_ref, *, ff_chunk):
    hs = hf_ref[...].astype(F32) + hb_ref[...].astype(F32)
    mixed = []
    for hh in range(N_M):
        sl = slice(hh * HD_M, (hh + 1) * HD_M)
        t = hs[:, sl]
        ms = jnp.mean(t * t, axis=-1, keepdims=True)
        mix = t * lax.rsqrt(ms + EPS) * ng_ref[:, sl] * _sigmoid(og_ref[:, sl].astype(F32))
        mixed.append(mix.astype(BF16))
    mixed.append(mb_ref[...])
    x1 = x_ref[...] + _dot(jnp.concatenate(mixed, axis=1), wo_ref[...])
    ms = jnp.mean(x1 * x1, axis=-1, keepdims=True)
    h2 = (x1 * lax.rsqrt(ms + EPS) * g2_ref[...]).astype(BF16)
    out_ref[...] = x1
    for j in range(w1_ref.shape[1] // ff_chunk):
        sl = slice(j * ff_chunk, (j + 1) * ff_chunk)
        u = jnp.maximum(_dot(h2, w1_ref[:, sl]), 0.0)
        out_ref[...] += _dot((u * u).astype(BF16), w2_ref[sl, :])


def _outffn(x, hf, hb, om, mixb, ng, wo, g2, w1, w2, layer, tm):
    n, d = x.shape
    dff = w1.shape[2]
    row = lambda i: (i, 0)
    return pl.pallas_call(
        functools.partial(_outffn_kernel, ff_chunk=1024),
        grid=(n // tm,),
        in_specs=[
            pl.BlockSpec((tm, d), row),
            pl.BlockSpec((tm, MW), row),
            pl.BlockSpec((tm, MW), row),
            pl.BlockSpec((tm, MW), row),
            pl.BlockSpec((tm, AW), row),
            _resident((1, MW)),
            _resident_layer((MW + AW, d), layer),
            _resident((1, d)),
            _resident_layer((d, dff), layer),
            _resident_layer((dff, d), layer),
        ],
        out_specs=pl.BlockSpec((tm, d), row),
        out_shape=jax.ShapeDtypeStruct((n, d), F32),
        compiler_params=pltpu.CompilerParams(
            dimension_semantics=("parallel",), vmem_limit_bytes=VMEM_LIMIT),
        name="outproj_ffn",
    )(x, hf, hb, om, mixb, ng, wo, g2, w1, w2)


def kernel(x, norm1_g, w_in, conv_w, conv_b, gate_b, mlstm_norm_g, q_norm_g, k_norm_g, lambdas,
           diff_norm_g, rel_bias, w_out, norm2_g, w_ff1, w_ff2):
    batch, seq, d = x.shape
    depth = w_in.shape[0]
    n = batch * seq
    tm = tile = ROW_TILE
    assert d == w_in.shape[1] and w_in.shape[2] == C_G + N_GATES + 3 * AW
    assert seq % tile == 0 and tile % MLSTM_CHUNK == 0 and tile >= REL_MAX_DIST
    assert seq // tile >= ATTN_WINDOW and (seq // tile - ATTN_WINDOW) % 2 == 0
    assert (seq // tile) % ATTN_Q_TILES == 0

    w_main = jnp.concatenate([w_in[:, :, :C_G], w_in[:, :, C_G + N_GATES:]], axis=-1).astype(BF16)
    w_gate = jnp.pad(w_in[:, :, C_G:C_G + N_GATES], ((0, 0), (0, 0), (0, LANES - N_GATES))).astype(BF16)
    w_out_b = w_out.astype(BF16)
    w_ff1_b = w_ff1.astype(BF16)
    w_ff2_b = w_ff2.astype(BF16)
    lane = jnp.arange(2 * LANES)
    bd = jnp.where((lane[:, None] // HD_A) == (lane[None, :] // HD_A), 1.0 / HD_A, 0.0).astype(BF16)

    bias_t = _bias_tiles(rel_bias.astype(F32), tile)

    xf = x.reshape(n, d)
    for l in range(depth):
        lam_init = 0.8 - 0.6 * math.exp(-0.3 * l)
        qg = jnp.tile(q_norm_g[l], 2 * LANES // HD_A)[None, :]
        kg = jnp.tile(k_norm_g[l], 2 * LANES // HD_A)[None, :]
        gb = jnp.pad(gate_b[l].reshape(1, N_GATES).astype(F32), ((0, 0), (0, LANES - N_GATES)))
        qtm, km, vtm, om, gate_cols, gate_rows, qn, k1, k2, vt = _inproj(
            xf, norm1_g[l][None, :], w_main, w_gate, l, gb, conv_w[l], conv_b[l][None, :],
            qg, kg, bd, seq, tm)
        hf, hb = _mlstm(qtm, km, vtm, gate_cols, gate_rows, batch, seq, tm)
        mixb = _attention(qn, k1, k2, vt, qg, kg, bias_t, rel_bias.astype(F32),
                          jnp.array([[lam_init, 1.0 - lam_init]], F32), lambdas[l],
                          diff_norm_g[l].reshape(N_A, 2 * HD_A, 1), batch, seq, tile)
        xf = _outffn(xf, hf, hb, om, mixb, mlstm_norm_g[l][None, :], w_out_b,
                     norm2_g[l][None, :], w_ff1_b, w_ff2_b, l, tm)
    return xf.reshape(batch, seq, d)
```

```python
import functools
import math

import jax
import jax.numpy as jnp
from jax import lax
from jax.experimental import pallas as pl
from jax.experimental.pallas import tpu as pltpu

F32 = jnp.float32
BF16 = jnp.bfloat16

N_M = 4
HD_M = 128
MW = N_M * HD_M
MLSTM_CHUNK = 256
CONV_W = 5
N_A = 4
HD_A = 64
AW = N_A * 2 * HD_A
REL_BUCKETS = 32
REL_MAX_DIST = 128
EPS = 1e-6
N_GATES = 4 * N_M
C_G = 4 * MW
LANES = 128
HALO = 16
VMEM_LIMIT = 56 * 1024 * 1024
LOG2E = math.log2(math.e)
VT_ROWS = HD_M + HALO
BIAS_ROWS = 3
LOGIT_SPAN_LIMIT = 96.0
ATTN_WINDOW = 4
ATTN_Q_TILES = 2
ROW_TILE = 512
FAR_BUCKET = REL_BUCKETS // 2 - 1


def _dot(a, b):
    return jnp.dot(a, b, preferred_element_type=F32)


def _dot_exact(a, b):
    return jnp.dot(a, b, preferred_element_type=F32, precision=lax.Precision.HIGHEST)


def _sigmoid(x):
    return 1.0 / (1.0 + jnp.exp(-x))


def _log_sigmoid(x):
    return jnp.minimum(x, 0.0) - jnp.log1p(jnp.exp(-jnp.abs(x)))


def _resident(shape):
    nd = len(shape)
    return pl.BlockSpec(shape, lambda *_: (0,) * nd, pipeline_mode=pl.Buffered(1))


def _resident_layer(shape, layer):
    nd = len(shape)
    return pl.BlockSpec((None,) + tuple(shape), lambda *_: (layer,) + (0,) * nd,
                        pipeline_mode=pl.Buffered(1))


def _inproj_kernel(xp_ref, x_ref, xn_ref, g1_ref, w_ref, wg_ref, gb_ref, cw_ref, cb_ref, qg_ref, kg_ref,
                   bd_ref,
                   qtm_ref, km_ref, vtm_ref, om_ref, gcol_ref, grow_ref, qn_ref, k1_ref, k2_ref, vt_ref,
                   ext_s, *, tiles_per_seq):
    rows = x_ref.shape[0]
    pos = pl.program_id(0) % tiles_per_seq

    def normed(xv):
        ms = jnp.mean(xv * xv, axis=-1, keepdims=True)
        return (xv * lax.rsqrt(ms + EPS) * g1_ref[...]).astype(BF16)

    x = x_ref[...]
    h = normed(x)

    def proj(c0, width):
        return _dot(h, w_ref[:, c0:c0 + width])

    h_prev = jnp.where(pos > 0, normed(xp_ref[...]), jnp.zeros((), BF16))
    h_next = jnp.where(pos < tiles_per_seq - 1, normed(xn_ref[...]), jnp.zeros((), BF16))
    ext_s[...] = _dot(jnp.concatenate([h_prev, h, h_next], axis=0), w_ref[:, 0:2 * MW])

    def conv_head(cb):
        cols = slice(cb * LANES, (cb + 1) * LANES)
        conv = cb_ref[:, cols]
        for j in range(CONV_W):
            off = HALO - CONV_W // 2 + j
            conv = conv + ext_s[off:off + rows, cols] * cw_ref[j:j + 1, cols]
        y = conv * _sigmoid(conv)
        if cb < N_M:
            qtm_ref[cols, :] = y.T.astype(BF16)
        else:
            km_ref[:, cb * LANES - MW:(cb + 1) * LANES - MW] = (y * (HD_M ** -0.5)).astype(BF16)

    ones_rows = jnp.ones((HALO, rows), BF16)
    pair = 2 * LANES

    def values_t(c0, out_ref):
        v4 = proj(c0, N_M * HD_M)
        for hh in range(N_M):
            out_ref[0, hh * VT_ROWS:hh * VT_ROWS + HD_M, :] = (
                v4[:, hh * LANES:(hh + 1) * LANES].T.astype(BF16))
            out_ref[0, hh * VT_ROWS + HD_M:(hh + 1) * VT_ROWS, :] = ones_rows

    def out_gate():
        om_ref[...] = proj(3 * MW, MW).astype(BF16)
        L = MLSTM_CHUNK
        gt = (_dot(h, wg_ref[...]) + gb_ref[...]).T[:N_GATES]
        logf = _log_sigmoid(gt)
        g_under_f = pltpu.roll(gt, N_M, 0)
        upper_f = (lax.broadcasted_iota(jnp.int32, (L, L), 0)
                   <= lax.broadcasted_iota(jnp.int32, (L, L), 1)).astype(F32)
        forward_rows = lax.broadcasted_iota(jnp.int32, (N_GATES, L), 0) < 2 * N_M
        for r in range(rows // L):
            sl = slice(r * L, (r + 1) * L)
            cum = _dot_exact(logf[:, sl], upper_f)
            suf = cum[:, L - 1:L] - cum + logf[:, sl]
            grow_ref[0:N_GATES, sl] = cum
            grow_ref[N_GATES:2 * N_GATES, sl] = gt[:, sl]
            grow_ref[2 * N_GATES:3 * N_GATES, sl] = suf
            src = jnp.where(forward_rows, cum, suf) - g_under_f[:, sl]
            src = jnp.concatenate([src, jnp.zeros((L - N_GATES, L), F32)], axis=0)
            gcol_ref[sl, :] = src.T[:, :N_GATES]

    bd = bd_ref[...]

    def qk_norm(c0, g_ref):
        t = proj(c0, AW)
        halves = []
        for hp in range(AW // pair):
            th = t[:, hp * pair:(hp + 1) * pair]
            msq = _dot((th * th).astype(BF16), bd)
            halves.append(th * lax.rsqrt(msq + EPS) * g_ref[...])
        return halves

    lane = lax.broadcasted_iota(jnp.int32, (rows, pair), 1) % LANES
    ones_cols = jnp.where(lane < HD_A + BIAS_ROWS, 1.0, 0.0)

    def attn_q():
        for hp, qn in enumerate(qk_norm(4 * MW, qg_ref)):
            qn_ref[:, hp * pair:(hp + 1) * pair] = (qn * (HD_A ** -0.5 * LOG2E)).astype(BF16)

    def attn_k():
        for hp, kn in enumerate(qk_norm(4 * MW + AW, kg_ref)):
            sl = slice(hp * pair, (hp + 1) * pair)
            k1_ref[:, sl] = jnp.where(lane < HD_A, kn, ones_cols).astype(BF16)
            k2_ref[:, sl] = jnp.where(lane < HD_A, pltpu.roll(kn, pair - HD_A, 1), ones_cols).astype(BF16)

    for cb in range(2 * MW // LANES):
        conv_head(cb)
    values_t(2 * MW, vtm_ref)
    out_gate()
    attn_q()
    attn_k()
    values_t(4 * MW + 2 * AW, vt_ref)


def _inproj(x, g1, w_main, w_gate, layer, gate_b, conv_w, conv_b, qg, kg, bd, seq, tm):
    n, d = x.shape
    wcols = w_main.shape[2]
    hpt = tm // HALO
    nhalo = n // HALO
    row = lambda i: (i, 0)
    vt_spec = pl.BlockSpec((1, N_A * VT_ROWS, tm), lambda i: (i, 0, 0))
    vt_shape = jax.ShapeDtypeStruct((n // tm, N_A * VT_ROWS, tm), BF16)
    return pl.pallas_call(
        functools.partial(_inproj_kernel, tiles_per_seq=seq // tm),
        grid=(n // tm,),
        in_specs=[
            pl.BlockSpec((HALO, d), lambda i: (jnp.maximum(i * hpt - 1, 0), 0)),
            pl.BlockSpec((tm, d), row),
            pl.BlockSpec((HALO, d), lambda i: (jnp.minimum((i + 1) * hpt, nhalo - 1), 0)),
            _resident((1, d)),
            _resident_layer((d, wcols), layer),
            _resident_layer((d, LANES), layer),
            _resident((1, LANES)),
            _resident((CONV_W, 2 * MW)),
            _resident((1, 2 * MW)),
            _resident((1, 2 * LANES)),
            _resident((1, 2 * LANES)),
            _resident((2 * LANES, 2 * LANES)),
        ],
        out_specs=[
            pl.BlockSpec((MW, tm), lambda i: (0, i)),
            pl.BlockSpec((tm, MW), row),
            vt_spec,
            pl.BlockSpec((tm, MW), row),
            pl.BlockSpec((tm, N_GATES), row),
            pl.BlockSpec((3 * N_GATES, tm), lambda i: (0, i)),
            pl.BlockSpec((tm, AW), row),
            pl.BlockSpec((tm, AW), row),
            pl.BlockSpec((tm, AW), row),
            vt_spec,
        ],
        out_shape=[
            jax.ShapeDtypeStruct((MW, n), BF16),
            jax.ShapeDtypeStruct((n, MW), BF16),
            vt_shape,
            jax.ShapeDtypeStruct((n, MW), BF16),
            jax.ShapeDtypeStruct((n, N_GATES), F32),
            jax.ShapeDtypeStruct((3 * N_GATES, n), F32),
            jax.ShapeDtypeStruct((n, AW), BF16),
            jax.ShapeDtypeStruct((n, AW), BF16),
            jax.ShapeDtypeStruct((n, AW), BF16),
            vt_shape,
        ],
        scratch_shapes=[pltpu.VMEM((tm + 2 * HALO, 2 * MW), F32)],
        compiler_params=pltpu.CompilerParams(
            dimension_semantics=("parallel",), vmem_limit_bytes=VMEM_LIMIT),
        name="inproj",
    )(x, x, x, g1, w_main, w_gate, gate_b, conv_w, conv_b, qg, kg, bd)


def _mlstm_kernel(qt_f, k_f, vt_f, gc_f, gr_f,
                  qt_b, k_b, vt_b, gc_b, gr_b,
                  hf_ref, hb_ref,
                  c_s, m_s):
    L = MLSTM_CHUNK
    chunks = k_f.shape[0] // L

    @pl.when(pl.program_id(1) == 0)
    def _():
        c_s[...] = jnp.zeros_like(c_s)
        m_s[...] = jnp.zeros_like(m_s)

    row = lax.broadcasted_iota(jnp.int32, (L, L), 0)
    col = lax.broadcasted_iota(jnp.int32, (L, L), 1)
    lower = col <= row
    upper = row <= col
    dirs = ((qt_f, k_f, vt_f, gc_f, gr_f, hf_ref), (qt_b, k_b, vt_b, gc_b, gr_b, hb_ref))
    state = [(c_s[idx], m_s[idx][:, :1]) for idx in range(2 * N_M)]

    for step in range(chunks):
        chains = []
        for d, (qt_ref, k_ref, vt_ref, gc_ref, gr_ref, out_ref) in enumerate(dirs):
            tok = slice(step * L, (step + 1) * L) if d == 0 else slice((chunks - 1 - step) * L,
                                                                       (chunks - step) * L)
            i_off = 2 * N_M * d
            f_off = i_off + N_M
            src = gc_ref[tok, :]
            cum_rows = gr_ref[0:N_GATES, tok] if d == 0 else gr_ref[2 * N_GATES:3 * N_GATES, tok]
            gate_rows = gr_ref[N_GATES:2 * N_GATES, tok]
            for hh in range(N_M):
                idx = d * N_M + hh
                chains.append(dict(
                    idx=idx, out_ref=out_ref, tok=tok, cols=slice(hh * HD_M, (hh + 1) * HD_M),
                    mask=upper if d == 0 else lower,
                    src_col=src[:, f_off + hh:f_off + hh + 1],
                    b_row=cum_rows[f_off + hh:f_off + hh + 1, :],
                    i_row=gate_rows[i_off + hh:i_off + hh + 1, :],
                    end=slice(L - 1, L) if d == 0 else slice(0, 1),
                    qt=qt_ref[hh * HD_M:(hh + 1) * HD_M, tok],
                    kb=k_ref[tok, hh * HD_M:(hh + 1) * HD_M],
                    vt=vt_ref[0, hh * VT_ROWS:(hh + 1) * VT_ROWS, tok],
                    c_prev=state[idx][0], m_prev=state[idx][1],
                ))

        for ch in chains:
            ch["s"] = _dot(ch["kb"], ch["qt"])
            ch["inter_num"] = _dot(ch["c_prev"].astype(BF16), ch["qt"])
        for ch in chains:
            dmat = jnp.where(ch["mask"], ch["b_row"] - ch["src_col"], -jnp.inf)
            inter = ch["b_row"] + ch["m_prev"]
            ch["m_t"] = jnp.maximum(inter, jnp.max(dmat, axis=0, keepdims=True))
            ch["scale"] = jnp.exp(inter - ch["m_t"])
            ch["w"] = (ch["s"] * jnp.exp(dmat - ch["m_t"])).astype(BF16)
        for ch in chains:
            num = ch["scale"] * ch["inter_num"] + _dot(ch["vt"], ch["w"])
            den = num[HD_M:HD_M + 1]
            ht = num[:HD_M] / jnp.maximum(jnp.abs(den), jnp.exp(-ch["m_t"]))
            ch["out_ref"][ch["tok"], ch["cols"]] = ht.T.astype(ch["out_ref"].dtype)
        for ch in chains:
            b_end = ch["b_row"][:, ch["end"]]
            g_row = b_end - ch["b_row"] + ch["i_row"]
            m_new = jnp.maximum(b_end + ch["m_prev"], jnp.max(g_row, axis=-1, keepdims=True))
            decay = jnp.exp(b_end + ch["m_prev"] - m_new)
            vt_w = (ch["vt"].astype(F32) * jnp.exp(g_row - m_new)).astype(BF16)
            state[ch["idx"]] = (decay * ch["c_prev"] + _dot(vt_w, ch["kb"]), m_new)

    for idx, (c_new, m_new) in enumerate(state):
        c_s[idx] = c_new
        m_s[idx] = jnp.broadcast_to(m_new, (1, LANES))


def _mlstm(qt, k, vt, gate_cols, gate_rows, batch, seq, tm):
    nt = seq // tm

    def tile_f(b, c):
        return b * nt + c

    def tile_b(b, c):
        return b * nt + (nt - 1 - c)

    def specs(tile):
        return [
            pl.BlockSpec((MW, tm), lambda b, c: (0, tile(b, c))),
            pl.BlockSpec((tm, MW), lambda b, c: (tile(b, c), 0)),
            pl.BlockSpec((1, N_M * VT_ROWS, tm), lambda b, c: (tile(b, c), 0, 0)),
            pl.BlockSpec((tm, N_GATES), lambda b, c: (tile(b, c), 0)),
            pl.BlockSpec((3 * N_GATES, tm), lambda b, c: (0, tile(b, c))),
        ]

    n = batch * seq
    return pl.pallas_call(
        _mlstm_kernel,
        grid=(batch, nt),
        in_specs=specs(tile_f) + specs(tile_b),
        out_specs=[
            pl.BlockSpec((tm, MW), lambda b, c: (tile_f(b, c), 0)),
            pl.BlockSpec((tm, MW), lambda b, c: (tile_b(b, c), 0)),
        ],
        out_shape=[jax.ShapeDtypeStruct((n, MW), BF16)] * 2,
        scratch_shapes=[
            pltpu.VMEM((2 * N_M, VT_ROWS, HD_M), F32),
            pltpu.VMEM((2 * N_M, 1, LANES), F32),
        ],
        compiler_params=pltpu.CompilerParams(
            dimension_semantics=("parallel", "arbitrary"), vmem_limit_bytes=VMEM_LIMIT),
        name="mlstm",
    )(qt, k, vt, gate_cols, gate_rows, qt, k, vt, gate_cols, gate_rows)


def _bias_kernel(rb_ref, out_ref, *, tile):
    hh = pl.program_id(0)
    dd = pl.program_id(1)
    kk = lax.broadcasted_iota(jnp.int32, (tile, tile), 0)
    qq = lax.broadcasted_iota(jnp.int32, (tile, tile), 1)
    rel = (dd - 1) * tile + kk - qq
    nb = REL_BUCKETS // 2
    max_exact = nb // 2
    n = jnp.abs(rel)
    nf = jnp.maximum(n, 1).astype(F32)
    large = max_exact + (jnp.log(nf / max_exact) / math.log(REL_MAX_DIST / max_exact)
                         * (nb - max_exact)).astype(jnp.int32)
    large = jnp.minimum(large, nb - 1)
    bucket = jnp.where(rel > 0, nb, 0) + jnp.where(n < max_exact, n, large)
    val = jnp.zeros((tile, tile), F32)
    for bkt in range(REL_BUCKETS):
        val = jnp.where(bucket == bkt, rb_ref[bkt, hh], val)
    out_ref[0, 0] = jnp.where(dd < 3, val * LOG2E, 0.0)


def _bias_tiles(rel_bias, tile):
    return pl.pallas_call(
        functools.partial(_bias_kernel, tile=tile),
        grid=(N_A, 4),
        in_specs=[pl.BlockSpec(memory_space=pltpu.SMEM)],
        out_specs=pl.BlockSpec((1, 1, tile, tile), lambda h, d: (h, d, 0, 0)),
        out_shape=jax.ShapeDtypeStruct((N_A, 4, tile, tile), F32),
        name="rel_bias_tiles",
    )(rel_bias)


def _sublane_partial_sums(e):
    return jnp.sum(e.reshape(e.shape[0] // 8, 8, e.shape[1]), axis=0)


def _attn_kernel(rb_ref, li_ref, q_ref, k1_ref, k2_ref, vt_ref, qg_ref, kg_ref, bias_ref, lam_ref, sg_ref,
                 o_ref,
                 qt_s, m_s, l_s, acc_s, p0_s, p1_s):
    tq = vt_ref.shape[2]
    for sub in range(q_ref.shape[0] // tq):
        rows = pl.ds(sub * tq, tq)
        _attn_query_tile(pl.program_id(2) * (q_ref.shape[0] // tq) + sub,
                         rb_ref, li_ref, q_ref.at[rows], k1_ref, k2_ref, vt_ref, qg_ref, kg_ref,
                         bias_ref, lam_ref, sg_ref, o_ref.at[rows],
                         qt_s.at[sub], m_s.at[sub], l_s.at[sub], acc_s.at[sub], p0_s.at[sub], p1_s.at[sub])


def _attn_query_tile(qi, rb_ref, li_ref, q_ref, k1_ref, k2_ref, vt_ref, qg_ref, kg_ref, bias_ref,
                     lam_ref, sg_ref, o_ref, qt_s, m_s, l_s, acc_s, p0_s, p1_s):
    hh = pl.program_id(0)
    tq = q_ref.shape[0]
    tk = vt_ref.shape[2]
    nk = vt_ref.shape[0]
    vd = 2 * HD_A
    LEFT, RIGHT, NEAR = 0, 1, 2

    qt = q_ref[...].astype(F32).T
    rid = lax.broadcasted_iota(jnp.int32, qt.shape, 0)
    bases = (qt, jnp.concatenate([qt[HD_A:], qt[:HD_A]], axis=0))
    consts = (rb_ref[FAR_BUCKET, hh] * LOG2E, rb_ref[REL_BUCKETS // 2 + FAR_BUCKET, hh] * LOG2E, 0.0)

    def build_queries(shift):
        for i, base in enumerate(bases):
            body = jnp.where(rid < HD_A, base, 0.0).astype(BF16)
            for kind in range(len(consts)):
                qt_s[2 * kind + i] = body
        rid_o = lax.broadcasted_iota(jnp.int32, (HALO, tq), 0)
        for kind, cst in enumerate(consts):
            rest = jnp.full((HALO, tq), cst - shift, F32)
            extra = jnp.zeros((HALO, tq), F32)
            for r in range(BIAS_ROWS):
                part = rest.astype(BF16).astype(F32)
                extra = jnp.where(rid_o == r, part, extra)
                rest = rest - part
            for i in range(2):
                qt_s[2 * kind + i, HD_A:HD_A + HALO, :] = extra.astype(BF16)

    acc_s[...] = jnp.zeros_like(acc_s)
    l_s[...] = jnp.zeros_like(l_s)

    def key_rows(j):
        return pl.ds(pl.multiple_of(j * tk, tk), tk)

    bucket_vals = [rb_ref[bkt, hh] for bkt in range(REL_BUCKETS)]
    bias_max = functools.reduce(jnp.maximum, bucket_vals) * LOG2E
    bias_min = functools.reduce(jnp.minimum, bucket_vals) * LOG2E
    reach = (HD_A ** 0.5 * LOG2E) * jnp.max(jnp.abs(qg_ref[...])) * jnp.max(jnp.abs(kg_ref[...]))
    shift = reach + bias_max
    span = 2.0 * reach + (bias_max - bias_min)
    bounded = span <= LOGIT_SPAN_LIMIT

    @pl.when(bounded)
    def _():
        build_queries(shift)

        def numerators(j, kind, p_buf, bias=None):
            for i, k_ref in enumerate((k1_ref, k2_ref)):
                s = _dot(k_ref[key_rows(j), :], qt_s[2 * kind + i])
                if bias is not None:
                    s = s + bias
                e = jnp.exp2(s)
                l_s[i] += _sublane_partial_sums(e)
                p_buf[i] = e.astype(BF16)

        def accumulate(j, p_buf):
            vt = vt_ref[j, :vd, :]
            for i in range(2):
                acc_s[i] += _dot(vt, p_buf[i])

        w0 = jnp.clip(qi - 1, 0, nk - ATTN_WINDOW)
        far_tile = lambda t: jnp.where(t < w0, t, t + ATTN_WINDOW)
        far_kind = lambda t: jnp.where(t < w0, LEFT, RIGHT)
        p_bufs = (p0_s, p1_s)
        for r in range(ATTN_WINDOW):
            j = w0 + r
            dlt = j - qi
            near = jnp.abs(dlt) <= 1
            kind = jnp.where(near, NEAR, jnp.where(dlt < 0, LEFT, RIGHT))
            numerators(j, kind, p_bufs[r % 2], bias_ref[0, jnp.where(near, dlt + 1, 3)])
            if r > 0:
                accumulate(j - 1, p_bufs[(r - 1) % 2])
        last_window = w0 + ATTN_WINDOW - 1

        n_far = nk - ATTN_WINDOW
        for t0 in range(0, n_far, 2):
            numerators(far_tile(t0), far_kind(t0), p0_s)
            accumulate(far_tile(t0 - 1) if t0 else last_window, p1_s)
            numerators(far_tile(t0 + 1), far_kind(t0 + 1), p1_s)
            accumulate(far_tile(t0), p0_s)
        accumulate(far_tile(n_far - 1) if n_far else last_window, p1_s)

    @pl.when(jnp.logical_not(bounded))
    def _():
        _online_softmax_tiles(qi, nk, build_queries, key_rows, k1_ref, k2_ref, vt_ref, bias_ref,
                              qt_s, m_s, l_s, acc_s)

    lv = lam_ref[...]
    lam = (jnp.exp(jnp.sum(lv[0:1] * lv[1:2], axis=-1, keepdims=True))
           - jnp.exp(jnp.sum(lv[2:3] * lv[3:4], axis=-1, keepdims=True)) + li_ref[0, 0])
    sums = [jnp.sum(l_s[i], axis=0, keepdims=True) for i in range(2)]
    o = acc_s[0] / sums[0] - lam * (acc_s[1] / sums[1])
    ms = jnp.mean(o * o, axis=0, keepdims=True)
    o = o * lax.rsqrt(ms + EPS) * sg_ref[0] * li_ref[0, 1]
    o_ref[...] = o.T.astype(o_ref.dtype)


def _online_softmax_tiles(qi, nk, build_queries, key_rows, k1_ref, k2_ref, vt_ref, bias_ref,
                          qt_s, m_s, l_s, acc_s):
    LEFT, RIGHT, NEAR = 0, 1, 2
    build_queries(0.0)
    m_s[...] = jnp.full_like(m_s, -jnp.inf)

    def one_tile(j, carry):
        dlt = j - qi
        near = jnp.abs(dlt) <= 1
        kind = jnp.where(near, NEAR, jnp.where(dlt < 0, LEFT, RIGHT))
        bias = bias_ref[0, jnp.where(near, dlt + 1, 3)]
        vt = vt_ref[j, :2 * HD_A, :]
        for i, k_ref in enumerate((k1_ref, k2_ref)):
            s = _dot(k_ref[key_rows(j), :], qt_s[2 * kind + i]) + bias
            m_old = m_s[i]
            m_new = jnp.maximum(m_old, jnp.max(s, axis=0, keepdims=True))
            alpha = jnp.exp2(m_old - m_new)
            e = jnp.exp2(s - m_new)
            l_s[i] = alpha * l_s[i] + _sublane_partial_sums(e)
            acc_s[i] = alpha * acc_s[i] + _dot(vt, e.astype(BF16))
            m_s[i] = m_new
        return carry

    lax.fori_loop(0, nk, one_tile, 0)


def _attention(qn, k1, k2, vt, qg, kg, bias_t, rel_bias, lam_init, lambdas, sub_g_col, batch, seq, tile):
    nt = seq // tile
    nq = nt // ATTN_Q_TILES
    vd = 2 * HD_A
    smem = pl.BlockSpec(memory_space=pltpu.SMEM)
    q_rows = ATTN_Q_TILES * tile
    return pl.pallas_call(
        _attn_kernel,
        grid=(N_A, batch, nq),
        in_specs=[
            smem,
            smem,
            pl.BlockSpec((q_rows, vd), lambda h, b, i: (b * nq + i, h)),
            pl.BlockSpec((seq, vd), lambda h, b, i: (b, h)),
            pl.BlockSpec((seq, vd), lambda h, b, i: (b, h)),
            pl.BlockSpec((nt, VT_ROWS, tile), lambda h, b, i: (b, h, 0)),
            pl.BlockSpec((1, 2 * LANES), lambda h, b, i: (0, 0)),
            pl.BlockSpec((1, 2 * LANES), lambda h, b, i: (0, 0)),
            pl.BlockSpec((1, 4, tile, tile), lambda h, b, i: (h, 0, 0, 0)),
            pl.BlockSpec((4, HD_A), lambda h, b, i: (0, 0)),
            pl.BlockSpec((1, vd, 1), lambda h, b, i: (h, 0, 0)),
        ],
        out_specs=pl.BlockSpec((q_rows, vd), lambda h, b, i: (b * nq + i, h)),
        out_shape=jax.ShapeDtypeStruct((batch * seq, AW), BF16),
        scratch_shapes=[
            pltpu.VMEM((ATTN_Q_TILES, 6, vd, tile), BF16),
            pltpu.VMEM((ATTN_Q_TILES, 2, 1, tile), F32),
            pltpu.VMEM((ATTN_Q_TILES, 2, 8, tile), F32),
            pltpu.VMEM((ATTN_Q_TILES, 2, vd, tile), F32),
            pltpu.VMEM((ATTN_Q_TILES, 2, tile, tile), BF16),
            pltpu.VMEM((ATTN_Q_TILES, 2, tile, tile), BF16),
        ],
        compiler_params=pltpu.CompilerParams(
            dimension_semantics=("parallel", "parallel", "arbitrary"),
            vmem_limit_bytes=VMEM_LIMIT),
        name="diff_attention",
    )(rel_bias, lam_init, qn, k1, k2, vt, qg, kg, bias_t, lambdas, sub_g_col)


def _outffn_kernel(x_ref, hf_ref, hb_ref, og_ref, mb_ref, ng_ref, wo_ref, g2_ref, w1_ref, w2_ref,
                   out_ref, *, ff_chunk):
    hs = hf_ref[...].astype(F32) + hb_ref[...].astype(F32)
    mixed = []
    for hh in range(N_M):
        sl = slice(hh * HD_M, (hh + 1) * HD_M)
        t = hs[:, sl]
        ms = jnp.mean(t * t, axis=-1, keepdims=True)
        mix = t * lax.rsqrt(ms + EPS) * ng_ref[:, sl] * _sigmoid(og_ref[:, sl].astype(F32))
        mixed.append(mix.astype(BF16))
    mixed.append(mb_ref[...])
    x1 = x_ref[...] + _dot(jnp.concatenate(mixed, axis=1), wo_ref[...])
    ms = jnp.mean(x1 * x1, axis=-1, keepdims=True)
    h2 = (x1 * lax.rsqrt(ms + EPS) * g2_ref[...]).astype(BF16)
    out_ref[...] = x1
    for j in range(w1_ref.shape[1] // ff_chunk):
        sl = slice(j * ff_chunk, (j + 1) * ff_chunk)
        u = jnp.maximum(_dot(h2, w1_ref[:, sl]), 0.0)
        out_ref[...] += _dot((u * u).astype(BF16), w2_ref[sl, :])


def _outffn(x, hf, hb, om, mixb, ng, wo, g2, w1, w2, layer, tm):
    n, d = x.shape
    dff = w1.shape[2]
    row = lambda i: (i, 0)
    return pl.pallas_call(
        functools.partial(_outffn_kernel, ff_chunk=1024),
        grid=(n // tm,),
        in_specs=[
            pl.BlockSpec((tm, d), row),
            pl.BlockSpec((tm, MW), row),
            pl.BlockSpec((tm, MW), row),
            pl.BlockSpec((tm, MW), row),
            pl.BlockSpec((tm, AW), row),
            _resident((1, MW)),
            _resident_layer((MW + AW, d), layer),
            _resident((1, d)),
            _resident_layer((d, dff), layer),
            _resident_layer((dff, d), layer),
        ],
        out_specs=pl.BlockSpec((tm, d), row),
        out_shape=jax.ShapeDtypeStruct((n, d), F32),
        compiler_params=pltpu.CompilerParams(
            dimension_semantics=("parallel",), vmem_limit_bytes=VMEM_LIMIT),
        name="outproj_ffn",
    )(x, hf, hb, om, mixb, ng, wo, g2, w1, w2)


def kernel(x, norm1_g, w_in, conv_w, conv_b, gate_b, mlstm_norm_g, q_norm_g, k_norm_g, lambdas,
           diff_norm_g, rel_bias, w_out, norm2_g, w_ff1, w_ff2):
    batch, seq, d = x.shape
    depth = w_in.shape[0]
    n = batch * seq
    tm = tile = ROW_TILE
    assert d == w_in.shape[1] and w_in.shape[2] == C_G + N_GATES + 3 * AW
    assert seq % tile == 0 and tile % MLSTM_CHUNK == 0 and tile >= REL_MAX_DIST
    assert seq // tile >= ATTN_WINDOW and (seq // tile - ATTN_WINDOW) % 2 == 0
    assert (seq // tile) % ATTN_Q_TILES == 0

    w_main = jnp.concatenate([w_in[:, :, :C_G], w_in[:, :, C_G + N_GATES:]], axis=-1).astype(BF16)
    w_gate = jnp.pad(w_in[:, :, C_G:C_G + N_GATES], ((0, 0), (0, 0), (0, LANES - N_GATES))).astype(BF16)
    w_out_b = w_out.astype(BF16)
    w_ff1_b = w_ff1.astype(BF16)
    w_ff2_b = w_ff2.astype(BF16)
    lane = jnp.arange(2 * LANES)
    bd = jnp.where((lane[:, None] // HD_A) == (lane[None, :] // HD_A), 1.0 / HD_A, 0.0).astype(BF16)

    bias_t = _bias_tiles(rel_bias.astype(F32), tile)

    xf = x.reshape(n, d)
    for l in range(depth):
        lam_init = 0.8 - 0.6 * math.exp(-0.3 * l)
        qg = jnp.tile(q_norm_g[l], 2 * LANES // HD_A)[None, :]
        kg = jnp.tile(k_norm_g[l], 2 * LANES // HD_A)[None, :]
        gb = jnp.pad(gate_b[l].reshape(1, N_GATES).astype(F32), ((0, 0), (0, LANES - N_GATES)))
        qtm, km, vtm, om, gate_cols, gate_rows, qn, k1, k2, vt = _inproj(
            xf, norm1_g[l][None, :], w_main, w_gate, l, gb, conv_w[l], conv_b[l][None, :],
            qg, kg, bd, seq, tm)
        hf, hb = _mlstm(qtm, km, vtm, gate_cols, gate_rows, batch, seq, tm)
        mixb = _attention(qn, k1, k2, vt, qg, kg, bias_t, rel_bias.astype(F32),
                          jnp.array([[lam_init, 1.0 - lam_init]], F32), lambdas[l],
                          diff_norm_g[l].reshape(N_A, 2 * HD_A, 1), batch, seq, tile)
        xf = _outffn(xf, hf, hb, om, mixb, mlstm_norm_g[l][None, :], w_out_b,
                     norm2_g[l][None, :], w_ff1_b, w_ff2_b, l, tm)
    return xf.reshape(batch, seq, d)
```

```python
import functools
import math

import jax
import jax.numpy as jnp
from jax import lax
from jax.experimental import pallas as pl
from jax.experimental.pallas import tpu as pltpu

F32 = jnp.float32
BF16 = jnp.bfloat16

N_M = 4
HD_M = 128
MW = N_M * HD_M
MLSTM_CHUNK = 256
CONV_W = 5
N_A = 4
HD_A = 64
AW = N_A * 2 * HD_A
REL_BUCKETS = 32
REL_MAX_DIST = 128
EPS = 1e-6
N_GATES = 4 * N_M
C_G = 4 * MW
LANES = 128
HALO = 16
VMEM_LIMIT = 56 * 1024 * 1024
LOG2E = math.log2(math.e)
VT_ROWS = HD_M + HALO
BIAS_ROWS = 3
LOGIT_SPAN_LIMIT = 96.0
ATTN_WINDOW = 4
ATTN_Q_TILES = 2
ROW_TILE = 512
FAR_BUCKET = REL_BUCKETS // 2 - 1


def _dot(a, b):
    return jnp.dot(a, b, preferred_element_type=F32)


def _dot_exact(a, b):
    return jnp.dot(a, b, preferred_element_type=F32, precision=lax.Precision.HIGHEST)


def _sigmoid(x):
    return 1.0 / (1.0 + jnp.exp(-x))


def _log_sigmoid(x):
    return jnp.minimum(x, 0.0) - jnp.log1p(jnp.exp(-jnp.abs(x)))


def _resident(shape):
    nd = len(shape)
    return pl.BlockSpec(shape, lambda *_: (0,) * nd, pipeline_mode=pl.Buffered(1))


def _resident_layer(shape, layer):
    nd = len(shape)
    return pl.BlockSpec((None,) + tuple(shape), lambda *_: (layer,) + (0,) * nd,
                        pipeline_mode=pl.Buffered(1))


def _inproj_kernel(xp_ref, x_ref, xn_ref, g1_ref, w_ref, wg_ref, gb_ref, cw_ref, cb_ref, qg_ref, kg_ref,
                   bd_ref,
                   qtm_ref, km_ref, vtm_ref, om_ref, gcol_ref, grow_ref, qn_ref, k1_ref, k2_ref, vt_ref,
                   ext_s, *, tiles_per_seq):
    rows = x_ref.shape[0]
    pos = pl.program_id(0) % tiles_per_seq

    def normed(xv):
        ms = jnp.mean(xv * xv, axis=-1, keepdims=True)
        return (xv * lax.rsqrt(ms + EPS) * g1_ref[...]).astype(BF16)

    x = x_ref[...]
    h = normed(x)

    def proj(c0, width):
        return _dot(h, w_ref[:, c0:c0 + width])

    h_prev = jnp.where(pos > 0, normed(xp_ref[...]), jnp.zeros((), BF16))
    h_next = jnp.where(pos < tiles_per_seq - 1, normed(xn_ref[...]), jnp.zeros((), BF16))
    ext_s[...] = _dot(jnp.concatenate([h_prev, h, h_next], axis=0), w_ref[:, 0:2 * MW])

    def conv_head(cb):
        cols = slice(cb * LANES, (cb + 1) * LANES)
        conv = cb_ref[:, cols]
        for j in range(CONV_W):
            off = HALO - CONV_W // 2 + j
            conv = conv + ext_s[off:off + rows, cols] * cw_ref[j:j + 1, cols]
        y = conv * _sigmoid(conv)
        if cb < N_M:
            qtm_ref[cols, :] = y.T.astype(BF16)
        else:
            km_ref[:, cb * LANES - MW:(cb + 1) * LANES - MW] = (y * (HD_M ** -0.5)).astype(BF16)

    ones_rows = jnp.ones((HALO, rows), BF16)
    pair = 2 * LANES

    def values_t(c0, out_ref):
        v4 = proj(c0, N_M * HD_M)
        for hh in range(N_M):
            out_ref[0, hh * VT_ROWS:hh * VT_ROWS + HD_M, :] = (
                v4[:, hh * LANES:(hh + 1) * LANES].T.astype(BF16))
            out_ref[0, hh * VT_ROWS + HD_M:(hh + 1) * VT_ROWS, :] = ones_rows

    def out_gate():
        om_ref[...] = proj(3 * MW, MW).astype(BF16)
        L = MLSTM_CHUNK
        gt = (_dot(h, wg_ref[...]) + gb_ref[...]).T[:N_GATES]
        logf = _log_sigmoid(gt)
        g_under_f = pltpu.roll(gt, N_M, 0)
        upper_f = (lax.broadcasted_iota(jnp.int32, (L, L), 0)
                   <= lax.broadcasted_iota(jnp.int32, (L, L), 1)).astype(F32)
        forward_rows = lax.broadcasted_iota(jnp.int32, (N_GATES, L), 0) < 2 * N_M
        for r in range(rows // L):
            sl = slice(r * L, (r + 1) * L)
            cum = _dot_exact(logf[:, sl], upper_f)
            suf = cum[:, L - 1:L] - cum + logf[:, sl]
            grow_ref[0:N_GATES, sl] = cum
            grow_ref[N_GATES:2 * N_GATES, sl] = gt[:, sl]
            grow_ref[2 * N_GATES:3 * N_GATES, sl] = suf
            src = jnp.where(forward_rows, cum, suf) - g_under_f[:, sl]
            src = jnp.concatenate([src, jnp.zeros((L - N_GATES, L), F32)], axis=0)
            gcol_ref[sl, :] = src.T[:, :N_GATES]

    bd = bd_ref[...]

    def qk_norm(c0, g_ref):
        t = proj(c0, AW)
        halves = []
        for hp in range(AW // pair):
            th = t[:, hp * pair:(hp + 1) * pair]
            msq = _dot((th * th).astype(BF16), bd)
            halves.append(th * lax.rsqrt(msq + EPS) * g_ref[...])
        return halves

    lane = lax.broadcasted_iota(jnp.int32, (rows, pair), 1) % LANES
    ones_cols = jnp.where(lane < HD_A + BIAS_ROWS, 1.0, 0.0)

    def attn_q():
        for hp, qn in enumerate(qk_norm(4 * MW, qg_ref)):
            qn_ref[:, hp * pair:(hp + 1) * pair] = (qn * (HD_A ** -0.5 * LOG2E)).astype(BF16)

    def attn_k():
        for hp, kn in enumerate(qk_norm(4 * MW + AW, kg_ref)):
            sl = slice(hp * pair, (hp + 1) * pair)
            k1_ref[:, sl] = jnp.where(lane < HD_A, kn, ones_cols).astype(BF16)
            k2_ref[:, sl] = jnp.where(lane < HD_A, pltpu.roll(kn, pair - HD_A, 1), ones_cols).astype(BF16)

    for cb in range(2 * MW // LANES):
        conv_head(cb)
    values_t(2 * MW, vtm_ref)
    out_gate()
    attn_q()
    attn_k()
    values_t(4 * MW + 2 * AW, vt_ref)


def _inproj(x, g1, w_main, w_gate, layer, gate_b, conv_w, conv_b, qg, kg, bd, seq, tm):
    n, d = x.shape
    wcols = w_main.shape[2]
    hpt = tm // HALO
    nhalo = n // HALO
    row = lambda i: (i, 0)
    vt_spec = pl.BlockSpec((1, N_A * VT_ROWS, tm), lambda i: (i, 0, 0))
    vt_shape = jax.ShapeDtypeStruct((n // tm, N_A * VT_ROWS, tm), BF16)
    return pl.pallas_call(
        functools.partial(_inproj_kernel, tiles_per_seq=seq // tm),
        grid=(n // tm,),
        in_specs=[
            pl.BlockSpec((HALO, d), lambda i: (jnp.maximum(i * hpt - 1, 0), 0)),
            pl.BlockSpec((tm, d), row),
            pl.BlockSpec((HALO, d), lambda i: (jnp.minimum((i + 1) * hpt, nhalo - 1), 0)),
            _resident((1, d)),
            _resident_layer((d, wcols), layer),
            _resident_layer((d, LANES), layer),
            _resident((1, LANES)),
            _resident((CONV_W, 2 * MW)),
            _resident((1, 2 * MW)),
            _resident((1, 2 * LANES)),
            _resident((1, 2 * LANES)),
            _resident((2 * LANES, 2 * LANES)),
        ],
        out_specs=[
            pl.BlockSpec((MW, tm), lambda i: (0, i)),
            pl.BlockSpec((tm, MW), row),
            vt_spec,
            pl.BlockSpec((tm, MW), row),
            pl.BlockSpec((tm, N_GATES), row),
            pl.BlockSpec((3 * N_GATES, tm), lambda i: (0, i)),
            pl.BlockSpec((tm, AW), row),
            pl.BlockSpec((tm, AW), row),
            pl.BlockSpec((tm, AW), row),
            vt_spec,
        ],
        out_shape=[
            jax.ShapeDtypeStruct((MW, n), BF16),
            jax.ShapeDtypeStruct((n, MW), BF16),
            vt_shape,
            jax.ShapeDtypeStruct((n, MW), BF16),
            jax.ShapeDtypeStruct((n, N_GATES), F32),
            jax.ShapeDtypeStruct((3 * N_GATES, n), F32),
            jax.ShapeDtypeStruct((n, AW), BF16),
            jax.ShapeDtypeStruct((n, AW), BF16),
            jax.ShapeDtypeStruct((n, AW), BF16),
            vt_shape,
        ],
        scratch_shapes=[pltpu.VMEM((tm + 2 * HALO, 2 * MW), F32)],
        compiler_params=pltpu.CompilerParams(
            dimension_semantics=("parallel",), vmem_limit_bytes=VMEM_LIMIT),
        name="inproj",
    )(x, x, x, g1, w_main, w_gate, gate_b, conv_w, conv_b, qg, kg, bd)


def _mlstm_kernel(qt_f, k_f, vt_f, gc_f, gr_f,
                  qt_b, k_b, vt_b, gc_b, gr_b,
                  hf_ref, hb_ref,
                  c_s, m_s):
    L = MLSTM_CHUNK
    chunks = k_f.shape[0] // L

    @pl.when(pl.program_id(1) == 0)
    def _():
        c_s[...] = jnp.zeros_like(c_s)
        m_s[...] = jnp.zeros_like(m_s)

    row = lax.broadcasted_iota(jnp.int32, (L, L), 0)
    col = lax.broadcasted_iota(jnp.int32, (L, L), 1)
    lower = col <= row
    upper = row <= col
    dirs = ((qt_f, k_f, vt_f, gc_f, gr_f, hf_ref), (qt_b, k_b, vt_b, gc_b, gr_b, hb_ref))
    state = [(c_s[idx], m_s[idx][:, :1]) for idx in range(2 * N_M)]

    for step in range(chunks):
        chains = []
        for d, (qt_ref, k_ref, vt_ref, gc_ref, gr_ref, out_ref) in enumerate(dirs):
            tok = slice(step * L, (step + 1) * L) if d == 0 else slice((chunks - 1 - step) * L,
                                                                       (chunks - step) * L)
            i_off = 2 * N_M * d
            f_off = i_off + N_M
            src = gc_ref[tok, :]
            cum_rows = gr_ref[0:N_GATES, tok] if d == 0 else gr_ref[2 * N_GATES:3 * N_GATES, tok]
            gate_rows = gr_ref[N_GATES:2 * N_GATES, tok]
            for hh in range(N_M):
                idx = d * N_M + hh
                chains.append(dict(
                    idx=idx, out_ref=out_ref, tok=tok, cols=slice(hh * HD_M, (hh + 1) * HD_M),
                    mask=upper if d == 0 else lower,
                    src_col=src[:, f_off + hh:f_off + hh + 1],
                    b_row=cum_rows[f_off + hh:f_off + hh + 1, :],
                    i_row=gate_rows[i_off + hh:i_off + hh + 1, :],
                    end=slice(L - 1, L) if d == 0 else slice(0, 1),
                    qt=qt_ref[hh * HD_M:(hh + 1) * HD_M, tok],
                    kb=k_ref[tok, hh * HD_M:(hh + 1) * HD_M],
                    vt=vt_ref[0, hh * VT_ROWS:(hh + 1) * VT_ROWS, tok],
                    c_prev=state[idx][0], m_prev=state[idx][1],
                ))

        for ch in chains:
            ch["s"] = _dot(ch["kb"], ch["qt"])
            ch["inter_num"] = _dot(ch["c_prev"].astype(BF16), ch["qt"])
        for ch in chains:
            dmat = jnp.where(ch["mask"], ch["b_row"] - ch["src_col"], -jnp.inf)
            inter = ch["b_row"] + ch["m_prev"]
            ch["m_t"] = jnp.maximum(inter, jnp.max(dmat, axis=0, keepdims=True))
            ch["scale"] = jnp.exp(inter - ch["m_t"])
            ch["w"] = (ch["s"] * jnp.exp(dmat - ch["m_t"])).astype(BF16)
        for ch in chains:
            num = ch["scale"] * ch["inter_num"] + _dot(ch["vt"], ch["w"])
            den = num[HD_M:HD_M + 1]
            ht = num[:HD_M] / jnp.maximum(jnp.abs(den), jnp.exp(-ch["m_t"]))
            ch["out_ref"][ch["tok"], ch["cols"]] = ht.T.astype(ch["out_ref"].dtype)
        for ch in chains:
            b_end = ch["b_row"][:, ch["end"]]
            g_row = b_end - ch["b_row"] + ch["i_row"]
            m_new = jnp.maximum(b_end + ch["m_prev"], jnp.max(g_row, axis=-1, keepdims=True))
            decay = jnp.exp(b_end + ch["m_prev"] - m_new)
            vt_w = (ch["vt"].astype(F32) * jnp.exp(g_row - m_new)).astype(BF16)
            state[ch["idx"]] = (decay * ch["c_prev"] + _dot(vt_w, ch["kb"]), m_new)

    for idx, (c_new, m_new) in enumerate(state):
        c_s[idx] = c_new
        m_s[idx] = jnp.broadcast_to(m_new, (1, LANES))


def _mlstm(qt, k, vt, gate_cols, gate_rows, batch, seq, tm):
    nt = seq // tm

    def tile_f(b, c):
        return b * nt + c

    def tile_b(b, c):
        return b * nt + (nt - 1 - c)

    def specs(tile):
        return [
            pl.BlockSpec((MW, tm), lambda b, c: (0, tile(b, c))),
            pl.BlockSpec((tm, MW), lambda b, c: (tile(b, c), 0)),
            pl.BlockSpec((1, N_M * VT_ROWS, tm), lambda b, c: (tile(b, c), 0, 0)),
            pl.BlockSpec((tm, N_GATES), lambda b, c: (tile(b, c), 0)),
            pl.BlockSpec((3 * N_GATES, tm), lambda b, c: (0, tile(b, c))),
        ]

    n = batch * seq
    return pl.pallas_call(
        _mlstm_kernel,
        grid=(batch, nt),
        in_specs=specs(tile_f) + specs(tile_b),
        out_specs=[
            pl.BlockSpec((tm, MW), lambda b, c: (tile_f(b, c), 0)),
            pl.BlockSpec((tm, MW), lambda b, c: (tile_b(b, c), 0)),
        ],
        out_shape=[jax.ShapeDtypeStruct((n, MW), BF16)] * 2,
        scratch_shapes=[
            pltpu.VMEM((2 * N_M, VT_ROWS, HD_M), F32),
            pltpu.VMEM((2 * N_M, 1, LANES), F32),
        ],
        compiler_params=pltpu.CompilerParams(
            dimension_semantics=("parallel", "arbitrary"), vmem_limit_bytes=VMEM_LIMIT),
        name="mlstm",
    )(qt, k, vt, gate_cols, gate_rows, qt, k, vt, gate_cols, gate_rows)


def _bias_kernel(rb_ref, out_ref, *, tile):
    hh = pl.program_id(0)
    dd = pl.program_id(1)
    kk = lax.broadcasted_iota(jnp.int32, (tile, tile), 0)
    qq = lax.broadcasted_iota(jnp.int32, (tile, tile), 1)
    rel = (dd - 1) * tile + kk - qq
    nb = REL_BUCKETS // 2
    max_exact = nb // 2
    n = jnp.abs(rel)
    nf = jnp.maximum(n, 1).astype(F32)
    large = max_exact + (jnp.log(nf / max_exact) / math.log(REL_MAX_DIST / max_exact)
                         * (nb - max_exact)).astype(jnp.int32)
    large = jnp.minimum(large, nb - 1)
    bucket = jnp.where(rel > 0, nb, 0) + jnp.where(n < max_exact, n, large)
    val = jnp.zeros((tile, tile), F32)
    for bkt in range(REL_BUCKETS):
        val = jnp.where(bucket == bkt, rb_ref[bkt, hh], val)
    out_ref[0, 0] = jnp.where(dd < 3, val * LOG2E, 0.0)


def _bias_tiles(rel_bias, tile):
    return pl.pallas_call(
        functools.partial(_bias_kernel, tile=tile),
        grid=(N_A, 4),
        in_specs=[pl.BlockSpec(memory_space=pltpu.SMEM)],
        out_specs=pl.BlockSpec((1, 1, tile, tile), lambda h, d: (h, d, 0, 0)),
        out_shape=jax.ShapeDtypeStruct((N_A, 4, tile, tile), F32),
        name="rel_bias_tiles",
    )(rel_bias)


def _sublane_partial_sums(e):
    return jnp.sum(e.reshape(e.shape[0] // 8, 8, e.shape[1]), axis=0)


def _attn_kernel(rb_ref, li_ref, q_ref, k1_ref, k2_ref, vt_ref, qg_ref, kg_ref, bias_ref, lam_ref, sg_ref,
                 o_ref,
                 qt_s, m_s, l_s, acc_s, p0_s, p1_s):
    tq = vt_ref.shape[2]
    for sub in range(q_ref.shape[0] // tq):
        rows = pl.ds(sub * tq, tq)
        _attn_query_tile(pl.program_id(2) * (q_ref.shape[0] // tq) + sub,
                         rb_ref, li_ref, q_ref.at[rows], k1_ref, k2_ref, vt_ref, qg_ref, kg_ref,
                         bias_ref, lam_ref, sg_ref, o_ref.at[rows],
                         qt_s.at[sub], m_s.at[sub], l_s.at[sub], acc_s.at[sub], p0_s.at[sub], p1_s.at[sub])


def _attn_query_tile(qi, rb_ref, li_ref, q_ref, k1_ref, k2_ref, vt_ref, qg_ref, kg_ref, bias_ref,
                     lam_ref, sg_ref, o_ref, qt_s, m_s, l_s, acc_s, p0_s, p1_s):
    hh = pl.program_id(0)
    tq = q_ref.shape[0]
    tk = vt_ref.shape[2]
    nk = vt_ref.shape[0]
    vd = 2 * HD_A
    LEFT, RIGHT, NEAR = 0, 1, 2

    qt = q_ref[...].astype(F32).T
    rid = lax.broadcasted_iota(jnp.int32, qt.shape, 0)
    bases = (qt, jnp.concatenate([qt[HD_A:], qt[:HD_A]], axis=0))
    consts = (rb_ref[FAR_BUCKET, hh] * LOG2E, rb_ref[REL_BUCKETS // 2 + FAR_BUCKET, hh] * LOG2E, 0.0)

    def build_queries(shift):
        for i, base in enumerate(bases):
            body = jnp.where(rid < HD_A, base, 0.0).astype(BF16)
            for kind in range(len(consts)):
                qt_s[2 * kind + i] = body
        rid_o = lax.broadcasted_iota(jnp.int32, (HALO, tq), 0)
        for kind, cst in enumerate(consts):
            rest = jnp.full((HALO, tq), cst - shift, F32)
            extra = jnp.zeros((HALO, tq), F32)
            for r in range(BIAS_ROWS):
                part = rest.astype(BF16).astype(F32)
                extra = jnp.where(rid_o == r, part, extra)
                rest = rest - part
            for i in range(2):
                qt_s[2 * kind + i, HD_A:HD_A + HALO, :] = extra.astype(BF16)

    acc_s[...] = jnp.zeros_like(acc_s)
    l_s[...] = jnp.zeros_like(l_s)

    def key_rows(j):
        return pl.ds(pl.multiple_of(j * tk, tk), tk)

    bucket_vals = [rb_ref[bkt, hh] for bkt in range(REL_BUCKETS)]
    bias_max = functools.reduce(jnp.maximum, bucket_vals) * LOG2E
    bias_min = functools.reduce(jnp.minimum, bucket_vals) * LOG2E
    reach = (HD_A ** 0.5 * LOG2E) * jnp.max(jnp.abs(qg_ref[...])) * jnp.max(jnp.abs(kg_ref[...]))
    shift = reach + bias_max
    span = 2.0 * reach + (bias_max - bias_min)
    bounded = span <= LOGIT_SPAN_LIMIT

    @pl.when(bounded)
    def _():
        build_queries(shift)

        def numerators(j, kind, p_buf, bias=None):
            for i, k_ref in enumerate((k1_ref, k2_ref)):
                s = _dot(k_ref[key_rows(j), :], qt_s[2 * kind + i])
                if bias is not None:
                    s = s + bias
                e = jnp.exp2(s)
                l_s[i] += _sublane_partial_sums(e)
                p_buf[i] = e.astype(BF16)

        def accumulate(j, p_buf):
            vt = vt_ref[j, :vd, :]
            for i in range(2):
                acc_s[i] += _dot(vt, p_buf[i])

        w0 = jnp.clip(qi - 1, 0, nk - ATTN_WINDOW)
        far_tile = lambda t: jnp.where(t < w0, t, t + ATTN_WINDOW)
        far_kind = lambda t: jnp.where(t < w0, LEFT, RIGHT)
        p_bufs = (p0_s, p1_s)
        for r in range(ATTN_WINDOW):
            j = w0 + r
            dlt = j - qi
            near = jnp.abs(dlt) <= 1
            kind = jnp.where(near, NEAR, jnp.where(dlt < 0, LEFT, RIGHT))
            numerators(j, kind, p_bufs[r % 2], bias_ref[0, jnp.where(near, dlt + 1, 3)])
            if r > 0:
                accumulate(j - 1, p_bufs[(r - 1) % 2])
        last_window = w0 + ATTN_WINDOW - 1

        n_far = nk - ATTN_WINDOW
        for t0 in range(0, n_far, 2):
            numerators(far_tile(t0), far_kind(t0), p0_s)
            accumulate(far_tile(t0 - 1) if t0 else last_window, p1_s)
            numerators(far_tile(t0 + 1), far_kind(t0 + 1), p1_s)
            accumulate(far_tile(t0), p0_s)
        accumulate(far_tile(n_far - 1) if n_far else last_window, p1_s)

    @pl.when(jnp.logical_not(bounded))
    def _():
        _online_softmax_tiles(qi, nk, build_queries, key_rows, k1_ref, k2_ref, vt_ref, bias_ref,
                              qt_s, m_s, l_s, acc_s)

    lv = lam_ref[...]
    lam = (jnp.exp(jnp.sum(lv[0:1] * lv[1:2], axis=-1, keepdims=True))
           - jnp.exp(jnp.sum(lv[2:3] * lv[3:4], axis=-1, keepdims=True)) + li_ref[0, 0])
    sums = [jnp.sum(l_s[i], axis=0, keepdims=True) for i in range(2)]
    o = acc_s[0] / sums[0] - lam * (acc_s[1] / sums[1])
    ms = jnp.mean(o * o, axis=0, keepdims=True)
    o = o * lax.rsqrt(ms + EPS) * sg_ref[0] * li_ref[0, 1]
    o_ref[...] = o.T.astype(o_ref.dtype)


def _online_softmax_tiles(qi, nk, build_queries, key_rows, k1_ref, k2_ref, vt_ref, bias_ref,
                          qt_s, m_s, l_s, acc_s):
    LEFT, RIGHT, NEAR = 0, 1, 2
    build_queries(0.0)
    m_s[...] = jnp.full_like(m_s, -jnp.inf)

    def one_tile(j, carry):
        dlt = j - qi
        near = jnp.abs(dlt) <= 1
        kind = jnp.where(near, NEAR, jnp.where(dlt < 0, LEFT, RIGHT))
        bias = bias_ref[0, jnp.where(near, dlt + 1, 3)]
        vt = vt_ref[j, :2 * HD_A, :]
        for i, k_ref in enumerate((k1_ref, k2_ref)):
            s = _dot(k_ref[key_rows(j), :], qt_s[2 * kind + i]) + bias
            m_old = m_s[i]
            m_new = jnp.maximum(m_old, jnp.max(s, axis=0, keepdims=True))
            alpha = jnp.exp2(m_old - m_new)
            e = jnp.exp2(s - m_new)
            l_s[i] = alpha * l_s[i] + _sublane_partial_sums(e)
            acc_s[i] = alpha * acc_s[i] + _dot(vt, e.astype(BF16))
            m_s[i] = m_new
        return carry

    lax.fori_loop(0, nk, one_tile, 0)


def _attention(qn, k1, k2, vt, qg, kg, bias_t, rel_bias, lam_init, lambdas, sub_g_col, batch, seq, tile):
    nt = seq // tile
    nq = nt // ATTN_Q_TILES
    vd = 2 * HD_A
    smem = pl.BlockSpec(memory_space=pltpu.SMEM)
    q_rows = ATTN_Q_TILES * tile
    return pl.pallas_call(
        _attn_kernel,
        grid=(N_A, batch, nq),
        in_specs=[
            smem,
            smem,
            pl.BlockSpec((q_rows, vd), lambda h, b, i: (b * nq + i, h)),
            pl.BlockSpec((seq, vd), lambda h, b, i: (b, h)),
            pl.BlockSpec((seq, vd), lambda h, b, i: (b, h)),
            pl.BlockSpec((nt, VT_ROWS, tile), lambda h, b, i: (b, h, 0)),
            pl.BlockSpec((1, 2 * LANES), lambda h, b, i: (0, 0)),
            pl.BlockSpec((1, 2 * LANES), lambda h, b, i: (0, 0)),
            pl.BlockSpec((1, 4, tile, tile), lambda h, b, i: (h, 0, 0, 0)),
            pl.BlockSpec((4, HD_A), lambda h, b, i: (0, 0)),
            pl.BlockSpec((1, vd, 1), lambda h, b, i: (h, 0, 0)),
        ],
        out_specs=pl.BlockSpec((q_rows, vd), lambda h, b, i: (b * nq + i, h)),
        out_shape=jax.ShapeDtypeStruct((batch * seq, AW), BF16),
        scratch_shapes=[
            pltpu.VMEM((ATTN_Q_TILES, 6, vd, tile), BF16),
            pltpu.VMEM((ATTN_Q_TILES, 2, 1, tile), F32),
            pltpu.VMEM((ATTN_Q_TILES, 2, 8, tile), F32),
            pltpu.VMEM((ATTN_Q_TILES, 2, vd, tile), F32),
            pltpu.VMEM((ATTN_Q_TILES, 2, tile, tile), BF16),
            pltpu.VMEM((ATTN_Q_TILES, 2, tile, tile), BF16),
        ],
        compiler_params=pltpu.CompilerParams(
            dimension_semantics=("parallel", "parallel", "arbitrary"),
            vmem_limit_bytes=VMEM_LIMIT),
        name="diff_attention",
    )(rel_bias, lam_init, qn, k1, k2, vt, qg, kg, bias_t, lambdas, sub_g_col)


def _outffn_kernel(x_ref, hf_ref, hb_ref, og_ref, mb_ref, ng_ref, wo_ref, g2_ref, w1_ref, w2_ref,
                   out_ref, *, ff_chunk):
    x1 = x_ref[...] + _dot(mb_ref[...], wo_ref[MW:, :])
    hs = hf_ref[...].astype(F32) + hb_ref[...].astype(F32)
    mixed = []
    for hh in range(N_M):
        sl = slice(hh * HD_M, (hh + 1) * HD_M)
        t = hs[:, sl]
        ms = jnp.mean(t * t, axis=-1, keepdims=True)
        mix = t * lax.rsqrt(ms + EPS) * ng_ref[:, sl] * _sigmoid(og_ref[:, sl].astype(F32))
        mixed.append(mix.astype(BF16))
    x1 = x1 + _dot(jnp.concatenate(mixed, axis=1), wo_ref[:MW, :])
    ms = jnp.mean(x1 * x1, axis=-1, keepdims=True)
    h2 = (x1 * lax.rsqrt(ms + EPS) * g2_ref[...]).astype(BF16)
    out_ref[...] = x1
    for j in range(w1_ref.shape[1] // ff_chunk):
        sl = slice(j * ff_chunk, (j + 1) * ff_chunk)
        u = jnp.maximum(_dot(h2, w1_ref[:, sl]), 0.0)
        out_ref[...] += _dot((u * u).astype(BF16), w2_ref[sl, :])


def _outffn(x, hf, hb, om, mixb, ng, wo, g2, w1, w2, layer, tm):
    n, d = x.shape
    dff = w1.shape[2]
    row = lambda i: (i, 0)
    return pl.pallas_call(
        functools.partial(_outffn_kernel, ff_chunk=1024),
        grid=(n // tm,),
        in_specs=[
            pl.BlockSpec((tm, d), row),
            pl.BlockSpec((tm, MW), row),
            pl.BlockSpec((tm, MW), row),
            pl.BlockSpec((tm, MW), row),
            pl.BlockSpec((tm, AW), row),
            _resident((1, MW)),
            _resident_layer((MW + AW, d), layer),
            _resident((1, d)),
            _resident_layer((d, dff), layer),
            _resident_layer((dff, d), layer),
        ],
        out_specs=pl.BlockSpec((tm, d), row),
        out_shape=jax.ShapeDtypeStruct((n, d), F32),
        compiler_params=pltpu.CompilerParams(
            dimension_semantics=("parallel",), vmem_limit_bytes=VMEM_LIMIT),
        name="outproj_ffn",
    )(x, hf, hb, om, mixb, ng, wo, g2, w1, w2)


def kernel(x, norm1_g, w_in, conv_w, conv_b, gate_b, mlstm_norm_g, q_norm_g, k_norm_g, lambdas,
           diff_norm_g, rel_bias, w_out, norm2_g, w_ff1, w_ff2):
    batch, seq, d = x.shape
    depth = w_in.shape[0]
    n = batch * seq
    tm = tile = ROW_TILE
    assert d == w_in.shape[1] and w_in.shape[2] == C_G + N_GATES + 3 * AW
    assert seq % tile == 0 and tile % MLSTM_CHUNK == 0 and tile >= REL_MAX_DIST
    assert seq // tile >= ATTN_WINDOW and (seq // tile - ATTN_WINDOW) % 2 == 0
    assert (seq // tile) % ATTN_Q_TILES == 0

    w_main = jnp.concatenate([w_in[:, :, :C_G], w_in[:, :, C_G + N_GATES:]], axis=-1).astype(BF16)
    w_gate = jnp.pad(w_in[:, :, C_G:C_G + N_GATES], ((0, 0), (0, 0), (0, LANES - N_GATES))).astype(BF16)
    w_out_b = w_out.astype(BF16)
    w_ff1_b = w_ff1.astype(BF16)
    w_ff2_b = w_ff2.astype(BF16)
    lane = jnp.arange(2 * LANES)
    bd = jnp.where((lane[:, None] // HD_A) == (lane[None, :] // HD_A), 1.0 / HD_A, 0.0).astype(BF16)

    bias_t = _bias_tiles(rel_bias.astype(F32), tile)

    xf = x.reshape(n, d)
    for l in range(depth):
        lam_init = 0.8 - 0.6 * math.exp(-0.3 * l)
        qg = jnp.tile(q_norm_g[l], 2 * LANES // HD_A)[None, :]
        kg = jnp.tile(k_norm_g[l], 2 * LANES // HD_A)[None, :]
        gb = jnp.pad(gate_b[l].reshape(1, N_GATES).astype(F32), ((0, 0), (0, LANES - N_GATES)))
        qtm, km, vtm, om, gate_cols, gate_rows, qn, k1, k2, vt = _inproj(
            xf, norm1_g[l][None, :], w_main, w_gate, l, gb, conv_w[l], conv_b[l][None, :],
            qg, kg, bd, seq, tm)
        hf, hb = _mlstm(qtm, km, vtm, gate_cols, gate_rows, batch, seq, tm)
        mixb = _attention(qn, k1, k2, vt, qg, kg, bias_t, rel_bias.astype(F32),
                          jnp.array([[lam_init, 1.0 - lam_init]], F32), lambdas[l],
                          diff_norm_g[l].reshape(N_A, 2 * HD_A, 1), batch, seq, tile)
        xf = _outffn(xf, hf, hb, om, mixb, mlstm_norm_g[l][None, :], w_out_b,
                     norm2_g[l][None, :], w_ff1_b, w_ff2_b, l, tm)
    return xf.reshape(batch, seq, d)
```

```python
import functools
import math

import jax
import jax.numpy as jnp
from jax import lax
from jax.experimental import pallas as pl
from jax.experimental.pallas import tpu as pltpu

F32 = jnp.float32
BF16 = jnp.bfloat16

N_M = 4
HD_M = 128
MW = N_M * HD_M
MLSTM_CHUNK = 256
CONV_W = 5
N_A = 4
HD_A = 64
AW = N_A * 2 * HD_A
REL_BUCKETS = 32
REL_MAX_DIST = 128
EPS = 1e-6
N_GATES = 4 * N_M
C_G = 4 * MW
LANES = 128
HALO = 16
VMEM_LIMIT = 56 * 1024 * 1024
LOG2E = math.log2(math.e)
VT_ROWS = HD_M + HALO
BIAS_ROWS = 3
LOGIT_SPAN_LIMIT = 96.0
ATTN_WINDOW = 4
ATTN_Q_TILES = 2
ROW_TILE = 512
FAR_BUCKET = REL_BUCKETS // 2 - 1


def _dot(a, b):
    return jnp.dot(a, b, preferred_element_type=F32)


def _dot_exact(a, b):
    return jnp.dot(a, b, preferred_element_type=F32, precision=lax.Precision.HIGHEST)


def _sigmoid(x):
    return 1.0 / (1.0 + jnp.exp(-x))


def _log_sigmoid(x):
    return jnp.minimum(x, 0.0) - jnp.log1p(jnp.exp(-jnp.abs(x)))


def _resident(shape):
    nd = len(shape)
    return pl.BlockSpec(shape, lambda *_: (0,) * nd, pipeline_mode=pl.Buffered(1))


def _resident_layer(shape, layer):
    nd = len(shape)
    return pl.BlockSpec((None,) + tuple(shape), lambda *_: (layer,) + (0,) * nd,
                        pipeline_mode=pl.Buffered(1))


def _inproj_kernel(xp_ref, x_ref, xn_ref, g1_ref, w_ref, wg_ref, gb_ref, cw_ref, cb_ref, qg_ref, kg_ref,
                   bd_ref,
                   qtm_ref, km_ref, vtm_ref, om_ref, gcol_ref, grow_ref, qn_ref, k1_ref, k2_ref, vt_ref,
                   ext_s, *, tiles_per_seq):
    rows = x_ref.shape[0]
    pos = pl.program_id(0) % tiles_per_seq

    def normed(xv):
        ms = jnp.mean(xv * xv, axis=-1, keepdims=True)
        return (xv * lax.rsqrt(ms + EPS) * g1_ref[...]).astype(BF16)

    x = x_ref[...]
    h = normed(x)

    def proj(c0, width):
        return _dot(h, w_ref[:, c0:c0 + width])

    h_prev = jnp.where(pos > 0, normed(xp_ref[...]), jnp.zeros((), BF16))
    h_next = jnp.where(pos < tiles_per_seq - 1, normed(xn_ref[...]), jnp.zeros((), BF16))
    ext_s[...] = _dot(jnp.concatenate([h_prev, h, h_next], axis=0), w_ref[:, 0:2 * MW])

    def conv_head(cb):
        cols = slice(cb * LANES, (cb + 1) * LANES)
        conv = cb_ref[:, cols]
        for j in range(CONV_W):
            off = HALO - CONV_W // 2 + j
            conv = conv + ext_s[off:off + rows, cols] * cw_ref[j:j + 1, cols]
        y = conv * _sigmoid(conv)
        if cb < N_M:
            qtm_ref[cols, :] = y.T.astype(BF16)
        else:
            km_ref[:, cb * LANES - MW:(cb + 1) * LANES - MW] = (y * (HD_M ** -0.5)).astype(BF16)

    ones_rows = jnp.ones((HALO, rows), BF16)
    pair = 2 * LANES

    def values_t(c0, out_ref):
        v4 = proj(c0, N_M * HD_M)
        for hh in range(N_M):
            out_ref[0, hh * VT_ROWS:hh * VT_ROWS + HD_M, :] = (
                v4[:, hh * LANES:(hh + 1) * LANES].T.astype(BF16))
            out_ref[0, hh * VT_ROWS + HD_M:(hh + 1) * VT_ROWS, :] = ones_rows

    def out_gate():
        om_ref[...] = proj(3 * MW, MW).astype(BF16)
        L = MLSTM_CHUNK
        gt = (_dot(h, wg_ref[...]) + gb_ref[...]).T[:N_GATES]
        logf = _log_sigmoid(gt)
        g_under_f = pltpu.roll(gt, N_M, 0)
        upper_f = (lax.broadcasted_iota(jnp.int32, (L, L), 0)
                   <= lax.broadcasted_iota(jnp.int32, (L, L), 1)).astype(F32)
        forward_rows = lax.broadcasted_iota(jnp.int32, (N_GATES, L), 0) < 2 * N_M
        for r in range(rows // L):
            sl = slice(r * L, (r + 1) * L)
            cum = _dot_exact(logf[:, sl], upper_f)
            suf = cum[:, L - 1:L] - cum + logf[:, sl]
            grow_ref[0:N_GATES, sl] = cum
            grow_ref[N_GATES:2 * N_GATES, sl] = gt[:, sl]
            grow_ref[2 * N_GATES:3 * N_GATES, sl] = suf
            src = jnp.where(forward_rows, cum, suf) - g_under_f[:, sl]
            src = jnp.concatenate([src, jnp.zeros((L - N_GATES, L), F32)], axis=0)
            gcol_ref[sl, :] = src.T[:, :N_GATES]

    bd = bd_ref[...]

    def qk_norm(c0, g_ref):
        t = proj(c0, AW)
        halves = []
        for hp in range(AW // pair):
            th = t[:, hp * pair:(hp + 1) * pair]
            msq = _dot((th * th).astype(BF16), bd)
            halves.append(th * lax.rsqrt(msq + EPS) * g_ref[...])
        return halves

    lane = lax.broadcasted_iota(jnp.int32, (rows, pair), 1) % LANES
    ones_cols = jnp.where(lane < HD_A + BIAS_ROWS, 1.0, 0.0)

    def attn_q():
        for hp, qn in enumerate(qk_norm(4 * MW, qg_ref)):
            qn_ref[:, hp * pair:(hp + 1) * pair] = (qn * (HD_A ** -0.5 * LOG2E)).astype(BF16)

    def attn_k():
        for hp, kn in enumerate(qk_norm(4 * MW + AW, kg_ref)):
            sl = slice(hp * pair, (hp + 1) * pair)
            k1_ref[:, sl] = jnp.where(lane < HD_A, kn, ones_cols).astype(BF16)
            k2_ref[:, sl] = jnp.where(lane < HD_A, pltpu.roll(kn, pair - HD_A, 1), ones_cols).astype(BF16)

    for cb in range(2 * MW // LANES):
        conv_head(cb)
    values_t(2 * MW, vtm_ref)
    out_gate()
    attn_q()
    attn_k()
    values_t(4 * MW + 2 * AW, vt_ref)


def _inproj(x, g1, w_main, w_gate, layer, gate_b, conv_w, conv_b, qg, kg, bd, seq, tm):
    n, d = x.shape
    wcols = w_main.shape[2]
    hpt = tm // HALO
    nhalo = n // HALO
    row = lambda i: (i, 0)
    vt_spec = pl.BlockSpec((1, N_A * VT_ROWS, tm), lambda i: (i, 0, 0))
    vt_shape = jax.ShapeDtypeStruct((n // tm, N_A * VT_ROWS, tm), BF16)
    return pl.pallas_call(
        functools.partial(_inproj_kernel, tiles_per_seq=seq // tm),
        grid=(n // tm,),
        in_specs=[
            pl.BlockSpec((HALO, d), lambda i: (jnp.maximum(i * hpt - 1, 0), 0)),
            pl.BlockSpec((tm, d), row),
            pl.BlockSpec((HALO, d), lambda i: (jnp.minimum((i + 1) * hpt, nhalo - 1), 0)),
            _resident((1, d)),
            _resident_layer((d, wcols), layer),
            _resident_layer((d, LANES), layer),
            _resident((1, LANES)),
            _resident((CONV_W, 2 * MW)),
            _resident((1, 2 * MW)),
            _resident((1, 2 * LANES)),
            _resident((1, 2 * LANES)),
            _resident((2 * LANES, 2 * LANES)),
        ],
        out_specs=[
            pl.BlockSpec((MW, tm), lambda i: (0, i)),
            pl.BlockSpec((tm, MW), row),
            vt_spec,
            pl.BlockSpec((tm, MW), row),
            pl.BlockSpec((tm, N_GATES), row),
            pl.BlockSpec((3 * N_GATES, tm), lambda i: (0, i)),
            pl.BlockSpec((tm, AW), row),
            pl.BlockSpec((tm, AW), row),
            pl.BlockSpec((tm, AW), row),
            vt_spec,
        ],
        out_shape=[
            jax.ShapeDtypeStruct((MW, n), BF16),
            jax.ShapeDtypeStruct((n, MW), BF16),
            vt_shape,
            jax.ShapeDtypeStruct((n, MW), BF16),
            jax.ShapeDtypeStruct((n, N_GATES), F32),
            jax.ShapeDtypeStruct((3 * N_GATES, n), F32),
            jax.ShapeDtypeStruct((n, AW), BF16),
            jax.ShapeDtypeStruct((n, AW), BF16),
            jax.ShapeDtypeStruct((n, AW), BF16),
            vt_shape,
        ],
        scratch_shapes=[pltpu.VMEM((tm + 2 * HALO, 2 * MW), F32)],
        compiler_params=pltpu.CompilerParams(
            dimension_semantics=("parallel",), vmem_limit_bytes=VMEM_LIMIT),
        name="inproj",
    )(x, x, x, g1, w_main, w_gate, gate_b, conv_w, conv_b, qg, kg, bd)


def _mlstm_kernel(qt_f, k_f, vt_f, gc_f, gr_f,
                  qt_b, k_b, vt_b, gc_b, gr_b,
                  hf_ref, hb_ref,
                  c_s, m_s):
    L = MLSTM_CHUNK
    chunks = k_f.shape[0] // L

    @pl.when(pl.program_id(1) == 0)
    def _():
        c_s[...] = jnp.zeros_like(c_s)
        m_s[...] = jnp.zeros_like(m_s)

    row = lax.broadcasted_iota(jnp.int32, (L, L), 0)
    col = lax.broadcasted_iota(jnp.int32, (L, L), 1)
    lower = col <= row
    upper = row <= col
    dirs = ((qt_f, k_f, vt_f, gc_f, gr_f, hf_ref), (qt_b, k_b, vt_b, gc_b, gr_b, hb_ref))
    state = [(c_s[idx], m_s[idx][:, :1]) for idx in range(2 * N_M)]

    for step in range(chunks):
        chains = []
        for d, (qt_ref, k_ref, vt_ref, gc_ref, gr_ref, out_ref) in enumerate(dirs):
            tok = slice(step * L, (step + 1) * L) if d == 0 else slice((chunks - 1 - step) * L,
                                                                       (chunks - step) * L)
            i_off = 2 * N_M * d
            f_off = i_off + N_M
            src = gc_ref[tok, :]
            cum_rows = gr_ref[0:N_GATES, tok] if d == 0 else gr_ref[2 * N_GATES:3 * N_GATES, tok]
            gate_rows = gr_ref[N_GATES:2 * N_GATES, tok]
            for hh in range(N_M):
                idx = d * N_M + hh
                chains.append(dict(
                    idx=idx, out_ref=out_ref, tok=tok, cols=slice(hh * HD_M, (hh + 1) * HD_M),
                    mask=upper if d == 0 else lower,
                    src_col=src[:, f_off + hh:f_off + hh + 1],
                    b_row=cum_rows[f_off + hh:f_off + hh + 1, :],
                    i_row=gate_rows[i_off + hh:i_off + hh + 1, :],
                    end=slice(L - 1, L) if d == 0 else slice(0, 1),
                    qt=qt_ref[hh * HD_M:(hh + 1) * HD_M, tok],
                    kb=k_ref[tok, hh * HD_M:(hh + 1) * HD_M],
                    vt=vt_ref[0, hh * VT_ROWS:(hh + 1) * VT_ROWS, tok],
                    c_prev=state[idx][0], m_prev=state[idx][1],
                ))

        for ch in chains:
            ch["s"] = _dot(ch["kb"], ch["qt"])
            ch["inter_num"] = _dot(ch["c_prev"].astype(BF16), ch["qt"])
        for ch in chains:
            dmat = jnp.where(ch["mask"], ch["b_row"] - ch["src_col"], -jnp.inf)
            inter = ch["b_row"] + ch["m_prev"]
            ch["m_t"] = jnp.maximum(inter, jnp.max(dmat, axis=0, keepdims=True))
            ch["scale"] = jnp.exp(inter - ch["m_t"])
            ch["w"] = (ch["s"] * jnp.exp(dmat - ch["m_t"])).astype(BF16)
        for ch in chains:
            num = ch["scale"] * ch["inter_num"] + _dot(ch["vt"], ch["w"])
            den = num[HD_M:HD_M + 1]
            ht = num[:HD_M] / jnp.maximum(jnp.abs(den), jnp.exp(-ch["m_t"]))
            ch["out_ref"][ch["tok"], ch["cols"]] = ht.T.astype(ch["out_ref"].dtype)
        for ch in chains:
            b_end = ch["b_row"][:, ch["end"]]
            g_row = b_end - ch["b_row"] + ch["i_row"]
            m_new = jnp.maximum(b_end + ch["m_prev"], jnp.max(g_row, axis=-1, keepdims=True))
            decay = jnp.exp(b_end + ch["m_prev"] - m_new)
            vt_w = (ch["vt"].astype(F32) * jnp.exp(g_row - m_new)).astype(BF16)
            state[ch["idx"]] = (decay * ch["c_prev"] + _dot(vt_w, ch["kb"]), m_new)

    for idx, (c_new, m_new) in enumerate(state):
        c_s[idx] = c_new
        m_s[idx] = jnp.broadcast_to(m_new, (1, LANES))


def _mlstm(qt, k, vt, gate_cols, gate_rows, batch, seq, tm):
    nt = seq // tm

    def tile_f(b, c):
        return b * nt + c

    def tile_b(b, c):
        return b * nt + (nt - 1 - c)

    def specs(tile):
        return [
            pl.BlockSpec((MW, tm), lambda b, c: (0, tile(b, c))),
            pl.BlockSpec((tm, MW), lambda b, c: (tile(b, c), 0)),
            pl.BlockSpec((1, N_M * VT_ROWS, tm), lambda b, c: (tile(b, c), 0, 0)),
            pl.BlockSpec((tm, N_GATES), lambda b, c: (tile(b, c), 0)),
            pl.BlockSpec((3 * N_GATES, tm), lambda b, c: (0, tile(b, c))),
        ]

    n = batch * seq
    return pl.pallas_call(
        _mlstm_kernel,
        grid=(batch, nt),
        in_specs=specs(tile_f) + specs(tile_b),
        out_specs=[
            pl.BlockSpec((tm, MW), lambda b, c: (tile_f(b, c), 0)),
            pl.BlockSpec((tm, MW), lambda b, c: (tile_b(b, c), 0)),
        ],
        out_shape=[jax.ShapeDtypeStruct((n, MW), BF16)] * 2,
        scratch_shapes=[
            pltpu.VMEM((2 * N_M, VT_ROWS, HD_M), F32),
            pltpu.VMEM((2 * N_M, 1, LANES), F32),
        ],
        compiler_params=pltpu.CompilerParams(
            dimension_semantics=("parallel", "arbitrary"), vmem_limit_bytes=VMEM_LIMIT),
        name="mlstm",
    )(qt, k, vt, gate_cols, gate_rows, qt, k, vt, gate_cols, gate_rows)


def _bias_kernel(rb_ref, out_ref, *, tile):
    hh = pl.program_id(0)
    dd = pl.program_id(1)
    kk = lax.broadcasted_iota(jnp.int32, (tile, tile), 0)
    qq = lax.broadcasted_iota(jnp.int32, (tile, tile), 1)
    rel = (dd - 1) * tile + kk - qq
    nb = REL_BUCKETS // 2
    max_exact = nb // 2
    n = jnp.abs(rel)
    nf = jnp.maximum(n, 1).astype(F32)
    large = max_exact + (jnp.log(nf / max_exact) / math.log(REL_MAX_DIST / max_exact)
                         * (nb - max_exact)).astype(jnp.int32)
    large = jnp.minimum(large, nb - 1)
    bucket = jnp.where(rel > 0, nb, 0) + jnp.where(n < max_exact, n, large)
    val = jnp.zeros((tile, tile), F32)
    for bkt in range(REL_BUCKETS):
        val = jnp.where(bucket == bkt, rb_ref[bkt, hh], val)
    out_ref[0, 0] = jnp.where(dd < 3, val * LOG2E, 0.0)


def _bias_tiles(rel_bias, tile):
    return pl.pallas_call(
        functools.partial(_bias_kernel, tile=tile),
        grid=(N_A, 4),
        in_specs=[pl.BlockSpec(memory_space=pltpu.SMEM)],
        out_specs=pl.BlockSpec((1, 1, tile, tile), lambda h, d: (h, d, 0, 0)),
        out_shape=jax.ShapeDtypeStruct((N_A, 4, tile, tile), F32),
        name="rel_bias_tiles",
    )(rel_bias)


def _sublane_partial_sums(e):
    return jnp.sum(e.reshape(e.shape[0] // 8, 8, e.shape[1]), axis=0)


def _attn_kernel(rb_ref, li_ref, q_ref, k1_ref, k2_ref, vt_ref, qg_ref, kg_ref, bias_ref, lam_ref, sg_ref,
                 o_ref,
                 qt_s, m_s, l_s, acc_s, p0_s, p1_s):
    tq = vt_ref.shape[2]
    for sub in range(q_ref.shape[0] // tq):
        rows = pl.ds(sub * tq, tq)
        _attn_query_tile(pl.program_id(2) * (q_ref.shape[0] // tq) + sub,
                         rb_ref, li_ref, q_ref.at[rows], k1_ref, k2_ref, vt_ref, qg_ref, kg_ref,
                         bias_ref, lam_ref, sg_ref, o_ref.at[rows],
                         qt_s.at[sub], m_s.at[sub], l_s.at[sub], acc_s.at[sub], p0_s.at[sub], p1_s.at[sub])


def _attn_query_tile(qi, rb_ref, li_ref, q_ref, k1_ref, k2_ref, vt_ref, qg_ref, kg_ref, bias_ref,
                     lam_ref, sg_ref, o_ref, qt_s, m_s, l_s, acc_s, p0_s, p1_s):
    hh = pl.program_id(0)
    tq = q_ref.shape[0]
    tk = vt_ref.shape[2]
    nk = vt_ref.shape[0]
    vd = 2 * HD_A
    LEFT, RIGHT, NEAR = 0, 1, 2

    qt = q_ref[...].astype(F32).T
    rid = lax.broadcasted_iota(jnp.int32, qt.shape, 0)
    bases = (qt, jnp.concatenate([qt[HD_A:], qt[:HD_A]], axis=0))
    consts = (rb_ref[FAR_BUCKET, hh] * LOG2E, rb_ref[REL_BUCKETS // 2 + FAR_BUCKET, hh] * LOG2E, 0.0)

    def build_queries(shift):
        for i, base in enumerate(bases):
            body = jnp.where(rid < HD_A, base, 0.0).astype(BF16)
            for kind in range(len(consts)):
                qt_s[2 * kind + i] = body
        rid_o = lax.broadcasted_iota(jnp.int32, (HALO, tq), 0)
        for kind, cst in enumerate(consts):
            rest = jnp.full((HALO, tq), cst - shift, F32)
            extra = jnp.zeros((HALO, tq), F32)
            for r in range(BIAS_ROWS):
                part = rest.astype(BF16).astype(F32)
                extra = jnp.where(rid_o == r, part, extra)
                rest = rest - part
            for i in range(2):
                qt_s[2 * kind + i, HD_A:HD_A + HALO, :] = extra.astype(BF16)

    def key_rows(j):
        return pl.ds(pl.multiple_of(j * tk, tk), tk)

    bucket_vals = [rb_ref[bkt, hh] for bkt in range(REL_BUCKETS)]
    bias_max = functools.reduce(jnp.maximum, bucket_vals) * LOG2E
    bias_min = functools.reduce(jnp.minimum, bucket_vals) * LOG2E
    reach = (HD_A ** 0.5 * LOG2E) * jnp.max(jnp.abs(qg_ref[...])) * jnp.max(jnp.abs(kg_ref[...]))
    shift = reach + bias_max
    span = 2.0 * reach + (bias_max - bias_min)
    bounded = span <= LOGIT_SPAN_LIMIT

    @pl.when(bounded)
    def _():
        build_queries(shift)

        def numerators(j, kind, p_buf, bias=None, first=False):
            for i, k_ref in enumerate((k1_ref, k2_ref)):
                s = _dot(k_ref[key_rows(j), :], qt_s[2 * kind + i])
                if bias is not None:
                    s = s + bias
                e = jnp.exp2(s)
                sums = _sublane_partial_sums(e)
                l_s[i] = sums if first else l_s[i] + sums
                p_buf[i] = e.astype(BF16)

        def accumulate(j, p_buf, first=False):
            vt = vt_ref[j, :vd, :]
            for i in range(2):
                pv = _dot(vt, p_buf[i])
                acc_s[i] = pv if first else acc_s[i] + pv

        w0 = jnp.clip(qi - 1, 0, nk - ATTN_WINDOW)
        far_tile = lambda t: jnp.where(t < w0, t, t + ATTN_WINDOW)
        far_kind = lambda t: jnp.where(t < w0, LEFT, RIGHT)
        p_bufs = (p0_s, p1_s)
        for r in range(ATTN_WINDOW):
            j = w0 + r
            dlt = j - qi
            near = jnp.abs(dlt) <= 1
            kind = jnp.where(near, NEAR, jnp.where(dlt < 0, LEFT, RIGHT))
            numerators(j, kind, p_bufs[r % 2], bias_ref[0, jnp.where(near, dlt + 1, 3)], first=r == 0)
            if r > 0:
                accumulate(j - 1, p_bufs[(r - 1) % 2], first=r == 1)
        last_window = w0 + ATTN_WINDOW - 1

        n_far = nk - ATTN_WINDOW
        for t0 in range(0, n_far, 2):
            numerators(far_tile(t0), far_kind(t0), p0_s)
            accumulate(far_tile(t0 - 1) if t0 else last_window, p1_s)
            numerators(far_tile(t0 + 1), far_kind(t0 + 1), p1_s)
            accumulate(far_tile(t0), p0_s)
        accumulate(far_tile(n_far - 1) if n_far else last_window, p1_s)

    @pl.when(jnp.logical_not(bounded))
    def _():
        _online_softmax_tiles(qi, nk, build_queries, key_rows, k1_ref, k2_ref, vt_ref, bias_ref,
                              qt_s, m_s, l_s, acc_s)

    lv = lam_ref[...]
    lam = (jnp.exp(jnp.sum(lv[0:1] * lv[1:2], axis=-1, keepdims=True))
           - jnp.exp(jnp.sum(lv[2:3] * lv[3:4], axis=-1, keepdims=True)) + li_ref[0, 0])
    sums = [jnp.sum(l_s[i], axis=0, keepdims=True) for i in range(2)]
    o = acc_s[0] / sums[0] - lam * (acc_s[1] / sums[1])
    ms = jnp.mean(o * o, axis=0, keepdims=True)
    o = o * lax.rsqrt(ms + EPS) * sg_ref[0] * li_ref[0, 1]
    o_ref[...] = o.T.astype(o_ref.dtype)


def _online_softmax_tiles(qi, nk, build_queries, key_rows, k1_ref, k2_ref, vt_ref, bias_ref,
                          qt_s, m_s, l_s, acc_s):
    LEFT, RIGHT, NEAR = 0, 1, 2
    build_queries(0.0)
    m_s[...] = jnp.full_like(m_s, -jnp.inf)
    acc_s[...] = jnp.zeros_like(acc_s)
    l_s[...] = jnp.zeros_like(l_s)

    def one_tile(j, carry):
        dlt = j - qi
        near = jnp.abs(dlt) <= 1
        kind = jnp.where(near, NEAR, jnp.where(dlt < 0, LEFT, RIGHT))
        bias = bias_ref[0, jnp.where(near, dlt + 1, 3)]
        vt = vt_ref[j, :2 * HD_A, :]
        for i, k_ref in enumerate((k1_ref, k2_ref)):
            s = _dot(k_ref[key_rows(j), :], qt_s[2 * kind + i]) + bias
            m_old = m_s[i]
            m_new = jnp.maximum(m_old, jnp.max(s, axis=0, keepdims=True))
            alpha = jnp.exp2(m_old - m_new)
            e = jnp.exp2(s - m_new)
            l_s[i] = alpha * l_s[i] + _sublane_partial_sums(e)
            acc_s[i] = alpha * acc_s[i] + _dot(vt, e.astype(BF16))
            m_s[i] = m_new
        return carry

    lax.fori_loop(0, nk, one_tile, 0)


def _attention(qn, k1, k2, vt, qg, kg, bias_t, rel_bias, lam_init, lambdas, sub_g_col, batch, seq, tile):
    nt = seq // tile
    nq = nt // ATTN_Q_TILES
    vd = 2 * HD_A
    smem = pl.BlockSpec(memory_space=pltpu.SMEM)
    q_rows = ATTN_Q_TILES * tile
    return pl.pallas_call(
        _attn_kernel,
        grid=(N_A, batch, nq),
        in_specs=[
            smem,
            smem,
            pl.BlockSpec((q_rows, vd), lambda h, b, i: (b * nq + i, h)),
            pl.BlockSpec((seq, vd), lambda h, b, i: (b, h)),
            pl.BlockSpec((seq, vd), lambda h, b, i: (b, h)),
            pl.BlockSpec((nt, VT_ROWS, tile), lambda h, b, i: (b, h, 0)),
            pl.BlockSpec((1, 2 * LANES), lambda h, b, i: (0, 0)),
            pl.BlockSpec((1, 2 * LANES), lambda h, b, i: (0, 0)),
            pl.BlockSpec((1, 4, tile, tile), lambda h, b, i: (h, 0, 0, 0)),
            pl.BlockSpec((4, HD_A), lambda h, b, i: (0, 0)),
            pl.BlockSpec((1, vd, 1), lambda h, b, i: (h, 0, 0)),
        ],
        out_specs=pl.BlockSpec((q_rows, vd), lambda h, b, i: (b * nq + i, h)),
        out_shape=jax.ShapeDtypeStruct((batch * seq, AW), BF16),
        scratch_shapes=[
            pltpu.VMEM((ATTN_Q_TILES, 6, vd, tile), BF16),
            pltpu.VMEM((ATTN_Q_TILES, 2, 1, tile), F32),
            pltpu.VMEM((ATTN_Q_TILES, 2, 8, tile), F32),
            pltpu.VMEM((ATTN_Q_TILES, 2, vd, tile), F32),
            pltpu.VMEM((ATTN_Q_TILES, 2, tile, tile), BF16),
            pltpu.VMEM((ATTN_Q_TILES, 2, tile, tile), BF16),
        ],
        compiler_params=pltpu.CompilerParams(
            dimension_semantics=("parallel", "parallel", "arbitrary"),
            vmem_limit_bytes=VMEM_LIMIT),
        name="diff_attention",
    )(rel_bias, lam_init, qn, k1, k2, vt, qg, kg, bias_t, lambdas, sub_g_col)


def _outffn_kernel(x_ref, hf_ref, hb_ref, og_ref, mb_ref, ng_ref, wo_ref, g2_ref, w1_ref, w2_ref,
                   out_ref, *, ff_chunk):
    x1 = x_ref[...] + _dot(mb_ref[...], wo_ref[MW:, :])
    hs = hf_ref[...].astype(F32) + hb_ref[...].astype(F32)
    mixed = []
    for hh in range(N_M):
        sl = slice(hh * HD_M, (hh + 1) * HD_M)
        t = hs[:, sl]
        ms = jnp.mean(t * t, axis=-1, keepdims=True)
        mix = t * lax.rsqrt(ms + EPS) * ng_ref[:, sl] * _sigmoid(og_ref[:, sl].astype(F32))
        mixed.append(mix.astype(BF16))
    x1 = x1 + _dot(jnp.concatenate(mixed, axis=1), wo_ref[:MW, :])
    ms = jnp.mean(x1 * x1, axis=-1, keepdims=True)
    h2 = (x1 * lax.rsqrt(ms + EPS) * g2_ref[...]).astype(BF16)
    out_ref[...] = x1
    for j in range(w1_ref.shape[1] // ff_chunk):
        sl = slice(j * ff_chunk, (j + 1) * ff_chunk)
        u = jnp.maximum(_dot(h2, w1_ref[:, sl]), 0.0)
        out_ref[...] += _dot((u * u).astype(BF16), w2_ref[sl, :])


def _outffn(x, hf, hb, om, mixb, ng, wo, g2, w1, w2, layer, tm):
    n, d = x.shape
    dff = w1.shape[2]
    row = lambda i: (i, 0)
    return pl.pallas_call(
        functools.partial(_outffn_kernel, ff_chunk=1024),
        grid=(n // tm,),
        in_specs=[
            pl.BlockSpec((tm, d), row),
            pl.BlockSpec((tm, MW), row),
            pl.BlockSpec((tm, MW), row),
            pl.BlockSpec((tm, MW), row),
            pl.BlockSpec((tm, AW), row),
            _resident((1, MW)),
            _resident_layer((MW + AW, d), layer),
            _resident((1, d)),
            _resident_layer((d, dff), layer),
            _resident_layer((dff, d), layer),
        ],
        out_specs=pl.BlockSpec((tm, d), row),
        out_shape=jax.ShapeDtypeStruct((n, d), F32),
        compiler_params=pltpu.CompilerParams(
            dimension_semantics=("parallel",), vmem_limit_bytes=VMEM_LIMIT),
        name="outproj_ffn",
    )(x, hf, hb, om, mixb, ng, wo, g2, w1, w2)


def kernel(x, norm1_g, w_in, conv_w, conv_b, gate_b, mlstm_norm_g, q_norm_g, k_norm_g, lambdas,
           diff_norm_g, rel_bias, w_out, norm2_g, w_ff1, w_ff2):
    batch, seq, d = x.shape
    depth = w_in.shape[0]
    n = batch * seq
    tm = tile = ROW_TILE
    assert d == w_in.shape[1] and w_in.shape[2] == C_G + N_GATES + 3 * AW
    assert seq % tile == 0 and tile % MLSTM_CHUNK == 0 and tile >= REL_MAX_DIST
    assert seq // tile >= ATTN_WINDOW and (seq // tile - ATTN_WINDOW) % 2 == 0
    assert (seq // tile) % ATTN_Q_TILES == 0

    w_main = jnp.concatenate([w_in[:, :, :C_G], w_in[:, :, C_G + N_GATES:]], axis=-1).astype(BF16)
    w_gate = jnp.pad(w_in[:, :, C_G:C_G + N_GATES], ((0, 0), (0, 0), (0, LANES - N_GATES))).astype(BF16)
    w_out_b = w_out.astype(BF16)
    w_ff1_b = w_ff1.astype(BF16)
    w_ff2_b = w_ff2.astype(BF16)
    lane = jnp.arange(2 * LANES)
    bd = jnp.where((lane[:, None] // HD_A) == (lane[None, :] // HD_A), 1.0 / HD_A, 0.0).astype(BF16)

    bias_t = _bias_tiles(rel_bias.astype(F32), tile)

    xf = x.reshape(n, d)
    for l in range(depth):
        lam_init = 0.8 - 0.6 * math.exp(-0.3 * l)
        qg = jnp.tile(q_norm_g[l], 2 * LANES // HD_A)[None, :]
        kg = jnp.tile(k_norm_g[l], 2 * LANES // HD_A)[None, :]
        gb = jnp.pad(gate_b[l].reshape(1, N_GATES).astype(F32), ((0, 0), (0, LANES - N_GATES)))
        qtm, km, vtm, om, gate_cols, gate_rows, qn, k1, k2, vt = _inproj(
            xf, norm1_g[l][None, :], w_main, w_gate, l, gb, conv_w[l], conv_b[l][None, :],
            qg, kg, bd, seq, tm)
        hf, hb = _mlstm(qtm, km, vtm, gate_cols, gate_rows, batch, seq, tm)
        mixb = _attention(qn, k1, k2, vt, qg, kg, bias_t, rel_bias.astype(F32),
                          jnp.array([[lam_init, 1.0 - lam_init]], F32), lambdas[l],
                          diff_norm_g[l].reshape(N_A, 2 * HD_A, 1), batch, seq, tile)
        xf = _outffn(xf, hf, hb, om, mixb, mlstm_norm_g[l][None, :], w_out_b,
                     norm2_g[l][None, :], w_ff1_b, w_ff2_b, l, tm)
    return xf.reshape(batch, seq, d)
```
